```python
import jax, jax.numpy as jnp
from jax import lax
import numpy as np

D_MODEL = 1024
BATCH = 8
SEQ = 2048
DEPTH = 1
DEC_BATCH = 128
DEC_SEQ = 4
PAST_LEN = 16384
PAGE_SIZE = 128

GM_WIDTH = D_MODEL
GM_GROUPS = 4
GM_CHUNK = 128
ML_HEADS = 4
ML_HEAD_DIM = D_MODEL // ML_HEADS
ML_WIDTH = ML_HEADS * ML_HEAD_DIM
ML_CHUNK = 64
CONV_W = 4
D_FF = -(-8 * D_MODEL // (3 * 256)) * 256
EPS = 1e-6
IN_SPLITS = (GM_WIDTH, GM_WIDTH, ML_WIDTH, ML_WIDTH, ML_WIDTH, ML_WIDTH, ML_HEADS, ML_HEADS, D_MODEL, D_MODEL)
IN_COLS = sum(IN_SPLITS)

kernel_name = "gated_gmlp_mlstm_hybrid_step"


def rmsnorm(x, g):
    xf = x.astype(jnp.float32)
    y = xf * lax.rsqrt(jnp.mean(xf * xf, axis=-1, keepdims=True) + EPS)
    return y.astype(x.dtype) * g


def layernorm(x, g, b):
    xf = x.astype(jnp.float32)
    mu = jnp.mean(xf, axis=-1, keepdims=True)
    var = jnp.mean(jnp.square(xf - mu), axis=-1, keepdims=True)
    return ((xf - mu) * lax.rsqrt(var + EPS)).astype(x.dtype) * g + b


def head_norm(h, g):
    mu = jnp.mean(h, axis=-1, keepdims=True)
    var = jnp.mean(jnp.square(h - mu), axis=-1, keepdims=True)
    return (h - mu) * lax.rsqrt(var + EPS) * g.reshape(ML_HEADS, ML_HEAD_DIM).astype(jnp.float32)


def gmlp_spatial(u, v, w_s, b_s, L):
    B, T, W = v.shape
    n_chunks = T // L
    ws = jnp.tril(w_s[:, :L, :L])
    vb = v.reshape(B, n_chunks, L, GM_GROUPS, W // GM_GROUPS)
    s = jnp.einsum('gts,bnsgc->bntgc', ws, vb) + b_s[:, :L].T[None, None, :, :, None]
    return u * s.reshape(B, T, W)


def causal_conv(xp, w, b, T):
    out = b
    for j in range(CONV_W):
        out = out + w[j] * xp[:, j:j + T]
    return out


def mlstm_chunkwise(q, k, v, i_pre, logf, C0, n0, m0, L):
    B, T, H, DH = q.shape
    N = T // L
    to_chunks = lambda a: a.reshape(B, N, L, H, DH).transpose(1, 0, 3, 2, 4)
    gate_chunks = lambda a: a.reshape(B, N, L, H).transpose(1, 0, 3, 2)
    mask = jnp.tril(jnp.ones((L, L), dtype=bool))

    def step(carry, inp):
        C, n, m = carry
        qc, kc, vc, ic, fc = inp
        bcum = jnp.cumsum(fc, axis=-1)
        dmat = bcum[..., :, None] - bcum[..., None, :] + ic[..., None, :]
        dmat = jnp.where(mask, dmat, -jnp.inf)
        inter = bcum + m[..., None]
        m_t = jnp.maximum(inter, jnp.max(dmat, axis=-1))
        w_inter = jnp.exp(inter - m_t)
        s = jnp.exp(dmat - m_t[..., None]) * jnp.einsum('bhtd,bhsd->bhts', qc, kc)
        num = w_inter[..., None] * jnp.einsum('bhvd,bhtd->bhtv', C, qc) + jnp.einsum('bhts,bhsv->bhtv', s, vc)
        den = w_inter * jnp.einsum('bhd,bhtd->bht', n, qc) + jnp.sum(s, axis=-1)
        h = num / jnp.maximum(jnp.abs(den), jnp.exp(-m_t))[..., None]
        b_last = bcum[..., -1]
        tail = b_last[..., None] - bcum + ic
        m_new = jnp.maximum(b_last + m, jnp.max(tail, axis=-1))
        decay = jnp.exp(b_last + m - m_new)
        w_in = jnp.exp(tail - m_new[..., None])
        C_new = decay[..., None, None] * C + jnp.einsum('bhs,bhsv,bhsd->bhvd', w_in, vc, kc)
        n_new = decay[..., None] * n + jnp.einsum('bhs,bhsd->bhd', w_in, kc)
        return (C_new, n_new, m_new), h

    (C, n, m), h = lax.scan(step, (C0, n0, m0),
                            (to_chunks(q), to_chunks(k), to_chunks(v), gate_chunks(i_pre), gate_chunks(logf)))
    h = h.transpose(1, 0, 3, 2, 4).reshape(B, T, H, DH)
    return h, C, n, m


def hybrid_layer(x, conv_state, C0, n0, m0, gm_chunk, ml_chunk,
                 g_norm1, w_in, b_i, b_f, ln_g, ln_b, w_s, b_s, conv_w, conv_b, hn_g,
                 b_gate, w_proj_a, w_proj_b, w_out, g_norm2, w_ffn_in, w_ffn_out):
    B, T, _ = x.shape
    h = rmsnorm(x, g_norm1)
    z = h @ w_in
    bounds = np.cumsum(IN_SPLITS)[:-1].tolist()
    u_gm, v_gm, q_raw, k_raw, v_ml, o_ml, i_raw, f_raw, ga, gb = jnp.split(z, bounds, axis=-1)

    u = jax.nn.gelu(u_gm, approximate=False)
    v_n = layernorm(jax.nn.gelu(v_gm, approximate=False), ln_g, ln_b)
    a_out = gmlp_spatial(u, v_n, w_s, b_s, gm_chunk)

    qk_raw = jnp.concatenate([q_raw, k_raw], axis=-1)
    qk_pad = jnp.concatenate([conv_state.astype(qk_raw.dtype), qk_raw], axis=1)
    conv_new = qk_pad[:, -(CONV_W - 1):]
    qk = jax.nn.silu(causal_conv(qk_pad, conv_w, conv_b, T))
    q = qk[..., :ML_WIDTH].reshape(B, T, ML_HEADS, ML_HEAD_DIM).astype(jnp.float32)
    k = (qk[..., ML_WIDTH:].reshape(B, T, ML_HEADS, ML_HEAD_DIM).astype(jnp.float32)
         * (ML_HEAD_DIM ** -0.5))
    vm = v_ml.reshape(B, T, ML_HEADS, ML_HEAD_DIM).astype(jnp.float32)
    i_pre = (i_raw + b_i).astype(jnp.float32)
    logf = jax.nn.log_sigmoid((f_raw + b_f).astype(jnp.float32))
    hm, C, n, m = mlstm_chunkwise(q, k, vm, i_pre, logf,
                                  C0.astype(jnp.float32), n0.astype(jnp.float32), m0.astype(jnp.float32),
                                  ml_chunk)
    hm = head_norm(hm, hn_g).reshape(B, T, ML_WIDTH).astype(x.dtype)
    b_out = jax.nn.sigmoid(o_ml) * hm

    merged = (jax.nn.sigmoid(ga + b_gate[0]) * (a_out @ w_proj_a)
              + jax.nn.sigmoid(gb + b_gate[1]) * (b_out @ w_proj_b))
    x = x + merged @ w_out

    h2 = rmsnorm(x, g_norm2)
    gt, up = jnp.split(h2 @ w_ffn_in, 2, axis=-1)
    x = x + (jax.nn.silu(gt) * up) @ w_ffn_out
    return x, v_n, conv_new, C, n, m


def setup_inputs(seed: int = 0) -> dict:
    key = jax.random.key(seed)
    ks = jax.random.split(key, 32)
    nrm = lambda k, shape, s: jax.random.normal(k, shape, jnp.float32) * s
    f32 = jnp.float32
    inputs = {
        "x_prompt": nrm(ks[0], (BATCH, SEQ, D_MODEL), 1.0),
        "x_sample": nrm(ks[1], (DEC_BATCH, DEC_SEQ, D_MODEL), 1.0),
        "state_conv": nrm(ks[2], (DEPTH, DEC_BATCH, CONV_W - 1, 2 * ML_WIDTH), 1.0),
        "state_C": nrm(ks[3], (DEPTH, DEC_BATCH, ML_HEADS, ML_HEAD_DIM, ML_HEAD_DIM), ML_HEAD_DIM ** -0.5),
        "state_n": nrm(ks[4], (DEPTH, DEC_BATCH, ML_HEADS, ML_HEAD_DIM), 0.5),
        "state_m": nrm(ks[5], (DEPTH, DEC_BATCH, ML_HEADS), 1.0),
        "g_norm1": 1.0 + nrm(ks[6], (DEPTH, D_MODEL), 0.02),
        "w_in": nrm(ks[7], (DEPTH, D_MODEL, IN_COLS), D_MODEL ** -0.5),
        "b_i": nrm(ks[8], (DEPTH, ML_HEADS), 0.1),
        "b_f": jnp.linspace(3.0, 6.0, ML_HEADS, dtype=f32)[None, :] + nrm(ks[9], (DEPTH, ML_HEADS), 0.1),
        "ln_g": 1.0 + nrm(ks[10], (DEPTH, GM_WIDTH), 0.02),
        "ln_b": nrm(ks[11], (DEPTH, GM_WIDTH), 0.02),
        "w_s": nrm(ks[12], (DEPTH, GM_GROUPS, GM_CHUNK, GM_CHUNK), GM_CHUNK ** -0.5),
        "b_s": 1.0 + nrm(ks[13], (DEPTH, GM_GROUPS, GM_CHUNK), 0.02),
        "conv_w": nrm(ks[14], (DEPTH, CONV_W, 2 * ML_WIDTH), CONV_W ** -0.5),
        "conv_b": nrm(ks[15], (DEPTH, 2 * ML_WIDTH), 0.02),
        "hn_g": 1.0 + nrm(ks[16], (DEPTH, ML_WIDTH), 0.02),
        "b_gate": nrm(ks[17], (DEPTH, 2, D_MODEL), 0.02),
        "w_proj_a": nrm(ks[18], (DEPTH, GM_WIDTH, D_MODEL), GM_WIDTH ** -0.5),
        "w_proj_b": nrm(ks[19], (DEPTH, ML_WIDTH, D_MODEL), ML_WIDTH ** -0.5),
        "w_out": nrm(ks[20], (DEPTH, D_MODEL, D_MODEL), D_MODEL ** -0.5),
        "g_norm2": 1.0 + nrm(ks[21], (DEPTH, D_MODEL), 0.02),
        "w_ffn_in": nrm(ks[22], (DEPTH, D_MODEL, 2 * D_FF), D_MODEL ** -0.5),
        "w_ffn_out": nrm(ks[23], (DEPTH, D_FF, D_MODEL), D_FF ** -0.5),
        "g_final": 1.0 + nrm(ks[24], (D_MODEL,), 0.02),
    }
    return inputs


def reference(x_prompt, x_sample, state_conv, state_C, state_n, state_m,
              g_norm1, w_in, b_i, b_f, ln_g, ln_b, w_s, b_s, conv_w, conv_b, hn_g,
              b_gate, w_proj_a, w_proj_b, w_out, g_norm2, w_ffn_in, w_ffn_out, g_final):
    Bp, Tp, _ = x_prompt.shape
    Ts = x_sample.shape[1]
    xp, xs = x_prompt, x_sample
    conv_p, C_p, n_p, m_p = [], [], [], []
    conv_s, C_s, n_s, m_s, v_s = [], [], [], [], []
    for l in range(DEPTH):
        w = (g_norm1[l], w_in[l], b_i[l], b_f[l], ln_g[l], ln_b[l], w_s[l], b_s[l], conv_w[l], conv_b[l],
             hn_g[l], b_gate[l], w_proj_a[l], w_proj_b[l], w_out[l], g_norm2[l], w_ffn_in[l], w_ffn_out[l])
        zc = jnp.zeros((Bp, CONV_W - 1, 2 * ML_WIDTH), xp.dtype)
        zC = jnp.zeros((Bp, ML_HEADS, ML_HEAD_DIM, ML_HEAD_DIM), jnp.float32)
        zn = jnp.zeros((Bp, ML_HEADS, ML_HEAD_DIM), jnp.float32)
        zm = jnp.zeros((Bp, ML_HEADS), jnp.float32)
        xp, _, cp, Cp, np_, mp = hybrid_layer(xp, zc, zC, zn, zm, GM_CHUNK, min(ML_CHUNK, Tp), *w)
        xs, vgs, cs, Cs, ns, ms = hybrid_layer(xs, state_conv[l], state_C[l], state_n[l], state_m[l],
                                               Ts, Ts, *w)
        conv_p.append(cp); C_p.append(Cp.astype(state_C.dtype)); n_p.append(np_.astype(state_n.dtype))
        m_p.append(mp.astype(state_m.dtype))
        conv_s.append(cs); C_s.append(Cs.astype(state_C.dtype)); n_s.append(ns.astype(state_n.dtype))
        m_s.append(ms.astype(state_m.dtype)); v_s.append(vgs)
    y_prompt = rmsnorm(xp, g_final)
    y_sample = rmsnorm(xs, g_final)
    return (y_prompt, y_sample,
            jnp.stack(conv_p), jnp.stack(C_p), jnp.stack(n_p), jnp.stack(m_p),
            jnp.stack(conv_s), jnp.stack(C_s), jnp.stack(n_s), jnp.stack(m_s), jnp.stack(v_s))
```

```python
import functools

import jax
import jax.numpy as jnp
from jax import lax
from jax.experimental import pallas as pl
from jax.experimental.pallas import tpu as pltpu

D_MODEL = 1024
GM_WIDTH = D_MODEL
GM_GROUPS = 4
GM_GROUP_W = GM_WIDTH // GM_GROUPS
GM_CHUNK = 128
ML_HEADS = 4
ML_HEAD_DIM = D_MODEL // ML_HEADS
ML_WIDTH = ML_HEADS * ML_HEAD_DIM
CONV_W = 4
D_FF = 2816
EPS = 1e-6

LANES = 128
SUBLANES = 8
GATE_COLS = LANES
ML_COLS = 4 * ML_WIDTH + GATE_COLS
VMEM_LIMIT = 56 * 1024 * 1024

F32 = jnp.float32
BF16 = jnp.bfloat16
NEG_BIG = -1e30


def _rms(x, g):
    return x * lax.rsqrt(jnp.mean(x * x, axis=-1, keepdims=True) + EPS) * g


def _gelu(x):
    return 0.5 * x * (1.0 + lax.erf(x * (2.0 ** -0.5)))


def _bdot(a, b):
    return jnp.dot(a, b, preferred_element_type=F32)


def _const_spec(shape):
    nd = len(shape)
    return pl.BlockSpec(shape, lambda *_: (0,) * nd, pipeline_mode=pl.Buffered(1))


def _gmlp_kernel(x_ref, g1_ref, wuv_ref, lng_ref, lnb_ref, ws_ref, bs_ref, wpa_ref, *rest,
                 chunk, emit_v):
    if emit_v:
        pa_ref, vn_ref, a_sc = rest
    else:
        pa_ref, a_sc = rest
    tm = x_ref.shape[0]
    blk = ws_ref.shape[1]
    h = _rms(x_ref[...], g1_ref[...]).astype(BF16)
    u = _gelu(_bdot(h, wuv_ref[:, :GM_WIDTH]))
    v = _gelu(_bdot(h, wuv_ref[:, GM_WIDTH:]))
    mu = jnp.mean(v, axis=-1, keepdims=True)
    vc = v - mu
    var = jnp.mean(vc * vc, axis=-1, keepdims=True)
    vn = vc * lax.rsqrt(var + EPS) * lng_ref[...] + lnb_ref[...]
    if emit_v:
        vn_ref[...] = vn
    vb = vn.astype(BF16)
    r = lax.broadcasted_iota(jnp.int32, (blk, blk), 0)
    c = lax.broadcasted_iota(jnp.int32, (blk, blk), 1)
    keep = c <= r
    if chunk < blk:
        sh = chunk.bit_length() - 1
        keep = jnp.logical_and(keep, (r >> sh) == (c >> sh))
    for g in range(GM_GROUPS):
        wsm = jnp.where(keep, ws_ref[g], 0.0).astype(BF16)
        cs = slice(g * GM_GROUP_W, (g + 1) * GM_GROUP_W)
        for i in range(tm // blk):
            rs = slice(i * blk, (i + 1) * blk)
            s = _bdot(wsm, vb[rs, cs]) + bs_ref[:, cs]
            a_sc[rs, cs] = (u[rs, cs] * s).astype(BF16)
    pa_ref[...] = _bdot(a_sc[...], wpa_ref[...])


def _gmlp_call(x, g1, wuv, lng, lnb, ws_t, bs_t, wpa, *, chunk, emit_v, tm):
    m = x.shape[0]
    blk = ws_t.shape[1]
    row = pl.BlockSpec((tm, D_MODEL), lambda i: (i, 0))
    out_shape = [jax.ShapeDtypeStruct((m, D_MODEL), F32)]
    out_specs = [row]
    if emit_v:
        out_shape.append(jax.ShapeDtypeStruct((m, GM_WIDTH), F32))
        out_specs.append(row)
    return pl.pallas_call(
        functools.partial(_gmlp_kernel, chunk=chunk, emit_v=emit_v),
        grid=(m // tm,),
        in_specs=[row, _const_spec((1, D_MODEL)), _const_spec((D_MODEL, 2 * GM_WIDTH)),
                  _const_spec((1, GM_WIDTH)), _const_spec((1, GM_WIDTH)),
                  _const_spec((GM_GROUPS, blk, blk)), _const_spec((blk, GM_WIDTH)),
                  _const_spec((GM_WIDTH, D_MODEL))],
        out_specs=out_specs,
        out_shape=out_shape,
        scratch_shapes=[pltpu.VMEM((tm, GM_WIDTH), BF16)],
        compiler_params=pltpu.CompilerParams(dimension_semantics=("arbitrary",),
                                             vmem_limit_bytes=VMEM_LIMIT),
        name="gmlp",
    )(x, g1, wuv, lng, lnb, ws_t, bs_t, wpa)


def _inproj_kernel(x_ref, g1_ref, w_ref, z_ref):
    h = _rms(x_ref[...], g1_ref[...]).astype(BF16)
    z_ref[...] = _bdot(h, w_ref[...])


def _inproj_call(x, g1, w, *, tm):
    m = x.shape[0]
    n = w.shape[1]
    return pl.pallas_call(
        _inproj_kernel,
        grid=(m // tm,),
        in_specs=[pl.BlockSpec((tm, D_MODEL), lambda i: (i, 0)), _const_spec((1, D_MODEL)),
                  _const_spec((D_MODEL, n))],
        out_specs=pl.BlockSpec((tm, n), lambda i: (i, 0)),
        out_shape=jax.ShapeDtypeStruct((m, n), F32),
        compiler_params=pltpu.CompilerParams(dimension_semantics=("arbitrary",),
                                             vmem_limit_bytes=VMEM_LIMIT),
        name="inproj",
    )(x, g1, w)


def _log_sigmoid(x):
    return jnp.minimum(x, 0.0) - jnp.log1p(jnp.exp(-jnp.abs(x)))


def _mlstm_kernel(*refs, L, t_valid, zero_init):
    if zero_init:
        (zq_ref, cst_ref, bif_ref, cw_ref, cb_ref, hng_ref,
         bo_ref, conv_ref, C_ref, n_ref, m_ref, xp) = refs
    else:
        (zq_ref, cst_ref, C0_ref, n0_ref, m0_ref, bif_ref, cw_ref, cb_ref, hng_ref,
         bo_ref, conv_ref, C_ref, n_ref, m_ref, xp) = refs

    @pl.when(pl.program_id(1) == 0)
    def _():
        xp[0:SUBLANES, :] = cst_ref[...]
        if zero_init:
            C_ref[...] = jnp.zeros(C_ref.shape, F32)
            n_ref[...] = jnp.zeros(n_ref.shape, F32)
            m_ref[...] = jnp.zeros(m_ref.shape, F32)
        else:
            C_ref[...] = C0_ref[...]
            n_ref[...] = n0_ref[...]
            m_ref[...] = m0_ref[...]

    xp[SUBLANES:SUBLANES + L, :] = zq_ref[:, 0:2 * ML_WIDTH]
    qk = cb_ref[...]
    for j in range(CONV_W):
        off = SUBLANES - (CONV_W - 1) + j
        qk = qk + cw_ref[j:j + 1, :] * xp[off:off + L, :]
    qk = qk * jax.nn.sigmoid(qk)
    conv_ref[...] = xp[SUBLANES + t_valid - (CONV_W - 1):SUBLANES + t_valid, :]
    xp[0:SUBLANES, :] = xp[L:L + SUBLANES, :]

    zif = zq_ref[:, 4 * ML_WIDTH:ML_COLS] + bif_ref[...]
    ipre = zif
    logf = pltpu.roll(_log_sigmoid(zif), GATE_COLS - ML_HEADS, axis=1)
    row = lax.broadcasted_iota(jnp.int32, (L, L), 0)
    col = lax.broadcasted_iota(jnp.int32, (L, L), 1)
    causal = col <= row
    if t_valid < L:
        live = lax.broadcasted_iota(jnp.int32, (L, GATE_COLS), 0) < t_valid
        ipre = jnp.where(live, ipre, NEG_BIG)
        logf = jnp.where(live, logf, 0.0)
    bcum = jnp.dot(causal.astype(F32), logf, preferred_element_type=F32,
                   precision=lax.Precision.HIGHEST)
    a = ipre - bcum
    a_t = a.T
    lane = lax.broadcasted_iota(jnp.int32, (1, GATE_COLS), 1)
    m_all = m_ref[...]
    m_new_row = m_all
    nt_dims = (((1,), (1,)), ((), ()))
    tn_dims = (((0,), (0,)), ((), ()))
    for hh in range(ML_HEADS):
        hs = slice(hh * ML_HEAD_DIM, (hh + 1) * ML_HEAD_DIM)
        a_r = a_t[hh:hh + 1, :]
        a_c = a[:, hh:hh + 1]
        b_c = bcum[:, hh:hh + 1]
        m_prev = m_all[:, hh:hh + 1]
        a2 = jnp.where(causal, a_r, -jnp.inf)
        mc = jnp.maximum(jnp.max(a2, axis=1, keepdims=True), m_prev)
        dm = jnp.exp(a2 - mc)
        w_inter = jnp.exp(m_prev - mc)
        m_last = mc[L - 1:L, :]
        w_col = jnp.exp(a_c - m_last)
        decay = jnp.exp(m_prev - m_last)

        q = qk[:, hs]
        k = qk[:, ML_WIDTH + hh * ML_HEAD_DIM:ML_WIDTH + (hh + 1) * ML_HEAD_DIM] * (ML_HEAD_DIM ** -0.5)
        v = zq_ref[:, 2 * ML_WIDTH + hh * ML_HEAD_DIM:2 * ML_WIDTH + (hh + 1) * ML_HEAD_DIM]
        o = zq_ref[:, 3 * ML_WIDTH + hh * ML_HEAD_DIM:3 * ML_WIDTH + (hh + 1) * ML_HEAD_DIM]
        if t_valid < L:
            live_w = lax.broadcasted_iota(jnp.int32, (L, ML_HEAD_DIM), 0) < t_valid
            k = jnp.where(live_w, k, 0.0)
            v = jnp.where(live_w, v, 0.0)
        qb = q.astype(BF16)
        kb = k.astype(BF16)
        c_old = C_ref[hh]
        n_old = n_ref[hh]
        s = dm * lax.dot_general(qb, kb, nt_dims, preferred_element_type=F32)
        num = (w_inter * lax.dot_general(qb, c_old.astype(BF16), nt_dims, preferred_element_type=F32)
               + _bdot(s.astype(BF16), v.astype(BF16)))
        den = (w_inter * jnp.sum(q * n_old, axis=1, keepdims=True)
               + jnp.sum(s, axis=1, keepdims=True))
        hcur = num / jnp.maximum(jnp.abs(den), jnp.exp(-(b_c + mc)))
        mu = jnp.mean(hcur, axis=1, keepdims=True)
        hc = hcur - mu
        var = jnp.mean(hc * hc, axis=1, keepdims=True)
        hn = hc * lax.rsqrt(var + EPS) * hng_ref[:, hs]
        bo_ref[:, hs] = (jax.nn.sigmoid(o) * hn).astype(BF16)

        vw = (v * w_col).astype(BF16)
        C_ref[hh] = decay * c_old + lax.dot_general(vw, kb, tn_dims, preferred_element_type=F32)
        n_ref[hh] = decay * n_old + jnp.sum(k * w_col, axis=0, keepdims=True)
        m_new_row = jnp.where(lane == hh, b_c[L - 1:L, :] + m_last, m_new_row)
    m_ref[...] = m_new_row


def _mlstm_call(zq, cst, state, bif, cw, cb, hng, *, L, t_valid):
    B, T, _ = zq.shape
    zero_init = state is None
    kern = functools.partial(_mlstm_kernel, L=L, t_valid=t_valid, zero_init=zero_init)
    per_b = lambda shape: pl.BlockSpec((None,) + shape, lambda b, t: (b,) + (0,) * len(shape))
    st_specs = [per_b((ML_HEADS, ML_HEAD_DIM, ML_HEAD_DIM)), per_b((ML_HEADS, 1, ML_HEAD_DIM)),
                per_b((1, GATE_COLS))]
    in_specs = [pl.BlockSpec((None, L, ML_COLS), lambda b, t: (b, t, 0)),
                per_b((SUBLANES, 2 * ML_WIDTH))]
    args = [zq, cst]
    if not zero_init:
        in_specs += st_specs
        args += list(state)
    in_specs += [_const_spec((1, GATE_COLS)), _const_spec((CONV_W, 2 * ML_WIDTH)),
                 _const_spec((1, 2 * ML_WIDTH)), _const_spec((1, ML_WIDTH))]
    args += [bif, cw, cb, hng]
    out_shape = [jax.ShapeDtypeStruct((B, T, ML_WIDTH), BF16),
                 jax.ShapeDtypeStruct((B, CONV_W - 1, 2 * ML_WIDTH), F32),
                 jax.ShapeDtypeStruct((B, ML_HEADS, ML_HEAD_DIM, ML_HEAD_DIM), F32),
                 jax.ShapeDtypeStruct((B, ML_HEADS, 1, ML_HEAD_DIM), F32),
                 jax.ShapeDtypeStruct((B, 1, GATE_COLS), F32)]
    out_specs = [pl.BlockSpec((None, L, ML_WIDTH), lambda b, t: (b, t, 0)),
                 per_b((CONV_W - 1, 2 * ML_WIDTH))] + st_specs
    return pl.pallas_call(
        kern,
        grid=(B, T // L),
        in_specs=in_specs,
        out_specs=out_specs,
        out_shape=out_shape,
        scratch_shapes=[pltpu.VMEM((SUBLANES + L, 2 * ML_WIDTH), F32)],
        compiler_params=pltpu.CompilerParams(dimension_semantics=("arbitrary", "arbitrary"),
                                             vmem_limit_bytes=VMEM_LIMIT),
        name="mlstm",
    )(*args)


def _merge_ffn_kernel(x_ref, pa_ref, bo_ref, g1_ref, wg_ref, bg_ref, wpb_ref, wout_ref,
                      g2_ref, wfi_ref, wfo_ref, gf_ref, y_ref):
    x = x_ref[...]
    h = _rms(x, g1_ref[...]).astype(BF16)
    gab = _bdot(h, wg_ref[...]) + bg_ref[...]
    pb = _bdot(bo_ref[...], wpb_ref[...])
    merged = (jax.nn.sigmoid(gab[:, :D_MODEL]) * pa_ref[...]
              + jax.nn.sigmoid(gab[:, D_MODEL:]) * pb)
    x1 = x + _bdot(merged.astype(BF16), wout_ref[...])
    h2 = _rms(x1, g2_ref[...]).astype(BF16)
    gu = _bdot(h2, wfi_ref[...])
    gt = gu[:, :D_FF]
    hid = (gt * jax.nn.sigmoid(gt) * gu[:, D_FF:]).astype(BF16)
    x2 = x1 + _bdot(hid, wfo_ref[...])
    y_ref[...] = _rms(x2, gf_ref[...])


def _merge_ffn_call(x, pa, bo, g1, wg, bg, wpb, wout, g2, wfi, wfo, gf, *, tm):
    m = x.shape[0]
    row = pl.BlockSpec((tm, D_MODEL), lambda i: (i, 0))
    return pl.pallas_call(
        _merge_ffn_kernel,
        grid=(m // tm,),
        in_specs=[row, row, row, _const_spec((1, D_MODEL)), _const_spec((D_MODEL, 2 * D_MODEL)),
                  _const_spec((1, 2 * D_MODEL)), _const_spec((ML_WIDTH, D_MODEL)),
                  _const_spec((D_MODEL, D_MODEL)), _const_spec((1, D_MODEL)),
                  _const_spec((D_MODEL, 2 * D_FF)), _const_spec((D_FF, D_MODEL)),
                  _const_spec((1, D_MODEL))],
        out_specs=row,
        out_shape=jax.ShapeDtypeStruct((m, D_MODEL), F32),
        compiler_params=pltpu.CompilerParams(dimension_semantics=("arbitrary",),
                                             vmem_limit_bytes=VMEM_LIMIT),
        name="merge_ffn",
    )(x, pa, bo, g1, wg, bg, wpb, wout, g2, wfi, wfo, gf)


SAMPLE_PAD_T = 16


def _spatial_tiles(w_s, b_s, chunk):
    reps = GM_CHUNK // chunk
    ws_t = jnp.tile(w_s[:, :chunk, :chunk], (1, reps, reps))
    bs_t = jnp.tile(b_s[:, :chunk].T, (reps, 1))
    bs_t = jnp.repeat(bs_t, GM_GROUP_W, axis=1)
    return ws_t, bs_t


def _layer(x, conv_state, state, seq_chunk, ml_chunk, t_valid, w, *, emit_v, tm, tm_ffn):
    B, T, _ = x.shape
    xf = x.reshape(B * T, D_MODEL)
    ws_t, bs_t = _spatial_tiles(w["w_s"], w["b_s"], seq_chunk)
    outs = _gmlp_call(xf, w["g1"], w["wuv"], w["lng"], w["lnb"], ws_t, bs_t, w["wpa"],
                      chunk=seq_chunk, emit_v=emit_v, tm=tm)
    pa = outs[0]
    vn = outs[1] if emit_v else None
    zq = _inproj_call(xf, w["g1"], w["wml"], tm=tm).reshape(B, T, ML_COLS)
    if T < ml_chunk:
        zq = jnp.pad(zq, ((0, 0), (0, ml_chunk - T), (0, 0)))
    cst = jnp.pad(conv_state, ((0, 0), (SUBLANES - (CONV_W - 1), 0), (0, 0)))
    bo, conv_new, C, n, m = _mlstm_call(zq, cst, state, w["bif"], w["cw"], w["cb"], w["hng"],
                                        L=ml_chunk, t_valid=t_valid)
    bo = bo[:, :T].reshape(B * T, ML_WIDTH)
    y = _merge_ffn_call(xf, pa, bo, w["g1"], w["wg"], w["bg"], w["wpb"], w["wout"], w["g2"],
                        w["wfi"], w["wfo"], w["gf"], tm=tm_ffn)
    return (y.reshape(B, T, D_MODEL), vn, conv_new, C,
            n.reshape(B, ML_HEADS, ML_HEAD_DIM), m[:, 0, :ML_HEADS])


def kernel(x_prompt, x_sample, state_conv, state_C, state_n, state_m, g_norm1, w_in, b_i, b_f, ln_g, ln_b, w_s, b_s, conv_w, conv_b, hn_g, b_gate, w_proj_a, w_proj_b, w_out, g_norm2, w_ffn_in, w_ffn_out, g_final):
    Bp, Tp, _ = x_prompt.shape
    Bs, Ts, _ = x_sample.shape
    win = w_in[0]
    c_uv = 2 * GM_WIDTH
    c_ml = c_uv + 4 * ML_WIDTH
    c_if = c_ml + 2 * ML_HEADS
    wif = jnp.pad(win[:, c_ml:c_if], ((0, 0), (0, GATE_COLS - 2 * ML_HEADS)))
    w = dict(
        g1=g_norm1[0][None], g2=g_norm2[0][None], gf=g_final[None],
        wuv=win[:, :c_uv].astype(BF16),
        wml=jnp.concatenate([win[:, c_uv:c_ml], wif], axis=1).astype(BF16),
        wg=win[:, c_if:].astype(BF16),
        bg=b_gate[0].reshape(1, 2 * D_MODEL),
        lng=ln_g[0][None], lnb=ln_b[0][None], w_s=w_s[0], b_s=b_s[0],
        bif=jnp.pad(jnp.concatenate([b_i[0], b_f[0]]), (0, GATE_COLS - 2 * ML_HEADS))[None],
        cw=conv_w[0], cb=conv_b[0][None], hng=hn_g[0][None],
        wpa=w_proj_a[0].astype(BF16), wpb=w_proj_b[0].astype(BF16), wout=w_out[0].astype(BF16),
        wfi=w_ffn_in[0].astype(BF16), wfo=w_ffn_out[0].astype(BF16),
    )
    zc = jnp.zeros((Bp, CONV_W - 1, 2 * ML_WIDTH), F32)
    yp, _, cp, Cp, np_, mp = _layer(x_prompt, zc, None, GM_CHUNK, 256, 256, w,
                                    emit_v=False, tm=512, tm_ffn=256)
    st = (state_C[0], state_n[0].reshape(Bs, ML_HEADS, 1, ML_HEAD_DIM),
          jnp.pad(state_m[0], ((0, 0), (0, GATE_COLS - ML_HEADS)))[:, None, :])
    ys, vs, cs, Cs, ns, ms = _layer(x_sample, state_conv[0], st, Ts, SAMPLE_PAD_T, Ts, w,
                                    emit_v=True, tm=Bs * Ts, tm_ffn=256)
    return (yp, ys, cp[None], Cp[None], np_[None], mp[None],
            cs[None], Cs[None], ns[None], ms[None], vs.reshape(Bs, Ts, GM_WIDTH)[None])
```

```python
import functools
import math

import jax
import jax.numpy as jnp
from jax import lax
from jax.experimental import pallas as pl
from jax.experimental.pallas import tpu as pltpu

D_MODEL = 1024
GM_WIDTH = D_MODEL
GM_GROUPS = 4
GM_GROUP_W = GM_WIDTH // GM_GROUPS
GM_CHUNK = 128
ML_HEADS = 4
ML_HEAD_DIM = D_MODEL // ML_HEADS
ML_WIDTH = ML_HEADS * ML_HEAD_DIM
CONV_W = 4
D_FF = 2816
EPS = 1e-6

LANES = 128
SUBLANES = 8
GATE_COLS = LANES
ML_COLS = 4 * ML_WIDTH + GATE_COLS
VMEM_LIMIT = 56 * 1024 * 1024

F32 = jnp.float32
BF16 = jnp.bfloat16
NEG_BIG = -1e30


def _rms(x, g):
    return x * lax.rsqrt(jnp.mean(x * x, axis=-1, keepdims=True) + EPS) * g


def _gelu(x):
    return 0.5 * x * (1.0 + lax.erf(x * (2.0 ** -0.5)))


def _bdot(a, b):
    return jnp.dot(a, b, preferred_element_type=F32)


def _const_spec(shape):
    nd = len(shape)
    return pl.BlockSpec(shape, lambda *_: (0,) * nd, pipeline_mode=pl.Buffered(1))


def _gmlp_kernel(x_ref, g1_ref, wuv_ref, lng_ref, lnb_ref, ws_ref, bs_ref, wpa_ref, *rest,
                 chunk, emit_v):
    if emit_v:
        pa_ref, vn_ref, a_sc = rest
    else:
        pa_ref, a_sc = rest
    tm = x_ref.shape[0]
    blk = ws_ref.shape[1]
    h = _rms(x_ref[...], g1_ref[...]).astype(BF16)
    u = _gelu(_bdot(h, wuv_ref[:, :GM_WIDTH]))
    v = _gelu(_bdot(h, wuv_ref[:, GM_WIDTH:]))
    mu = jnp.mean(v, axis=-1, keepdims=True)
    vc = v - mu
    var = jnp.mean(vc * vc, axis=-1, keepdims=True)
    vn = vc * lax.rsqrt(var + EPS) * lng_ref[...] + lnb_ref[...]
    if emit_v:
        vn_ref[...] = vn
    vb = vn.astype(BF16)
    r = lax.broadcasted_iota(jnp.int32, (blk, blk), 0)
    c = lax.broadcasted_iota(jnp.int32, (blk, blk), 1)
    keep = c <= r
    if chunk < blk:
        sh = chunk.bit_length() - 1
        keep = jnp.logical_and(keep, (r >> sh) == (c >> sh))
    for g in range(GM_GROUPS):
        wsm = jnp.where(keep, ws_ref[g], 0.0).astype(BF16)
        cs = slice(g * GM_GROUP_W, (g + 1) * GM_GROUP_W)
        for i in range(tm // blk):
            rs = slice(i * blk, (i + 1) * blk)
            s = _bdot(wsm, vb[rs, cs]) + bs_ref[:, cs]
            a_sc[rs, cs] = (u[rs, cs] * s).astype(BF16)
    pa_ref[...] = _bdot(a_sc[...], wpa_ref[...])


def _gmlp_call(x, g1, wuv, lng, lnb, ws_t, bs_t, wpa, *, chunk, emit_v, tm):
    m = x.shape[0]
    blk = ws_t.shape[1]
    row = pl.BlockSpec((tm, D_MODEL), lambda i: (i, 0))
    out_shape = [jax.ShapeDtypeStruct((m, D_MODEL), F32)]
    out_specs = [row]
    if emit_v:
        out_shape.append(jax.ShapeDtypeStruct((m, GM_WIDTH), F32))
        out_specs.append(row)
    return pl.pallas_call(
        functools.partial(_gmlp_kernel, chunk=chunk, emit_v=emit_v),
        grid=(m // tm,),
        in_specs=[row, _const_spec((1, D_MODEL)), _const_spec((D_MODEL, 2 * GM_WIDTH)),
                  _const_spec((1, GM_WIDTH)), _const_spec((1, GM_WIDTH)),
                  _const_spec((GM_GROUPS, blk, blk)), _const_spec((blk, GM_WIDTH)),
                  _const_spec((GM_WIDTH, D_MODEL))],
        out_specs=out_specs,
        out_shape=out_shape,
        scratch_shapes=[pltpu.VMEM((tm, GM_WIDTH), BF16)],
        compiler_params=pltpu.CompilerParams(dimension_semantics=("arbitrary",),
                                             vmem_limit_bytes=VMEM_LIMIT),
        name="gmlp",
    )(x, g1, wuv, lng, lnb, ws_t, bs_t, wpa)


def _inproj_kernel(x_ref, g1_ref, w_ref, z_ref):
    h = _rms(x_ref[...], g1_ref[...]).astype(BF16)
    z_ref[...] = _bdot(h, w_ref[...])


def _inproj_call(x, g1, w, *, tm):
    m = x.shape[0]
    n = w.shape[1]
    return pl.pallas_call(
        _inproj_kernel,
        grid=(m // tm,),
        in_specs=[pl.BlockSpec((tm, D_MODEL), lambda i: (i, 0)), _const_spec((1, D_MODEL)),
                  _const_spec((D_MODEL, n))],
        out_specs=pl.BlockSpec((tm, n), lambda i: (i, 0)),
        out_shape=jax.ShapeDtypeStruct((m, n), F32),
        compiler_params=pltpu.CompilerParams(dimension_semantics=("arbitrary",),
                                             vmem_limit_bytes=VMEM_LIMIT),
        name="inproj",
    )(x, g1, w)


def _log_sigmoid(x):
    return jnp.minimum(x, 0.0) - jnp.log1p(jnp.exp(-jnp.abs(x)))


def _mlstm_kernel(*refs, L, t_valid, zero_init):
    if zero_init:
        (zq_ref, cst_ref, bif_ref, cw_ref, cb_ref, hng_ref,
         bo_ref, conv_ref, C_ref, n_ref, m_ref, xp) = refs
    else:
        (zq_ref, cst_ref, C0_ref, n0_ref, m0_ref, bif_ref, cw_ref, cb_ref, hng_ref,
         bo_ref, conv_ref, C_ref, n_ref, m_ref, xp) = refs

    @pl.when(pl.program_id(1) == 0)
    def _():
        xp[0:SUBLANES, :] = cst_ref[...]
        if zero_init:
            C_ref[...] = jnp.zeros(C_ref.shape, F32)
            n_ref[...] = jnp.zeros(n_ref.shape, F32)
            m_ref[...] = jnp.zeros(m_ref.shape, F32)
        else:
            C_ref[...] = C0_ref[...]
            n_ref[...] = n0_ref[...]
            m_ref[...] = m0_ref[...]

    xp[SUBLANES:SUBLANES + L, :] = zq_ref[:, 0:2 * ML_WIDTH]
    qk = cb_ref[...]
    for j in range(CONV_W):
        off = SUBLANES - (CONV_W - 1) + j
        qk = qk + cw_ref[j:j + 1, :] * xp[off:off + L, :]
    qk = qk * jax.nn.sigmoid(qk)
    conv_ref[...] = xp[SUBLANES + t_valid - (CONV_W - 1):SUBLANES + t_valid, :]
    xp[0:SUBLANES, :] = xp[L:L + SUBLANES, :]

    zif = zq_ref[:, 4 * ML_WIDTH:ML_COLS] + bif_ref[...]
    ipre = zif
    logf = pltpu.roll(_log_sigmoid(zif), GATE_COLS - ML_HEADS, axis=1)
    row = lax.broadcasted_iota(jnp.int32, (L, L), 0)
    col = lax.broadcasted_iota(jnp.int32, (L, L), 1)
    causal = col <= row
    if t_valid < L:
        live = lax.broadcasted_iota(jnp.int32, (L, GATE_COLS), 0) < t_valid
        ipre = jnp.where(live, ipre, NEG_BIG)
        logf = jnp.where(live, logf, 0.0)
    bcum = jnp.dot(causal.astype(F32), logf, preferred_element_type=F32,
                   precision=lax.Precision.HIGHEST)
    a = ipre - bcum
    a_t = a.T
    lane = lax.broadcasted_iota(jnp.int32, (1, GATE_COLS), 1)
    m_all = m_ref[...]
    m_new_row = m_all
    nt_dims = (((1,), (1,)), ((), ()))
    tn_dims = (((0,), (0,)), ((), ()))
    for hh in range(ML_HEADS):
        hs = slice(hh * ML_HEAD_DIM, (hh + 1) * ML_HEAD_DIM)
        a_r = a_t[hh:hh + 1, :]
        a_c = a[:, hh:hh + 1]
        b_c = bcum[:, hh:hh + 1]
        m_prev = m_all[:, hh:hh + 1]
        a2 = jnp.where(causal, a_r, -jnp.inf)
        mc = jnp.maximum(jnp.max(a2, axis=1, keepdims=True), m_prev)
        dm = jnp.exp(a2 - mc)
        w_inter = jnp.exp(m_prev - mc)
        m_last = mc[L - 1:L, :]
        w_col = jnp.exp(a_c - m_last)
        decay = jnp.exp(m_prev - m_last)

        q = qk[:, hs]
        k = qk[:, ML_WIDTH + hh * ML_HEAD_DIM:ML_WIDTH + (hh + 1) * ML_HEAD_DIM] * (ML_HEAD_DIM ** -0.5)
        v = zq_ref[:, 2 * ML_WIDTH + hh * ML_HEAD_DIM:2 * ML_WIDTH + (hh + 1) * ML_HEAD_DIM]
        o = zq_ref[:, 3 * ML_WIDTH + hh * ML_HEAD_DIM:3 * ML_WIDTH + (hh + 1) * ML_HEAD_DIM]
        if t_valid < L:
            live_w = lax.broadcasted_iota(jnp.int32, (L, ML_HEAD_DIM), 0) < t_valid
            k = jnp.where(live_w, k, 0.0)
            v = jnp.where(live_w, v, 0.0)
        qb = q.astype(BF16)
        kb = k.astype(BF16)
        c_old = C_ref[hh]
        n_old = n_ref[hh]
        s = dm * lax.dot_general(qb, kb, nt_dims, preferred_element_type=F32)
        num = (w_inter * lax.dot_general(qb, c_old.astype(BF16), nt_dims, preferred_element_type=F32)
               + _bdot(s.astype(BF16), v.astype(BF16)))
        den = (w_inter * jnp.sum(q * n_old, axis=1, keepdims=True)
               + jnp.sum(s, axis=1, keepdims=True))
        hcur = num / jnp.maximum(jnp.abs(den), jnp.exp(-(b_c + mc)))
        mu = jnp.mean(hcur, axis=1, keepdims=True)
        hc = hcur - mu
        var = jnp.mean(hc * hc, axis=1, keepdims=True)
        hn = hc * lax.rsqrt(var + EPS) * hng_ref[:, hs]
        bo_ref[:, hs] = (jax.nn.sigmoid(o) * hn).astype(BF16)

        vw = (v * w_col).astype(BF16)
        C_ref[hh] = decay * c_old + lax.dot_general(vw, kb, tn_dims, preferred_element_type=F32)
        n_ref[hh] = decay * n_old + jnp.sum(k * w_col, axis=0, keepdims=True)
        m_new_row = jnp.where(lane == hh, b_c[L - 1:L, :] + m_last, m_new_row)
    m_ref[...] = m_new_row


def _mlstm_call(zq, cst, state, bif, cw, cb, hng, *, L, t_valid):
    B, T, _ = zq.shape
    zero_init = state is None
    kern = functools.partial(_mlstm_kernel, L=L, t_valid=t_valid, zero_init=zero_init)
    per_b = lambda shape: pl.BlockSpec((None,) + shape, lambda b, t: (b,) + (0,) * len(shape))
    st_specs = [per_b((ML_HEADS, ML_HEAD_DIM, ML_HEAD_DIM)), per_b((ML_HEADS, 1, ML_HEAD_DIM)),
                per_b((1, GATE_COLS))]
    in_specs = [pl.BlockSpec((None, L, ML_COLS), lambda b, t: (b, t, 0)),
                per_b((SUBLANES, 2 * ML_WIDTH))]
    args = [zq, cst]
    if not zero_init:
        in_specs += st_specs
        args += list(state)
    in_specs += [_const_spec((1, GATE_COLS)), _const_spec((CONV_W, 2 * ML_WIDTH)),
                 _const_spec((1, 2 * ML_WIDTH)), _const_spec((1, ML_WIDTH))]
    args += [bif, cw, cb, hng]
    out_shape = [jax.ShapeDtypeStruct((B, T, ML_WIDTH), BF16),
                 jax.ShapeDtypeStruct((B, CONV_W - 1, 2 * ML_WIDTH), F32),
                 jax.ShapeDtypeStruct((B, ML_HEADS, ML_HEAD_DIM, ML_HEAD_DIM), F32),
                 jax.ShapeDtypeStruct((B, ML_HEADS, 1, ML_HEAD_DIM), F32),
                 jax.ShapeDtypeStruct((B, 1, GATE_COLS), F32)]
    out_specs = [pl.BlockSpec((None, L, ML_WIDTH), lambda b, t: (b, t, 0)),
                 per_b((CONV_W - 1, 2 * ML_WIDTH))] + st_specs
    return pl.pallas_call(
        kern,
        grid=(B, T // L),
        in_specs=in_specs,
        out_specs=out_specs,
        out_shape=out_shape,
        scratch_shapes=[pltpu.VMEM((SUBLANES + L, 2 * ML_WIDTH), F32)],
        compiler_params=pltpu.CompilerParams(dimension_semantics=("arbitrary", "arbitrary"),
                                             vmem_limit_bytes=VMEM_LIMIT),
        name="mlstm",
    )(*args)


PL = 256
PG = PL // SUBLANES
TAIL = (CONV_W - 1) * SUBLANES
LN_INV_K_SCALE = 0.5 * math.log(ML_HEAD_DIM)


def _perm_rows(j):
    return pl.ds((PL // 4) * (j % 4) + j // 4, SUBLANES, stride=SUBLANES)


def _mixer_ml_kernel(x_ref, g1_ref, wml_ref, bif_ref, cw_ref, cb_ref, hng_ref,
                     bo_ref, conv_ref, C_ref, n_ref, m_ref, xs, us, tail):
    t = pl.program_id(1)

    @pl.when(t == 0)
    def _():
        tail[...] = jnp.zeros(tail.shape, F32)
        C_ref[...] = jnp.zeros(C_ref.shape, F32)
        n_ref[...] = jnp.zeros(n_ref.shape, F32)
        m_ref[...] = jnp.zeros(m_ref.shape, F32)

    n_lane_chunks = D_MODEL // LANES
    for j in range(PG):
        for c in range(n_lane_chunks):
            xs[c, _perm_rows(j), :] = x_ref[j * SUBLANES:(j + 1) * SUBLANES, c * LANES:(c + 1) * LANES]
    xp = jnp.concatenate([xs[c] for c in range(n_lane_chunks)], axis=1)
    h = _rms(xp, g1_ref[...]).astype(BF16)
    z = _bdot(h, wml_ref[...])

    zqk_tail = z[PL - TAIL:, 0:2 * ML_WIDTH]
    sub = lax.broadcasted_iota(jnp.int32, (SUBLANES, 2 * ML_WIDTH), 0)
    wrapped = []
    for g in range(CONV_W - 1):
        cur = pltpu.roll(zqk_tail[g * SUBLANES:(g + 1) * SUBLANES], 1, axis=0)
        prev = pltpu.roll(tail[g * SUBLANES:(g + 1) * SUBLANES, :], 1, axis=0)
        wrapped.append(jnp.where(sub == 0, prev, cur))
    wrapped = jnp.concatenate(wrapped, axis=0)
    tail[...] = zqk_tail
    conv_ref[...] = jnp.concatenate(
        [zqk_tail[g * SUBLANES + SUBLANES - 1:(g + 1) * SUBLANES, :] for g in range(CONV_W - 1)], axis=0)

    def conv_silu(c0, width):
        cs = slice(c0, c0 + width)
        acc = cb_ref[:, cs] + cw_ref[CONV_W - 1:CONV_W, cs] * z[:, cs]
        for d in range(1, CONV_W):
            shifted = jnp.concatenate(
                [wrapped[TAIL - d * SUBLANES:, cs], z[:PL - d * SUBLANES, cs]], axis=0)
            acc = acc + cw_ref[CONV_W - 1 - d:CONV_W - d, cs] * shifted
        return acc * jax.nn.sigmoid(acc)

    pr = lax.broadcasted_iota(jnp.int32, (PL, PL), 0)
    pc = lax.broadcasted_iota(jnp.int32, (PL, PL), 1)
    causal = ((pc >> 3) + PG * (pc & 7)) <= ((pr >> 3) + PG * (pr & 7))

    zif = z[:, 4 * ML_WIDTH:ML_COLS] + bif_ref[...]
    logf = pltpu.roll(_log_sigmoid(zif), GATE_COLS - ML_HEADS, axis=1)
    bcum = jnp.dot(causal.astype(F32), logf, preferred_element_type=F32,
                   precision=lax.Precision.HIGHEST)
    a = zif - bcum
    a_t = a.T
    lane = lax.broadcasted_iota(jnp.int32, (1, GATE_COLS), 1)
    m_all = m_ref[...]
    m_new_row = m_all
    nt_dims = (((1,), (1,)), ((), ()))
    tn_dims = (((0,), (0,)), ((), ()))
    cph = ML_HEAD_DIM // LANES
    for hh in range(ML_HEADS):
        hs = slice(hh * ML_HEAD_DIM, (hh + 1) * ML_HEAD_DIM)
        a_r = a_t[hh:hh + 1, :]
        a_c = a[:, hh:hh + 1]
        b_c = bcum[:, hh:hh + 1]
        m_prev = m_all[:, hh:hh + 1]
        a2 = jnp.where(causal, a_r, -jnp.inf)
        mc = jnp.maximum(jnp.max(a2, axis=1, keepdims=True), m_prev)
        dm = jnp.exp(a2 - (mc + LN_INV_K_SCALE))
        w_inter = jnp.exp(m_prev - mc)
        m_last = mc[PL - 1:PL, :]
        w_col = jnp.exp(a_c - (m_last + LN_INV_K_SCALE))
        decay = jnp.exp(m_prev - m_last)

        q = conv_silu(hh * ML_HEAD_DIM, ML_HEAD_DIM)
        k = conv_silu(ML_WIDTH + hh * ML_HEAD_DIM, ML_HEAD_DIM)
        v = z[:, 2 * ML_WIDTH + hh * ML_HEAD_DIM:2 * ML_WIDTH + (hh + 1) * ML_HEAD_DIM]
        o = z[:, 3 * ML_WIDTH + hh * ML_HEAD_DIM:3 * ML_WIDTH + (hh + 1) * ML_HEAD_DIM]
        qb = q.astype(BF16)
        kb = k.astype(BF16)
        c_old = C_ref[hh]
        n_old = n_ref[hh]
        s = dm * lax.dot_general(qb, kb, nt_dims, preferred_element_type=F32)
        n_rows = jnp.broadcast_to(n_old, (LANES, ML_HEAD_DIM)).astype(BF16)
        qn = lax.dot_general(qb, n_rows, nt_dims, preferred_element_type=F32)[:, 0:1]
        num = (w_inter * lax.dot_general(qb, c_old.astype(BF16), nt_dims, preferred_element_type=F32)
               + _bdot(s.astype(BF16), v.astype(BF16)))
        den = w_inter * qn + jnp.sum(s, axis=1, keepdims=True)
        hcur = num / jnp.maximum(jnp.abs(den), jnp.exp(-(b_c + mc)))
        mu = jnp.mean(hcur, axis=1, keepdims=True)
        hc = hcur - mu
        var = jnp.mean(hc * hc, axis=1, keepdims=True)
        out = jax.nn.sigmoid(o) * (hc * lax.rsqrt(var + EPS) * hng_ref[:, hs])
        for cc in range(cph):
            us[hh * cph + cc] = out[:, cc * LANES:(cc + 1) * LANES]

        vw = (v * w_col).astype(BF16)
        C_ref[hh] = decay * c_old + lax.dot_general(vw, kb, tn_dims, preferred_element_type=F32)
        n_ref[hh] = decay * n_old + jnp.sum(k * w_col, axis=0, keepdims=True)
        m_new_row = jnp.where(lane == hh, b_c[PL - 1:PL, :] + m_last, m_new_row)
    m_ref[...] = m_new_row
    for c in range(ML_WIDTH // LANES):
        for j in range(0, PG, 2):
            pair = jnp.concatenate([us[c, _perm_rows(j), :], us[c, _perm_rows(j + 1), :]], axis=0)
            bo_ref[j * SUBLANES:(j + 2) * SUBLANES, c * LANES:(c + 1) * LANES] = pair.astype(BF16)


def _mixer_ml_call(x, g1, wml, bif, cw, cb, hng):
    B, T, _ = x.shape
    per_b = lambda shape: pl.BlockSpec((None,) + shape, lambda b, t: (b,) + (0,) * len(shape))
    out_shape = [jax.ShapeDtypeStruct((B, T, ML_WIDTH), BF16),
                 jax.ShapeDtypeStruct((B, CONV_W - 1, 2 * ML_WIDTH), F32),
                 jax.ShapeDtypeStruct((B, ML_HEADS, ML_HEAD_DIM, ML_HEAD_DIM), F32),
                 jax.ShapeDtypeStruct((B, ML_HEADS, 1, ML_HEAD_DIM), F32),
                 jax.ShapeDtypeStruct((B, 1, GATE_COLS), F32)]
    out_specs = [pl.BlockSpec((None, PL, ML_WIDTH), lambda b, t: (b, t, 0)),
                 per_b((CONV_W - 1, 2 * ML_WIDTH)),
                 per_b((ML_HEADS, ML_HEAD_DIM, ML_HEAD_DIM)), per_b((ML_HEADS, 1, ML_HEAD_DIM)),
                 per_b((1, GATE_COLS))]
    return pl.pallas_call(
        _mixer_ml_kernel,
        grid=(B, T // PL),
        in_specs=[pl.BlockSpec((None, PL, D_MODEL), lambda b, t: (b, t, 0)),
                  _const_spec((1, D_MODEL)), _const_spec((D_MODEL, ML_COLS)),
                  _const_spec((1, GATE_COLS)), _const_spec((CONV_W, 2 * ML_WIDTH)),
                  _const_spec((1, 2 * ML_WIDTH)), _const_spec((1, ML_WIDTH))],
        out_specs=out_specs,
        out_shape=out_shape,
        scratch_shapes=[pltpu.VMEM((D_MODEL // LANES, PL, LANES), F32),
                        pltpu.VMEM((ML_WIDTH // LANES, PL, LANES), F32),
                        pltpu.VMEM((TAIL, 2 * ML_WIDTH), F32)],
        compiler_params=pltpu.CompilerParams(dimension_semantics=("arbitrary", "arbitrary"),
                                             vmem_limit_bytes=VMEM_LIMIT),
        name="mixer_ml",
    )(x, g1, wml, bif, cw, cb, hng)


def _merge_ffn_kernel(x_ref, pa_ref, bo_ref, g1_ref, wg_ref, bg_ref, wpb_ref, wout_ref,
                      g2_ref, wfi_ref, wfo_ref, gf_ref, y_ref):
    x = x_ref[...]
    h = _rms(x, g1_ref[...]).astype(BF16)
    gab = _bdot(h, wg_ref[...]) + bg_ref[...]
    pb = _bdot(bo_ref[...], wpb_ref[...])
    merged = (jax.nn.sigmoid(gab[:, :D_MODEL]) * pa_ref[...]
              + jax.nn.sigmoid(gab[:, D_MODEL:]) * pb)
    x1 = x + _bdot(merged.astype(BF16), wout_ref[...])
    h2 = _rms(x1, g2_ref[...]).astype(BF16)
    gu = _bdot(h2, wfi_ref[...])
    gt = gu[:, :D_FF]
    hid = (gt * jax.nn.sigmoid(gt) * gu[:, D_FF:]).astype(BF16)
    x2 = x1 + _bdot(hid, wfo_ref[...])
    y_ref[...] = _rms(x2, gf_ref[...])


def _merge_ffn_call(x, pa, bo, g1, wg, bg, wpb, wout, g2, wfi, wfo, gf, *, tm):
    m = x.shape[0]
    row = pl.BlockSpec((tm, D_MODEL), lambda i: (i, 0))
    return pl.pallas_call(
        _merge_ffn_kernel,
        grid=(m // tm,),
        in_specs=[row, row, row, _const_spec((1, D_MODEL)), _const_spec((D_MODEL, 2 * D_MODEL)),
                  _const_spec((1, 2 * D_MODEL)), _const_spec((ML_WIDTH, D_MODEL)),
                  _const_spec((D_MODEL, D_MODEL)), _const_spec((1, D_MODEL)),
                  _const_spec((D_MODEL, 2 * D_FF)), _const_spec((D_FF, D_MODEL)),
                  _const_spec((1, D_MODEL))],
        out_specs=row,
        out_shape=jax.ShapeDtypeStruct((m, D_MODEL), F32),
        compiler_params=pltpu.CompilerParams(dimension_semantics=("arbitrary",),
                                             vmem_limit_bytes=VMEM_LIMIT),
        name="merge_ffn",
    )(x, pa, bo, g1, wg, bg, wpb, wout, g2, wfi, wfo, gf)


SAMPLE_PAD_T = 16


def _spatial_tiles(w_s, b_s, chunk):
    reps = GM_CHUNK // chunk
    ws_t = jnp.tile(w_s[:, :chunk, :chunk], (1, reps, reps))
    bs_t = jnp.tile(b_s[:, :chunk].T, (reps, 1))
    bs_t = jnp.repeat(bs_t, GM_GROUP_W, axis=1)
    return ws_t, bs_t


def _layer(x, conv_state, state, seq_chunk, ml_chunk, t_valid, w, *, emit_v, tm, tm_ffn):
    B, T, _ = x.shape
    xf = x.reshape(B * T, D_MODEL)
    ws_t, bs_t = _spatial_tiles(w["w_s"], w["b_s"], seq_chunk)
    outs = _gmlp_call(xf, w["g1"], w["wuv"], w["lng"], w["lnb"], ws_t, bs_t, w["wpa"],
                      chunk=seq_chunk, emit_v=emit_v, tm=tm)
    pa = outs[0]
    vn = outs[1] if emit_v else None
    if state is None:
        bo, conv_new, C, n, m = _mixer_ml_call(x, w["g1"], w["wml"], w["bif"], w["cw"], w["cb"],
                                               w["hng"])
    else:
        zq = _inproj_call(xf, w["g1"], w["wml"], tm=tm).reshape(B, T, ML_COLS)
        if T < ml_chunk:
            zq = jnp.pad(zq, ((0, 0), (0, ml_chunk - T), (0, 0)))
        cst = jnp.pad(conv_state, ((0, 0), (SUBLANES - (CONV_W - 1), 0), (0, 0)))
        bo, conv_new, C, n, m = _mlstm_call(zq, cst, state, w["bif"], w["cw"], w["cb"], w["hng"],
                                            L=ml_chunk, t_valid=t_valid)
    bo = bo[:, :T].reshape(B * T, ML_WIDTH)
    y = _merge_ffn_call(xf, pa, bo, w["g1"], w["wg"], w["bg"], w["wpb"], w["wout"], w["g2"],
                        w["wfi"], w["wfo"], w["gf"], tm=tm_ffn)
    return (y.reshape(B, T, D_MODEL), vn, conv_new, C,
            n.reshape(B, ML_HEADS, ML_HEAD_DIM), m[:, 0, :ML_HEADS])


def kernel(x_prompt, x_sample, state_conv, state_C, state_n, state_m, g_norm1, w_in, b_i, b_f, ln_g, ln_b, w_s, b_s, conv_w, conv_b, hn_g, b_gate, w_proj_a, w_proj_b, w_out, g_norm2, w_ffn_in, w_ffn_out, g_final):
    Bp, Tp, _ = x_prompt.shape
    Bs, Ts, _ = x_sample.shape
    win = w_in[0]
    c_uv = 2 * GM_WIDTH
    c_ml = c_uv + 4 * ML_WIDTH
    c_if = c_ml + 2 * ML_HEADS
    wif = jnp.pad(win[:, c_ml:c_if], ((0, 0), (0, GATE_COLS - 2 * ML_HEADS)))
    w = dict(
        g1=g_norm1[0][None], g2=g_norm2[0][None], gf=g_final[None],
        wuv=win[:, :c_uv].astype(BF16),
        wml=jnp.concatenate([win[:, c_uv:c_ml], wif], axis=1).astype(BF16),
        wg=win[:, c_if:].astype(BF16),
        bg=b_gate[0].reshape(1, 2 * D_MODEL),
        lng=ln_g[0][None], lnb=ln_b[0][None], w_s=w_s[0], b_s=b_s[0],
        bif=jnp.pad(jnp.concatenate([b_i[0], b_f[0]]), (0, GATE_COLS - 2 * ML_HEADS))[None],
        cw=conv_w[0], cb=conv_b[0][None], hng=hn_g[0][None],
        wpa=w_proj_a[0].astype(BF16), wpb=w_proj_b[0].astype(BF16), wout=w_out[0].astype(BF16),
        wfi=w_ffn_in[0].astype(BF16), wfo=w_ffn_out[0].astype(BF16),
    )
    zc = jnp.zeros((Bp, CONV_W - 1, 2 * ML_WIDTH), F32)
    yp, _, cp, Cp, np_, mp = _layer(x_prompt, zc, None, GM_CHUNK, 256, 256, w,
                                    emit_v=False, tm=512, tm_ffn=256)
    st = (state_C[0], state_n[0].reshape(Bs, ML_HEADS, 1, ML_HEAD_DIM),
          jnp.pad(state_m[0], ((0, 0), (0, GATE_COLS - ML_HEADS)))[:, None, :])
    ys, vs, cs, Cs, ns, ms = _layer(x_sample, state_conv[0], st, Ts, SAMPLE_PAD_T, Ts, w,
                                    emit_v=True, tm=Bs * Ts, tm_ffn=256)
    return (yp, ys, cp[None], Cp[None], np_[None], mp[None],
            cs[None], Cs[None], ns[None], ms[None], vs.reshape(Bs, Ts, GM_WIDTH)[None])
```

```python
import functools
import math

import jax
import jax.numpy as jnp
from jax import lax
from jax.experimental import pallas as pl
from jax.experimental.pallas import tpu as pltpu

D_MODEL = 1024
GM_WIDTH = D_MODEL
GM_GROUPS = 4
GM_GROUP_W = GM_WIDTH // GM_GROUPS
GM_CHUNK = 128
ML_HEADS = 4
ML_HEAD_DIM = D_MODEL // ML_HEADS
ML_WIDTH = ML_HEADS * ML_HEAD_DIM
CONV_W = 4
D_FF = 2816
EPS = 1e-6

LANES = 128
SUBLANES = 8
BF16_ROWS = 16
GATE_COLS = LANES
ML_COLS = 4 * ML_WIDTH + GATE_COLS
VMEM_LIMIT = 56 * 1024 * 1024

F32 = jnp.float32
BF16 = jnp.bfloat16
NEG_BIG = -1e30
LN_INV_K_SCALE = 0.5 * math.log(ML_HEAD_DIM)
NT_DIMS = (((1,), (1,)), ((), ()))
TN_DIMS = (((0,), (0,)), ((), ()))


def _rms(x, g):
    return x * lax.rsqrt(jnp.mean(x * x, axis=-1, keepdims=True) + EPS) * g


def _gelu(x):
    return 0.5 * x * (1.0 + lax.erf(x * (2.0 ** -0.5)))


def _log_sigmoid(x):
    return jnp.minimum(x, 0.0) - jnp.log1p(jnp.exp(-jnp.abs(x)))


def _bdot(a, b):
    return jnp.dot(a, b, preferred_element_type=F32)


def _const_spec(shape):
    nd = len(shape)
    return pl.BlockSpec(shape, lambda *_: (0,) * nd, pipeline_mode=pl.Buffered(1))


def _gmlp_kernel(x_ref, g1_ref, wuv_ref, lng_ref, lnb_ref, ws_ref, bs_ref, wpa_ref, *rest,
                 chunk, emit_v):
    if emit_v:
        pa_ref, vn_ref, a_sc = rest
    else:
        pa_ref, a_sc = rest
    tm = x_ref.shape[0]
    blk = ws_ref.shape[1]
    h = _rms(x_ref[...], g1_ref[...]).astype(BF16)
    u = _gelu(_bdot(h, wuv_ref[:, :GM_WIDTH]))
    v = _gelu(_bdot(h, wuv_ref[:, GM_WIDTH:]))
    mu = jnp.mean(v, axis=-1, keepdims=True)
    vc = v - mu
    var = jnp.mean(vc * vc, axis=-1, keepdims=True)
    vn = vc * lax.rsqrt(var + EPS) * lng_ref[...] + lnb_ref[...]
    if emit_v:
        vn_ref[...] = vn
    vb = vn.astype(BF16)
    r = lax.broadcasted_iota(jnp.int32, (blk, blk), 0)
    c = lax.broadcasted_iota(jnp.int32, (blk, blk), 1)
    keep = c <= r
    if chunk < blk:
        sh = chunk.bit_length() - 1
        keep = jnp.logical_and(keep, (r >> sh) == (c >> sh))
    for g in range(GM_GROUPS):
        wsm = jnp.where(keep, ws_ref[g], 0.0).astype(BF16)
        cs = slice(g * GM_GROUP_W, (g + 1) * GM_GROUP_W)
        for i in range(tm // blk):
            rs = slice(i * blk, (i + 1) * blk)
            s = _bdot(wsm, vb[rs, cs]) + bs_ref[:, cs]
            a_sc[rs, cs] = (u[rs, cs] * s).astype(BF16)
    pa_ref[...] = _bdot(a_sc[...], wpa_ref[...])


def _gmlp_call(x, g1, wuv, lng, lnb, ws_t, bs_t, wpa, *, chunk, emit_v, tm):
    m = x.shape[0]
    blk = ws_t.shape[1]
    row = pl.BlockSpec((tm, D_MODEL), lambda i: (i, 0))
    out_shape = [jax.ShapeDtypeStruct((m, D_MODEL), F32)]
    out_specs = [row]
    if emit_v:
        out_shape.append(jax.ShapeDtypeStruct((m, GM_WIDTH), F32))
        out_specs.append(row)
    return pl.pallas_call(
        functools.partial(_gmlp_kernel, chunk=chunk, emit_v=emit_v),
        grid=(m // tm,),
        in_specs=[row, _const_spec((1, D_MODEL)), _const_spec((D_MODEL, 2 * GM_WIDTH)),
                  _const_spec((1, GM_WIDTH)), _const_spec((1, GM_WIDTH)),
                  _const_spec((GM_GROUPS, blk, blk)), _const_spec((blk, GM_WIDTH)),
                  _const_spec((GM_WIDTH, D_MODEL))],
        out_specs=out_specs,
        out_shape=out_shape,
        scratch_shapes=[pltpu.VMEM((tm, GM_WIDTH), BF16)],
        compiler_params=pltpu.CompilerParams(dimension_semantics=("arbitrary",),
                                             vmem_limit_bytes=VMEM_LIMIT),
        name="gmlp",
    )(x, g1, wuv, lng, lnb, ws_t, bs_t, wpa)


def _inproj_kernel(x_ref, g1_ref, w_ref, z_ref):
    h = _rms(x_ref[...], g1_ref[...]).astype(BF16)
    z_ref[...] = _bdot(h, w_ref[...])


def _inproj_call(x, g1, w, *, tm):
    m = x.shape[0]
    n = w.shape[1]
    return pl.pallas_call(
        _inproj_kernel,
        grid=(m // tm,),
        in_specs=[pl.BlockSpec((tm, D_MODEL), lambda i: (i, 0)), _const_spec((1, D_MODEL)),
                  _const_spec((D_MODEL, n))],
        out_specs=pl.BlockSpec((tm, n), lambda i: (i, 0)),
        out_shape=jax.ShapeDtypeStruct((m, n), F32),
        compiler_params=pltpu.CompilerParams(dimension_semantics=("arbitrary",),
                                             vmem_limit_bytes=VMEM_LIMIT),
        name="inproj",
    )(x, g1, w)


def _mlstm_heads(q_of, k_of, v_of, o_of, causal, ipre, bcum, groups, m0_of, C0_of, n0_of, hng_ref):
    heads = range(ML_HEADS)
    single = len(groups) == 1

    def rows_of(x, g):
        return x if single else x[groups[g][0]:groups[g][0] + groups[g][1]]

    def per_row(vals):
        if single:
            return vals[0]
        return jnp.concatenate([jnp.broadcast_to(v, (groups[g][1], v.shape[1]))
                                for g, v in enumerate(vals)], axis=0)

    a = ipre - bcum
    a_t = a.T
    m_rows = per_row([m0_of(g) for g in range(len(groups))])

    a2 = [jnp.where(causal, a_t[h:h + 1, :], -jnp.inf) for h in heads]
    mc = [jnp.maximum(jnp.max(a2[h], axis=1, keepdims=True), m_rows[:, h:h + 1]) for h in heads]
    m_last = [[mc[h][grp[2]:grp[2] + 1, :] for grp in groups] for h in heads]
    m_last_rows = [per_row(m_last[h]) for h in heads]
    dm = [jnp.exp(a2[h] - (mc[h] + LN_INV_K_SCALE)) for h in heads]
    w_inter = [jnp.exp(m_rows[:, h:h + 1] - mc[h]) for h in heads]
    w_col = [jnp.exp(a[:, h:h + 1] - (m_last_rows[h] + LN_INV_K_SCALE)) for h in heads]
    decay = [[jnp.exp(m0_of(g)[:, h:h + 1] - m_last[h][g]) for g in range(len(groups))] for h in heads]

    q = [q_of(h) for h in heads]
    k = [k_of(h) for h in heads]
    v = [v_of(h) for h in heads]
    qb = [x.astype(BF16) for x in q]
    kb = [x.astype(BF16) for x in k]
    vb = [x.astype(BF16) for x in v]
    vw = [(v[h] * w_col[h]).astype(BF16) for h in heads]
    c_old = [[C0_of(g, h) for h in heads] for g in range(len(groups))]
    n_old = [[n0_of(g, h) for h in heads] for g in range(len(groups))]

    qk = [lax.dot_general(qb[h], kb[h], NT_DIMS, preferred_element_type=F32) for h in heads]
    qc = [[lax.dot_general(rows_of(qb[h], g), c_old[g][h].astype(BF16), NT_DIMS,
                           preferred_element_type=F32) for g in range(len(groups))] for h in heads]
    if single:
        qn = [lax.dot_general(qb[h], jnp.broadcast_to(n_old[0][h], (LANES, ML_HEAD_DIM)).astype(BF16),
                              NT_DIMS, preferred_element_type=F32)[:, 0:1] for h in heads]
    else:
        qn = [jnp.sum(q[h] * per_row([n_old[g][h] for g in range(len(groups))]), axis=1, keepdims=True)
              for h in heads]
    s = [dm[h] * qk[h] for h in heads]
    sv = [_bdot(s[h].astype(BF16), vb[h]) for h in heads]
    cupd = [[lax.dot_general(rows_of(vw[h], g), rows_of(kb[h], g), TN_DIMS, preferred_element_type=F32)
             for g in range(len(groups))] for h in heads]

    outs = []
    for h in heads:
        qc_rows = qc[h][0] if single else jnp.concatenate(qc[h], axis=0)
        num = w_inter[h] * qc_rows + sv[h]
        den = w_inter[h] * qn[h] + jnp.sum(s[h], axis=1, keepdims=True)
        hcur = num / jnp.maximum(jnp.abs(den), jnp.exp(-(bcum[:, h:h + 1] + mc[h])))
        mu = jnp.mean(hcur, axis=1, keepdims=True)
        hc = hcur - mu
        var = jnp.mean(hc * hc, axis=1, keepdims=True)
        hs = slice(h * ML_HEAD_DIM, (h + 1) * ML_HEAD_DIM)
        outs.append(jax.nn.sigmoid(o_of(h)) * (hc * lax.rsqrt(var + EPS) * hng_ref[:, hs]))

    lane = lax.broadcasted_iota(jnp.int32, (1, GATE_COLS), 1)
    C_new, n_new, m_new = [], [], []
    for g, grp in enumerate(groups):
        C_new.append([decay[h][g] * c_old[g][h] + cupd[h][g] for h in heads])
        n_new.append([decay[h][g] * n_old[g][h]
                      + jnp.sum(rows_of(k[h] * w_col[h], g), axis=0, keepdims=True) for h in heads])
        row = m0_of(g)
        for h in heads:
            row = jnp.where(lane == h, bcum[grp[2]:grp[2] + 1, h:h + 1] + m_last[h][g], row)
        m_new.append(row)
    return outs, C_new, n_new, m_new


def _mlstm_kernel(zq_ref, cst_ref, C0_ref, n0_ref, m0_ref, bif_ref, cw_ref, cb_ref, hng_ref,
                  bo_ref, conv_ref, C_ref, n_ref, m_ref, zp, xp, *, t_valid):
    nb = zq_ref.shape[0]
    L = BF16_ROWS
    R = nb * L

    @pl.when(pl.program_id(0) == 0)
    def _():
        zp[...] = jnp.zeros(zp.shape, F32)

    for bb in range(nb):
        zp[bb, 0:t_valid, :] = zq_ref[bb]
        xp[bb, SUBLANES - (CONV_W - 1):SUBLANES, :] = cst_ref[bb]
        xp[bb, SUBLANES:2 * SUBLANES, :] = zp[bb, 0:SUBLANES, 0:2 * ML_WIDTH]
        conv_ref[bb] = xp[bb, SUBLANES + t_valid - (CONV_W - 1):SUBLANES + t_valid, :]
    qk_rows = []
    for bb in range(nb):
        acc = cb_ref[...]
        for j in range(CONV_W):
            off = SUBLANES - (CONV_W - 1) + j
            acc = acc + cw_ref[j:j + 1, :] * xp[bb, off:off + SUBLANES, :]
        qk_rows += [acc, jnp.zeros((L - SUBLANES, 2 * ML_WIDTH), F32)]
    qk = jnp.concatenate(qk_rows, axis=0)
    qk = qk * jax.nn.sigmoid(qk)

    def cols(c0, width):
        return zp[:, :, c0:c0 + width].reshape(R, width)

    zif = cols(4 * ML_WIDTH, GATE_COLS) + bif_ref[...]
    live = (lax.broadcasted_iota(jnp.int32, (R, GATE_COLS), 0) & (L - 1)) < t_valid
    ipre = jnp.where(live, zif, NEG_BIG)
    logf = jnp.where(live, pltpu.roll(_log_sigmoid(zif), GATE_COLS - ML_HEADS, axis=1), 0.0)
    r = lax.broadcasted_iota(jnp.int32, (R, R), 0)
    c = lax.broadcasted_iota(jnp.int32, (R, R), 1)
    sh = L.bit_length() - 1
    causal = jnp.logical_and(c <= r, (r >> sh) == (c >> sh))
    bcum = jnp.dot(causal.astype(F32), logf, preferred_element_type=F32,
                   precision=lax.Precision.HIGHEST)
    live_w = (lax.broadcasted_iota(jnp.int32, (R, ML_HEAD_DIM), 0) & (L - 1)) < t_valid

    def head_cols(x, base, h):
        return x[:, base + h * ML_HEAD_DIM:base + (h + 1) * ML_HEAD_DIM]

    groups = [(bb * L, L, bb * L + L - 1) for bb in range(nb)]
    outs, C_new, n_new, m_new = _mlstm_heads(
        q_of=lambda h: head_cols(qk, 0, h),
        k_of=lambda h: jnp.where(live_w, head_cols(qk, ML_WIDTH, h), 0.0),
        v_of=lambda h: jnp.where(live_w, cols(2 * ML_WIDTH + h * ML_HEAD_DIM, ML_HEAD_DIM), 0.0),
        o_of=lambda h: cols(3 * ML_WIDTH + h * ML_HEAD_DIM, ML_HEAD_DIM),
        causal=causal, ipre=ipre, bcum=bcum, groups=groups,
        m0_of=lambda g: m0_ref[g], C0_of=lambda g, h: C0_ref[g, h], n0_of=lambda g, h: n0_ref[g, h],
        hng_ref=hng_ref)
    for bb in range(nb):
        for h in range(ML_HEADS):
            hs = slice(h * ML_HEAD_DIM, (h + 1) * ML_HEAD_DIM)
            bo_ref[bb, :, hs] = outs[h][bb * L:bb * L + t_valid, :]
            C_ref[bb, h] = C_new[bb][h]
            n_ref[bb, h] = n_new[bb][h]
        m_ref[bb] = m_new[bb]


def _mlstm_call(zq, cst, state, bif, cw, cb, hng, *, nb):
    B, T, _ = zq.shape
    kern = functools.partial(_mlstm_kernel, t_valid=T)
    per_b = lambda shape: pl.BlockSpec((nb,) + shape, lambda b: (b,) + (0,) * len(shape))
    st_specs = [per_b((ML_HEADS, ML_HEAD_DIM, ML_HEAD_DIM)), per_b((ML_HEADS, 1, ML_HEAD_DIM)),
                per_b((1, GATE_COLS))]
    in_specs = ([per_b((T, ML_COLS)), per_b((CONV_W - 1, 2 * ML_WIDTH))] + st_specs
                + [_const_spec((1, GATE_COLS)), _const_spec((CONV_W, 2 * ML_WIDTH)),
                   _const_spec((1, 2 * ML_WIDTH)), _const_spec((1, ML_WIDTH))])
    out_shape = [jax.ShapeDtypeStruct((B, T, ML_WIDTH), F32),
                 jax.ShapeDtypeStruct((B, CONV_W - 1, 2 * ML_WIDTH), F32),
                 jax.ShapeDtypeStruct((B, ML_HEADS, ML_HEAD_DIM, ML_HEAD_DIM), F32),
                 jax.ShapeDtypeStruct((B, ML_HEADS, 1, ML_HEAD_DIM), F32),
                 jax.ShapeDtypeStruct((B, 1, GATE_COLS), F32)]
    out_specs = [per_b((T, ML_WIDTH)), per_b((CONV_W - 1, 2 * ML_WIDTH))] + st_specs
    return pl.pallas_call(
        kern,
        grid=(B // nb,),
        in_specs=in_specs,
        out_specs=out_specs,
        out_shape=out_shape,
        scratch_shapes=[pltpu.VMEM((nb, BF16_ROWS, ML_COLS), F32),
                        pltpu.VMEM((nb, 2 * SUBLANES, 2 * ML_WIDTH), F32)],
        compiler_params=pltpu.CompilerParams(dimension_semantics=("arbitrary",),
                                             vmem_limit_bytes=VMEM_LIMIT),
        name="mlstm",
    )(zq, cst, *state, bif, cw, cb, hng)


PL = 256
PG = PL // SUBLANES
TAIL = (CONV_W - 1) * SUBLANES


def _perm_rows(j):
    return pl.ds((PL // 4) * (j % 4) + j // 4, SUBLANES, stride=SUBLANES)


def _mixer_ml_kernel(x_ref, g1_ref, wml_ref, bif_ref, cw_ref, cb_ref, hng_ref,
                     bo_ref, conv_ref, C_ref, n_ref, m_ref, xs, us, tail):
    t = pl.program_id(1)

    @pl.when(t == 0)
    def _():
        tail[...] = jnp.zeros(tail.shape, F32)
        C_ref[...] = jnp.zeros(C_ref.shape, F32)
        n_ref[...] = jnp.zeros(n_ref.shape, F32)
        m_ref[...] = jnp.zeros(m_ref.shape, F32)

    n_lane_chunks = D_MODEL // LANES
    for j in range(PG):
        for c in range(n_lane_chunks):
            xs[c, _perm_rows(j), :] = x_ref[j * SUBLANES:(j + 1) * SUBLANES, c * LANES:(c + 1) * LANES]
    xp = jnp.concatenate([xs[c] for c in range(n_lane_chunks)], axis=1)
    h = _rms(xp, g1_ref[...]).astype(BF16)
    z = _bdot(h, wml_ref[...])

    zqk_tail = z[PL - TAIL:, 0:2 * ML_WIDTH]
    sub = lax.broadcasted_iota(jnp.int32, (SUBLANES, 2 * ML_WIDTH), 0)
    wrapped = []
    for g in range(CONV_W - 1):
        cur = pltpu.roll(zqk_tail[g * SUBLANES:(g + 1) * SUBLANES], 1, axis=0)
        prev = pltpu.roll(tail[g * SUBLANES:(g + 1) * SUBLANES, :], 1, axis=0)
        wrapped.append(jnp.where(sub == 0, prev, cur))
    wrapped = jnp.concatenate(wrapped, axis=0)
    tail[...] = zqk_tail
    conv_ref[...] = jnp.concatenate(
        [zqk_tail[g * SUBLANES + SUBLANES - 1:(g + 1) * SUBLANES, :] for g in range(CONV_W - 1)], axis=0)

    def conv_silu(c0, width):
        cs = slice(c0, c0 + width)
        acc = cb_ref[:, cs] + cw_ref[CONV_W - 1:CONV_W, cs] * z[:, cs]
        for d in range(1, CONV_W):
            shifted = jnp.concatenate(
                [wrapped[TAIL - d * SUBLANES:, cs], z[:PL - d * SUBLANES, cs]], axis=0)
            acc = acc + cw_ref[CONV_W - 1 - d:CONV_W - d, cs] * shifted
        return acc * jax.nn.sigmoid(acc)

    pr = lax.broadcasted_iota(jnp.int32, (PL, PL), 0)
    pc = lax.broadcasted_iota(jnp.int32, (PL, PL), 1)
    causal = ((pc >> 3) + PG * (pc & 7)) <= ((pr >> 3) + PG * (pr & 7))

    zif = z[:, 4 * ML_WIDTH:ML_COLS] + bif_ref[...]
    logf = pltpu.roll(_log_sigmoid(zif), GATE_COLS - ML_HEADS, axis=1)
    bcum = jnp.dot(causal.astype(F32), logf, preferred_element_type=F32,
                   precision=lax.Precision.HIGHEST)

    def head_cols(base, h):
        return z[:, base + h * ML_HEAD_DIM:base + (h + 1) * ML_HEAD_DIM]

    outs, C_new, n_new, m_new = _mlstm_heads(
        q_of=lambda h: conv_silu(h * ML_HEAD_DIM, ML_HEAD_DIM),
        k_of=lambda h: conv_silu(ML_WIDTH + h * ML_HEAD_DIM, ML_HEAD_DIM),
        v_of=lambda h: head_cols(2 * ML_WIDTH, h),
        o_of=lambda h: head_cols(3 * ML_WIDTH, h),
        causal=causal, ipre=zif, bcum=bcum, groups=[(0, PL, PL - 1)],
        m0_of=lambda g: m_ref[...], C0_of=lambda g, h: C_ref[h], n0_of=lambda g, h: n_ref[h],
        hng_ref=hng_ref)
    cph = ML_HEAD_DIM // LANES
    for h in range(ML_HEADS):
        for cc in range(cph):
            us[h * cph + cc] = outs[h][:, cc * LANES:(cc + 1) * LANES]
        C_ref[h] = C_new[0][h]
        n_ref[h] = n_new[0][h]
    m_ref[...] = m_new[0]
    for c in range(ML_WIDTH // LANES):
        for j in range(0, PG, 2):
            pair = jnp.concatenate([us[c, _perm_rows(j), :], us[c, _perm_rows(j + 1), :]], axis=0)
            bo_ref[j * SUBLANES:(j + 2) * SUBLANES, c * LANES:(c + 1) * LANES] = pair.astype(BF16)


def _mixer_ml_call(x, g1, wml, bif, cw, cb, hng):
    B, T, _ = x.shape
    per_b = lambda shape: pl.BlockSpec((None,) + shape, lambda b, t: (b,) + (0,) * len(shape))
    out_shape = [jax.ShapeDtypeStruct((B, T, ML_WIDTH), BF16),
                 jax.ShapeDtypeStruct((B, CONV_W - 1, 2 * ML_WIDTH), F32),
                 jax.ShapeDtypeStruct((B, ML_HEADS, ML_HEAD_DIM, ML_HEAD_DIM), F32),
                 jax.ShapeDtypeStruct((B, ML_HEADS, 1, ML_HEAD_DIM), F32),
                 jax.ShapeDtypeStruct((B, 1, GATE_COLS), F32)]
    out_specs = [pl.BlockSpec((None, PL, ML_WIDTH), lambda b, t: (b, t, 0)),
                 per_b((CONV_W - 1, 2 * ML_WIDTH)),
                 per_b((ML_HEADS, ML_HEAD_DIM, ML_HEAD_DIM)), per_b((ML_HEADS, 1, ML_HEAD_DIM)),
                 per_b((1, GATE_COLS))]
    return pl.pallas_call(
        _mixer_ml_kernel,
        grid=(B, T // PL),
        in_specs=[pl.BlockSpec((None, PL, D_MODEL), lambda b, t: (b, t, 0)),
                  _const_spec((1, D_MODEL)), _const_spec((D_MODEL, ML_COLS)),
                  _const_spec((1, GATE_COLS)), _const_spec((CONV_W, 2 * ML_WIDTH)),
                  _const_spec((1, 2 * ML_WIDTH)), _const_spec((1, ML_WIDTH))],
        out_specs=out_specs,
        out_shape=out_shape,
        scratch_shapes=[pltpu.VMEM((D_MODEL // LANES, PL, LANES), F32),
                        pltpu.VMEM((ML_WIDTH // LANES, PL, LANES), F32),
                        pltpu.VMEM((TAIL, 2 * ML_WIDTH), F32)],
        compiler_params=pltpu.CompilerParams(dimension_semantics=("arbitrary", "arbitrary"),
                                             vmem_limit_bytes=VMEM_LIMIT),
        name="mixer_ml",
    )(x, g1, wml, bif, cw, cb, hng)


def _merge_ffn_kernel(x_ref, pa_ref, bo_ref, g1_ref, wg_ref, bg_ref, wpb_ref, wout_ref,
                      g2_ref, wfi_ref, wfo_ref, gf_ref, y_ref):
    x = x_ref[...]
    h = _rms(x, g1_ref[...]).astype(BF16)
    gab = _bdot(h, wg_ref[...]) + bg_ref[...]
    pb = _bdot(bo_ref[...].astype(BF16), wpb_ref[...])
    merged = (jax.nn.sigmoid(gab[:, :D_MODEL]) * pa_ref[...]
              + jax.nn.sigmoid(gab[:, D_MODEL:]) * pb)
    x1 = x + _bdot(merged.astype(BF16), wout_ref[...])
    h2 = _rms(x1, g2_ref[...]).astype(BF16)
    gu = _bdot(h2, wfi_ref[...])
    gt = gu[:, :D_FF]
    hid = (gt * jax.nn.sigmoid(gt) * gu[:, D_FF:]).astype(BF16)
    x2 = x1 + _bdot(hid, wfo_ref[...])
    y_ref[...] = _rms(x2, gf_ref[...])


def _merge_ffn_call(x, pa, bo, g1, wg, bg, wpb, wout, g2, wfi, wfo, gf, *, tm):
    m = x.shape[0]
    row = pl.BlockSpec((tm, D_MODEL), lambda i: (i, 0))
    return pl.pallas_call(
        _merge_ffn_kernel,
        grid=(m // tm,),
        in_specs=[row, row, row, _const_spec((1, D_MODEL)), _const_spec((D_MODEL, 2 * D_MODEL)),
                  _const_spec((1, 2 * D_MODEL)), _const_spec((ML_WIDTH, D_MODEL)),
                  _const_spec((D_MODEL, D_MODEL)), _const_spec((1, D_MODEL)),
                  _const_spec((D_MODEL, 2 * D_FF)), _const_spec((D_FF, D_MODEL)),
                  _const_spec((1, D_MODEL))],
        out_specs=row,
        out_shape=jax.ShapeDtypeStruct((m, D_MODEL), F32),
        compiler_params=pltpu.CompilerParams(dimension_semantics=("arbitrary",),
                                             vmem_limit_bytes=VMEM_LIMIT),
        name="merge_ffn",
    )(x, pa, bo, g1, wg, bg, wpb, wout, g2, wfi, wfo, gf)


SAMPLE_SEQS_PER_STEP = 8
TM_MIX = 512
TM_FFN = 256


def _spatial_tiles(w_s, b_s, chunk):
    reps = GM_CHUNK // chunk
    ws_t = jnp.tile(w_s[:, :chunk, :chunk], (1, reps, reps))
    bs_t = jnp.tile(b_s[:, :chunk].T, (reps, 1))
    bs_t = jnp.repeat(bs_t, GM_GROUP_W, axis=1)
    return ws_t, bs_t


def _gmlp_branch(xf, w, chunk, emit_v):
    ws_t, bs_t = _spatial_tiles(w["w_s"], w["b_s"], chunk)
    return _gmlp_call(xf, w["g1"], w["wuv"], w["lng"], w["lnb"], ws_t, bs_t, w["wpa"],
                      chunk=chunk, emit_v=emit_v, tm=TM_MIX)


def _merge_branch(xf, pa, bo, w):
    return _merge_ffn_call(xf, pa, bo, w["g1"], w["wg"], w["bg"], w["wpb"], w["wout"], w["g2"],
                           w["wfi"], w["wfo"], w["gf"], tm=TM_FFN)


def kernel(x_prompt, x_sample, state_conv, state_C, state_n, state_m, g_norm1, w_in, b_i, b_f, ln_g, ln_b, w_s, b_s, conv_w, conv_b, hn_g, b_gate, w_proj_a, w_proj_b, w_out, g_norm2, w_ffn_in, w_ffn_out, g_final):
    Bp, Tp, _ = x_prompt.shape
    Bs, Ts, _ = x_sample.shape
    win = w_in[0]
    c_uv = 2 * GM_WIDTH
    c_ml = c_uv + 4 * ML_WIDTH
    c_if = c_ml + 2 * ML_HEADS
    wif = jnp.pad(win[:, c_ml:c_if], ((0, 0), (0, GATE_COLS - 2 * ML_HEADS)))
    w = dict(
        g1=g_norm1[0][None], g2=g_norm2[0][None], gf=g_final[None],
        wuv=win[:, :c_uv].astype(BF16),
        wml=jnp.concatenate([win[:, c_uv:c_ml], wif], axis=1).astype(BF16),
        wg=win[:, c_if:].astype(BF16),
        bg=b_gate[0].reshape(1, 2 * D_MODEL),
        lng=ln_g[0][None], lnb=ln_b[0][None], w_s=w_s[0], b_s=b_s[0],
        bif=jnp.pad(jnp.concatenate([b_i[0], b_f[0]]), (0, GATE_COLS - 2 * ML_HEADS))[None],
        cw=conv_w[0], cb=conv_b[0][None], hng=hn_g[0][None],
        wpa=w_proj_a[0].astype(BF16), wpb=w_proj_b[0].astype(BF16), wout=w_out[0].astype(BF16),
        wfi=w_ffn_in[0].astype(BF16), wfo=w_ffn_out[0].astype(BF16),
    )

    xpf = x_prompt.reshape(Bp * Tp, D_MODEL)
    (pa_p,) = _gmlp_branch(xpf, w, GM_CHUNK, False)
    bo_p, conv_p, C_p, n_p, m_p = _mixer_ml_call(x_prompt, w["g1"], w["wml"], w["bif"], w["cw"],
                                                 w["cb"], w["hng"])
    y_p = _merge_branch(xpf, pa_p, bo_p.reshape(Bp * Tp, ML_WIDTH), w)

    xsf = x_sample.reshape(Bs * Ts, D_MODEL)
    pa_s, vn_s = _gmlp_branch(xsf, w, Ts, True)
    zq = _inproj_call(xsf, w["g1"], w["wml"], tm=TM_MIX).reshape(Bs, Ts, ML_COLS)
    st = (state_C[0], state_n[0].reshape(Bs, ML_HEADS, 1, ML_HEAD_DIM),
          jnp.pad(state_m[0], ((0, 0), (0, GATE_COLS - ML_HEADS)))[:, None, :])
    bo_s, conv_s, C_s, n_s, m_s = _mlstm_call(zq, state_conv[0], st, w["bif"], w["cw"], w["cb"],
                                              w["hng"], nb=SAMPLE_SEQS_PER_STEP)
    y_s = _merge_branch(xsf, pa_s, bo_s.reshape(Bs * Ts, ML_WIDTH), w)

    def states(C, n, m, B):
        return C[None], n.reshape(1, B, ML_HEADS, ML_HEAD_DIM), m[None, :, 0, :ML_HEADS]

    return (y_p.reshape(Bp, Tp, D_MODEL), y_s.reshape(Bs, Ts, D_MODEL),
            conv_p[None], *states(C_p, n_p, m_p, Bp),
            conv_s[None], *states(C_s, n_s, m_s, Bs),
            vn_s.reshape(1, Bs, Ts, GM_WIDTH))
```

```python
import functools
import math

import jax
import jax.numpy as jnp
import numpy as np
from jax import lax
from jax.experimental import pallas as pl
from jax.experimental.pallas import tpu as pltpu

D_MODEL = 1024
GM_WIDTH = D_MODEL
GM_GROUPS = 4
GM_GROUP_W = GM_WIDTH // GM_GROUPS
GM_CHUNK = 128
ML_HEADS = 4
ML_HEAD_DIM = D_MODEL // ML_HEADS
ML_WIDTH = ML_HEADS * ML_HEAD_DIM
CONV_W = 4
D_FF = 2816
EPS = 1e-6

LANES = 128
SUBLANES = 8
BF16_ROWS = 16
GATE_COLS = LANES
ML_COLS = 4 * ML_WIDTH + GATE_COLS
VMEM_LIMIT = 56 * 1024 * 1024

F32 = jnp.float32
BF16 = jnp.bfloat16
NEG_BIG = -1e30
LN_INV_K_SCALE = 0.5 * math.log(ML_HEAD_DIM)
NT_DIMS = (((1,), (1,)), ((), ()))
TN_DIMS = (((0,), (0,)), ((), ()))


def _rms(x, g):
    return x * lax.rsqrt(jnp.mean(x * x, axis=-1, keepdims=True) + EPS) * g


def _gelu(x):
    return 0.5 * x * (1.0 + lax.erf(x * (2.0 ** -0.5)))


def _log_sigmoid(x):
    return jnp.minimum(x, 0.0) - jnp.log1p(jnp.exp(-jnp.abs(x)))


def _bdot(a, b):
    return jnp.dot(a, b, preferred_element_type=F32)


def _const_spec(shape):
    nd = len(shape)
    return pl.BlockSpec(shape, lambda *_: (0,) * nd, pipeline_mode=pl.Buffered(1))


def _gmlp_kernel(x_ref, g1_ref, wuv_ref, lng_ref, lnb_ref, ws_ref, bs_ref, wpa_ref, *rest,
                 chunk, emit_v):
    if emit_v:
        pa_ref, vn_ref, a_sc = rest
    else:
        pa_ref, a_sc = rest
    tm = x_ref.shape[0]
    blk = ws_ref.shape[1]
    h = _rms(x_ref[...], g1_ref[...]).astype(BF16)
    u = _gelu(_bdot(h, wuv_ref[:, :GM_WIDTH]))
    v = _gelu(_bdot(h, wuv_ref[:, GM_WIDTH:]))
    mu = jnp.mean(v, axis=-1, keepdims=True)
    vc = v - mu
    var = jnp.mean(vc * vc, axis=-1, keepdims=True)
    vn = vc * lax.rsqrt(var + EPS) * lng_ref[...] + lnb_ref[...]
    if emit_v:
        vn_ref[...] = vn
    vb = vn.astype(BF16)
    r = lax.broadcasted_iota(jnp.int32, (blk, blk), 0)
    c = lax.broadcasted_iota(jnp.int32, (blk, blk), 1)
    keep = c <= r
    if chunk < blk:
        sh = chunk.bit_length() - 1
        keep = jnp.logical_and(keep, (r >> sh) == (c >> sh))
    for g in range(GM_GROUPS):
        wsm = jnp.where(keep, ws_ref[g], 0.0).astype(BF16)
        cs = slice(g * GM_GROUP_W, (g + 1) * GM_GROUP_W)
        for i in range(tm // blk):
            rs = slice(i * blk, (i + 1) * blk)
            s = _bdot(wsm, vb[rs, cs]) + bs_ref[:, cs]
            a_sc[rs, cs] = (u[rs, cs] * s).astype(BF16)
    pa_ref[...] = _bdot(a_sc[...], wpa_ref[...])


def _gmlp_call(x, g1, w_all, lng, lnb, ws_t, bs_t, wpa, *, chunk, emit_v, tm):
    m = x.shape[0]
    blk = ws_t.shape[1]
    row = pl.BlockSpec((tm, D_MODEL), lambda i: (i, 0))
    out_shape = [jax.ShapeDtypeStruct((m, D_MODEL), F32)]
    out_specs = [row]
    if emit_v:
        out_shape.append(jax.ShapeDtypeStruct((m, GM_WIDTH), F32))
        out_specs.append(row)
    return pl.pallas_call(
        functools.partial(_gmlp_kernel, chunk=chunk, emit_v=emit_v),
        grid=(m // tm,),
        in_specs=[row, _const_spec((1, D_MODEL)),
                  pl.BlockSpec((D_MODEL, 2 * GM_WIDTH), lambda i: (0, 0), pipeline_mode=pl.Buffered(1)),
                  _const_spec((1, GM_WIDTH)), _const_spec((1, GM_WIDTH)),
                  _const_spec((GM_GROUPS, blk, blk)), _const_spec((blk, GM_WIDTH)),
                  _const_spec((GM_WIDTH, D_MODEL))],
        out_specs=out_specs,
        out_shape=out_shape,
        scratch_shapes=[pltpu.VMEM((tm, GM_WIDTH), BF16)],
        compiler_params=pltpu.CompilerParams(dimension_semantics=("arbitrary",),
                                             vmem_limit_bytes=VMEM_LIMIT),
        name="gmlp",
    )(x, g1, w_all, lng, lnb, ws_t, bs_t, wpa)


def _ml_weight_specs():
    wide = 2 * ML_WIDTH
    assert (2 * GM_WIDTH) % wide == 0 and (2 * GM_WIDTH + 4 * ML_WIDTH) % GATE_COLS == 0
    first = 2 * GM_WIDTH // wide
    col_block = lambda width, idx: pl.BlockSpec((D_MODEL, width), lambda *_: (0, idx),
                                                pipeline_mode=pl.Buffered(1))
    return [col_block(wide, first), col_block(wide, first + 1),
            col_block(GATE_COLS, (2 * GM_WIDTH + 4 * ML_WIDTH) // GATE_COLS)]


def _inproj_kernel(x_ref, g1_ref, wqk_ref, wvo_ref, wif_ref, z_ref):
    h = _rms(x_ref[...], g1_ref[...]).astype(BF16)
    z_ref[:, 0:2 * ML_WIDTH] = _bdot(h, wqk_ref[...])
    z_ref[:, 2 * ML_WIDTH:4 * ML_WIDTH] = _bdot(h, wvo_ref[...])
    z_ref[:, 4 * ML_WIDTH:ML_COLS] = _bdot(h, wif_ref[...])


def _inproj_call(x, g1, w_all, *, tm):
    m = x.shape[0]
    return pl.pallas_call(
        _inproj_kernel,
        grid=(m // tm,),
        in_specs=[pl.BlockSpec((tm, D_MODEL), lambda i: (i, 0)), _const_spec((1, D_MODEL))]
        + _ml_weight_specs(),
        out_specs=pl.BlockSpec((tm, ML_COLS), lambda i: (i, 0)),
        out_shape=jax.ShapeDtypeStruct((m, ML_COLS), F32),
        compiler_params=pltpu.CompilerParams(dimension_semantics=("arbitrary",),
                                             vmem_limit_bytes=VMEM_LIMIT),
        name="inproj",
    )(x, g1, w_all, w_all, w_all)


def _mlstm_heads(q_of, k_of, v_of, o_of, causal, ipre, bcum, groups, m0_of, C0_of, n0_of, hng_ref):
    heads = range(ML_HEADS)
    single = len(groups) == 1

    def rows_of(x, g):
        return x if single else x[groups[g][0]:groups[g][0] + groups[g][1]]

    def per_row(vals):
        if single:
            return vals[0]
        return jnp.concatenate([jnp.broadcast_to(v, (groups[g][1], v.shape[1]))
                                for g, v in enumerate(vals)], axis=0)

    a = ipre - bcum
    a_t = a.T
    m_rows = per_row([m0_of(g) for g in range(len(groups))])

    a2 = [jnp.where(causal, a_t[h:h + 1, :], -jnp.inf) for h in heads]
    mc = [jnp.maximum(jnp.max(a2[h], axis=1, keepdims=True), m_rows[:, h:h + 1]) for h in heads]
    m_last = [[mc[h][grp[2]:grp[2] + 1, :] for grp in groups] for h in heads]
    m_last_rows = [per_row(m_last[h]) for h in heads]
    dm = [jnp.exp(a2[h] - (mc[h] + LN_INV_K_SCALE)) for h in heads]
    w_inter = [jnp.exp(m_rows[:, h:h + 1] - mc[h]) for h in heads]
    w_col = [jnp.exp(a[:, h:h + 1] - (m_last_rows[h] + LN_INV_K_SCALE)) for h in heads]
    decay = [[jnp.exp(m0_of(g)[:, h:h + 1] - m_last[h][g]) for g in range(len(groups))] for h in heads]

    q = [q_of(h) for h in heads]
    k = [k_of(h) for h in heads]
    v = [v_of(h) for h in heads]
    qb = [x.astype(BF16) for x in q]
    kb = [x.astype(BF16) for x in k]
    vb = [x.astype(BF16) for x in v]
    vw = [(v[h] * w_col[h]).astype(BF16) for h in heads]
    c_old = [[C0_of(g, h) for h in heads] for g in range(len(groups))]
    n_old = [[n0_of(g, h) for h in heads] for g in range(len(groups))]

    qk = [lax.dot_general(qb[h], kb[h], NT_DIMS, preferred_element_type=F32) for h in heads]
    qc = [[lax.dot_general(rows_of(qb[h], g), c_old[g][h].astype(BF16), NT_DIMS,
                           preferred_element_type=F32) for g in range(len(groups))] for h in heads]
    if single:
        qn = [lax.dot_general(qb[h], jnp.broadcast_to(n_old[0][h], (LANES, ML_HEAD_DIM)).astype(BF16),
                              NT_DIMS, preferred_element_type=F32)[:, 0:1] for h in heads]
    else:
        qn = [jnp.sum(q[h] * per_row([n_old[g][h] for g in range(len(groups))]), axis=1, keepdims=True)
              for h in heads]
    s = [dm[h] * qk[h] for h in heads]
    sv = [_bdot(s[h].astype(BF16), vb[h]) for h in heads]
    cupd = [[lax.dot_general(rows_of(vw[h], g), rows_of(kb[h], g), TN_DIMS, preferred_element_type=F32)
             for g in range(len(groups))] for h in heads]

    outs = []
    for h in heads:
        qc_rows = qc[h][0] if single else jnp.concatenate(qc[h], axis=0)
        num = w_inter[h] * qc_rows + sv[h]
        den = w_inter[h] * qn[h] + jnp.sum(s[h], axis=1, keepdims=True)
        hcur = num / jnp.maximum(jnp.abs(den), jnp.exp(-(bcum[:, h:h + 1] + mc[h])))
        mu = jnp.mean(hcur, axis=1, keepdims=True)
        hc = hcur - mu
        var = jnp.mean(hc * hc, axis=1, keepdims=True)
        hs = slice(h * ML_HEAD_DIM, (h + 1) * ML_HEAD_DIM)
        outs.append(jax.nn.sigmoid(o_of(h)) * (hc * lax.rsqrt(var + EPS) * hng_ref[:, hs]))

    C_new, n_new, m_new = [], [], []
    for g, grp in enumerate(groups):
        C_new.append([decay[h][g] * c_old[g][h] + cupd[h][g] for h in heads])
        n_new.append([decay[h][g] * n_old[g][h]
                      + jnp.sum(rows_of(k[h] * w_col[h], g), axis=0, keepdims=True) for h in heads])
        row = m0_of(g)
        lane = lax.broadcasted_iota(jnp.int32, row.shape, 1)
        for h in heads:
            row = jnp.where(lane == h, bcum[grp[2]:grp[2] + 1, h:h + 1] + m_last[h][g], row)
        m_new.append(row)
    return outs, C_new, n_new, m_new


def _mlstm_kernel(zq_ref, cst_ref, C0_ref, n0_ref, m0_ref, bif_ref, cw_ref, cb_ref, hng_ref,
                  bo_ref, conv_ref, C_ref, n_ref, m_ref, zp, xp, *, t_valid):
    nb = C0_ref.shape[0]
    L = BF16_ROWS
    R = nb * L

    @pl.when(pl.program_id(0) == 0)
    def _():
        zp[...] = jnp.zeros(zp.shape, F32)

    for bb in range(nb):
        zp[bb, 0:t_valid, :] = zq_ref[bb * t_valid:(bb + 1) * t_valid, :]
        for j in range(CONV_W - 1):
            row = SUBLANES - (CONV_W - 1) + j
            xp[bb, row:row + 1, :] = cst_ref[j, bb:bb + 1, :]
        xp[bb, SUBLANES:2 * SUBLANES, :] = zp[bb, 0:SUBLANES, 0:2 * ML_WIDTH]
        for j in range(CONV_W - 1):
            row = SUBLANES + t_valid - (CONV_W - 1) + j
            conv_ref[j, bb:bb + 1, :] = xp[bb, row:row + 1, :]
    qk_rows = []
    for bb in range(nb):
        acc = cb_ref[...]
        for j in range(CONV_W):
            off = SUBLANES - (CONV_W - 1) + j
            acc = acc + cw_ref[j:j + 1, :] * xp[bb, off:off + SUBLANES, :]
        qk_rows += [acc, jnp.zeros((L - SUBLANES, 2 * ML_WIDTH), F32)]
    qk = jnp.concatenate(qk_rows, axis=0)
    qk = qk * jax.nn.sigmoid(qk)

    def cols(c0, width):
        return zp[:, :, c0:c0 + width].reshape(R, width)

    zif = cols(4 * ML_WIDTH, GATE_COLS) + bif_ref[...]
    live = (lax.broadcasted_iota(jnp.int32, (R, GATE_COLS), 0) & (L - 1)) < t_valid
    ipre = jnp.where(live, zif, NEG_BIG)
    logf = jnp.where(live, pltpu.roll(_log_sigmoid(zif), GATE_COLS - ML_HEADS, axis=1), 0.0)
    r = lax.broadcasted_iota(jnp.int32, (R, R), 0)
    c = lax.broadcasted_iota(jnp.int32, (R, R), 1)
    sh = L.bit_length() - 1
    causal = jnp.logical_and(c <= r, (r >> sh) == (c >> sh))
    bcum = jnp.dot(causal.astype(F32), logf, preferred_element_type=F32,
                   precision=lax.Precision.HIGHEST)
    live_w = (lax.broadcasted_iota(jnp.int32, (R, ML_HEAD_DIM), 0) & (L - 1)) < t_valid

    def head_cols(x, base, h):
        return x[:, base + h * ML_HEAD_DIM:base + (h + 1) * ML_HEAD_DIM]

    groups = [(bb * L, L, bb * L + L - 1) for bb in range(nb)]
    outs, C_new, n_new, m_new = _mlstm_heads(
        q_of=lambda h: head_cols(qk, 0, h),
        k_of=lambda h: jnp.where(live_w, head_cols(qk, ML_WIDTH, h), 0.0),
        v_of=lambda h: jnp.where(live_w, cols(2 * ML_WIDTH + h * ML_HEAD_DIM, ML_HEAD_DIM), 0.0),
        o_of=lambda h: cols(3 * ML_WIDTH + h * ML_HEAD_DIM, ML_HEAD_DIM),
        causal=causal, ipre=ipre, bcum=bcum, groups=groups,
        m0_of=lambda g: m0_ref[g:g + 1, :], C0_of=lambda g, h: C0_ref[g, h],
        n0_of=lambda g, h: n0_ref[g, h:h + 1, :], hng_ref=hng_ref)
    for bb in range(nb):
        for h in range(ML_HEADS):
            hs = slice(h * ML_HEAD_DIM, (h + 1) * ML_HEAD_DIM)
            bo_ref[bb * t_valid:(bb + 1) * t_valid, hs] = outs[h][bb * L:bb * L + t_valid, :]
            C_ref[bb, h] = C_new[bb][h]
            n_ref[bb, h:h + 1, :] = n_new[bb][h]
        m_ref[bb:bb + 1, :] = m_new[bb]


def _mlstm_call(zq, cst, state, bif, cw, cb, hng, *, t_valid, nb):
    T = t_valid
    B = zq.shape[0] // T
    kern = functools.partial(_mlstm_kernel, t_valid=T)
    per_b = lambda shape: pl.BlockSpec((nb,) + shape, lambda b: (b,) + (0,) * len(shape))
    rows = lambda width: pl.BlockSpec((nb * T, width), lambda b: (b, 0))
    conv_spec = pl.BlockSpec((CONV_W - 1, nb, 2 * ML_WIDTH), lambda b: (0, b, 0))
    st_specs = [per_b((ML_HEADS, ML_HEAD_DIM, ML_HEAD_DIM)), per_b((ML_HEADS, ML_HEAD_DIM)),
                per_b((ML_HEADS,))]
    in_specs = ([rows(ML_COLS), conv_spec] + st_specs
                + [_const_spec((1, GATE_COLS)), _const_spec((CONV_W, 2 * ML_WIDTH)),
                   _const_spec((1, 2 * ML_WIDTH)), _const_spec((1, ML_WIDTH))])
    out_shape = [jax.ShapeDtypeStruct((B * T, ML_WIDTH), F32),
                 jax.ShapeDtypeStruct((CONV_W - 1, B, 2 * ML_WIDTH), F32),
                 jax.ShapeDtypeStruct((B, ML_HEADS, ML_HEAD_DIM, ML_HEAD_DIM), F32),
                 jax.ShapeDtypeStruct((B, ML_HEADS, ML_HEAD_DIM), F32),
                 jax.ShapeDtypeStruct((B, ML_HEADS), F32)]
    out_specs = [rows(ML_WIDTH), conv_spec] + st_specs
    return pl.pallas_call(
        kern,
        grid=(B // nb,),
        in_specs=in_specs,
        out_specs=out_specs,
        out_shape=out_shape,
        scratch_shapes=[pltpu.VMEM((nb, BF16_ROWS, ML_COLS), F32),
                        pltpu.VMEM((nb, 2 * SUBLANES, 2 * ML_WIDTH), F32)],
        compiler_params=pltpu.CompilerParams(dimension_semantics=("arbitrary",),
                                             vmem_limit_bytes=VMEM_LIMIT),
        name="mlstm",
    )(zq, cst, *state, bif, cw, cb, hng)


PL = 256
PG = PL // SUBLANES
TAIL = (CONV_W - 1) * SUBLANES


def _perm_rows(j):
    return pl.ds((PL // 4) * (j % 4) + j // 4, SUBLANES, stride=SUBLANES)


def _mixer_ml_kernel(x_ref, g1_ref, wqk_ref, wvo_ref, wif_ref, bif_ref, cw_ref, cb_ref, hng_ref,
                     bo_ref, conv_ref, C_ref, n_ref, m_ref, xs, us, tail):
    t = pl.program_id(1)

    @pl.when(t == 0)
    def _():
        tail[...] = jnp.zeros(tail.shape, F32)
        C_ref[...] = jnp.zeros(C_ref.shape, F32)
        n_ref[...] = jnp.zeros(n_ref.shape, F32)
        m_ref[...] = jnp.zeros(m_ref.shape, F32)

    n_lane_chunks = D_MODEL // LANES
    for j in range(PG):
        for c in range(n_lane_chunks):
            xs[c, _perm_rows(j), :] = x_ref[j * SUBLANES:(j + 1) * SUBLANES, c * LANES:(c + 1) * LANES]
    xp = jnp.concatenate([xs[c] for c in range(n_lane_chunks)], axis=1)
    h = _rms(xp, g1_ref[...]).astype(BF16)
    zqk = _bdot(h, wqk_ref[...])
    zvo = _bdot(h, wvo_ref[...])
    zif = _bdot(h, wif_ref[...]) + bif_ref[...]

    zqk_tail = zqk[PL - TAIL:, :]
    sub = lax.broadcasted_iota(jnp.int32, (SUBLANES, 2 * ML_WIDTH), 0)
    wrapped = []
    for g in range(CONV_W - 1):
        cur = pltpu.roll(zqk_tail[g * SUBLANES:(g + 1) * SUBLANES], 1, axis=0)
        prev = pltpu.roll(tail[g * SUBLANES:(g + 1) * SUBLANES, :], 1, axis=0)
        wrapped.append(jnp.where(sub == 0, prev, cur))
    wrapped = jnp.concatenate(wrapped, axis=0)
    tail[...] = zqk_tail
    conv_ref[...] = jnp.concatenate(
        [zqk_tail[g * SUBLANES + SUBLANES - 1:(g + 1) * SUBLANES, :] for g in range(CONV_W - 1)], axis=0)

    def conv_silu(c0, width):
        cs = slice(c0, c0 + width)
        acc = cb_ref[:, cs] + cw_ref[CONV_W - 1:CONV_W, cs] * zqk[:, cs]
        for d in range(1, CONV_W):
            shifted = jnp.concatenate(
                [wrapped[TAIL - d * SUBLANES:, cs], zqk[:PL - d * SUBLANES, cs]], axis=0)
            acc = acc + cw_ref[CONV_W - 1 - d:CONV_W - d, cs] * shifted
        return acc * jax.nn.sigmoid(acc)

    pr = lax.broadcasted_iota(jnp.int32, (PL, PL), 0)
    pc = lax.broadcasted_iota(jnp.int32, (PL, PL), 1)
    causal = ((pc >> 3) + PG * (pc & 7)) <= ((pr >> 3) + PG * (pr & 7))

    logf = pltpu.roll(_log_sigmoid(zif), GATE_COLS - ML_HEADS, axis=1)
    run, partial = None, []
    for n in range(PG):
        blk = logf[n * SUBLANES:(n + 1) * SUBLANES, :]
        run = blk if run is None else run + blk
        partial.append(run)
    sub_g = lax.broadcasted_iota(jnp.int32, (SUBLANES, GATE_COLS), 0)
    incl = run
    for step in (1, 2, 4):
        incl = incl + jnp.where(sub_g >= step, pltpu.roll(incl, step, axis=0), 0.0)
    earlier = incl - run
    bcum = jnp.concatenate([p + earlier for p in partial], axis=0)

    def head_cols(base, h):
        return zvo[:, base + h * ML_HEAD_DIM:base + (h + 1) * ML_HEAD_DIM]

    outs, C_new, n_new, m_new = _mlstm_heads(
        q_of=lambda h: conv_silu(h * ML_HEAD_DIM, ML_HEAD_DIM),
        k_of=lambda h: conv_silu(ML_WIDTH + h * ML_HEAD_DIM, ML_HEAD_DIM),
        v_of=lambda h: head_cols(0, h),
        o_of=lambda h: head_cols(ML_WIDTH, h),
        causal=causal, ipre=zif, bcum=bcum, groups=[(0, PL, PL - 1)],
        m0_of=lambda g: m_ref[...], C0_of=lambda g, h: C_ref[h], n0_of=lambda g, h: n_ref[h:h + 1, :],
        hng_ref=hng_ref)
    cph = ML_HEAD_DIM // LANES
    for h in range(ML_HEADS):
        for cc in range(cph):
            us[h * cph + cc] = outs[h][:, cc * LANES:(cc + 1) * LANES]
        C_ref[h] = C_new[0][h]
        n_ref[h:h + 1, :] = n_new[0][h]
    m_ref[...] = m_new[0]
    for c in range(ML_WIDTH // LANES):
        for j in range(0, PG, 2):
            pair = jnp.concatenate([us[c, _perm_rows(j), :], us[c, _perm_rows(j + 1), :]], axis=0)
            bo_ref[j * SUBLANES:(j + 2) * SUBLANES, c * LANES:(c + 1) * LANES] = pair.astype(BF16)


def _mixer_ml_call(x, g1, w_all, bif, cw, cb, hng):
    B, T, _ = x.shape
    per_b = lambda shape: pl.BlockSpec((None,) + shape, lambda b, t: (b,) + (0,) * len(shape))
    out_shape = [jax.ShapeDtypeStruct((B, T, ML_WIDTH), BF16),
                 jax.ShapeDtypeStruct((B, CONV_W - 1, 2 * ML_WIDTH), F32),
                 jax.ShapeDtypeStruct((B, ML_HEADS, ML_HEAD_DIM, ML_HEAD_DIM), F32),
                 jax.ShapeDtypeStruct((B, ML_HEADS, ML_HEAD_DIM), F32),
                 jax.ShapeDtypeStruct((B, 1, ML_HEADS), F32)]
    out_specs = [pl.BlockSpec((None, PL, ML_WIDTH), lambda b, t: (b, t, 0)),
                 per_b((CONV_W - 1, 2 * ML_WIDTH)),
                 per_b((ML_HEADS, ML_HEAD_DIM, ML_HEAD_DIM)), per_b((ML_HEADS, ML_HEAD_DIM)),
                 per_b((1, ML_HEADS))]
    return pl.pallas_call(
        _mixer_ml_kernel,
        grid=(B, T // PL),
        in_specs=[pl.BlockSpec((None, PL, D_MODEL), lambda b, t: (b, t, 0)),
                  _const_spec((1, D_MODEL))] + _ml_weight_specs()
        + [_const_spec((1, GATE_COLS)), _const_spec((CONV_W, 2 * ML_WIDTH)),
           _const_spec((1, 2 * ML_WIDTH)), _const_spec((1, ML_WIDTH))],
        out_specs=out_specs,
        out_shape=out_shape,
        scratch_shapes=[pltpu.VMEM((D_MODEL // LANES, PL, LANES), F32),
                        pltpu.VMEM((ML_WIDTH // LANES, PL, LANES), F32),
                        pltpu.VMEM((TAIL, 2 * ML_WIDTH), F32)],
        compiler_params=pltpu.CompilerParams(dimension_semantics=("arbitrary", "arbitrary"),
                                             vmem_limit_bytes=VMEM_LIMIT),
        name="mixer_ml",
    )(x, g1, w_all, w_all, w_all, bif, cw, cb, hng)


def _merge_ffn_kernel(x_ref, pa_ref, bo_ref, g1_ref, wg_ref, bg_ref, wpb_ref, wout_ref,
                      g2_ref, wfi_ref, wfo_ref, gf_ref, y_ref):
    x = x_ref[...]
    h = _rms(x, g1_ref[...]).astype(BF16)
    gab = _bdot(h, wg_ref[...]) + bg_ref[...]
    pb = _bdot(bo_ref[...].astype(BF16), wpb_ref[...])
    merged = (jax.nn.sigmoid(gab[:, :D_MODEL]) * pa_ref[...]
              + jax.nn.sigmoid(gab[:, D_MODEL:]) * pb)
    x1 = x + _bdot(merged.astype(BF16), wout_ref[...])
    h2 = _rms(x1, g2_ref[...]).astype(BF16)
    gu = _bdot(h2, wfi_ref[...])
    gt = gu[:, :D_FF]
    hid = (gt * jax.nn.sigmoid(gt) * gu[:, D_FF:]).astype(BF16)
    x2 = x1 + _bdot(hid, wfo_ref[...])
    y_ref[...] = _rms(x2, gf_ref[...])


def _merge_ffn_call(x, pa, bo, g1, wg, bg, wpb, wout, g2, wfi, wfo, gf, *, tm):
    m = x.shape[0]
    row = pl.BlockSpec((tm, D_MODEL), lambda i: (i, 0))
    return pl.pallas_call(
        _merge_ffn_kernel,
        grid=(m // tm,),
        in_specs=[row, row, row, _const_spec((1, D_MODEL)), _const_spec((D_MODEL, 2 * D_MODEL)),
                  _const_spec((1, 2 * D_MODEL)), _const_spec((ML_WIDTH, D_MODEL)),
                  _const_spec((D_MODEL, D_MODEL)), _const_spec((1, D_MODEL)),
                  _const_spec((D_MODEL, 2 * D_FF)), _const_spec((D_FF, D_MODEL)),
                  _const_spec((1, D_MODEL))],
        out_specs=row,
        out_shape=jax.ShapeDtypeStruct((m, D_MODEL), F32),
        compiler_params=pltpu.CompilerParams(dimension_semantics=("arbitrary",),
                                             vmem_limit_bytes=VMEM_LIMIT),
        name="merge_ffn",
    )(x, pa, bo, g1, wg, bg, wpb, wout, g2, wfi, wfo, gf)


SAMPLE_SEQS_PER_STEP = 8
TM_MIX = 512
TM_FFN = 256


def _spatial_tiles(w_s, b_s, chunk):
    if chunk == GM_CHUNK:
        ws_t, b_pos = w_s[:, :chunk, :chunk], b_s[:, :chunk].T
    else:
        onehot = jnp.asarray(np.arange(GM_CHUNK)[:, None] % chunk == np.arange(chunk)[None, :], F32)
        hp = lax.Precision.HIGHEST
        ws_t = jnp.einsum("ri,gij,cj->grc", onehot, w_s[:, :chunk, :chunk], onehot, precision=hp)
        b_pos = jnp.dot(onehot, b_s[:, :chunk].T, precision=hp)
    bs_t = jnp.repeat(b_pos, GM_GROUP_W, axis=1)
    return ws_t, bs_t


def _gmlp_branch(xf, w, chunk, emit_v):
    ws_t, bs_t = _spatial_tiles(w["w_s"], w["b_s"], chunk)
    return _gmlp_call(xf, w["g1"], w["w_all"], w["lng"], w["lnb"], ws_t, bs_t, w["wpa"],
                      chunk=chunk, emit_v=emit_v, tm=TM_MIX)


def _merge_branch(xf, pa, bo, w):
    return _merge_ffn_call(xf, pa, bo, w["g1"], w["wg"], w["bg"], w["wpb"], w["wout"], w["g2"],
                           w["wfi"], w["wfo"], w["gf"], tm=TM_FFN)


def kernel(x_prompt, x_sample, state_conv, state_C, state_n, state_m, g_norm1, w_in, b_i, b_f, ln_g, ln_b, w_s, b_s, conv_w, conv_b, hn_g, b_gate, w_proj_a, w_proj_b, w_out, g_norm2, w_ffn_in, w_ffn_out, g_final):
    Bp, Tp, _ = x_prompt.shape
    Bs, Ts, _ = x_sample.shape
    win = w_in[0]
    c_if = 2 * GM_WIDTH + 4 * ML_WIDTH + 2 * ML_HEADS
    w = dict(
        g1=g_norm1[0][None], g2=g_norm2[0][None], gf=g_final[None],
        w_all=win.astype(BF16),
        wg=win[:, c_if:].astype(BF16),
        bg=b_gate[0].reshape(1, 2 * D_MODEL),
        lng=ln_g[0][None], lnb=ln_b[0][None], w_s=w_s[0], b_s=b_s[0],
        bif=jnp.pad(jnp.concatenate([b_i[0], b_f[0]]), (0, GATE_COLS - 2 * ML_HEADS))[None],
        cw=conv_w[0], cb=conv_b[0][None], hng=hn_g[0][None],
        wpa=w_proj_a[0].astype(BF16), wpb=w_proj_b[0].astype(BF16), wout=w_out[0].astype(BF16),
        wfi=w_ffn_in[0].astype(BF16), wfo=w_ffn_out[0].astype(BF16),
    )

    xpf = x_prompt.reshape(Bp * Tp, D_MODEL)
    (pa_p,) = _gmlp_branch(xpf, w, GM_CHUNK, False)
    bo_p, conv_p, C_p, n_p, m_p = _mixer_ml_call(x_prompt, w["g1"], w["w_all"], w["bif"], w["cw"],
                                                 w["cb"], w["hng"])
    y_p = _merge_branch(xpf, pa_p, bo_p.reshape(Bp * Tp, ML_WIDTH), w)

    xsf = x_sample.reshape(Bs * Ts, D_MODEL)
    pa_s, vn_s = _gmlp_branch(xsf, w, Ts, True)
    zq = _inproj_call(xsf, w["g1"], w["w_all"], tm=TM_MIX)
    st = (state_C[0], state_n[0], state_m[0])
    bo_s, conv_s, C_s, n_s, m_s = _mlstm_call(zq, jnp.transpose(state_conv[0], (1, 0, 2)), st, w["bif"],
                                              w["cw"], w["cb"], w["hng"], t_valid=Ts,
                                              nb=SAMPLE_SEQS_PER_STEP)
    y_s = _merge_branch(xsf, pa_s, bo_s, w)

    return (y_p.reshape(Bp, Tp, D_MODEL), y_s.reshape(Bs, Ts, D_MODEL),
            conv_p[None], C_p[None], n_p[None], m_p.reshape(1, Bp, ML_HEADS),
            jnp.transpose(conv_s, (1, 0, 2))[None], C_s[None], n_s[None], m_s[None],
            vn_s.reshape(1, Bs, Ts, GM_WIDTH))
```

```python
import functools
import math

import jax
import jax.numpy as jnp
import numpy as np
from jax import lax
from jax.experimental import pallas as pl
from jax.experimental.pallas import tpu as pltpu

D_MODEL = 1024
GM_WIDTH = D_MODEL
GM_GROUPS = 4
GM_GROUP_W = GM_WIDTH // GM_GROUPS
GM_CHUNK = 128
ML_HEADS = 4
ML_HEAD_DIM = D_MODEL // ML_HEADS
ML_WIDTH = ML_HEADS * ML_HEAD_DIM
CONV_W = 4
D_FF = 2816
EPS = 1e-6

LANES = 128
SUBLANES = 8
BF16_ROWS = 16
GATE_COLS = LANES
ML_COLS = 4 * ML_WIDTH + GATE_COLS
VMEM_LIMIT = 56 * 1024 * 1024

F32 = jnp.float32
BF16 = jnp.bfloat16
NEG_BIG = -1e30
LN_INV_K_SCALE = 0.5 * math.log(ML_HEAD_DIM)
NT_DIMS = (((1,), (1,)), ((), ()))
TN_DIMS = (((0,), (0,)), ((), ()))


def _rms(x, g):
    return x * lax.rsqrt(jnp.mean(x * x, axis=-1, keepdims=True) + EPS) * g


def _gelu(x):
    return 0.5 * x * (1.0 + lax.erf(x * (2.0 ** -0.5)))


def _log_sigmoid(x):
    return jnp.minimum(x, 0.0) - jnp.log1p(jnp.exp(-jnp.abs(x)))


def _bdot(a, b):
    return jnp.dot(a, b, preferred_element_type=F32)


def _const_spec(shape):
    nd = len(shape)
    return pl.BlockSpec(shape, lambda *_: (0,) * nd, pipeline_mode=pl.Buffered(1))


def _gmlp_kernel(x_ref, g1_ref, wuv_ref, lng_ref, lnb_ref, ws_ref, bs_ref, wpa_ref, *rest,
                 chunk, emit_v, emit_hperm):
    rest = list(rest)
    pa_ref = rest.pop(0)
    vn_ref = rest.pop(0) if emit_v else None
    hp_ref = rest.pop(0) if emit_hperm else None
    a_sc = rest.pop(0)
    tm = x_ref.shape[0]
    blk = ws_ref.shape[1]
    hf = _rms(x_ref[...], g1_ref[...])
    h = hf.astype(BF16)
    if emit_hperm:
        hs = rest.pop(0)
        for j in range(tm // SUBLANES):
            base = (j // PG) * PL
            dst = _perm_rows(j % PG)
            for c in range(D_MODEL // LANES):
                hs[c, pl.ds(base + dst.start, SUBLANES, stride=SUBLANES), :] = (
                    hf[j * SUBLANES:(j + 1) * SUBLANES, c * LANES:(c + 1) * LANES])
        for c in range(D_MODEL // LANES):
            hp_ref[:, c * LANES:(c + 1) * LANES] = hs[c].astype(BF16)
    u = _gelu(_bdot(h, wuv_ref[:, :GM_WIDTH]))
    v = _gelu(_bdot(h, wuv_ref[:, GM_WIDTH:]))
    mu = jnp.mean(v, axis=-1, keepdims=True)
    vc = v - mu
    var = jnp.mean(vc * vc, axis=-1, keepdims=True)
    vn = vc * lax.rsqrt(var + EPS) * lng_ref[...] + lnb_ref[...]
    if emit_v:
        vn_ref[...] = vn
    vb = vn.astype(BF16)
    r = lax.broadcasted_iota(jnp.int32, (blk, blk), 0)
    c = lax.broadcasted_iota(jnp.int32, (blk, blk), 1)
    keep = c <= r
    if chunk < blk:
        sh = chunk.bit_length() - 1
        keep = jnp.logical_and(keep, (r >> sh) == (c >> sh))
    for g in range(GM_GROUPS):
        wsm = jnp.where(keep, ws_ref[g], 0.0).astype(BF16)
        cs = slice(g * GM_GROUP_W, (g + 1) * GM_GROUP_W)
        for i in range(tm // blk):
            rs = slice(i * blk, (i + 1) * blk)
            s = _bdot(wsm, vb[rs, cs]) + bs_ref[:, cs]
            a_sc[rs, cs] = (u[rs, cs] * s).astype(BF16)
    pa_ref[...] = _bdot(a_sc[...], wpa_ref[...])


def _gmlp_call(x, g1, w_all, lng, lnb, ws_t, bs_t, wpa, *, chunk, emit_v, emit_hperm, tm):
    m = x.shape[0]
    blk = ws_t.shape[1]
    row = pl.BlockSpec((tm, D_MODEL), lambda i: (i, 0))
    out_shape = [jax.ShapeDtypeStruct((m, D_MODEL), F32)]
    out_specs = [row]
    scratch = [pltpu.VMEM((tm, GM_WIDTH), BF16)]
    if emit_v:
        out_shape.append(jax.ShapeDtypeStruct((m, GM_WIDTH), F32))
        out_specs.append(row)
    if emit_hperm:
        assert tm % PL == 0
        out_shape.append(jax.ShapeDtypeStruct((m, D_MODEL), BF16))
        out_specs.append(row)
        scratch.append(pltpu.VMEM((D_MODEL // LANES, tm, LANES), F32))
    return pl.pallas_call(
        functools.partial(_gmlp_kernel, chunk=chunk, emit_v=emit_v, emit_hperm=emit_hperm),
        grid=(m // tm,),
        in_specs=[row, _const_spec((1, D_MODEL)),
                  pl.BlockSpec((D_MODEL, 2 * GM_WIDTH), lambda i: (0, 0), pipeline_mode=pl.Buffered(1)),
                  _const_spec((1, GM_WIDTH)), _const_spec((1, GM_WIDTH)),
                  _const_spec((GM_GROUPS, blk, blk)), _const_spec((blk, GM_WIDTH)),
                  _const_spec((GM_WIDTH, D_MODEL))],
        out_specs=out_specs,
        out_shape=out_shape,
        scratch_shapes=scratch,
        compiler_params=pltpu.CompilerParams(dimension_semantics=("arbitrary",),
                                             vmem_limit_bytes=VMEM_LIMIT),
        name="gmlp",
    )(x, g1, w_all, lng, lnb, ws_t, bs_t, wpa)


def _ml_weight_specs():
    wide = 2 * ML_WIDTH
    assert (2 * GM_WIDTH) % wide == 0 and (2 * GM_WIDTH + 4 * ML_WIDTH) % GATE_COLS == 0
    first = 2 * GM_WIDTH // wide
    col_block = lambda width, idx: pl.BlockSpec((D_MODEL, width), lambda *_: (0, idx),
                                                pipeline_mode=pl.Buffered(1))
    return [col_block(wide, first), col_block(wide, first + 1),
            col_block(GATE_COLS, (2 * GM_WIDTH + 4 * ML_WIDTH) // GATE_COLS)]


def _inproj_kernel(x_ref, g1_ref, wqk_ref, wvo_ref, wif_ref, z_ref):
    h = _rms(x_ref[...], g1_ref[...]).astype(BF16)
    z_ref[:, 0:2 * ML_WIDTH] = _bdot(h, wqk_ref[...])
    z_ref[:, 2 * ML_WIDTH:4 * ML_WIDTH] = _bdot(h, wvo_ref[...])
    z_ref[:, 4 * ML_WIDTH:ML_COLS] = _bdot(h, wif_ref[...])


def _inproj_call(x, g1, w_all, *, tm):
    m = x.shape[0]
    return pl.pallas_call(
        _inproj_kernel,
        grid=(m // tm,),
        in_specs=[pl.BlockSpec((tm, D_MODEL), lambda i: (i, 0)), _const_spec((1, D_MODEL))]
        + _ml_weight_specs(),
        out_specs=pl.BlockSpec((tm, ML_COLS), lambda i: (i, 0)),
        out_shape=jax.ShapeDtypeStruct((m, ML_COLS), F32),
        compiler_params=pltpu.CompilerParams(dimension_semantics=("arbitrary",),
                                             vmem_limit_bytes=VMEM_LIMIT),
        name="inproj",
    )(x, g1, w_all, w_all, w_all)


def _mlstm_heads(q_of, k_of, v_of, o_of, causal, ipre, bcum, groups, m0_of, C0_of, n0_of, hng_ref,
                 fill=None):
    n_groups = len(groups)
    single = n_groups == 1
    fill = fill or (lambda: None)

    def rows_of(x, g):
        return x if single else x[groups[g][0]:groups[g][0] + groups[g][1]]

    def per_row(vals):
        if single:
            return vals[0]
        return jnp.concatenate([jnp.broadcast_to(v, (groups[g][1], v.shape[1]))
                                for g, v in enumerate(vals)], axis=0)

    a = ipre - bcum
    a_t = a.T
    m_rows = per_row([m0_of(g) for g in range(n_groups)])

    def prepare(h):
        p = {}
        a2 = jnp.where(causal, a_t[h:h + 1, :], -jnp.inf)
        p["mc"] = mc = jnp.maximum(jnp.max(a2, axis=1, keepdims=True), m_rows[:, h:h + 1])
        p["m_last"] = m_last = [mc[grp[2]:grp[2] + 1, :] for grp in groups]
        p["dm"] = jnp.exp(a2 - (mc + LN_INV_K_SCALE))
        p["w_inter"] = jnp.exp(m_rows[:, h:h + 1] - mc)
        p["w_col"] = w_col = jnp.exp(a[:, h:h + 1] - (per_row(m_last) + LN_INV_K_SCALE))
        p["decay"] = [jnp.exp(m0_of(g)[:, h:h + 1] - m_last[g]) for g in range(n_groups)]
        p["q"] = q = q_of(h)
        p["k"] = k = k_of(h)
        v = v_of(h)
        p["qb"], p["kb"], p["vb"] = q.astype(BF16), k.astype(BF16), v.astype(BF16)
        p["vw"] = (v * w_col).astype(BF16)
        p["c_old"] = [C0_of(g, h) for g in range(n_groups)]
        p["n_old"] = [n0_of(g, h) for g in range(n_groups)]
        return p

    def first_matmuls(p):
        p["qk"] = lax.dot_general(p["qb"], p["kb"], NT_DIMS, preferred_element_type=F32)
        p["qc"] = [lax.dot_general(rows_of(p["qb"], g), p["c_old"][g].astype(BF16), NT_DIMS,
                                   preferred_element_type=F32) for g in range(n_groups)]
        if single:
            n_rows = jnp.broadcast_to(p["n_old"][0], (LANES, ML_HEAD_DIM)).astype(BF16)
            p["qn"] = lax.dot_general(p["qb"], n_rows, NT_DIMS, preferred_element_type=F32)[:, 0:1]
        else:
            p["qn"] = jnp.sum(p["q"] * per_row(p["n_old"]), axis=1, keepdims=True)

    def second_matmuls(p):
        p["s"] = s = p["dm"] * p["qk"]
        p["sv"] = _bdot(s.astype(BF16), p["vb"])
        p["cupd"] = [lax.dot_general(rows_of(p["vw"], g), rows_of(p["kb"], g), TN_DIMS,
                                     preferred_element_type=F32) for g in range(n_groups)]

    def finish(h, p):
        qc_rows = p["qc"][0] if single else jnp.concatenate(p["qc"], axis=0)
        num = p["w_inter"] * qc_rows + p["sv"]
        den = p["w_inter"] * p["qn"] + jnp.sum(p["s"], axis=1, keepdims=True)
        hcur = num / jnp.maximum(jnp.abs(den), jnp.exp(-(bcum[:, h:h + 1] + p["mc"])))
        mu = jnp.mean(hcur, axis=1, keepdims=True)
        hc = hcur - mu
        var = jnp.mean(hc * hc, axis=1, keepdims=True)
        hs = slice(h * ML_HEAD_DIM, (h + 1) * ML_HEAD_DIM)
        out = jax.nn.sigmoid(o_of(h)) * (hc * lax.rsqrt(var + EPS) * hng_ref[:, hs])
        kw = p["k"] * p["w_col"]
        c_new = [p["decay"][g] * p["c_old"][g] + p["cupd"][g] for g in range(n_groups)]
        n_new = [p["decay"][g] * p["n_old"][g] + jnp.sum(rows_of(kw, g), axis=0, keepdims=True)
                 for g in range(n_groups)]
        return out, c_new, n_new

    H = ML_HEADS
    per_head = [None] * H
    done = [None] * H
    if single:
        per_head[0] = prepare(0)
        fill()
        first_matmuls(per_head[0])
        if H > 1:
            per_head[1] = prepare(1)
        fill()
        for h in range(H):
            second_matmuls(per_head[h])
            if h + 1 < H:
                first_matmuls(per_head[h + 1])
            fill()
            if h + 2 < H:
                per_head[h + 2] = prepare(h + 2)
            done[h] = finish(h, per_head[h])
            fill()
    else:
        per_head = [prepare(h) for h in range(H)]
        for stage in (first_matmuls, second_matmuls):
            for h in range(H):
                stage(per_head[h])
        done = [finish(h, per_head[h]) for h in range(H)]

    outs = [done[h][0] for h in range(H)]
    C_new = [[done[h][1][g] for h in range(H)] for g in range(n_groups)]
    n_new = [[done[h][2][g] for h in range(H)] for g in range(n_groups)]
    m_new = []
    for g, grp in enumerate(groups):
        row = m0_of(g)
        lane = lax.broadcasted_iota(jnp.int32, row.shape, 1)
        for h in range(H):
            row = jnp.where(lane == h, bcum[grp[2]:grp[2] + 1, h:h + 1] + per_head[h]["m_last"][g], row)
        m_new.append(row)
    return outs, C_new, n_new, m_new


def _mlstm_kernel(zq_ref, cst_ref, C0_ref, n0_ref, m0_ref, bif_ref, cw_ref, cb_ref, hng_ref,
                  bo_ref, conv_ref, C_ref, n_ref, m_ref, zp, xp, *, t_valid):
    nb = C0_ref.shape[0]
    L = BF16_ROWS
    R = nb * L

    @pl.when(pl.program_id(0) == 0)
    def _():
        zp[...] = jnp.zeros(zp.shape, F32)

    for bb in range(nb):
        zp[bb, 0:t_valid, :] = zq_ref[bb * t_valid:(bb + 1) * t_valid, :]
        for j in range(CONV_W - 1):
            row = SUBLANES - (CONV_W - 1) + j
            xp[bb, row:row + 1, :] = cst_ref[j, bb:bb + 1, :]
        xp[bb, SUBLANES:2 * SUBLANES, :] = zp[bb, 0:SUBLANES, 0:2 * ML_WIDTH]
        for j in range(CONV_W - 1):
            row = SUBLANES + t_valid - (CONV_W - 1) + j
            conv_ref[j, bb:bb + 1, :] = xp[bb, row:row + 1, :]
    qk_rows = []
    for bb in range(nb):
        acc = cb_ref[...]
        for j in range(CONV_W):
            off = SUBLANES - (CONV_W - 1) + j
            acc = acc + cw_ref[j:j + 1, :] * xp[bb, off:off + SUBLANES, :]
        qk_rows += [acc, jnp.zeros((L - SUBLANES, 2 * ML_WIDTH), F32)]
    qk = jnp.concatenate(qk_rows, axis=0)
    qk = qk * jax.nn.sigmoid(qk)

    def cols(c0, width):
        return zp[:, :, c0:c0 + width].reshape(R, width)

    zif = cols(4 * ML_WIDTH, GATE_COLS) + bif_ref[...]
    live = (lax.broadcasted_iota(jnp.int32, (R, GATE_COLS), 0) & (L - 1)) < t_valid
    ipre = jnp.where(live, zif, NEG_BIG)
    logf = jnp.where(live, pltpu.roll(_log_sigmoid(zif), GATE_COLS - ML_HEADS, axis=1), 0.0)
    r = lax.broadcasted_iota(jnp.int32, (R, R), 0)
    c = lax.broadcasted_iota(jnp.int32, (R, R), 1)
    sh = L.bit_length() - 1
    causal = jnp.logical_and(c <= r, (r >> sh) == (c >> sh))
    bcum = jnp.dot(causal.astype(F32), logf, preferred_element_type=F32,
                   precision=lax.Precision.HIGHEST)
    live_w = (lax.broadcasted_iota(jnp.int32, (R, ML_HEAD_DIM), 0) & (L - 1)) < t_valid

    def head_cols(x, base, h):
        return x[:, base + h * ML_HEAD_DIM:base + (h + 1) * ML_HEAD_DIM]

    groups = [(bb * L, L, bb * L + L - 1) for bb in range(nb)]
    outs, C_new, n_new, m_new = _mlstm_heads(
        q_of=lambda h: head_cols(qk, 0, h),
        k_of=lambda h: jnp.where(live_w, head_cols(qk, ML_WIDTH, h), 0.0),
        v_of=lambda h: jnp.where(live_w, cols(2 * ML_WIDTH + h * ML_HEAD_DIM, ML_HEAD_DIM), 0.0),
        o_of=lambda h: cols(3 * ML_WIDTH + h * ML_HEAD_DIM, ML_HEAD_DIM),
        causal=causal, ipre=ipre, bcum=bcum, groups=groups,
        m0_of=lambda g: m0_ref[g:g + 1, :], C0_of=lambda g, h: C0_ref[g, h],
        n0_of=lambda g, h: n0_ref[g, h:h + 1, :], hng_ref=hng_ref)
    for bb in range(nb):
        for h in range(ML_HEADS):
            hs = slice(h * ML_HEAD_DIM, (h + 1) * ML_HEAD_DIM)
            bo_ref[bb * t_valid:(bb + 1) * t_valid, hs] = outs[h][bb * L:bb * L + t_valid, :]
            C_ref[bb, h] = C_new[bb][h]
            n_ref[bb, h:h + 1, :] = n_new[bb][h]
        m_ref[bb:bb + 1, :] = m_new[bb]


def _mlstm_call(zq, cst, state, bif, cw, cb, hng, *, t_valid, nb):
    T = t_valid
    B = zq.shape[0] // T
    kern = functools.partial(_mlstm_kernel, t_valid=T)
    per_b = lambda shape: pl.BlockSpec((nb,) + shape, lambda b: (b,) + (0,) * len(shape))
    rows = lambda width: pl.BlockSpec((nb * T, width), lambda b: (b, 0))
    conv_spec = pl.BlockSpec((CONV_W - 1, nb, 2 * ML_WIDTH), lambda b: (0, b, 0))
    st_specs = [per_b((ML_HEADS, ML_HEAD_DIM, ML_HEAD_DIM)), per_b((ML_HEADS, ML_HEAD_DIM)),
                per_b((ML_HEADS,))]
    in_specs = ([rows(ML_COLS), conv_spec] + st_specs
                + [_const_spec((1, GATE_COLS)), _const_spec((CONV_W, 2 * ML_WIDTH)),
                   _const_spec((1, 2 * ML_WIDTH)), _const_spec((1, ML_WIDTH))])
    out_shape = [jax.ShapeDtypeStruct((B * T, ML_WIDTH), F32),
                 jax.ShapeDtypeStruct((CONV_W - 1, B, 2 * ML_WIDTH), F32),
                 jax.ShapeDtypeStruct((B, ML_HEADS, ML_HEAD_DIM, ML_HEAD_DIM), F32),
                 jax.ShapeDtypeStruct((B, ML_HEADS, ML_HEAD_DIM), F32),
                 jax.ShapeDtypeStruct((B, ML_HEADS), F32)]
    out_specs = [rows(ML_WIDTH), conv_spec] + st_specs
    return pl.pallas_call(
        kern,
        grid=(B // nb,),
        in_specs=in_specs,
        out_specs=out_specs,
        out_shape=out_shape,
        scratch_shapes=[pltpu.VMEM((nb, BF16_ROWS, ML_COLS), F32),
                        pltpu.VMEM((nb, 2 * SUBLANES, 2 * ML_WIDTH), F32)],
        compiler_params=pltpu.CompilerParams(dimension_semantics=("arbitrary",),
                                             vmem_limit_bytes=VMEM_LIMIT),
        name="mlstm",
    )(zq, cst, *state, bif, cw, cb, hng)


PL = 256
PG = PL // SUBLANES
TAIL = (CONV_W - 1) * SUBLANES
PROJ_PIECE_COLS = 512


def _perm_rows(j):
    return pl.ds((PL // 4) * (j % 4) + j // 4, SUBLANES, stride=SUBLANES)


def _mixer_ml_kernel(h_ref, wqk_ref, wvo_ref, wif_ref, bif_ref, cw_ref, cb_ref, hng_ref,
                     bo_ref, conv_ref, C_ref, n_ref, m_ref, zqk_buf, zvo_buf, zif_buf, us, tail,
                     *, chunks_per_seq):
    s = pl.program_id(0)

    @pl.when(s == 0)
    def _():
        zqk_buf[1] = jnp.zeros(zqk_buf.shape[1:], F32)
        zvo_buf[1] = jnp.zeros(zvo_buf.shape[1:], F32)
        zif_buf[1] = jnp.zeros(zif_buf.shape[1:], F32)

    @pl.when(jnp.maximum(s - 1, 0) % chunks_per_seq == 0)
    def _():
        tail[...] = jnp.zeros(tail.shape, F32)
        C_ref[...] = jnp.zeros(C_ref.shape, F32)
        n_ref[...] = jnp.zeros(n_ref.shape, F32)
        m_ref[...] = jnp.zeros(m_ref.shape, F32)

    step = functools.partial(_mixer_ml_step, h_ref, wqk_ref, wvo_ref, wif_ref, bif_ref, cw_ref,
                             cb_ref, hng_ref, bo_ref, conv_ref, C_ref, n_ref, m_ref,
                             zqk_buf, zvo_buf, zif_buf, us, tail)

    @pl.when(s % 2 == 0)
    def _():
        step(rd=1, wr=0)

    @pl.when(s % 2 == 1)
    def _():
        step(rd=0, wr=1)


def _mixer_ml_step(h_ref, wqk_ref, wvo_ref, wif_ref, bif_ref, cw_ref, cb_ref, hng_ref,
                   bo_ref, conv_ref, C_ref, n_ref, m_ref, zqk_buf, zvo_buf, zif_buf, us, tail,
                   *, rd, wr):
    h = h_ref[...]

    def piece(w_ref, buf, c0, width):
        def emit():
            buf[wr, :, c0:c0 + width] = _bdot(h, w_ref[:, c0:c0 + width])
        return emit

    pieces = [piece(w_ref, buf, c0, PROJ_PIECE_COLS)
              for w_ref, buf in ((wqk_ref, zqk_buf), (wvo_ref, zvo_buf))
              for c0 in range(0, 2 * ML_WIDTH, PROJ_PIECE_COLS)]
    pieces.append(piece(wif_ref, zif_buf, 0, GATE_COLS))
    pieces = iter(pieces)

    def fill():
        emit = next(pieces, None)
        if emit is not None:
            emit()

    fill()
    zqk = zqk_buf.at[rd]
    zvo = zvo_buf.at[rd]
    zif = zif_buf[rd] + bif_ref[...]

    zqk_tail = zqk[PL - TAIL:, :]
    sub = lax.broadcasted_iota(jnp.int32, (SUBLANES, 2 * ML_WIDTH), 0)
    wrapped = []
    for g in range(CONV_W - 1):
        cur = pltpu.roll(zqk_tail[g * SUBLANES:(g + 1) * SUBLANES], 1, axis=0)
        prev = pltpu.roll(tail[g * SUBLANES:(g + 1) * SUBLANES, :], 1, axis=0)
        wrapped.append(jnp.where(sub == 0, prev, cur))
    wrapped = jnp.concatenate(wrapped, axis=0)
    tail[...] = zqk_tail
    conv_ref[...] = jnp.concatenate(
        [zqk_tail[g * SUBLANES + SUBLANES - 1:(g + 1) * SUBLANES, :] for g in range(CONV_W - 1)], axis=0)

    def conv_silu(c0, width):
        cs = slice(c0, c0 + width)
        acc = cb_ref[:, cs] + cw_ref[CONV_W - 1:CONV_W, cs] * zqk[:, cs]
        for d in range(1, CONV_W):
            shifted = jnp.concatenate(
                [wrapped[TAIL - d * SUBLANES:, cs], zqk[:PL - d * SUBLANES, cs]], axis=0)
            acc = acc + cw_ref[CONV_W - 1 - d:CONV_W - d, cs] * shifted
        return acc * jax.nn.sigmoid(acc)

    pr = lax.broadcasted_iota(jnp.int32, (PL, PL), 0)
    pc = lax.broadcasted_iota(jnp.int32, (PL, PL), 1)
    causal = ((pc >> 3) + PG * (pc & 7)) <= ((pr >> 3) + PG * (pr & 7))

    logf = pltpu.roll(_log_sigmoid(zif), GATE_COLS - ML_HEADS, axis=1)
    run, partial = None, []
    for n in range(PG):
        blk = logf[n * SUBLANES:(n + 1) * SUBLANES, :]
        run = blk if run is None else run + blk
        partial.append(run)
    sub_g = lax.broadcasted_iota(jnp.int32, (SUBLANES, GATE_COLS), 0)
    incl = run
    for step in (1, 2, 4):
        incl = incl + jnp.where(sub_g >= step, pltpu.roll(incl, step, axis=0), 0.0)
    earlier = incl - run
    bcum = jnp.concatenate([p + earlier for p in partial], axis=0)

    def head_cols(base, h):
        return zvo[:, base + h * ML_HEAD_DIM:base + (h + 1) * ML_HEAD_DIM]

    outs, C_new, n_new, m_new = _mlstm_heads(
        q_of=lambda h: conv_silu(h * ML_HEAD_DIM, ML_HEAD_DIM),
        k_of=lambda h: conv_silu(ML_WIDTH + h * ML_HEAD_DIM, ML_HEAD_DIM),
        v_of=lambda h: head_cols(0, h),
        o_of=lambda h: head_cols(ML_WIDTH, h),
        causal=causal, ipre=zif, bcum=bcum, groups=[(0, PL, PL - 1)],
        m0_of=lambda g: m_ref[...], C0_of=lambda g, h: C_ref[h], n0_of=lambda g, h: n_ref[h:h + 1, :],
        hng_ref=hng_ref, fill=fill)
    while next(pieces, None) is not None:
        raise AssertionError("projection pieces left over")
    cph = ML_HEAD_DIM // LANES
    for h in range(ML_HEADS):
        for cc in range(cph):
            us[h * cph + cc] = outs[h][:, cc * LANES:(cc + 1) * LANES]
        C_ref[h] = C_new[0][h]
        n_ref[h:h + 1, :] = n_new[0][h]
    m_ref[...] = m_new[0]
    for c in range(ML_WIDTH // LANES):
        for j in range(0, PG, 2):
            pair = jnp.concatenate([us[c, _perm_rows(j), :], us[c, _perm_rows(j + 1), :]], axis=0)
            bo_ref[j * SUBLANES:(j + 2) * SUBLANES, c * LANES:(c + 1) * LANES] = pair.astype(BF16)


def _mixer_ml_call(hp, w_all, bif, cw, cb, hng):
    B, T, _ = hp.shape
    cps = T // PL
    n_chunks = B * cps
    proj = lambda s: jnp.minimum(s, n_chunks - 1)
    math = lambda s: jnp.maximum(s - 1, 0)
    per_b = lambda shape: pl.BlockSpec((None,) + shape,
                                       lambda s: (math(s) // cps,) + (0,) * len(shape))
    out_shape = [jax.ShapeDtypeStruct((B, T, ML_WIDTH), BF16),
                 jax.ShapeDtypeStruct((B, CONV_W - 1, 2 * ML_WIDTH), F32),
                 jax.ShapeDtypeStruct((B, ML_HEADS, ML_HEAD_DIM, ML_HEAD_DIM), F32),
                 jax.ShapeDtypeStruct((B, ML_HEADS, ML_HEAD_DIM), F32),
                 jax.ShapeDtypeStruct((B, 1, ML_HEADS), F32)]
    out_specs = [pl.BlockSpec((None, PL, ML_WIDTH), lambda s: (math(s) // cps, math(s) % cps, 0)),
                 per_b((CONV_W - 1, 2 * ML_WIDTH)),
                 per_b((ML_HEADS, ML_HEAD_DIM, ML_HEAD_DIM)), per_b((ML_HEADS, ML_HEAD_DIM)),
                 per_b((1, ML_HEADS))]
    return pl.pallas_call(
        functools.partial(_mixer_ml_kernel, chunks_per_seq=cps),
        grid=(n_chunks + 1,),
        in_specs=[pl.BlockSpec((None, PL, D_MODEL), lambda s: (proj(s) // cps, proj(s) % cps, 0))]
        + _ml_weight_specs()
        + [_const_spec((1, GATE_COLS)), _const_spec((CONV_W, 2 * ML_WIDTH)),
           _const_spec((1, 2 * ML_WIDTH)), _const_spec((1, ML_WIDTH))],
        out_specs=out_specs,
        out_shape=out_shape,
        scratch_shapes=[pltpu.VMEM((2, PL, 2 * ML_WIDTH), F32),
                        pltpu.VMEM((2, PL, 2 * ML_WIDTH), F32),
                        pltpu.VMEM((2, PL, GATE_COLS), F32),
                        pltpu.VMEM((ML_WIDTH // LANES, PL, LANES), F32),
                        pltpu.VMEM((TAIL, 2 * ML_WIDTH), F32)],
        compiler_params=pltpu.CompilerParams(dimension_semantics=("arbitrary",),
                                             vmem_limit_bytes=VMEM_LIMIT),
        name="mixer_ml",
    )(hp, w_all, w_all, w_all, bif, cw, cb, hng)


def _merge_ffn_kernel(x_ref, pa_ref, bo_ref, g1_ref, wg_ref, bg_ref, wpb_ref, wout_ref,
                      g2_ref, wfi_ref, wfo_ref, gf_ref, y_ref):
    x = x_ref[...]
    h = _rms(x, g1_ref[...]).astype(BF16)
    gab = _bdot(h, wg_ref[...]) + bg_ref[...]
    pb = _bdot(bo_ref[...].astype(BF16), wpb_ref[...])
    merged = (jax.nn.sigmoid(gab[:, :D_MODEL]) * pa_ref[...]
              + jax.nn.sigmoid(gab[:, D_MODEL:]) * pb)
    x1 = x + _bdot(merged.astype(BF16), wout_ref[...])
    h2 = _rms(x1, g2_ref[...]).astype(BF16)
    gu = _bdot(h2, wfi_ref[...])
    gt = gu[:, :D_FF]
    hid = (gt * jax.nn.sigmoid(gt) * gu[:, D_FF:]).astype(BF16)
    x2 = x1 + _bdot(hid, wfo_ref[...])
    y_ref[...] = _rms(x2, gf_ref[...])


def _merge_ffn_call(x, pa, bo, g1, wg, bg, wpb, wout, g2, wfi, wfo, gf, *, tm):
    m = x.shape[0]
    row = pl.BlockSpec((tm, D_MODEL), lambda i: (i, 0))
    return pl.pallas_call(
        _merge_ffn_kernel,
        grid=(m // tm,),
        in_specs=[row, row, row, _const_spec((1, D_MODEL)), _const_spec((D_MODEL, 2 * D_MODEL)),
                  _const_spec((1, 2 * D_MODEL)), _const_spec((ML_WIDTH, D_MODEL)),
                  _const_spec((D_MODEL, D_MODEL)), _const_spec((1, D_MODEL)),
                  _const_spec((D_MODEL, 2 * D_FF)), _const_spec((D_FF, D_MODEL)),
                  _const_spec((1, D_MODEL))],
        out_specs=row,
        out_shape=jax.ShapeDtypeStruct((m, D_MODEL), F32),
        compiler_params=pltpu.CompilerParams(dimension_semantics=("arbitrary",),
                                             vmem_limit_bytes=VMEM_LIMIT),
        name="merge_ffn",
    )(x, pa, bo, g1, wg, bg, wpb, wout, g2, wfi, wfo, gf)


SAMPLE_SEQS_PER_STEP = 8
TM_MIX = 512
TM_FFN = 256


def _spatial_tiles(w_s, b_s, chunk):
    if chunk == GM_CHUNK:
        ws_t, b_pos = w_s[:, :chunk, :chunk], b_s[:, :chunk].T
    else:
        onehot = jnp.asarray(np.arange(GM_CHUNK)[:, None] % chunk == np.arange(chunk)[None, :], F32)
        hp = lax.Precision.HIGHEST
        ws_t = jnp.einsum("ri,gij,cj->grc", onehot, w_s[:, :chunk, :chunk], onehot, precision=hp)
        b_pos = jnp.dot(onehot, b_s[:, :chunk].T, precision=hp)
    bs_t = jnp.repeat(b_pos, GM_GROUP_W, axis=1)
    return ws_t, bs_t


def _gmlp_branch(xf, w, chunk, *, emit_v, emit_hperm):
    ws_t, bs_t = _spatial_tiles(w["w_s"], w["b_s"], chunk)
    return _gmlp_call(xf, w["g1"], w["w_all"], w["lng"], w["lnb"], ws_t, bs_t, w["wpa"],
                      chunk=chunk, emit_v=emit_v, emit_hperm=emit_hperm, tm=TM_MIX)


def _merge_branch(xf, pa, bo, w):
    return _merge_ffn_call(xf, pa, bo, w["g1"], w["wg"], w["bg"], w["wpb"], w["wout"], w["g2"],
                           w["wfi"], w["wfo"], w["gf"], tm=TM_FFN)


def kernel(x_prompt, x_sample, state_conv, state_C, state_n, state_m, g_norm1, w_in, b_i, b_f, ln_g, ln_b, w_s, b_s, conv_w, conv_b, hn_g, b_gate, w_proj_a, w_proj_b, w_out, g_norm2, w_ffn_in, w_ffn_out, g_final):
    Bp, Tp, _ = x_prompt.shape
    Bs, Ts, _ = x_sample.shape
    win = w_in[0]
    c_if = 2 * GM_WIDTH + 4 * ML_WIDTH + 2 * ML_HEADS
    w = dict(
        g1=g_norm1[0][None], g2=g_norm2[0][None], gf=g_final[None],
        w_all=win.astype(BF16),
        wg=win[:, c_if:].astype(BF16),
        bg=b_gate[0].reshape(1, 2 * D_MODEL),
        lng=ln_g[0][None], lnb=ln_b[0][None], w_s=w_s[0], b_s=b_s[0],
        bif=jnp.pad(jnp.concatenate([b_i[0], b_f[0]]), (0, GATE_COLS - 2 * ML_HEADS))[None],
        cw=conv_w[0], cb=conv_b[0][None], hng=hn_g[0][None],
        wpa=w_proj_a[0].astype(BF16), wpb=w_proj_b[0].astype(BF16), wout=w_out[0].astype(BF16),
        wfi=w_ffn_in[0].astype(BF16), wfo=w_ffn_out[0].astype(BF16),
    )

    xpf = x_prompt.reshape(Bp * Tp, D_MODEL)
    pa_p, hp_p = _gmlp_branch(xpf, w, GM_CHUNK, emit_v=False, emit_hperm=True)
    bo_p, conv_p, C_p, n_p, m_p = _mixer_ml_call(hp_p.reshape(Bp, Tp, D_MODEL), w["w_all"], w["bif"],
                                                 w["cw"], w["cb"], w["hng"])
    y_p = _merge_branch(xpf, pa_p, bo_p.reshape(Bp * Tp, ML_WIDTH), w)

    xsf = x_sample.reshape(Bs * Ts, D_MODEL)
    pa_s, vn_s = _gmlp_branch(xsf, w, Ts, emit_v=True, emit_hperm=False)
    zq = _inproj_call(xsf, w["g1"], w["w_all"], tm=TM_MIX)
    st = (state_C[0], state_n[0], state_m[0])
    bo_s, conv_s, C_s, n_s, m_s = _mlstm_call(zq, jnp.transpose(state_conv[0], (1, 0, 2)), st, w["bif"],
                                              w["cw"], w["cb"], w["hng"], t_valid=Ts,
                                              nb=SAMPLE_SEQS_PER_STEP)
    y_s = _merge_branch(xsf, pa_s, bo_s, w)

    return (y_p.reshape(Bp, Tp, D_MODEL), y_s.reshape(Bs, Ts, D_MODEL),
            conv_p[None], C_p[None], n_p[None], m_p.reshape(1, Bp, ML_HEADS),
            jnp.transpose(conv_s, (1, 0, 2))[None], C_s[None], n_s[None], m_s[None],
            vn_s.reshape(1, Bs, Ts, GM_WIDTH))
```

```python
import functools
import math

import jax
import jax.numpy as jnp
import numpy as np
from jax import lax
from jax.experimental import pallas as pl
from jax.experimental.pallas import tpu as pltpu

D_MODEL = 1024
GM_WIDTH = D_MODEL
GM_GROUPS = 4
GM_GROUP_W = GM_WIDTH // GM_GROUPS
GM_CHUNK = 128
ML_HEADS = 4
ML_HEAD_DIM = D_MODEL // ML_HEADS
ML_WIDTH = ML_HEADS * ML_HEAD_DIM
CONV_W = 4
D_FF = 2816
EPS = 1e-6

LANES = 128
SUBLANES = 8
BF16_ROWS = 16
GATE_COLS = LANES
ML_COLS = 4 * ML_WIDTH + GATE_COLS
VMEM_LIMIT = 56 * 1024 * 1024
GMLP_SUB_ROWS = 256
FFN_SUB_ROWS = 256

F32 = jnp.float32
BF16 = jnp.bfloat16
NEG_BIG = -1e30
LN_INV_K_SCALE = 0.5 * math.log(ML_HEAD_DIM)
NT_DIMS = (((1,), (1,)), ((), ()))
TN_DIMS = (((0,), (0,)), ((), ()))


def _rms(x, g):
    return x * lax.rsqrt(jnp.mean(x * x, axis=-1, keepdims=True) + EPS) * g


def _gelu(x):
    return 0.5 * x * (1.0 + lax.erf(x * (2.0 ** -0.5)))


def _log_sigmoid(x):
    return jnp.minimum(x, 0.0) - jnp.log1p(jnp.exp(-jnp.abs(x)))


def _bdot(a, b):
    return jnp.dot(a, b, preferred_element_type=F32)


def _const_spec(shape):
    nd = len(shape)
    return pl.BlockSpec(shape, lambda *_: (0,) * nd, pipeline_mode=pl.Buffered(1))


def _gmlp_kernel(x_ref, g1_ref, wuv_ref, lng_ref, lnb_ref, ws_ref, bs_ref, wpa_ref, *rest,
                 chunk, emit_v, emit_hperm):
    rest = list(rest)
    pa_ref = rest.pop(0)
    vn_ref = rest.pop(0) if emit_v else None
    hp_ref = rest.pop(0) if emit_hperm else None
    a_sc = rest.pop(0)
    tm = x_ref.shape[0]
    blk = ws_ref.shape[1]
    hs = rest.pop(0) if emit_hperm else None
    sub = GMLP_SUB_ROWS
    subs = [slice(r0, r0 + sub) for r0 in range(0, tm, sub)]
    hb = []
    for rs in subs:
        hf = _rms(x_ref[rs, :], g1_ref[...])
        hb.append(hf.astype(BF16))
        if emit_hperm:
            for j in range(rs.start // SUBLANES, rs.stop // SUBLANES):
                base = (j // PG) * PL
                dst = _perm_rows(j % PG)
                lo = j * SUBLANES - rs.start
                for c in range(D_MODEL // LANES):
                    hs[c, pl.ds(base + dst.start, SUBLANES, stride=SUBLANES), :] = (
                        hf[lo:lo + SUBLANES, c * LANES:(c + 1) * LANES])
            for c in range(D_MODEL // LANES):
                hp_ref[rs, c * LANES:(c + 1) * LANES] = hs[c, rs, :].astype(BF16)
    zu = [_bdot(h, wuv_ref[:, :GM_WIDTH]) for h in hb]
    zv = [_bdot(h, wuv_ref[:, GM_WIDTH:]) for h in hb]
    r = lax.broadcasted_iota(jnp.int32, (blk, blk), 0)
    c = lax.broadcasted_iota(jnp.int32, (blk, blk), 1)
    keep = c <= r
    if chunk < blk:
        sh = chunk.bit_length() - 1
        keep = jnp.logical_and(keep, (r >> sh) == (c >> sh))
    wsm = [jnp.where(keep, ws_ref[g], 0.0).astype(BF16) for g in range(GM_GROUPS)]
    for si, rs in enumerate(subs):
        u = _gelu(zu[si])
        v = _gelu(zv[si])
        mu = jnp.mean(v, axis=-1, keepdims=True)
        vc = v - mu
        var = jnp.mean(vc * vc, axis=-1, keepdims=True)
        vn = vc * lax.rsqrt(var + EPS) * lng_ref[...] + lnb_ref[...]
        if emit_v:
            vn_ref[rs, :] = vn
        vb = vn.astype(BF16)
        for g in range(GM_GROUPS):
            cs = slice(g * GM_GROUP_W, (g + 1) * GM_GROUP_W)
            for i in range(sub // blk):
                ls = slice(i * blk, (i + 1) * blk)
                s = _bdot(wsm[g], vb[ls, cs]) + bs_ref[:, cs]
                a_sc[rs.start + i * blk:rs.start + (i + 1) * blk, cs] = (u[ls, cs] * s).astype(BF16)
        pa_ref[rs, :] = _bdot(a_sc[rs, :], wpa_ref[...])


def _gmlp_call(x, g1, w_all, lng, lnb, ws_t, bs_t, wpa, *, chunk, emit_v, emit_hperm, tm):
    m = x.shape[0]
    blk = ws_t.shape[1]
    row = pl.BlockSpec((tm, D_MODEL), lambda i: (i, 0))
    out_shape = [jax.ShapeDtypeStruct((m, D_MODEL), F32)]
    out_specs = [row]
    scratch = [pltpu.VMEM((tm, GM_WIDTH), BF16)]
    if emit_v:
        out_shape.append(jax.ShapeDtypeStruct((m, GM_WIDTH), F32))
        out_specs.append(row)
    if emit_hperm:
        assert tm % PL == 0 and GMLP_SUB_ROWS == PL
        out_shape.append(jax.ShapeDtypeStruct((m, D_MODEL), BF16))
        out_specs.append(row)
        scratch.append(pltpu.VMEM((D_MODEL // LANES, tm, LANES), F32))
    return pl.pallas_call(
        functools.partial(_gmlp_kernel, chunk=chunk, emit_v=emit_v, emit_hperm=emit_hperm),
        grid=(m // tm,),
        in_specs=[row, _const_spec((1, D_MODEL)),
                  pl.BlockSpec((D_MODEL, 2 * GM_WIDTH), lambda i: (0, 0), pipeline_mode=pl.Buffered(1)),
                  _const_spec((1, GM_WIDTH)), _const_spec((1, GM_WIDTH)),
                  _const_spec((GM_GROUPS, blk, blk)), _const_spec((blk, GM_WIDTH)),
                  _const_spec((GM_WIDTH, D_MODEL))],
        out_specs=out_specs,
        out_shape=out_shape,
        scratch_shapes=scratch,
        compiler_params=pltpu.CompilerParams(dimension_semantics=("arbitrary",),
                                             vmem_limit_bytes=VMEM_LIMIT),
        name="gmlp",
    )(x, g1, w_all, lng, lnb, ws_t, bs_t, wpa)


def _ml_weight_specs():
    wide = 2 * ML_WIDTH
    assert (2 * GM_WIDTH) % wide == 0 and (2 * GM_WIDTH + 4 * ML_WIDTH) % GATE_COLS == 0
    first = 2 * GM_WIDTH // wide
    col_block = lambda width, idx: pl.BlockSpec((D_MODEL, width), lambda *_: (0, idx),
                                                pipeline_mode=pl.Buffered(1))
    return [col_block(wide, first), col_block(wide, first + 1),
            col_block(GATE_COLS, (2 * GM_WIDTH + 4 * ML_WIDTH) // GATE_COLS)]


def _inproj_kernel(x_ref, g1_ref, wqk_ref, wvo_ref, wif_ref, z_ref):
    h = _rms(x_ref[...], g1_ref[...]).astype(BF16)
    z_ref[:, 0:2 * ML_WIDTH] = _bdot(h, wqk_ref[...])
    z_ref[:, 2 * ML_WIDTH:4 * ML_WIDTH] = _bdot(h, wvo_ref[...])
    z_ref[:, 4 * ML_WIDTH:ML_COLS] = _bdot(h, wif_ref[...])


def _inproj_call(x, g1, w_all, *, tm):
    m = x.shape[0]
    return pl.pallas_call(
        _inproj_kernel,
        grid=(m // tm,),
        in_specs=[pl.BlockSpec((tm, D_MODEL), lambda i: (i, 0)), _const_spec((1, D_MODEL))]
        + _ml_weight_specs(),
        out_specs=pl.BlockSpec((tm, ML_COLS), lambda i: (i, 0)),
        out_shape=jax.ShapeDtypeStruct((m, ML_COLS), F32),
        compiler_params=pltpu.CompilerParams(dimension_semantics=("arbitrary",),
                                             vmem_limit_bytes=VMEM_LIMIT),
        name="inproj",
    )(x, g1, w_all, w_all, w_all)


def _mlstm_heads(q_of, k_of, v_of, o_of, causal, ipre, bcum, groups, m0_of, C0_of, n0_of, hng_ref,
                 fill=None):
    n_groups = len(groups)
    single = n_groups == 1
    fill = fill or (lambda: None)

    def rows_of(x, g):
        return x if single else x[groups[g][0]:groups[g][0] + groups[g][1]]

    def per_row(vals):
        if single:
            return vals[0]
        return jnp.concatenate([jnp.broadcast_to(v, (groups[g][1], v.shape[1]))
                                for g, v in enumerate(vals)], axis=0)

    a = ipre - bcum
    a_t = a.T
    m_rows = per_row([m0_of(g) for g in range(n_groups)])

    def prepare(h):
        p = {}
        a2 = jnp.where(causal, a_t[h:h + 1, :], -jnp.inf)
        p["mc"] = mc = jnp.maximum(jnp.max(a2, axis=1, keepdims=True), m_rows[:, h:h + 1])
        p["m_last"] = m_last = [mc[grp[2]:grp[2] + 1, :] for grp in groups]
        p["dm"] = jnp.exp(a2 - (mc + LN_INV_K_SCALE))
        p["w_inter"] = jnp.exp(m_rows[:, h:h + 1] - mc)
        p["w_col"] = w_col = jnp.exp(a[:, h:h + 1] - (per_row(m_last) + LN_INV_K_SCALE))
        p["decay"] = [jnp.exp(m0_of(g)[:, h:h + 1] - m_last[g]) for g in range(n_groups)]
        p["q"] = q = q_of(h)
        p["k"] = k = k_of(h)
        v = v_of(h)
        p["qb"], p["kb"], p["vb"] = q.astype(BF16), k.astype(BF16), v.astype(BF16)
        p["vw"] = (v * w_col).astype(BF16)
        p["c_old"] = [C0_of(g, h) for g in range(n_groups)]
        p["n_old"] = [n0_of(g, h) for g in range(n_groups)]
        return p

    def first_matmuls(p):
        p["qk"] = lax.dot_general(p["qb"], p["kb"], NT_DIMS, preferred_element_type=F32)
        p["qc"] = [lax.dot_general(rows_of(p["qb"], g), p["c_old"][g].astype(BF16), NT_DIMS,
                                   preferred_element_type=F32) for g in range(n_groups)]
        if single:
            n_rows = jnp.broadcast_to(p["n_old"][0], (LANES, ML_HEAD_DIM)).astype(BF16)
            p["qn"] = lax.dot_general(p["qb"], n_rows, NT_DIMS, preferred_element_type=F32)[:, 0:1]
        else:
            p["qn"] = jnp.sum(p["q"] * per_row(p["n_old"]), axis=1, keepdims=True)

    def second_matmuls(p):
        p["s"] = s = p["dm"] * p["qk"]
        p["sv"] = _bdot(s.astype(BF16), p["vb"])
        p["cupd"] = [lax.dot_general(rows_of(p["vw"], g), rows_of(p["kb"], g), TN_DIMS,
                                     preferred_element_type=F32) for g in range(n_groups)]

    def finish(h, p):
        qc_rows = p["qc"][0] if single else jnp.concatenate(p["qc"], axis=0)
        num = p["w_inter"] * qc_rows + p["sv"]
        den = p["w_inter"] * p["qn"] + jnp.sum(p["s"], axis=1, keepdims=True)
        hcur = num / jnp.maximum(jnp.abs(den), jnp.exp(-(bcum[:, h:h + 1] + p["mc"])))
        mu = jnp.mean(hcur, axis=1, keepdims=True)
        hc = hcur - mu
        var = jnp.mean(hc * hc, axis=1, keepdims=True)
        hs = slice(h * ML_HEAD_DIM, (h + 1) * ML_HEAD_DIM)
        out = jax.nn.sigmoid(o_of(h)) * (hc * lax.rsqrt(var + EPS) * hng_ref[:, hs])
        kw = p["k"] * p["w_col"]
        c_new = [p["decay"][g] * p["c_old"][g] + p["cupd"][g] for g in range(n_groups)]
        n_new = [p["decay"][g] * p["n_old"][g] + jnp.sum(rows_of(kw, g), axis=0, keepdims=True)
                 for g in range(n_groups)]
        return out, c_new, n_new

    H = ML_HEADS
    per_head = [None] * H
    done = [None] * H
    if single:
        per_head[0] = prepare(0)
        fill()
        first_matmuls(per_head[0])
        if H > 1:
            per_head[1] = prepare(1)
        fill()
        for h in range(H):
            second_matmuls(per_head[h])
            if h + 1 < H:
                first_matmuls(per_head[h + 1])
            fill()
            if h + 2 < H:
                per_head[h + 2] = prepare(h + 2)
            done[h] = finish(h, per_head[h])
            fill()
    else:
        per_head = [prepare(h) for h in range(H)]
        for stage in (first_matmuls, second_matmuls):
            for h in range(H):
                stage(per_head[h])
        done = [finish(h, per_head[h]) for h in range(H)]

    outs = [done[h][0] for h in range(H)]
    C_new = [[done[h][1][g] for h in range(H)] for g in range(n_groups)]
    n_new = [[done[h][2][g] for h in range(H)] for g in range(n_groups)]
    m_new = []
    for g, grp in enumerate(groups):
        row = m0_of(g)
        lane = lax.broadcasted_iota(jnp.int32, row.shape, 1)
        for h in range(H):
            row = jnp.where(lane == h, bcum[grp[2]:grp[2] + 1, h:h + 1] + per_head[h]["m_last"][g], row)
        m_new.append(row)
    return outs, C_new, n_new, m_new


def _mlstm_kernel(zq_ref, cst_ref, C0_ref, n0_ref, m0_ref, bif_ref, cw_ref, cb_ref, hng_ref,
                  bo_ref, conv_ref, C_ref, n_ref, m_ref, zp, xp, *, t_valid):
    nb = C0_ref.shape[0]
    L = BF16_ROWS
    R = nb * L

    @pl.when(pl.program_id(0) == 0)
    def _():
        zp[...] = jnp.zeros(zp.shape, F32)

    for bb in range(nb):
        zp[bb, 0:t_valid, :] = zq_ref[bb * t_valid:(bb + 1) * t_valid, :]
        for j in range(CONV_W - 1):
            row = SUBLANES - (CONV_W - 1) + j
            xp[bb, row:row + 1, :] = cst_ref[j, bb:bb + 1, :]
        xp[bb, SUBLANES:2 * SUBLANES, :] = zp[bb, 0:SUBLANES, 0:2 * ML_WIDTH]
        for j in range(CONV_W - 1):
            row = SUBLANES + t_valid - (CONV_W - 1) + j
            conv_ref[j, bb:bb + 1, :] = xp[bb, row:row + 1, :]
    qk_rows = []
    for bb in range(nb):
        acc = cb_ref[...]
        for j in range(CONV_W):
            off = SUBLANES - (CONV_W - 1) + j
            acc = acc + cw_ref[j:j + 1, :] * xp[bb, off:off + SUBLANES, :]
        qk_rows += [acc, jnp.zeros((L - SUBLANES, 2 * ML_WIDTH), F32)]
    qk = jnp.concatenate(qk_rows, axis=0)
    qk = qk * jax.nn.sigmoid(qk)

    def cols(c0, width):
        return zp[:, :, c0:c0 + width].reshape(R, width)

    zif = cols(4 * ML_WIDTH, GATE_COLS) + bif_ref[...]
    live = (lax.broadcasted_iota(jnp.int32, (R, GATE_COLS), 0) & (L - 1)) < t_valid
    ipre = jnp.where(live, zif, NEG_BIG)
    logf = jnp.where(live, pltpu.roll(_log_sigmoid(zif), GATE_COLS - ML_HEADS, axis=1), 0.0)
    r = lax.broadcasted_iota(jnp.int32, (R, R), 0)
    c = lax.broadcasted_iota(jnp.int32, (R, R), 1)
    sh = L.bit_length() - 1
    causal = jnp.logical_and(c <= r, (r >> sh) == (c >> sh))
    bcum = jnp.dot(causal.astype(F32), logf, preferred_element_type=F32,
                   precision=lax.Precision.HIGHEST)
    live_w = (lax.broadcasted_iota(jnp.int32, (R, ML_HEAD_DIM), 0) & (L - 1)) < t_valid

    def head_cols(x, base, h):
        return x[:, base + h * ML_HEAD_DIM:base + (h + 1) * ML_HEAD_DIM]

    groups = [(bb * L, L, bb * L + L - 1) for bb in range(nb)]
    outs, C_new, n_new, m_new = _mlstm_heads(
        q_of=lambda h: head_cols(qk, 0, h),
        k_of=lambda h: jnp.where(live_w, head_cols(qk, ML_WIDTH, h), 0.0),
        v_of=lambda h: jnp.where(live_w, cols(2 * ML_WIDTH + h * ML_HEAD_DIM, ML_HEAD_DIM), 0.0),
        o_of=lambda h: cols(3 * ML_WIDTH + h * ML_HEAD_DIM, ML_HEAD_DIM),
        causal=causal, ipre=ipre, bcum=bcum, groups=groups,
        m0_of=lambda g: m0_ref[g:g + 1, :], C0_of=lambda g, h: C0_ref[g, h],
        n0_of=lambda g, h: n0_ref[g, h:h + 1, :], hng_ref=hng_ref)
    for bb in range(nb):
        for h in range(ML_HEADS):
            hs = slice(h * ML_HEAD_DIM, (h + 1) * ML_HEAD_DIM)
            bo_ref[bb * t_valid:(bb + 1) * t_valid, hs] = outs[h][bb * L:bb * L + t_valid, :]
            C_ref[bb, h] = C_new[bb][h]
            n_ref[bb, h:h + 1, :] = n_new[bb][h]
        m_ref[bb:bb + 1, :] = m_new[bb]


def _mlstm_call(zq, cst, state, bif, cw, cb, hng, *, t_valid, nb):
    T = t_valid
    B = zq.shape[0] // T
    kern = functools.partial(_mlstm_kernel, t_valid=T)
    per_b = lambda shape: pl.BlockSpec((nb,) + shape, lambda b: (b,) + (0,) * len(shape))
    rows = lambda width: pl.BlockSpec((nb * T, width), lambda b: (b, 0))
    conv_spec = pl.BlockSpec((CONV_W - 1, nb, 2 * ML_WIDTH), lambda b: (0, b, 0))
    st_specs = [per_b((ML_HEADS, ML_HEAD_DIM, ML_HEAD_DIM)), per_b((ML_HEADS, ML_HEAD_DIM)),
                per_b((ML_HEADS,))]
    in_specs = ([rows(ML_COLS), conv_spec] + st_specs
                + [_const_spec((1, GATE_COLS)), _const_spec((CONV_W, 2 * ML_WIDTH)),
                   _const_spec((1, 2 * ML_WIDTH)), _const_spec((1, ML_WIDTH))])
    out_shape = [jax.ShapeDtypeStruct((B * T, ML_WIDTH), F32),
                 jax.ShapeDtypeStruct((CONV_W - 1, B, 2 * ML_WIDTH), F32),
                 jax.ShapeDtypeStruct((B, ML_HEADS, ML_HEAD_DIM, ML_HEAD_DIM), F32),
                 jax.ShapeDtypeStruct((B, ML_HEADS, ML_HEAD_DIM), F32),
                 jax.ShapeDtypeStruct((B, ML_HEADS), F32)]
    out_specs = [rows(ML_WIDTH), conv_spec] + st_specs
    return pl.pallas_call(
        kern,
        grid=(B // nb,),
        in_specs=in_specs,
        out_specs=out_specs,
        out_shape=out_shape,
        scratch_shapes=[pltpu.VMEM((nb, BF16_ROWS, ML_COLS), F32),
                        pltpu.VMEM((nb, 2 * SUBLANES, 2 * ML_WIDTH), F32)],
        compiler_params=pltpu.CompilerParams(dimension_semantics=("arbitrary",),
                                             vmem_limit_bytes=VMEM_LIMIT),
        name="mlstm",
    )(zq, cst, *state, bif, cw, cb, hng)


PL = 256
PG = PL // SUBLANES
TAIL = (CONV_W - 1) * SUBLANES
PROJ_PIECE_COLS = 512


def _perm_rows(j):
    return pl.ds((PL // 4) * (j % 4) + j // 4, SUBLANES, stride=SUBLANES)


def _mixer_ml_kernel(h_ref, wqk_ref, wvo_ref, wif_ref, bif_ref, cw_ref, cb_ref, hng_ref,
                     bo_ref, conv_ref, C_ref, n_ref, m_ref, zqk_buf, zvo_buf, zif_buf, us, tail,
                     *, chunks_per_seq):
    s = pl.program_id(0)

    @pl.when(s == 0)
    def _():
        zqk_buf[1] = jnp.zeros(zqk_buf.shape[1:], F32)
        zvo_buf[1] = jnp.zeros(zvo_buf.shape[1:], F32)
        zif_buf[1] = jnp.zeros(zif_buf.shape[1:], F32)

    @pl.when(jnp.maximum(s - 1, 0) % chunks_per_seq == 0)
    def _():
        tail[...] = jnp.zeros(tail.shape, F32)
        C_ref[...] = jnp.zeros(C_ref.shape, F32)
        n_ref[...] = jnp.zeros(n_ref.shape, F32)
        m_ref[...] = jnp.zeros(m_ref.shape, F32)

    step = functools.partial(_mixer_ml_step, h_ref, wqk_ref, wvo_ref, wif_ref, bif_ref, cw_ref,
                             cb_ref, hng_ref, bo_ref, conv_ref, C_ref, n_ref, m_ref,
                             zqk_buf, zvo_buf, zif_buf, us, tail)

    @pl.when(s % 2 == 0)
    def _():
        step(rd=1, wr=0)

    @pl.when(s % 2 == 1)
    def _():
        step(rd=0, wr=1)


def _mixer_ml_step(h_ref, wqk_ref, wvo_ref, wif_ref, bif_ref, cw_ref, cb_ref, hng_ref,
                   bo_ref, conv_ref, C_ref, n_ref, m_ref, zqk_buf, zvo_buf, zif_buf, us, tail,
                   *, rd, wr):
    h = h_ref[...]

    def piece(w_ref, buf, c0, width):
        def emit():
            buf[wr, :, c0:c0 + width] = _bdot(h, w_ref[:, c0:c0 + width])
        return emit

    pieces = [piece(w_ref, buf, c0, PROJ_PIECE_COLS)
              for w_ref, buf in ((wqk_ref, zqk_buf), (wvo_ref, zvo_buf))
              for c0 in range(0, 2 * ML_WIDTH, PROJ_PIECE_COLS)]
    pieces.append(piece(wif_ref, zif_buf, 0, GATE_COLS))
    pieces = iter(pieces)

    def fill():
        emit = next(pieces, None)
        if emit is not None:
            emit()

    fill()
    zqk = zqk_buf.at[rd]
    zvo = zvo_buf.at[rd]
    zif = zif_buf[rd] + bif_ref[...]

    zqk_tail = zqk[PL - TAIL:, :]
    sub = lax.broadcasted_iota(jnp.int32, (SUBLANES, 2 * ML_WIDTH), 0)
    wrapped = []
    for g in range(CONV_W - 1):
        cur = pltpu.roll(zqk_tail[g * SUBLANES:(g + 1) * SUBLANES], 1, axis=0)
        prev = pltpu.roll(tail[g * SUBLANES:(g + 1) * SUBLANES, :], 1, axis=0)
        wrapped.append(jnp.where(sub == 0, prev, cur))
    wrapped = jnp.concatenate(wrapped, axis=0)
    tail[...] = zqk_tail
    conv_ref[...] = jnp.concatenate(
        [zqk_tail[g * SUBLANES + SUBLANES - 1:(g + 1) * SUBLANES, :] for g in range(CONV_W - 1)], axis=0)

    def conv_silu(c0, width):
        cs = slice(c0, c0 + width)
        acc = cb_ref[:, cs] + cw_ref[CONV_W - 1:CONV_W, cs] * zqk[:, cs]
        for d in range(1, CONV_W):
            shifted = jnp.concatenate(
                [wrapped[TAIL - d * SUBLANES:, cs], zqk[:PL - d * SUBLANES, cs]], axis=0)
            acc = acc + cw_ref[CONV_W - 1 - d:CONV_W - d, cs] * shifted
        return acc * jax.nn.sigmoid(acc)

    pr = lax.broadcasted_iota(jnp.int32, (PL, PL), 0)
    pc = lax.broadcasted_iota(jnp.int32, (PL, PL), 1)
    causal = ((pc >> 3) + PG * (pc & 7)) <= ((pr >> 3) + PG * (pr & 7))

    logf = pltpu.roll(_log_sigmoid(zif), GATE_COLS - ML_HEADS, axis=1)
    run, partial = None, []
    for n in range(PG):
        blk = logf[n * SUBLANES:(n + 1) * SUBLANES, :]
        run = blk if run is None else run + blk
        partial.append(run)
    sub_g = lax.broadcasted_iota(jnp.int32, (SUBLANES, GATE_COLS), 0)
    incl = run
    for step in (1, 2, 4):
        incl = incl + jnp.where(sub_g >= step, pltpu.roll(incl, step, axis=0), 0.0)
    earlier = incl - run
    bcum = jnp.concatenate([p + earlier for p in partial], axis=0)

    def head_cols(base, h):
        return zvo[:, base + h * ML_HEAD_DIM:base + (h + 1) * ML_HEAD_DIM]

    outs, C_new, n_new, m_new = _mlstm_heads(
        q_of=lambda h: conv_silu(h * ML_HEAD_DIM, ML_HEAD_DIM),
        k_of=lambda h: conv_silu(ML_WIDTH + h * ML_HEAD_DIM, ML_HEAD_DIM),
        v_of=lambda h: head_cols(0, h),
        o_of=lambda h: head_cols(ML_WIDTH, h),
        causal=causal, ipre=zif, bcum=bcum, groups=[(0, PL, PL - 1)],
        m0_of=lambda g: m_ref[...], C0_of=lambda g, h: C_ref[h], n0_of=lambda g, h: n_ref[h:h + 1, :],
        hng_ref=hng_ref, fill=fill)
    while next(pieces, None) is not None:
        raise AssertionError("projection pieces left over")
    cph = ML_HEAD_DIM // LANES
    for h in range(ML_HEADS):
        for cc in range(cph):
            us[h * cph + cc] = outs[h][:, cc * LANES:(cc + 1) * LANES]
        C_ref[h] = C_new[0][h]
        n_ref[h:h + 1, :] = n_new[0][h]
    m_ref[...] = m_new[0]
    for c in range(ML_WIDTH // LANES):
        for j in range(0, PG, 2):
            pair = jnp.concatenate([us[c, _perm_rows(j), :], us[c, _perm_rows(j + 1), :]], axis=0)
            bo_ref[j * SUBLANES:(j + 2) * SUBLANES, c * LANES:(c + 1) * LANES] = pair.astype(BF16)


def _mixer_ml_call(hp, w_all, bif, cw, cb, hng):
    B, T, _ = hp.shape
    cps = T // PL
    n_chunks = B * cps
    proj = lambda s: jnp.minimum(s, n_chunks - 1)
    math = lambda s: jnp.maximum(s - 1, 0)
    per_b = lambda shape: pl.BlockSpec((None,) + shape,
                                       lambda s: (math(s) // cps,) + (0,) * len(shape))
    out_shape = [jax.ShapeDtypeStruct((B, T, ML_WIDTH), BF16),
                 jax.ShapeDtypeStruct((B, CONV_W - 1, 2 * ML_WIDTH), F32),
                 jax.ShapeDtypeStruct((B, ML_HEADS, ML_HEAD_DIM, ML_HEAD_DIM), F32),
                 jax.ShapeDtypeStruct((B, ML_HEADS, ML_HEAD_DIM), F32),
                 jax.ShapeDtypeStruct((B, 1, ML_HEADS), F32)]
    out_specs = [pl.BlockSpec((None, PL, ML_WIDTH), lambda s: (math(s) // cps, math(s) % cps, 0)),
                 per_b((CONV_W - 1, 2 * ML_WIDTH)),
                 per_b((ML_HEADS, ML_HEAD_DIM, ML_HEAD_DIM)), per_b((ML_HEADS, ML_HEAD_DIM)),
                 per_b((1, ML_HEADS))]
    return pl.pallas_call(
        functools.partial(_mixer_ml_kernel, chunks_per_seq=cps),
        grid=(n_chunks + 1,),
        in_specs=[pl.BlockSpec((None, PL, D_MODEL), lambda s: (proj(s) // cps, proj(s) % cps, 0))]
        + _ml_weight_specs()
        + [_const_spec((1, GATE_COLS)), _const_spec((CONV_W, 2 * ML_WIDTH)),
           _const_spec((1, 2 * ML_WIDTH)), _const_spec((1, ML_WIDTH))],
        out_specs=out_specs,
        out_shape=out_shape,
        scratch_shapes=[pltpu.VMEM((2, PL, 2 * ML_WIDTH), F32),
                        pltpu.VMEM((2, PL, 2 * ML_WIDTH), F32),
                        pltpu.VMEM((2, PL, GATE_COLS), F32),
                        pltpu.VMEM((ML_WIDTH // LANES, PL, LANES), F32),
                        pltpu.VMEM((TAIL, 2 * ML_WIDTH), F32)],
        compiler_params=pltpu.CompilerParams(dimension_semantics=("arbitrary",),
                                             vmem_limit_bytes=VMEM_LIMIT),
        name="mixer_ml",
    )(hp, w_all, w_all, w_all, bif, cw, cb, hng)


def _merge_ffn_kernel(x_ref, pa_ref, bo_ref, g1_ref, wg_ref, bg_ref, wpb_ref, wout_ref,
                      g2_ref, wfi_ref, wfo_ref, gf_ref, y_ref):
    tm = x_ref.shape[0]
    subs = [slice(r0, r0 + FFN_SUB_ROWS) for r0 in range(0, tm, FFN_SUB_ROWS)]
    x = [x_ref[rs, :] for rs in subs]
    h = [_rms(xi, g1_ref[...]).astype(BF16) for xi in x]
    gab = [_bdot(hi, wg_ref[...]) + bg_ref[...] for hi in h]
    pb = [_bdot(bo_ref[rs, :].astype(BF16), wpb_ref[...]) for rs in subs]
    merged = [(jax.nn.sigmoid(g[:, :D_MODEL]) * pa_ref[rs, :]
               + jax.nn.sigmoid(g[:, D_MODEL:]) * p).astype(BF16) for g, p, rs in zip(gab, pb, subs)]
    x1 = [xi + _bdot(mi, wout_ref[...]) for xi, mi in zip(x, merged)]
    h2 = [_rms(xi, g2_ref[...]).astype(BF16) for xi in x1]
    gu = [_bdot(hi, wfi_ref[...]) for hi in h2]
    hid = [(g[:, :D_FF] * jax.nn.sigmoid(g[:, :D_FF]) * g[:, D_FF:]).astype(BF16) for g in gu]
    x2 = [xi + _bdot(hi, wfo_ref[...]) for xi, hi in zip(x1, hid)]
    for rs, xi in zip(subs, x2):
        y_ref[rs, :] = _rms(xi, gf_ref[...])


def _merge_ffn_call(x, pa, bo, g1, wg, bg, wpb, wout, g2, wfi, wfo, gf, *, tm):
    m = x.shape[0]
    row = pl.BlockSpec((tm, D_MODEL), lambda i: (i, 0))
    return pl.pallas_call(
        _merge_ffn_kernel,
        grid=(m // tm,),
        in_specs=[row, row, row, _const_spec((1, D_MODEL)), _const_spec((D_MODEL, 2 * D_MODEL)),
                  _const_spec((1, 2 * D_MODEL)), _const_spec((ML_WIDTH, D_MODEL)),
                  _const_spec((D_MODEL, D_MODEL)), _const_spec((1, D_MODEL)),
                  _const_spec((D_MODEL, 2 * D_FF)), _const_spec((D_FF, D_MODEL)),
                  _const_spec((1, D_MODEL))],
        out_specs=row,
        out_shape=jax.ShapeDtypeStruct((m, D_MODEL), F32),
        compiler_params=pltpu.CompilerParams(dimension_semantics=("arbitrary",),
                                             vmem_limit_bytes=VMEM_LIMIT),
        name="merge_ffn",
    )(x, pa, bo, g1, wg, bg, wpb, wout, g2, wfi, wfo, gf)


SAMPLE_SEQS_PER_STEP = 8
TM_MIX = 512
TM_FFN = 512


def _spatial_tiles(w_s, b_s, chunk):
    if chunk == GM_CHUNK:
        ws_t, b_pos = w_s[:, :chunk, :chunk], b_s[:, :chunk].T
    else:
        onehot = jnp.asarray(np.arange(GM_CHUNK)[:, None] % chunk == np.arange(chunk)[None, :], F32)
        hp = lax.Precision.HIGHEST
        ws_t = jnp.einsum("ri,gij,cj->grc", onehot, w_s[:, :chunk, :chunk], onehot, precision=hp)
        b_pos = jnp.dot(onehot, b_s[:, :chunk].T, precision=hp)
    bs_t = jnp.repeat(b_pos, GM_GROUP_W, axis=1)
    return ws_t, bs_t


def _gmlp_branch(xf, w, chunk, *, emit_v, emit_hperm):
    ws_t, bs_t = _spatial_tiles(w["w_s"], w["b_s"], chunk)
    return _gmlp_call(xf, w["g1"], w["w_all"], w["lng"], w["lnb"], ws_t, bs_t, w["wpa"],
                      chunk=chunk, emit_v=emit_v, emit_hperm=emit_hperm, tm=TM_MIX)


def _merge_branch(xf, pa, bo, w):
    return _merge_ffn_call(xf, pa, bo, w["g1"], w["wg"], w["bg"], w["wpb"], w["wout"], w["g2"],
                           w["wfi"], w["wfo"], w["gf"], tm=TM_FFN)


def kernel(x_prompt, x_sample, state_conv, state_C, state_n, state_m, g_norm1, w_in, b_i, b_f, ln_g, ln_b, w_s, b_s, conv_w, conv_b, hn_g, b_gate, w_proj_a, w_proj_b, w_out, g_norm2, w_ffn_in, w_ffn_out, g_final):
    Bp, Tp, _ = x_prompt.shape
    Bs, Ts, _ = x_sample.shape
    win = w_in[0]
    c_if = 2 * GM_WIDTH + 4 * ML_WIDTH + 2 * ML_HEADS
    w = dict(
        g1=g_norm1[0][None], g2=g_norm2[0][None], gf=g_final[None],
        w_all=win.astype(BF16),
        wg=win[:, c_if:].astype(BF16),
        bg=b_gate[0].reshape(1, 2 * D_MODEL),
        lng=ln_g[0][None], lnb=ln_b[0][None], w_s=w_s[0], b_s=b_s[0],
        bif=jnp.pad(jnp.concatenate([b_i[0], b_f[0]]), (0, GATE_COLS - 2 * ML_HEADS))[None],
        cw=conv_w[0], cb=conv_b[0][None], hng=hn_g[0][None],
        wpa=w_proj_a[0].astype(BF16), wpb=w_proj_b[0].astype(BF16), wout=w_out[0].astype(BF16),
        wfi=w_ffn_in[0].astype(BF16), wfo=w_ffn_out[0].astype(BF16),
    )

    xpf = x_prompt.reshape(Bp * Tp, D_MODEL)
    pa_p, hp_p = _gmlp_branch(xpf, w, GM_CHUNK, emit_v=False, emit_hperm=True)
    bo_p, conv_p, C_p, n_p, m_p = _mixer_ml_call(hp_p.reshape(Bp, Tp, D_MODEL), w["w_all"], w["bif"],
                                                 w["cw"], w["cb"], w["hng"])
    y_p = _merge_branch(xpf, pa_p, bo_p.reshape(Bp * Tp, ML_WIDTH), w)

    xsf = x_sample.reshape(Bs * Ts, D_MODEL)
    pa_s, vn_s = _gmlp_branch(xsf, w, Ts, emit_v=True, emit_hperm=False)
    zq = _inproj_call(xsf, w["g1"], w["w_all"], tm=TM_MIX)
    st = (state_C[0], state_n[0], state_m[0])
    bo_s, conv_s, C_s, n_s, m_s = _mlstm_call(zq, jnp.transpose(state_conv[0], (1, 0, 2)), st, w["bif"],
                                              w["cw"], w["cb"], w["hng"], t_valid=Ts,
                                              nb=SAMPLE_SEQS_PER_STEP)
    y_s = _merge_branch(xsf, pa_s, bo_s, w)

    return (y_p.reshape(Bp, Tp, D_MODEL), y_s.reshape(Bs, Ts, D_MODEL),
            conv_p[None], C_p[None], n_p[None], m_p.reshape(1, Bp, ML_HEADS),
            jnp.transpose(conv_s, (1, 0, 2))[None], C_s[None], n_s[None], m_s[None],
            vn_s.reshape(1, Bs, Ts, GM_WIDTH))
```

```python
import functools
import math

import jax
import jax.numpy as jnp
import numpy as np
from jax import lax
from jax.experimental import pallas as pl
from jax.experimental.pallas import tpu as pltpu

D_MODEL = 1024
GM_WIDTH = D_MODEL
GM_GROUPS = 4
GM_GROUP_W = GM_WIDTH // GM_GROUPS
GM_CHUNK = 128
ML_HEADS = 4
ML_HEAD_DIM = D_MODEL // ML_HEADS
ML_WIDTH = ML_HEADS * ML_HEAD_DIM
CONV_W = 4
D_FF = 2816
EPS = 1e-6

LANES = 128
SUBLANES = 8
BF16_ROWS = 16
GATE_COLS = LANES
ML_COLS = 4 * ML_WIDTH + GATE_COLS
VMEM_LIMIT = 56 * 1024 * 1024
GMLP_SUB_ROWS = 256
FFN_SUB_ROWS = 256

F32 = jnp.float32
BF16 = jnp.bfloat16
NEG_BIG = -1e30
LN_INV_K_SCALE = 0.5 * math.log(ML_HEAD_DIM)
NT_DIMS = (((1,), (1,)), ((), ()))
TN_DIMS = (((0,), (0,)), ((), ()))


def _rms(x, g):
    return x * lax.rsqrt(jnp.mean(x * x, axis=-1, keepdims=True) + EPS) * g


def _gelu(x):
    return 0.5 * x * (1.0 + lax.erf(x * (2.0 ** -0.5)))


def _log_sigmoid(x):
    return jnp.minimum(x, 0.0) - jnp.log1p(jnp.exp(-jnp.abs(x)))


def _bdot(a, b):
    return jnp.dot(a, b, preferred_element_type=F32)


def _const_spec(shape):
    nd = len(shape)
    return pl.BlockSpec(shape, lambda *_: (0,) * nd, pipeline_mode=pl.Buffered(1))


def _gmlp_kernel(x_ref, g1_ref, wuv_ref, lng_ref, lnb_ref, ws_ref, bs_ref, wpa_ref, *rest,
                 chunk, emit_v, emit_hperm, n_cast):
    rest = list(rest)
    cast_src = [rest.pop(0) for _ in range(n_cast)]
    pa_ref = rest.pop(0)
    vn_ref = rest.pop(0) if emit_v else None
    hp_ref = rest.pop(0) if emit_hperm else None
    for src in cast_src:
        rest.pop(0)[...] = src[...].astype(BF16)
    a_sc = rest.pop(0)
    tm = x_ref.shape[0]
    blk = ws_ref.shape[1]
    hs = rest.pop(0) if emit_hperm else None
    sub = GMLP_SUB_ROWS
    subs = [slice(r0, r0 + sub) for r0 in range(0, tm, sub)]
    hb = []
    for rs in subs:
        hf = _rms(x_ref[rs, :], g1_ref[...])
        hb.append(hf.astype(BF16))
        if emit_hperm:
            for j in range(rs.start // SUBLANES, rs.stop // SUBLANES):
                base = (j // PG) * PL
                dst = _perm_rows(j % PG)
                lo = j * SUBLANES - rs.start
                for c in range(D_MODEL // LANES):
                    hs[c, pl.ds(base + dst.start, SUBLANES, stride=SUBLANES), :] = (
                        hf[lo:lo + SUBLANES, c * LANES:(c + 1) * LANES])
            for c in range(D_MODEL // LANES):
                hp_ref[rs, c * LANES:(c + 1) * LANES] = hs[c, rs, :].astype(BF16)
    zu = [_bdot(h, wuv_ref[:, :GM_WIDTH]) for h in hb]
    zv = [_bdot(h, wuv_ref[:, GM_WIDTH:]) for h in hb]
    r = lax.broadcasted_iota(jnp.int32, (blk, blk), 0)
    c = lax.broadcasted_iota(jnp.int32, (blk, blk), 1)
    keep = c <= r
    if chunk < blk:
        sh = chunk.bit_length() - 1
        keep = jnp.logical_and(keep, (r >> sh) == (c >> sh))
    wsm = [jnp.where(keep, ws_ref[g], 0.0).astype(BF16) for g in range(GM_GROUPS)]
    for si, rs in enumerate(subs):
        u = _gelu(zu[si])
        v = _gelu(zv[si])
        mu = jnp.mean(v, axis=-1, keepdims=True)
        vc = v - mu
        var = jnp.mean(vc * vc, axis=-1, keepdims=True)
        vn = vc * lax.rsqrt(var + EPS) * lng_ref[...] + lnb_ref[...]
        if emit_v:
            vn_ref[rs, :] = vn
        vb = vn.astype(BF16)
        for g in range(GM_GROUPS):
            cs = slice(g * GM_GROUP_W, (g + 1) * GM_GROUP_W)
            for i in range(sub // blk):
                ls = slice(i * blk, (i + 1) * blk)
                s = _bdot(wsm[g], vb[ls, cs]) + bs_ref[:, cs]
                a_sc[rs.start + i * blk:rs.start + (i + 1) * blk, cs] = (u[ls, cs] * s).astype(BF16)
        pa_ref[rs, :] = _bdot(a_sc[rs, :], wpa_ref[...])


def _cast_block_spec(n_rows, n_cols, steps):
    n_blocks = steps
    while n_rows % n_blocks or (n_rows // n_blocks) % BF16_ROWS:
        n_blocks //= 2
    per = steps // n_blocks
    return pl.BlockSpec((n_rows // n_blocks, n_cols), lambda i: (i // per, 0))


def _gmlp_call(x, g1, w_all, lng, lnb, ws_t, bs_t, wpa, *, chunk, emit_v, emit_hperm, tm,
               cast_weights=()):
    m = x.shape[0]
    blk = ws_t.shape[1]
    steps = m // tm
    assert steps & (steps - 1) == 0
    row = pl.BlockSpec((tm, D_MODEL), lambda i: (i, 0))
    out_shape = [jax.ShapeDtypeStruct((m, D_MODEL), F32)]
    out_specs = [row]
    scratch = [pltpu.VMEM((tm, GM_WIDTH), BF16)]
    cast_specs = [_cast_block_spec(cw.shape[0], cw.shape[1], steps) for cw in cast_weights]
    if emit_v:
        out_shape.append(jax.ShapeDtypeStruct((m, GM_WIDTH), F32))
        out_specs.append(row)
    if emit_hperm:
        assert tm % PL == 0 and GMLP_SUB_ROWS == PL
        out_shape.append(jax.ShapeDtypeStruct((m, D_MODEL), BF16))
        out_specs.append(row)
        scratch.append(pltpu.VMEM((D_MODEL // LANES, tm, LANES), F32))
    out_shape += [jax.ShapeDtypeStruct(cw.shape, BF16) for cw in cast_weights]
    out_specs += cast_specs
    return pl.pallas_call(
        functools.partial(_gmlp_kernel, chunk=chunk, emit_v=emit_v, emit_hperm=emit_hperm,
                          n_cast=len(cast_weights)),
        grid=(steps,),
        in_specs=[row, _const_spec((1, D_MODEL)),
                  pl.BlockSpec((D_MODEL, 2 * GM_WIDTH), lambda i: (0, 0), pipeline_mode=pl.Buffered(1)),
                  _const_spec((1, GM_WIDTH)), _const_spec((1, GM_WIDTH)),
                  _const_spec((GM_GROUPS, blk, blk)), _const_spec((blk, GM_WIDTH)),
                  _const_spec((GM_WIDTH, D_MODEL))] + cast_specs,
        out_specs=out_specs,
        out_shape=out_shape,
        scratch_shapes=scratch,
        compiler_params=pltpu.CompilerParams(dimension_semantics=("arbitrary",),
                                             vmem_limit_bytes=VMEM_LIMIT),
        name="gmlp",
    )(x, g1, w_all, lng, lnb, ws_t, bs_t, wpa, *cast_weights)


def _ml_weight_specs():
    wide = 2 * ML_WIDTH
    assert (2 * GM_WIDTH) % wide == 0 and (2 * GM_WIDTH + 4 * ML_WIDTH) % GATE_COLS == 0
    first = 2 * GM_WIDTH // wide
    col_block = lambda width, idx: pl.BlockSpec((D_MODEL, width), lambda *_: (0, idx),
                                                pipeline_mode=pl.Buffered(1))
    return [col_block(wide, first), col_block(wide, first + 1),
            col_block(GATE_COLS, (2 * GM_WIDTH + 4 * ML_WIDTH) // GATE_COLS)]


def _inproj_kernel(x_ref, g1_ref, wqk_ref, wvo_ref, wif_ref, z_ref):
    h = _rms(x_ref[...], g1_ref[...]).astype(BF16)
    z_ref[:, 0:2 * ML_WIDTH] = _bdot(h, wqk_ref[...])
    z_ref[:, 2 * ML_WIDTH:4 * ML_WIDTH] = _bdot(h, wvo_ref[...])
    z_ref[:, 4 * ML_WIDTH:ML_COLS] = _bdot(h, wif_ref[...])


def _inproj_call(x, g1, w_all, *, tm):
    m = x.shape[0]
    return pl.pallas_call(
        _inproj_kernel,
        grid=(m // tm,),
        in_specs=[pl.BlockSpec((tm, D_MODEL), lambda i: (i, 0)), _const_spec((1, D_MODEL))]
        + _ml_weight_specs(),
        out_specs=pl.BlockSpec((tm, ML_COLS), lambda i: (i, 0)),
        out_shape=jax.ShapeDtypeStruct((m, ML_COLS), F32),
        compiler_params=pltpu.CompilerParams(dimension_semantics=("arbitrary",),
                                             vmem_limit_bytes=VMEM_LIMIT),
        name="inproj",
    )(x, g1, w_all, w_all, w_all)


def _mlstm_heads(q_of, k_of, v_of, o_of, causal, ipre, bcum, groups, m0_of, C0_of, n0_of, hng_ref,
                 fill=None):
    n_groups = len(groups)
    single = n_groups == 1
    fill = fill or (lambda: None)

    def rows_of(x, g):
        return x if single else x[groups[g][0]:groups[g][0] + groups[g][1]]

    def per_row(vals):
        if single:
            return vals[0]
        return jnp.concatenate([jnp.broadcast_to(v, (groups[g][1], v.shape[1]))
                                for g, v in enumerate(vals)], axis=0)

    a = ipre - bcum
    a_t = a.T
    m_rows = per_row([m0_of(g) for g in range(n_groups)])

    def prepare(h):
        p = {}
        a2 = jnp.where(causal, a_t[h:h + 1, :], -jnp.inf)
        p["mc"] = mc = jnp.maximum(jnp.max(a2, axis=1, keepdims=True), m_rows[:, h:h + 1])
        p["m_last"] = m_last = [mc[grp[2]:grp[2] + 1, :] for grp in groups]
        p["dm"] = jnp.exp(a2 - (mc + LN_INV_K_SCALE))
        p["w_inter"] = jnp.exp(m_rows[:, h:h + 1] - mc)
        p["w_col"] = w_col = jnp.exp(a[:, h:h + 1] - (per_row(m_last) + LN_INV_K_SCALE))
        p["decay"] = [jnp.exp(m0_of(g)[:, h:h + 1] - m_last[g]) for g in range(n_groups)]
        p["q"] = q = q_of(h)
        p["k"] = k = k_of(h)
        v = v_of(h)
        p["qb"], p["kb"], p["vb"] = q.astype(BF16), k.astype(BF16), v.astype(BF16)
        p["vw"] = (v * w_col).astype(BF16)
        p["c_old"] = [C0_of(g, h) for g in range(n_groups)]
        p["n_old"] = [n0_of(g, h) for g in range(n_groups)]
        return p

    def first_matmuls(p):
        p["qk"] = lax.dot_general(p["qb"], p["kb"], NT_DIMS, preferred_element_type=F32)
        p["qc"] = [lax.dot_general(rows_of(p["qb"], g), p["c_old"][g].astype(BF16), NT_DIMS,
                                   preferred_element_type=F32) for g in range(n_groups)]
        if single:
            n_rows = jnp.broadcast_to(p["n_old"][0], (LANES, ML_HEAD_DIM)).astype(BF16)
            p["qn"] = lax.dot_general(p["qb"], n_rows, NT_DIMS, preferred_element_type=F32)[:, 0:1]
        else:
            p["qn"] = jnp.sum(p["q"] * per_row(p["n_old"]), axis=1, keepdims=True)

    def second_matmuls(p):
        p["s"] = s = p["dm"] * p["qk"]
        p["sv"] = _bdot(s.astype(BF16), p["vb"])
        p["cupd"] = [lax.dot_general(rows_of(p["vw"], g), rows_of(p["kb"], g), TN_DIMS,
                                     preferred_element_type=F32) for g in range(n_groups)]

    def finish(h, p):
        qc_rows = p["qc"][0] if single else jnp.concatenate(p["qc"], axis=0)
        num = p["w_inter"] * qc_rows + p["sv"]
        den = p["w_inter"] * p["qn"] + jnp.sum(p["s"], axis=1, keepdims=True)
        hcur = num / jnp.maximum(jnp.abs(den), jnp.exp(-(bcum[:, h:h + 1] + p["mc"])))
        mu = jnp.mean(hcur, axis=1, keepdims=True)
        hc = hcur - mu
        var = jnp.mean(hc * hc, axis=1, keepdims=True)
        hs = slice(h * ML_HEAD_DIM, (h + 1) * ML_HEAD_DIM)
        out = jax.nn.sigmoid(o_of(h)) * (hc * lax.rsqrt(var + EPS) * hng_ref[:, hs])
        kw = p["k"] * p["w_col"]
        c_new = [p["decay"][g] * p["c_old"][g] + p["cupd"][g] for g in range(n_groups)]
        n_new = [p["decay"][g] * p["n_old"][g] + jnp.sum(rows_of(kw, g), axis=0, keepdims=True)
                 for g in range(n_groups)]
        return out, c_new, n_new

    H = ML_HEADS
    per_head = [None] * H
    done = [None] * H
    if single:
        per_head[0] = prepare(0)
        fill()
        first_matmuls(per_head[0])
        if H > 1:
            per_head[1] = prepare(1)
        fill()
        for h in range(H):
            second_matmuls(per_head[h])
            if h + 1 < H:
                first_matmuls(per_head[h + 1])
            fill()
            if h + 2 < H:
                per_head[h + 2] = prepare(h + 2)
            done[h] = finish(h, per_head[h])
            fill()
    else:
        per_head = [prepare(h) for h in range(H)]
        for stage in (first_matmuls, second_matmuls):
            for h in range(H):
                stage(per_head[h])
        done = [finish(h, per_head[h]) for h in range(H)]

    outs = [done[h][0] for h in range(H)]
    C_new = [[done[h][1][g] for h in range(H)] for g in range(n_groups)]
    n_new = [[done[h][2][g] for h in range(H)] for g in range(n_groups)]
    m_new = []
    for g, grp in enumerate(groups):
        row = m0_of(g)
        lane = lax.broadcasted_iota(jnp.int32, row.shape, 1)
        for h in range(H):
            row = jnp.where(lane == h, bcum[grp[2]:grp[2] + 1, h:h + 1] + per_head[h]["m_last"][g], row)
        m_new.append(row)
    return outs, C_new, n_new, m_new


def _mlstm_kernel(zq_ref, cst_ref, C0_ref, n0_ref, m0_ref, bif_ref, cw_ref, cb_ref, hng_ref,
                  bo_ref, conv_ref, C_ref, n_ref, m_ref, zp, xp, *, t_valid):
    nb = C0_ref.shape[0]
    L = BF16_ROWS
    R = nb * L

    @pl.when(pl.program_id(0) == 0)
    def _():
        zp[...] = jnp.zeros(zp.shape, F32)

    for bb in range(nb):
        zp[bb, 0:t_valid, :] = zq_ref[bb * t_valid:(bb + 1) * t_valid, :]
        for j in range(CONV_W - 1):
            row = SUBLANES - (CONV_W - 1) + j
            xp[bb, row:row + 1, :] = cst_ref[j, bb:bb + 1, :]
        xp[bb, SUBLANES:2 * SUBLANES, :] = zp[bb, 0:SUBLANES, 0:2 * ML_WIDTH]
        for j in range(CONV_W - 1):
            row = SUBLANES + t_valid - (CONV_W - 1) + j
            conv_ref[j, bb:bb + 1, :] = xp[bb, row:row + 1, :]
    qk_rows = []
    for bb in range(nb):
        acc = cb_ref[...]
        for j in range(CONV_W):
            off = SUBLANES - (CONV_W - 1) + j
            acc = acc + cw_ref[j:j + 1, :] * xp[bb, off:off + SUBLANES, :]
        qk_rows += [acc, jnp.zeros((L - SUBLANES, 2 * ML_WIDTH), F32)]
    qk = jnp.concatenate(qk_rows, axis=0)
    qk = qk * jax.nn.sigmoid(qk)

    def cols(c0, width):
        return zp[:, :, c0:c0 + width].reshape(R, width)

    zif = cols(4 * ML_WIDTH, GATE_COLS) + bif_ref[...]
    live = (lax.broadcasted_iota(jnp.int32, (R, GATE_COLS), 0) & (L - 1)) < t_valid
    ipre = jnp.where(live, zif, NEG_BIG)
    logf = jnp.where(live, pltpu.roll(_log_sigmoid(zif), GATE_COLS - ML_HEADS, axis=1), 0.0)
    r = lax.broadcasted_iota(jnp.int32, (R, R), 0)
    c = lax.broadcasted_iota(jnp.int32, (R, R), 1)
    sh = L.bit_length() - 1
    causal = jnp.logical_and(c <= r, (r >> sh) == (c >> sh))
    bcum = jnp.dot(causal.astype(F32), logf, preferred_element_type=F32,
                   precision=lax.Precision.HIGHEST)
    live_w = (lax.broadcasted_iota(jnp.int32, (R, ML_HEAD_DIM), 0) & (L - 1)) < t_valid

    def head_cols(x, base, h):
        return x[:, base + h * ML_HEAD_DIM:base + (h + 1) * ML_HEAD_DIM]

    groups = [(bb * L, L, bb * L + L - 1) for bb in range(nb)]
    outs, C_new, n_new, m_new = _mlstm_heads(
        q_of=lambda h: head_cols(qk, 0, h),
        k_of=lambda h: jnp.where(live_w, head_cols(qk, ML_WIDTH, h), 0.0),
        v_of=lambda h: jnp.where(live_w, cols(2 * ML_WIDTH + h * ML_HEAD_DIM, ML_HEAD_DIM), 0.0),
        o_of=lambda h: cols(3 * ML_WIDTH + h * ML_HEAD_DIM, ML_HEAD_DIM),
        causal=causal, ipre=ipre, bcum=bcum, groups=groups,
        m0_of=lambda g: m0_ref[g:g + 1, :], C0_of=lambda g, h: C0_ref[g, h],
        n0_of=lambda g, h: n0_ref[g, h:h + 1, :], hng_ref=hng_ref)
    for bb in range(nb):
        for h in range(ML_HEADS):
            hs = slice(h * ML_HEAD_DIM, (h + 1) * ML_HEAD_DIM)
            bo_ref[bb * t_valid:(bb + 1) * t_valid, hs] = outs[h][bb * L:bb * L + t_valid, :]
            C_ref[bb, h] = C_new[bb][h]
            n_ref[bb, h:h + 1, :] = n_new[bb][h]
        m_ref[bb:bb + 1, :] = m_new[bb]


def _mlstm_call(zq, cst, state, bif, cw, cb, hng, *, t_valid, nb):
    T = t_valid
    B = zq.shape[0] // T
    kern = functools.partial(_mlstm_kernel, t_valid=T)
    per_b = lambda shape: pl.BlockSpec((nb,) + shape, lambda b: (b,) + (0,) * len(shape))
    rows = lambda width: pl.BlockSpec((nb * T, width), lambda b: (b, 0))
    conv_spec = pl.BlockSpec((CONV_W - 1, nb, 2 * ML_WIDTH), lambda b: (0, b, 0))
    st_specs = [per_b((ML_HEADS, ML_HEAD_DIM, ML_HEAD_DIM)), per_b((ML_HEADS, ML_HEAD_DIM)),
                per_b((ML_HEADS,))]
    in_specs = ([rows(ML_COLS), conv_spec] + st_specs
                + [_const_spec((1, GATE_COLS)), _const_spec((CONV_W, 2 * ML_WIDTH)),
                   _const_spec((1, 2 * ML_WIDTH)), _const_spec((1, ML_WIDTH))])
    out_shape = [jax.ShapeDtypeStruct((B * T, ML_WIDTH), F32),
                 jax.ShapeDtypeStruct((CONV_W - 1, B, 2 * ML_WIDTH), F32),
                 jax.ShapeDtypeStruct((B, ML_HEADS, ML_HEAD_DIM, ML_HEAD_DIM), F32),
                 jax.ShapeDtypeStruct((B, ML_HEADS, ML_HEAD_DIM), F32),
                 jax.ShapeDtypeStruct((B, ML_HEADS), F32)]
    out_specs = [rows(ML_WIDTH), conv_spec] + st_specs
    return pl.pallas_call(
        kern,
        grid=(B // nb,),
        in_specs=in_specs,
        out_specs=out_specs,
        out_shape=out_shape,
        scratch_shapes=[pltpu.VMEM((nb, BF16_ROWS, ML_COLS), F32),
                        pltpu.VMEM((nb, 2 * SUBLANES, 2 * ML_WIDTH), F32)],
        compiler_params=pltpu.CompilerParams(dimension_semantics=("arbitrary",),
                                             vmem_limit_bytes=VMEM_LIMIT),
        name="mlstm",
    )(zq, cst, *state, bif, cw, cb, hng)


PL = 256
PG = PL // SUBLANES
TAIL = (CONV_W - 1) * SUBLANES
PROJ_PIECE_COLS = 512


def _perm_rows(j):
    return pl.ds((PL // 4) * (j % 4) + j // 4, SUBLANES, stride=SUBLANES)


def _mixer_ml_kernel(h_ref, wqk_ref, wvo_ref, wif_ref, bif_ref, cw_ref, cb_ref, hng_ref,
                     bo_ref, conv_ref, C_ref, n_ref, m_ref, zqk_buf, zvo_buf, zif_buf, us, tail,
                     *, chunks_per_seq):
    s = pl.program_id(0)

    @pl.when(s == 0)
    def _():
        zqk_buf[1] = jnp.zeros(zqk_buf.shape[1:], F32)
        zvo_buf[1] = jnp.zeros(zvo_buf.shape[1:], F32)
        zif_buf[1] = jnp.zeros(zif_buf.shape[1:], F32)

    @pl.when(jnp.maximum(s - 1, 0) % chunks_per_seq == 0)
    def _():
        tail[...] = jnp.zeros(tail.shape, F32)
        C_ref[...] = jnp.zeros(C_ref.shape, F32)
        n_ref[...] = jnp.zeros(n_ref.shape, F32)
        m_ref[...] = jnp.zeros(m_ref.shape, F32)

    step = functools.partial(_mixer_ml_step, h_ref, wqk_ref, wvo_ref, wif_ref, bif_ref, cw_ref,
                             cb_ref, hng_ref, bo_ref, conv_ref, C_ref, n_ref, m_ref,
                             zqk_buf, zvo_buf, zif_buf, us, tail)

    @pl.when(s % 2 == 0)
    def _():
        step(rd=1, wr=0)

    @pl.when(s % 2 == 1)
    def _():
        step(rd=0, wr=1)


def _mixer_ml_step(h_ref, wqk_ref, wvo_ref, wif_ref, bif_ref, cw_ref, cb_ref, hng_ref,
                   bo_ref, conv_ref, C_ref, n_ref, m_ref, zqk_buf, zvo_buf, zif_buf, us, tail,
                   *, rd, wr):
    h = h_ref[...]

    def piece(w_ref, buf, c0, width):
        def emit():
            buf[wr, :, c0:c0 + width] = _bdot(h, w_ref[:, c0:c0 + width])
        return emit

    pieces = [piece(w_ref, buf, c0, PROJ_PIECE_COLS)
              for w_ref, buf in ((wqk_ref, zqk_buf), (wvo_ref, zvo_buf))
              for c0 in range(0, 2 * ML_WIDTH, PROJ_PIECE_COLS)]
    pieces.append(piece(wif_ref, zif_buf, 0, GATE_COLS))
    pieces = iter(pieces)

    def fill():
        emit = next(pieces, None)
        if emit is not None:
            emit()

    fill()
    zqk = zqk_buf.at[rd]
    zvo = zvo_buf.at[rd]
    zif = zif_buf[rd] + bif_ref[...]

    zqk_tail = zqk[PL - TAIL:, :]
    sub = lax.broadcasted_iota(jnp.int32, (SUBLANES, 2 * ML_WIDTH), 0)
    wrapped = []
    for g in range(CONV_W - 1):
        cur = pltpu.roll(zqk_tail[g * SUBLANES:(g + 1) * SUBLANES], 1, axis=0)
        prev = pltpu.roll(tail[g * SUBLANES:(g + 1) * SUBLANES, :], 1, axis=0)
        wrapped.append(jnp.where(sub == 0, prev, cur))
    wrapped = jnp.concatenate(wrapped, axis=0)
    tail[...] = zqk_tail
    conv_ref[...] = jnp.concatenate(
        [zqk_tail[g * SUBLANES + SUBLANES - 1:(g + 1) * SUBLANES, :] for g in range(CONV_W - 1)], axis=0)

    def conv_silu(c0, width):
        cs = slice(c0, c0 + width)
        acc = cb_ref[:, cs] + cw_ref[CONV_W - 1:CONV_W, cs] * zqk[:, cs]
        for d in range(1, CONV_W):
            shifted = jnp.concatenate(
                [wrapped[TAIL - d * SUBLANES:, cs], zqk[:PL - d * SUBLANES, cs]], axis=0)
            acc = acc + cw_ref[CONV_W - 1 - d:CONV_W - d, cs] * shifted
        return acc * jax.nn.sigmoid(acc)

    pr = lax.broadcasted_iota(jnp.int32, (PL, PL), 0)
    pc = lax.broadcasted_iota(jnp.int32, (PL, PL), 1)
    causal = ((pc >> 3) + PG * (pc & 7)) <= ((pr >> 3) + PG * (pr & 7))

    logf = pltpu.roll(_log_sigmoid(zif), GATE_COLS - ML_HEADS, axis=1)
    run, partial = None, []
    for n in range(PG):
        blk = logf[n * SUBLANES:(n + 1) * SUBLANES, :]
        run = blk if run is None else run + blk
        partial.append(run)
    sub_g = lax.broadcasted_iota(jnp.int32, (SUBLANES, GATE_COLS), 0)
    incl = run
    for step in (1, 2, 4):
        incl = incl + jnp.where(sub_g >= step, pltpu.roll(incl, step, axis=0), 0.0)
    earlier = incl - run
    bcum = jnp.concatenate([p + earlier for p in partial], axis=0)

    def head_cols(base, h):
        return zvo[:, base + h * ML_HEAD_DIM:base + (h + 1) * ML_HEAD_DIM]

    outs, C_new, n_new, m_new = _mlstm_heads(
        q_of=lambda h: conv_silu(h * ML_HEAD_DIM, ML_HEAD_DIM),
        k_of=lambda h: conv_silu(ML_WIDTH + h * ML_HEAD_DIM, ML_HEAD_DIM),
        v_of=lambda h: head_cols(0, h),
        o_of=lambda h: head_cols(ML_WIDTH, h),
        causal=causal, ipre=zif, bcum=bcum, groups=[(0, PL, PL - 1)],
        m0_of=lambda g: m_ref[...], C0_of=lambda g, h: C_ref[h], n0_of=lambda g, h: n_ref[h:h + 1, :],
        hng_ref=hng_ref, fill=fill)
    while next(pieces, None) is not None:
        raise AssertionError("projection pieces left over")
    cph = ML_HEAD_DIM // LANES
    for h in range(ML_HEADS):
        for cc in range(cph):
            us[h * cph + cc] = outs[h][:, cc * LANES:(cc + 1) * LANES]
        C_ref[h] = C_new[0][h]
        n_ref[h:h + 1, :] = n_new[0][h]
    m_ref[...] = m_new[0]
    for c in range(ML_WIDTH // LANES):
        for j in range(0, PG, 2):
            pair = jnp.concatenate([us[c, _perm_rows(j), :], us[c, _perm_rows(j + 1), :]], axis=0)
            bo_ref[j * SUBLANES:(j + 2) * SUBLANES, c * LANES:(c + 1) * LANES] = pair.astype(BF16)


def _mixer_ml_call(hp, w_all, bif, cw, cb, hng):
    B, T, _ = hp.shape
    cps = T // PL
    n_chunks = B * cps
    proj = lambda s: jnp.minimum(s, n_chunks - 1)
    math = lambda s: jnp.maximum(s - 1, 0)
    per_b = lambda shape: pl.BlockSpec((None,) + shape,
                                       lambda s: (math(s) // cps,) + (0,) * len(shape))
    out_shape = [jax.ShapeDtypeStruct((B, T, ML_WIDTH), BF16),
                 jax.ShapeDtypeStruct((B, CONV_W - 1, 2 * ML_WIDTH), F32),
                 jax.ShapeDtypeStruct((B, ML_HEADS, ML_HEAD_DIM, ML_HEAD_DIM), F32),
                 jax.ShapeDtypeStruct((B, ML_HEADS, ML_HEAD_DIM), F32),
                 jax.ShapeDtypeStruct((B, 1, ML_HEADS), F32)]
    out_specs = [pl.BlockSpec((None, PL, ML_WIDTH), lambda s: (math(s) // cps, math(s) % cps, 0)),
                 per_b((CONV_W - 1, 2 * ML_WIDTH)),
                 per_b((ML_HEADS, ML_HEAD_DIM, ML_HEAD_DIM)), per_b((ML_HEADS, ML_HEAD_DIM)),
                 per_b((1, ML_HEADS))]
    return pl.pallas_call(
        functools.partial(_mixer_ml_kernel, chunks_per_seq=cps),
        grid=(n_chunks + 1,),
        in_specs=[pl.BlockSpec((None, PL, D_MODEL), lambda s: (proj(s) // cps, proj(s) % cps, 0))]
        + _ml_weight_specs()
        + [_const_spec((1, GATE_COLS)), _const_spec((CONV_W, 2 * ML_WIDTH)),
           _const_spec((1, 2 * ML_WIDTH)), _const_spec((1, ML_WIDTH))],
        out_specs=out_specs,
        out_shape=out_shape,
        scratch_shapes=[pltpu.VMEM((2, PL, 2 * ML_WIDTH), F32),
                        pltpu.VMEM((2, PL, 2 * ML_WIDTH), F32),
                        pltpu.VMEM((2, PL, GATE_COLS), F32),
                        pltpu.VMEM((ML_WIDTH // LANES, PL, LANES), F32),
                        pltpu.VMEM((TAIL, 2 * ML_WIDTH), F32)],
        compiler_params=pltpu.CompilerParams(dimension_semantics=("arbitrary",),
                                             vmem_limit_bytes=VMEM_LIMIT),
        name="mixer_ml",
    )(hp, w_all, w_all, w_all, bif, cw, cb, hng)


def _merge_ffn_kernel(x_ref, pa_ref, bo_ref, g1_ref, wg_ref, bg_ref, wpb_ref, wout_ref,
                      g2_ref, wfi_ref, wfo_ref, gf_ref, y_ref):
    tm = x_ref.shape[0]
    subs = [slice(r0, r0 + FFN_SUB_ROWS) for r0 in range(0, tm, FFN_SUB_ROWS)]
    x = [x_ref[rs, :] for rs in subs]
    h = [_rms(xi, g1_ref[...]).astype(BF16) for xi in x]
    gab = [_bdot(hi, wg_ref[...]) + bg_ref[...] for hi in h]
    pb = [_bdot(bo_ref[rs, :].astype(BF16), wpb_ref[...]) for rs in subs]
    merged = [(jax.nn.sigmoid(g[:, :D_MODEL]) * pa_ref[rs, :]
               + jax.nn.sigmoid(g[:, D_MODEL:]) * p).astype(BF16) for g, p, rs in zip(gab, pb, subs)]
    x1 = [xi + _bdot(mi, wout_ref[...]) for xi, mi in zip(x, merged)]
    h2 = [_rms(xi, g2_ref[...]).astype(BF16) for xi in x1]
    gu = [_bdot(hi, wfi_ref[...]) for hi in h2]
    hid = [(g[:, :D_FF] * jax.nn.sigmoid(g[:, :D_FF]) * g[:, D_FF:]).astype(BF16) for g in gu]
    x2 = [xi + _bdot(hi, wfo_ref[...]) for xi, hi in zip(x1, hid)]
    for rs, xi in zip(subs, x2):
        y_ref[rs, :] = _rms(xi, gf_ref[...])


def _merge_ffn_call(x, pa, bo, g1, wg, bg, wpb, wout, g2, wfi, wfo, gf, *, tm):
    m = x.shape[0]
    row = pl.BlockSpec((tm, D_MODEL), lambda i: (i, 0))
    return pl.pallas_call(
        _merge_ffn_kernel,
        grid=(m // tm,),
        in_specs=[row, row, row, _const_spec((1, D_MODEL)), _const_spec((D_MODEL, 2 * D_MODEL)),
                  _const_spec((1, 2 * D_MODEL)), _const_spec((ML_WIDTH, D_MODEL)),
                  _const_spec((D_MODEL, D_MODEL)), _const_spec((1, D_MODEL)),
                  _const_spec((D_MODEL, 2 * D_FF)), _const_spec((D_FF, D_MODEL)),
                  _const_spec((1, D_MODEL))],
        out_specs=row,
        out_shape=jax.ShapeDtypeStruct((m, D_MODEL), F32),
        compiler_params=pltpu.CompilerParams(dimension_semantics=("arbitrary",),
                                             vmem_limit_bytes=VMEM_LIMIT),
        name="merge_ffn",
    )(x, pa, bo, g1, wg, bg, wpb, wout, g2, wfi, wfo, gf)


SAMPLE_SEQS_PER_STEP = 8
TM_MIX = 512
TM_FFN = 512


def _spatial_tiles(w_s, b_s, chunk):
    if chunk == GM_CHUNK:
        ws_t, b_pos = w_s[:, :chunk, :chunk], b_s[:, :chunk].T
    else:
        onehot = jnp.asarray(np.arange(GM_CHUNK)[:, None] % chunk == np.arange(chunk)[None, :], F32)
        hp = lax.Precision.HIGHEST
        ws_t = jnp.einsum("ri,gij,cj->grc", onehot, w_s[:, :chunk, :chunk], onehot, precision=hp)
        b_pos = jnp.dot(onehot, b_s[:, :chunk].T, precision=hp)
    bs_t = jnp.repeat(b_pos, GM_GROUP_W, axis=1)
    return ws_t, bs_t


def _gmlp_branch(xf, w, chunk, *, emit_v, emit_hperm, cast_weights=()):
    ws_t, bs_t = _spatial_tiles(w["w_s"], w["b_s"], chunk)
    return _gmlp_call(xf, w["g1"], w["w_all"], w["lng"], w["lnb"], ws_t, bs_t, w["wpa"],
                      chunk=chunk, emit_v=emit_v, emit_hperm=emit_hperm, tm=TM_MIX,
                      cast_weights=cast_weights)


def _merge_branch(xf, pa, bo, w):
    return _merge_ffn_call(xf, pa, bo, w["g1"], w["wg"], w["bg"], w["wpb"], w["wout"], w["g2"],
                           w["wfi"], w["wfo"], w["gf"], tm=TM_FFN)


def kernel(x_prompt, x_sample, state_conv, state_C, state_n, state_m, g_norm1, w_in, b_i, b_f, ln_g, ln_b, w_s, b_s, conv_w, conv_b, hn_g, b_gate, w_proj_a, w_proj_b, w_out, g_norm2, w_ffn_in, w_ffn_out, g_final):
    Bp, Tp, _ = x_prompt.shape
    Bs, Ts, _ = x_sample.shape
    win = w_in[0]
    c_if = 2 * GM_WIDTH + 4 * ML_WIDTH + 2 * ML_HEADS
    w = dict(
        g1=g_norm1[0][None], g2=g_norm2[0][None], gf=g_final[None],
        w_all=win.astype(BF16),
        wg=win[:, c_if:].astype(BF16),
        bg=b_gate[0].reshape(1, 2 * D_MODEL),
        lng=ln_g[0][None], lnb=ln_b[0][None], w_s=w_s[0], b_s=b_s[0],
        bif=jnp.pad(jnp.concatenate([b_i[0], b_f[0]]), (0, GATE_COLS - 2 * ML_HEADS))[None],
        cw=conv_w[0], cb=conv_b[0][None], hng=hn_g[0][None],
        wpa=w_proj_a[0].astype(BF16),
    )

    xpf = x_prompt.reshape(Bp * Tp, D_MODEL)
    pa_p, hp_p, w["wpb"], w["wout"], w["wfi"], w["wfo"] = _gmlp_branch(
        xpf, w, GM_CHUNK, emit_v=False, emit_hperm=True,
        cast_weights=(w_proj_b[0], w_out[0], w_ffn_in[0], w_ffn_out[0]))
    bo_p, conv_p, C_p, n_p, m_p = _mixer_ml_call(hp_p.reshape(Bp, Tp, D_MODEL), w["w_all"], w["bif"],
                                                 w["cw"], w["cb"], w["hng"])
    y_p = _merge_branch(xpf, pa_p, bo_p.reshape(Bp * Tp, ML_WIDTH), w)

    xsf = x_sample.reshape(Bs * Ts, D_MODEL)
    pa_s, vn_s = _gmlp_branch(xsf, w, Ts, emit_v=True, emit_hperm=False)
    zq = _inproj_call(xsf, w["g1"], w["w_all"], tm=TM_MIX)
    st = (state_C[0], state_n[0], state_m[0])
    bo_s, conv_s, C_s, n_s, m_s = _mlstm_call(zq, jnp.transpose(state_conv[0], (1, 0, 2)), st, w["bif"],
                                              w["cw"], w["cb"], w["hng"], t_valid=Ts,
                                              nb=SAMPLE_SEQS_PER_STEP)
    y_s = _merge_branch(xsf, pa_s, bo_s, w)

    return (y_p.reshape(Bp, Tp, D_MODEL), y_s.reshape(Bs, Ts, D_MODEL),
            conv_p[None], C_p[None], n_p[None], m_p.reshape(1, Bp, ML_HEADS),
            jnp.transpose(conv_s, (1, 0, 2))[None], C_s[None], n_s[None], m_s[None],
            vn_s.reshape(1, Bs, Ts, GM_WIDTH))
```

```python
import functools
import math

import jax
import jax.numpy as jnp
import numpy as np
from jax import lax
from jax.experimental import pallas as pl
from jax.experimental.pallas import tpu as pltpu

D_MODEL = 1024
GM_WIDTH = D_MODEL
GM_GROUPS = 4
GM_GROUP_W = GM_WIDTH // GM_GROUPS
GM_CHUNK = 128
ML_HEADS = 4
ML_HEAD_DIM = D_MODEL // ML_HEADS
ML_WIDTH = ML_HEADS * ML_HEAD_DIM
CONV_W = 4
D_FF = 2816
EPS = 1e-6

LANES = 128
SUBLANES = 8
BF16_ROWS = 16
GATE_COLS = LANES
ML_COLS = 4 * ML_WIDTH + GATE_COLS
VMEM_LIMIT = 56 * 1024 * 1024
GMLP_SUB_ROWS = 256
FFN_SUB_ROWS = 256

F32 = jnp.float32
BF16 = jnp.bfloat16
NEG_BIG = -1e30
LN_INV_K_SCALE = 0.5 * math.log(ML_HEAD_DIM)
NT_DIMS = (((1,), (1,)), ((), ()))
TN_DIMS = (((0,), (0,)), ((), ()))


def _rms(x, g):
    return x * lax.rsqrt(jnp.mean(x * x, axis=-1, keepdims=True) + EPS) * g


def _gelu(x):
    return 0.5 * x * (1.0 + lax.erf(x * (2.0 ** -0.5)))


def _log_sigmoid(x):
    return jnp.minimum(x, 0.0) - jnp.log1p(jnp.exp(-jnp.abs(x)))


def _bdot(a, b):
    return jnp.dot(a, b, preferred_element_type=F32)


def _const_spec(shape):
    nd = len(shape)
    return pl.BlockSpec(shape, lambda *_: (0,) * nd, pipeline_mode=pl.Buffered(1))


def _gmlp_kernel(x_ref, g1_ref, wuv_ref, lng_ref, lnb_ref, ws_ref, bs_ref, wpa_ref, *rest,
                 chunk, emit_v, emit_hperm, n_cast):
    rest = list(rest)
    cast_src = [rest.pop(0) for _ in range(n_cast)]
    pa_ref = rest.pop(0)
    vn_ref = rest.pop(0) if emit_v else None
    hp_ref = rest.pop(0) if emit_hperm else None
    for src in cast_src:
        rest.pop(0)[...] = src[...].astype(BF16)
    a_sc = rest.pop(0)
    tm = x_ref.shape[0]
    blk = ws_ref.shape[1]
    hs = rest.pop(0) if emit_hperm else None
    sub = GMLP_SUB_ROWS
    subs = [slice(r0, r0 + sub) for r0 in range(0, tm, sub)]
    hb = []
    for rs in subs:
        hf = _rms(x_ref[rs, :], g1_ref[...])
        hb.append(hf.astype(BF16))
        if emit_hperm:
            for j in range(rs.start // SUBLANES, rs.stop // SUBLANES):
                base = (j // PG) * PL
                dst = _perm_rows(j % PG)
                lo = j * SUBLANES - rs.start
                for c in range(D_MODEL // LANES):
                    hs[c, pl.ds(base + dst.start, SUBLANES, stride=SUBLANES), :] = (
                        hf[lo:lo + SUBLANES, c * LANES:(c + 1) * LANES])
            for c in range(D_MODEL // LANES):
                hp_ref[rs, c * LANES:(c + 1) * LANES] = hs[c, rs, :].astype(BF16)
    zu = [_bdot(h, wuv_ref[:, :GM_WIDTH]) for h in hb]
    zv = [_bdot(h, wuv_ref[:, GM_WIDTH:]) for h in hb]
    r = lax.broadcasted_iota(jnp.int32, (blk, blk), 0)
    c = lax.broadcasted_iota(jnp.int32, (blk, blk), 1)
    keep = c <= r
    if chunk < blk:
        sh = chunk.bit_length() - 1
        keep = jnp.logical_and(keep, (r >> sh) == (c >> sh))
    wsm = [jnp.where(keep, ws_ref[g], 0.0).astype(BF16) for g in range(GM_GROUPS)]
    for si, rs in enumerate(subs):
        u = _gelu(zu[si])
        v = _gelu(zv[si])
        mu = jnp.mean(v, axis=-1, keepdims=True)
        vc = v - mu
        var = jnp.mean(vc * vc, axis=-1, keepdims=True)
        vn = vc * lax.rsqrt(var + EPS) * lng_ref[...] + lnb_ref[...]
        if emit_v:
            vn_ref[rs, :] = vn
        vb = vn.astype(BF16)
        for g in range(GM_GROUPS):
            cs = slice(g * GM_GROUP_W, (g + 1) * GM_GROUP_W)
            for i in range(sub // blk):
                ls = slice(i * blk, (i + 1) * blk)
                s = _bdot(wsm[g], vb[ls, cs]) + bs_ref[:, cs]
                a_sc[rs.start + i * blk:rs.start + (i + 1) * blk, cs] = (u[ls, cs] * s).astype(BF16)
        pa_ref[rs, :] = _bdot(a_sc[rs, :], wpa_ref[...])


def _cast_block_spec(n_rows, n_cols, steps):
    n_blocks = steps
    while n_rows % n_blocks or (n_rows // n_blocks) % BF16_ROWS:
        n_blocks //= 2
    per = steps // n_blocks
    return pl.BlockSpec((n_rows // n_blocks, n_cols), lambda i: (i // per, 0))


def _gmlp_call(x, g1, w_all, lng, lnb, ws_t, bs_t, wpa, *, chunk, emit_v, emit_hperm, tm,
               cast_weights=()):
    m = x.shape[0]
    blk = ws_t.shape[1]
    steps = m // tm
    assert steps & (steps - 1) == 0
    row = pl.BlockSpec((tm, D_MODEL), lambda i: (i, 0))
    out_shape = [jax.ShapeDtypeStruct((m, D_MODEL), F32)]
    out_specs = [row]
    scratch = [pltpu.VMEM((tm, GM_WIDTH), BF16)]
    cast_specs = [_cast_block_spec(cw.shape[0], cw.shape[1], steps) for cw in cast_weights]
    if emit_v:
        out_shape.append(jax.ShapeDtypeStruct((m, GM_WIDTH), F32))
        out_specs.append(row)
    if emit_hperm:
        assert tm % PL == 0 and GMLP_SUB_ROWS == PL
        out_shape.append(jax.ShapeDtypeStruct((m, D_MODEL), BF16))
        out_specs.append(row)
        scratch.append(pltpu.VMEM((D_MODEL // LANES, tm, LANES), F32))
    out_shape += [jax.ShapeDtypeStruct(cw.shape, BF16) for cw in cast_weights]
    out_specs += cast_specs
    return pl.pallas_call(
        functools.partial(_gmlp_kernel, chunk=chunk, emit_v=emit_v, emit_hperm=emit_hperm,
                          n_cast=len(cast_weights)),
        grid=(steps,),
        in_specs=[row, _const_spec((1, D_MODEL)),
                  pl.BlockSpec((D_MODEL, 2 * GM_WIDTH), lambda i: (0, 0), pipeline_mode=pl.Buffered(1)),
                  _const_spec((1, GM_WIDTH)), _const_spec((1, GM_WIDTH)),
                  _const_spec((GM_GROUPS, blk, blk)), _const_spec((blk, GM_WIDTH)),
                  _const_spec((GM_WIDTH, D_MODEL))] + cast_specs,
        out_specs=out_specs,
        out_shape=out_shape,
        scratch_shapes=scratch,
        compiler_params=pltpu.CompilerParams(dimension_semantics=("arbitrary",),
                                             vmem_limit_bytes=VMEM_LIMIT),
        name="gmlp",
    )(x, g1, w_all, lng, lnb, ws_t, bs_t, wpa, *cast_weights)


def _ml_weight_specs():
    wide = 2 * ML_WIDTH
    assert (2 * GM_WIDTH) % wide == 0 and (2 * GM_WIDTH + 4 * ML_WIDTH) % GATE_COLS == 0
    first = 2 * GM_WIDTH // wide
    col_block = lambda width, idx: pl.BlockSpec((D_MODEL, width), lambda *_: (0, idx),
                                                pipeline_mode=pl.Buffered(1))
    return [col_block(wide, first), col_block(wide, first + 1),
            col_block(GATE_COLS, (2 * GM_WIDTH + 4 * ML_WIDTH) // GATE_COLS)]


def _inproj_kernel(x_ref, g1_ref, wqk_ref, wvo_ref, wif_ref, z_ref):
    h = _rms(x_ref[...], g1_ref[...]).astype(BF16)
    z_ref[:, 0:2 * ML_WIDTH] = _bdot(h, wqk_ref[...])
    z_ref[:, 2 * ML_WIDTH:4 * ML_WIDTH] = _bdot(h, wvo_ref[...])
    z_ref[:, 4 * ML_WIDTH:ML_COLS] = _bdot(h, wif_ref[...])


def _inproj_call(x, g1, w_all, *, tm):
    m = x.shape[0]
    return pl.pallas_call(
        _inproj_kernel,
        grid=(m // tm,),
        in_specs=[pl.BlockSpec((tm, D_MODEL), lambda i: (i, 0)), _const_spec((1, D_MODEL))]
        + _ml_weight_specs(),
        out_specs=pl.BlockSpec((tm, ML_COLS), lambda i: (i, 0)),
        out_shape=jax.ShapeDtypeStruct((m, ML_COLS), F32),
        compiler_params=pltpu.CompilerParams(dimension_semantics=("arbitrary",),
                                             vmem_limit_bytes=VMEM_LIMIT),
        name="inproj",
    )(x, g1, w_all, w_all, w_all)


def _mlstm_heads(q_of, k_of, v_of, o_of, causal, ipre, bcum, groups, m0_of, C0_of, n0_of, hng_ref,
                 fill=None):
    n_groups = len(groups)
    single = n_groups == 1
    fill = fill or (lambda: None)

    def rows_of(x, g):
        return x if single else x[groups[g][0]:groups[g][0] + groups[g][1]]

    def per_row(vals):
        if single:
            return vals[0]
        return jnp.concatenate([jnp.broadcast_to(v, (groups[g][1], v.shape[1]))
                                for g, v in enumerate(vals)], axis=0)

    a = ipre - bcum
    a_t = a.T
    m_rows = per_row([m0_of(g) for g in range(n_groups)])

    def prepare(h):
        p = {}
        a2 = jnp.where(causal, a_t[h:h + 1, :], -jnp.inf)
        p["mc"] = mc = jnp.maximum(jnp.max(a2, axis=1, keepdims=True), m_rows[:, h:h + 1])
        p["m_last"] = m_last = [mc[grp[2]:grp[2] + 1, :] for grp in groups]
        p["dm"] = jnp.exp(a2 - (mc + LN_INV_K_SCALE))
        p["w_inter"] = jnp.exp(m_rows[:, h:h + 1] - mc)
        p["w_col"] = w_col = jnp.exp(a[:, h:h + 1] - (per_row(m_last) + LN_INV_K_SCALE))
        p["decay"] = [jnp.exp(m0_of(g)[:, h:h + 1] - m_last[g]) for g in range(n_groups)]
        p["q"] = q = q_of(h)
        p["k"] = k = k_of(h)
        v = v_of(h)
        p["qb"], p["kb"], p["vb"] = q.astype(BF16), k.astype(BF16), v.astype(BF16)
        p["vw"] = (v * w_col).astype(BF16)
        p["c_old"] = [C0_of(g, h) for g in range(n_groups)]
        p["n_old"] = [n0_of(g, h) for g in range(n_groups)]
        return p

    def first_matmuls(p):
        p["qk"] = lax.dot_general(p["qb"], p["kb"], NT_DIMS, preferred_element_type=F32)
        p["qc"] = [lax.dot_general(rows_of(p["qb"], g), p["c_old"][g].astype(BF16), NT_DIMS,
                                   preferred_element_type=F32) for g in range(n_groups)]
        if single:
            n_rows = jnp.broadcast_to(p["n_old"][0], (LANES, ML_HEAD_DIM)).astype(BF16)
            p["qn"] = lax.dot_general(p["qb"], n_rows, NT_DIMS, preferred_element_type=F32)[:, 0:1]
        else:
            p["qn"] = jnp.sum(p["q"] * per_row(p["n_old"]), axis=1, keepdims=True)

    def second_matmuls(p):
        p["s"] = s = p["dm"] * p["qk"]
        p["sv"] = _bdot(s.astype(BF16), p["vb"])
        p["cupd"] = [lax.dot_general(rows_of(p["vw"], g), rows_of(p["kb"], g), TN_DIMS,
                                     preferred_element_type=F32) for g in range(n_groups)]

    def finish(h, p):
        qc_rows = p["qc"][0] if single else jnp.concatenate(p["qc"], axis=0)
        num = p["w_inter"] * qc_rows + p["sv"]
        den = p["w_inter"] * p["qn"] + jnp.sum(p["s"], axis=1, keepdims=True)
        hcur = num / jnp.maximum(jnp.abs(den), jnp.exp(-(bcum[:, h:h + 1] + p["mc"])))
        mu = jnp.mean(hcur, axis=1, keepdims=True)
        hc = hcur - mu
        var = jnp.mean(hc * hc, axis=1, keepdims=True)
        hs = slice(h * ML_HEAD_DIM, (h + 1) * ML_HEAD_DIM)
        out = jax.nn.sigmoid(o_of(h)) * (hc * lax.rsqrt(var + EPS) * hng_ref[:, hs])
        kw = p["k"] * p["w_col"]
        c_new = [p["decay"][g] * p["c_old"][g] + p["cupd"][g] for g in range(n_groups)]
        n_new = [p["decay"][g] * p["n_old"][g] + jnp.sum(rows_of(kw, g), axis=0, keepdims=True)
                 for g in range(n_groups)]
        return out, c_new, n_new

    H = ML_HEADS
    per_head = [None] * H
    done = [None] * H
    if single:
        per_head[0] = prepare(0)
        fill()
        first_matmuls(per_head[0])
        if H > 1:
            per_head[1] = prepare(1)
        fill()
        for h in range(H):
            second_matmuls(per_head[h])
            if h + 1 < H:
                first_matmuls(per_head[h + 1])
            fill()
            if h + 2 < H:
                per_head[h + 2] = prepare(h + 2)
            done[h] = finish(h, per_head[h])
            fill()
    else:
        per_head = [prepare(h) for h in range(H)]
        for stage in (first_matmuls, second_matmuls):
            for h in range(H):
                stage(per_head[h])
        done = [finish(h, per_head[h]) for h in range(H)]

    outs = [done[h][0] for h in range(H)]
    C_new = [[done[h][1][g] for h in range(H)] for g in range(n_groups)]
    n_new = [[done[h][2][g] for h in range(H)] for g in range(n_groups)]
    m_new = []
    for g, grp in enumerate(groups):
        row = m0_of(g)
        lane = lax.broadcasted_iota(jnp.int32, row.shape, 1)
        for h in range(H):
            row = jnp.where(lane == h, bcum[grp[2]:grp[2] + 1, h:h + 1] + per_head[h]["m_last"][g], row)
        m_new.append(row)
    return outs, C_new, n_new, m_new


def _mlstm_kernel(zq_ref, cst_ref, C0_ref, n0_ref, m0_ref, bif_ref, cw_ref, cb_ref, hng_ref,
                  bo_ref, conv_ref, C_ref, n_ref, m_ref, zp, xp, *, t_valid):
    nb = C0_ref.shape[0]
    L = BF16_ROWS
    R = nb * L

    @pl.when(pl.program_id(0) == 0)
    def _():
        zp[...] = jnp.zeros(zp.shape, F32)

    for bb in range(nb):
        zp[bb, 0:t_valid, :] = zq_ref[bb * t_valid:(bb + 1) * t_valid, :]
        for j in range(CONV_W - 1):
            row = SUBLANES - (CONV_W - 1) + j
            xp[bb, row:row + 1, :] = cst_ref[j, bb:bb + 1, :]
        xp[bb, SUBLANES:2 * SUBLANES, :] = zp[bb, 0:SUBLANES, 0:2 * ML_WIDTH]
        for j in range(CONV_W - 1):
            row = SUBLANES + t_valid - (CONV_W - 1) + j
            conv_ref[j, bb:bb + 1, :] = xp[bb, row:row + 1, :]
    qk_rows = []
    for bb in range(nb):
        acc = cb_ref[...]
        for j in range(CONV_W):
            off = SUBLANES - (CONV_W - 1) + j
            acc = acc + cw_ref[j:j + 1, :] * xp[bb, off:off + SUBLANES, :]
        qk_rows += [acc, jnp.zeros((L - SUBLANES, 2 * ML_WIDTH), F32)]
    qk = jnp.concatenate(qk_rows, axis=0)
    qk = qk * jax.nn.sigmoid(qk)

    def cols(c0, width):
        return zp[:, :, c0:c0 + width].reshape(R, width)

    zif = cols(4 * ML_WIDTH, GATE_COLS) + bif_ref[...]
    live = (lax.broadcasted_iota(jnp.int32, (R, GATE_COLS), 0) & (L - 1)) < t_valid
    ipre = jnp.where(live, zif, NEG_BIG)
    logf = jnp.where(live, pltpu.roll(_log_sigmoid(zif), GATE_COLS - ML_HEADS, axis=1), 0.0)
    r = lax.broadcasted_iota(jnp.int32, (R, R), 0)
    c = lax.broadcasted_iota(jnp.int32, (R, R), 1)
    sh = L.bit_length() - 1
    causal = jnp.logical_and(c <= r, (r >> sh) == (c >> sh))
    bcum = jnp.dot(causal.astype(F32), logf, preferred_element_type=F32,
                   precision=lax.Precision.HIGHEST)
    live_w = (lax.broadcasted_iota(jnp.int32, (R, ML_HEAD_DIM), 0) & (L - 1)) < t_valid

    def head_cols(x, base, h):
        return x[:, base + h * ML_HEAD_DIM:base + (h + 1) * ML_HEAD_DIM]

    groups = [(bb * L, L, bb * L + L - 1) for bb in range(nb)]
    outs, C_new, n_new, m_new = _mlstm_heads(
        q_of=lambda h: head_cols(qk, 0, h),
        k_of=lambda h: jnp.where(live_w, head_cols(qk, ML_WIDTH, h), 0.0),
        v_of=lambda h: jnp.where(live_w, cols(2 * ML_WIDTH + h * ML_HEAD_DIM, ML_HEAD_DIM), 0.0),
        o_of=lambda h: cols(3 * ML_WIDTH + h * ML_HEAD_DIM, ML_HEAD_DIM),
        causal=causal, ipre=ipre, bcum=bcum, groups=groups,
        m0_of=lambda g: m0_ref[g:g + 1, :], C0_of=lambda g, h: C0_ref[g, h],
        n0_of=lambda g, h: n0_ref[g, h:h + 1, :], hng_ref=hng_ref)
    for bb in range(nb):
        for h in range(ML_HEADS):
            hs = slice(h * ML_HEAD_DIM, (h + 1) * ML_HEAD_DIM)
            bo_ref[bb * t_valid:(bb + 1) * t_valid, hs] = outs[h][bb * L:bb * L + t_valid, :]
            C_ref[bb, h] = C_new[bb][h]
            n_ref[bb, h:h + 1, :] = n_new[bb][h]
        m_ref[bb:bb + 1, :] = m_new[bb]


def _mlstm_call(zq, cst, state, bif, cw, cb, hng, *, t_valid, nb):
    T = t_valid
    B = zq.shape[0] // T
    kern = functools.partial(_mlstm_kernel, t_valid=T)
    per_b = lambda shape: pl.BlockSpec((nb,) + shape, lambda b: (b,) + (0,) * len(shape))
    rows = lambda width: pl.BlockSpec((nb * T, width), lambda b: (b, 0))
    conv_spec = pl.BlockSpec((CONV_W - 1, nb, 2 * ML_WIDTH), lambda b: (0, b, 0))
    st_specs = [per_b((ML_HEADS, ML_HEAD_DIM, ML_HEAD_DIM)), per_b((ML_HEADS, ML_HEAD_DIM)),
                per_b((ML_HEADS,))]
    in_specs = ([rows(ML_COLS), conv_spec] + st_specs
                + [_const_spec((1, GATE_COLS)), _const_spec((CONV_W, 2 * ML_WIDTH)),
                   _const_spec((1, 2 * ML_WIDTH)), _const_spec((1, ML_WIDTH))])
    out_shape = [jax.ShapeDtypeStruct((B * T, ML_WIDTH), F32),
                 jax.ShapeDtypeStruct((CONV_W - 1, B, 2 * ML_WIDTH), F32),
                 jax.ShapeDtypeStruct((B, ML_HEADS, ML_HEAD_DIM, ML_HEAD_DIM), F32),
                 jax.ShapeDtypeStruct((B, ML_HEADS, ML_HEAD_DIM), F32),
                 jax.ShapeDtypeStruct((B, ML_HEADS), F32)]
    out_specs = [rows(ML_WIDTH), conv_spec] + st_specs
    return pl.pallas_call(
        kern,
        grid=(B // nb,),
        in_specs=in_specs,
        out_specs=out_specs,
        out_shape=out_shape,
        scratch_shapes=[pltpu.VMEM((nb, BF16_ROWS, ML_COLS), F32),
                        pltpu.VMEM((nb, 2 * SUBLANES, 2 * ML_WIDTH), F32)],
        compiler_params=pltpu.CompilerParams(dimension_semantics=("arbitrary",),
                                             vmem_limit_bytes=VMEM_LIMIT),
        name="mlstm",
    )(zq, cst, *state, bif, cw, cb, hng)


PL = 256
PG = PL // SUBLANES
TAIL = (CONV_W - 1) * SUBLANES
PROJ_PIECE_COLS = 512


def _perm_rows(j):
    return pl.ds((PL // 4) * (j % 4) + j // 4, SUBLANES, stride=SUBLANES)


def _mixer_ml_kernel(h_ref, wqk_ref, wvo_ref, wif_ref, bif_ref, cw_ref, cb_ref, hng_ref,
                     bo_ref, conv_ref, C_ref, n_ref, m_ref, zqk_buf, zvo_buf, zif_buf, tail,
                     *, chunks_per_seq):
    s = pl.program_id(0)

    @pl.when(s == 0)
    def _():
        zqk_buf[1] = jnp.zeros(zqk_buf.shape[1:], F32)
        zvo_buf[1] = jnp.zeros(zvo_buf.shape[1:], F32)
        zif_buf[1] = jnp.zeros(zif_buf.shape[1:], F32)

    @pl.when(jnp.maximum(s - 1, 0) % chunks_per_seq == 0)
    def _():
        tail[...] = jnp.zeros(tail.shape, F32)
        C_ref[...] = jnp.zeros(C_ref.shape, F32)
        n_ref[...] = jnp.zeros(n_ref.shape, F32)
        m_ref[...] = jnp.zeros(m_ref.shape, F32)

    step = functools.partial(_mixer_ml_step, h_ref, wqk_ref, wvo_ref, wif_ref, bif_ref, cw_ref,
                             cb_ref, hng_ref, bo_ref, conv_ref, C_ref, n_ref, m_ref,
                             zqk_buf, zvo_buf, zif_buf, tail)

    @pl.when(s % 2 == 0)
    def _():
        step(rd=1, wr=0)

    @pl.when(s % 2 == 1)
    def _():
        step(rd=0, wr=1)


def _mixer_ml_step(h_ref, wqk_ref, wvo_ref, wif_ref, bif_ref, cw_ref, cb_ref, hng_ref,
                   bo_ref, conv_ref, C_ref, n_ref, m_ref, zqk_buf, zvo_buf, zif_buf, tail,
                   *, rd, wr):
    h = h_ref[...]

    def piece(w_ref, buf, c0, width):
        def emit():
            buf[wr, :, c0:c0 + width] = _bdot(h, w_ref[:, c0:c0 + width])
        return emit

    pieces = [piece(w_ref, buf, c0, PROJ_PIECE_COLS)
              for w_ref, buf in ((wqk_ref, zqk_buf), (wvo_ref, zvo_buf))
              for c0 in range(0, 2 * ML_WIDTH, PROJ_PIECE_COLS)]
    pieces.append(piece(wif_ref, zif_buf, 0, GATE_COLS))
    pieces = iter(pieces)

    def fill():
        emit = next(pieces, None)
        if emit is not None:
            emit()

    fill()
    zqk = zqk_buf.at[rd]
    zvo = zvo_buf.at[rd]
    zif = zif_buf[rd] + bif_ref[...]

    zqk_tail = zqk[PL - TAIL:, :]
    sub = lax.broadcasted_iota(jnp.int32, (SUBLANES, 2 * ML_WIDTH), 0)
    wrapped = []
    for g in range(CONV_W - 1):
        cur = pltpu.roll(zqk_tail[g * SUBLANES:(g + 1) * SUBLANES], 1, axis=0)
        prev = pltpu.roll(tail[g * SUBLANES:(g + 1) * SUBLANES, :], 1, axis=0)
        wrapped.append(jnp.where(sub == 0, prev, cur))
    wrapped = jnp.concatenate(wrapped, axis=0)
    tail[...] = zqk_tail
    conv_ref[...] = jnp.concatenate(
        [zqk_tail[g * SUBLANES + SUBLANES - 1:(g + 1) * SUBLANES, :] for g in range(CONV_W - 1)], axis=0)

    def conv_silu(c0, width):
        cs = slice(c0, c0 + width)
        acc = cb_ref[:, cs] + cw_ref[CONV_W - 1:CONV_W, cs] * zqk[:, cs]
        for d in range(1, CONV_W):
            shifted = jnp.concatenate(
                [wrapped[TAIL - d * SUBLANES:, cs], zqk[:PL - d * SUBLANES, cs]], axis=0)
            acc = acc + cw_ref[CONV_W - 1 - d:CONV_W - d, cs] * shifted
        return acc * jax.nn.sigmoid(acc)

    pr = lax.broadcasted_iota(jnp.int32, (PL, PL), 0)
    pc = lax.broadcasted_iota(jnp.int32, (PL, PL), 1)
    causal = ((pc >> 3) + PG * (pc & 7)) <= ((pr >> 3) + PG * (pr & 7))

    logf = pltpu.roll(_log_sigmoid(zif), GATE_COLS - ML_HEADS, axis=1)
    run, partial = None, []
    for n in range(PG):
        blk = logf[n * SUBLANES:(n + 1) * SUBLANES, :]
        run = blk if run is None else run + blk
        partial.append(run)
    sub_g = lax.broadcasted_iota(jnp.int32, (SUBLANES, GATE_COLS), 0)
    incl = run
    for step in (1, 2, 4):
        incl = incl + jnp.where(sub_g >= step, pltpu.roll(incl, step, axis=0), 0.0)
    earlier = incl - run
    bcum = jnp.concatenate([p + earlier for p in partial], axis=0)

    def head_cols(base, h):
        return zvo[:, base + h * ML_HEAD_DIM:base + (h + 1) * ML_HEAD_DIM]

    outs, C_new, n_new, m_new = _mlstm_heads(
        q_of=lambda h: conv_silu(h * ML_HEAD_DIM, ML_HEAD_DIM),
        k_of=lambda h: conv_silu(ML_WIDTH + h * ML_HEAD_DIM, ML_HEAD_DIM),
        v_of=lambda h: head_cols(0, h),
        o_of=lambda h: head_cols(ML_WIDTH, h),
        causal=causal, ipre=zif, bcum=bcum, groups=[(0, PL, PL - 1)],
        m0_of=lambda g: m_ref[...], C0_of=lambda g, h: C_ref[h], n0_of=lambda g, h: n_ref[h:h + 1, :],
        hng_ref=hng_ref, fill=fill)
    while next(pieces, None) is not None:
        raise AssertionError("projection pieces left over")
    for h in range(ML_HEADS):
        bo_ref[:, h * ML_HEAD_DIM:(h + 1) * ML_HEAD_DIM] = outs[h].astype(BF16)
        C_ref[h] = C_new[0][h]
        n_ref[h:h + 1, :] = n_new[0][h]
    m_ref[...] = m_new[0]


def _mixer_ml_call(hp, w_all, bif, cw, cb, hng):
    B, T, _ = hp.shape
    cps = T // PL
    n_chunks = B * cps
    proj = lambda s: jnp.minimum(s, n_chunks - 1)
    math = lambda s: jnp.maximum(s - 1, 0)
    per_b = lambda shape: pl.BlockSpec((None,) + shape,
                                       lambda s: (math(s) // cps,) + (0,) * len(shape))
    out_shape = [jax.ShapeDtypeStruct((B, T, ML_WIDTH), BF16),
                 jax.ShapeDtypeStruct((B, CONV_W - 1, 2 * ML_WIDTH), F32),
                 jax.ShapeDtypeStruct((B, ML_HEADS, ML_HEAD_DIM, ML_HEAD_DIM), F32),
                 jax.ShapeDtypeStruct((B, ML_HEADS, ML_HEAD_DIM), F32),
                 jax.ShapeDtypeStruct((B, 1, ML_HEADS), F32)]
    out_specs = [pl.BlockSpec((None, PL, ML_WIDTH), lambda s: (math(s) // cps, math(s) % cps, 0)),
                 per_b((CONV_W - 1, 2 * ML_WIDTH)),
                 per_b((ML_HEADS, ML_HEAD_DIM, ML_HEAD_DIM)), per_b((ML_HEADS, ML_HEAD_DIM)),
                 per_b((1, ML_HEADS))]
    return pl.pallas_call(
        functools.partial(_mixer_ml_kernel, chunks_per_seq=cps),
        grid=(n_chunks + 1,),
        in_specs=[pl.BlockSpec((None, PL, D_MODEL), lambda s: (proj(s) // cps, proj(s) % cps, 0))]
        + _ml_weight_specs()
        + [_const_spec((1, GATE_COLS)), _const_spec((CONV_W, 2 * ML_WIDTH)),
           _const_spec((1, 2 * ML_WIDTH)), _const_spec((1, ML_WIDTH))],
        out_specs=out_specs,
        out_shape=out_shape,
        scratch_shapes=[pltpu.VMEM((2, PL, 2 * ML_WIDTH), F32),
                        pltpu.VMEM((2, PL, 2 * ML_WIDTH), F32),
                        pltpu.VMEM((2, PL, GATE_COLS), F32),
                        pltpu.VMEM((TAIL, 2 * ML_WIDTH), F32)],
        compiler_params=pltpu.CompilerParams(dimension_semantics=("arbitrary",),
                                             vmem_limit_bytes=VMEM_LIMIT),
        name="mixer_ml",
    )(hp, w_all, w_all, w_all, bif, cw, cb, hng)


def _merge_ffn_kernel(x_ref, pa_ref, bo_ref, g1_ref, wg_ref, bg_ref, wpb_ref, wout_ref,
                      g2_ref, wfi_ref, wfo_ref, gf_ref, y_ref, *scratch, bo_permuted):
    tm = x_ref.shape[0]
    subs = [slice(r0, r0 + FFN_SUB_ROWS) for r0 in range(0, tm, FFN_SUB_ROWS)]
    x = [x_ref[rs, :] for rs in subs]
    h = [_rms(xi, g1_ref[...]).astype(BF16) for xi in x]
    gab = [_bdot(hi, wg_ref[...]) + bg_ref[...] for hi in h]
    pb = [_bdot(bo_ref[rs, :].astype(BF16), wpb_ref[...]) for rs in subs]
    if bo_permuted:
        (us,) = scratch
        for si, rs in enumerate(subs):
            for c in range(D_MODEL // LANES):
                us[c, rs, :] = pb[si][:, c * LANES:(c + 1) * LANES]
            pb[si] = jnp.concatenate(
                [jnp.concatenate([us[c, pl.ds(rs.start + _perm_rows(j).start, SUBLANES, stride=SUBLANES), :]
                                  for j in range(PG)], axis=0)
                 for c in range(D_MODEL // LANES)], axis=1)
    merged = [(jax.nn.sigmoid(g[:, :D_MODEL]) * pa_ref[rs, :]
               + jax.nn.sigmoid(g[:, D_MODEL:]) * p).astype(BF16) for g, p, rs in zip(gab, pb, subs)]
    x1 = [xi + _bdot(mi, wout_ref[...]) for xi, mi in zip(x, merged)]
    h2 = [_rms(xi, g2_ref[...]).astype(BF16) for xi in x1]
    gu = [_bdot(hi, wfi_ref[...]) for hi in h2]
    hid = [(g[:, :D_FF] * jax.nn.sigmoid(g[:, :D_FF]) * g[:, D_FF:]).astype(BF16) for g in gu]
    x2 = [xi + _bdot(hi, wfo_ref[...]) for xi, hi in zip(x1, hid)]
    for rs, xi in zip(subs, x2):
        y_ref[rs, :] = _rms(xi, gf_ref[...])


def _merge_ffn_call(x, pa, bo, g1, wg, bg, wpb, wout, g2, wfi, wfo, gf, *, tm, bo_permuted):
    m = x.shape[0]
    row = pl.BlockSpec((tm, D_MODEL), lambda i: (i, 0))
    assert not bo_permuted or FFN_SUB_ROWS == PL
    return pl.pallas_call(
        functools.partial(_merge_ffn_kernel, bo_permuted=bo_permuted),
        grid=(m // tm,),
        in_specs=[row, row, row, _const_spec((1, D_MODEL)), _const_spec((D_MODEL, 2 * D_MODEL)),
                  _const_spec((1, 2 * D_MODEL)), _const_spec((ML_WIDTH, D_MODEL)),
                  _const_spec((D_MODEL, D_MODEL)), _const_spec((1, D_MODEL)),
                  _const_spec((D_MODEL, 2 * D_FF)), _const_spec((D_FF, D_MODEL)),
                  _const_spec((1, D_MODEL))],
        out_specs=row,
        out_shape=jax.ShapeDtypeStruct((m, D_MODEL), F32),
        scratch_shapes=[pltpu.VMEM((D_MODEL // LANES, tm, LANES), F32)] if bo_permuted else [],
        compiler_params=pltpu.CompilerParams(dimension_semantics=("arbitrary",),
                                             vmem_limit_bytes=VMEM_LIMIT),
        name="merge_ffn",
    )(x, pa, bo, g1, wg, bg, wpb, wout, g2, wfi, wfo, gf)


SAMPLE_SEQS_PER_STEP = 8
TM_MIX = 512
TM_GMLP = 512
TM_FFN = 512


def _spatial_tiles(w_s, b_s, chunk):
    if chunk == GM_CHUNK:
        ws_t, b_pos = w_s[:, :chunk, :chunk], b_s[:, :chunk].T
    else:
        onehot = jnp.asarray(np.arange(GM_CHUNK)[:, None] % chunk == np.arange(chunk)[None, :], F32)
        hp = lax.Precision.HIGHEST
        ws_t = jnp.einsum("ri,gij,cj->grc", onehot, w_s[:, :chunk, :chunk], onehot, precision=hp)
        b_pos = jnp.dot(onehot, b_s[:, :chunk].T, precision=hp)
    bs_t = jnp.repeat(b_pos, GM_GROUP_W, axis=1)
    return ws_t, bs_t


def _gmlp_branch(xf, w, chunk, *, emit_v, emit_hperm, cast_weights=()):
    ws_t, bs_t = _spatial_tiles(w["w_s"], w["b_s"], chunk)
    return _gmlp_call(xf, w["g1"], w["w_all"], w["lng"], w["lnb"], ws_t, bs_t, w["wpa"],
                      chunk=chunk, emit_v=emit_v, emit_hperm=emit_hperm,
                      tm=min(TM_GMLP, xf.shape[0]), cast_weights=cast_weights)


def _merge_branch(xf, pa, bo, w, *, bo_permuted):
    return _merge_ffn_call(xf, pa, bo, w["g1"], w["wg"], w["bg"], w["wpb"], w["wout"], w["g2"],
                           w["wfi"], w["wfo"], w["gf"], tm=TM_FFN, bo_permuted=bo_permuted)


def kernel(x_prompt, x_sample, state_conv, state_C, state_n, state_m, g_norm1, w_in, b_i, b_f, ln_g, ln_b, w_s, b_s, conv_w, conv_b, hn_g, b_gate, w_proj_a, w_proj_b, w_out, g_norm2, w_ffn_in, w_ffn_out, g_final):
    Bp, Tp, _ = x_prompt.shape
    Bs, Ts, _ = x_sample.shape
    win = w_in[0]
    c_if = 2 * GM_WIDTH + 4 * ML_WIDTH + 2 * ML_HEADS
    w = dict(
        g1=g_norm1[0][None], g2=g_norm2[0][None], gf=g_final[None],
        w_all=win.astype(BF16),
        wg=win[:, c_if:].astype(BF16),
        bg=b_gate[0].reshape(1, 2 * D_MODEL),
        lng=ln_g[0][None], lnb=ln_b[0][None], w_s=w_s[0], b_s=b_s[0],
        bif=jnp.pad(jnp.concatenate([b_i[0], b_f[0]]), (0, GATE_COLS - 2 * ML_HEADS))[None],
        cw=conv_w[0], cb=conv_b[0][None], hng=hn_g[0][None],
        wpa=w_proj_a[0].astype(BF16),
    )

    xpf = x_prompt.reshape(Bp * Tp, D_MODEL)
    pa_p, hp_p, w["wpb"], w["wout"], w["wfi"], w["wfo"] = _gmlp_branch(
        xpf, w, GM_CHUNK, emit_v=False, emit_hperm=True,
        cast_weights=(w_proj_b[0], w_out[0], w_ffn_in[0], w_ffn_out[0]))
    bo_p, conv_p, C_p, n_p, m_p = _mixer_ml_call(hp_p.reshape(Bp, Tp, D_MODEL), w["w_all"], w["bif"],
                                                 w["cw"], w["cb"], w["hng"])
    y_p = _merge_branch(xpf, pa_p, bo_p.reshape(Bp * Tp, ML_WIDTH), w, bo_permuted=True)

    xsf = x_sample.reshape(Bs * Ts, D_MODEL)
    pa_s, vn_s = _gmlp_branch(xsf, w, Ts, emit_v=True, emit_hperm=False)
    zq = _inproj_call(xsf, w["g1"], w["w_all"], tm=TM_MIX)
    st = (state_C[0], state_n[0], state_m[0])
    bo_s, conv_s, C_s, n_s, m_s = _mlstm_call(zq, jnp.transpose(state_conv[0], (1, 0, 2)), st, w["bif"],
                                              w["cw"], w["cb"], w["hng"], t_valid=Ts,
                                              nb=SAMPLE_SEQS_PER_STEP)
    y_s = _merge_branch(xsf, pa_s, bo_s, w, bo_permuted=False)

    return (y_p.reshape(Bp, Tp, D_MODEL), y_s.reshape(Bs, Ts, D_MODEL),
            conv_p[None], C_p[None], n_p[None], m_p.reshape(1, Bp, ML_HEADS),
            jnp.transpose(conv_s, (1, 0, 2))[None], C_s[None], n_s[None], m_s[None],
            vn_s.reshape(1, Bs, Ts, GM_WIDTH))
```

```python
import functools
import math

import jax
import jax.numpy as jnp
import numpy as np
from jax import lax
from jax.experimental import pallas as pl
from jax.experimental.pallas import tpu as pltpu

D_MODEL = 1024
GM_WIDTH = D_MODEL
GM_GROUPS = 4
GM_GROUP_W = GM_WIDTH // GM_GROUPS
GM_CHUNK = 128
ML_HEADS = 4
ML_HEAD_DIM = D_MODEL // ML_HEADS
ML_WIDTH = ML_HEADS * ML_HEAD_DIM
CONV_W = 4
D_FF = 2816
EPS = 1e-6

LANES = 128
SUBLANES = 8
BF16_ROWS = 16
GATE_COLS = LANES
ML_COLS = 4 * ML_WIDTH + GATE_COLS
VMEM_LIMIT = 56 * 1024 * 1024
GMLP_SUB_ROWS = 256
FFN_SUB_ROWS = 256

F32 = jnp.float32
BF16 = jnp.bfloat16
NEG_BIG = -1e30
LN_INV_K_SCALE = 0.5 * math.log(ML_HEAD_DIM)
NT_DIMS = (((1,), (1,)), ((), ()))
TN_DIMS = (((0,), (0,)), ((), ()))


def _rms(x, g):
    return x * lax.rsqrt(jnp.mean(x * x, axis=-1, keepdims=True) + EPS) * g


def _gelu(x):
    return 0.5 * x * (1.0 + lax.erf(x * (2.0 ** -0.5)))


def _log_sigmoid(x):
    return jnp.minimum(x, 0.0) - jnp.log1p(jnp.exp(-jnp.abs(x)))


def _bdot(a, b):
    return jnp.dot(a, b, preferred_element_type=F32)


def _const_spec(shape):
    nd = len(shape)
    return pl.BlockSpec(shape, lambda *_: (0,) * nd, pipeline_mode=pl.Buffered(1))


def _gmlp_kernel(x_ref, g1_ref, wuv_ref, lng_ref, lnb_ref, ws_ref, bs_ref, wpa_ref, *rest,
                 chunk, emit_v, emit_hperm, n_cast, n_tcast):
    rest = list(rest)
    cast_src = [rest.pop(0) for _ in range(n_cast)]
    tcast_src = [rest.pop(0) for _ in range(n_tcast)]
    pa_ref = rest.pop(0)
    vn_ref = rest.pop(0) if emit_v else None
    hp_ref = rest.pop(0) if emit_hperm else None
    for src in cast_src:
        rest.pop(0)[...] = src[...].astype(BF16)
    for src in tcast_src:
        rest.pop(0)[...] = src[...].T.astype(BF16)
    a_sc = rest.pop(0)
    tm = x_ref.shape[0]
    blk = ws_ref.shape[1]
    hs = rest.pop(0) if emit_hperm else None
    sub = GMLP_SUB_ROWS
    subs = [slice(r0, r0 + sub) for r0 in range(0, tm, sub)]
    hb = []
    for rs in subs:
        hf = _rms(x_ref[rs, :], g1_ref[...])
        hb.append(hf.astype(BF16))
        if emit_hperm:
            for j in range(rs.start // SUBLANES, rs.stop // SUBLANES):
                base = (j // PG) * PL
                dst = _perm_rows(j % PG)
                lo = j * SUBLANES - rs.start
                for c in range(D_MODEL // LANES):
                    hs[c, pl.ds(base + dst.start, SUBLANES, stride=SUBLANES), :] = (
                        hf[lo:lo + SUBLANES, c * LANES:(c + 1) * LANES])
            for c in range(D_MODEL // LANES):
                hp_ref[rs, c * LANES:(c + 1) * LANES] = hs[c, rs, :].astype(BF16)
    zu = [_bdot(h, wuv_ref[:, :GM_WIDTH]) for h in hb]
    zv = [_bdot(h, wuv_ref[:, GM_WIDTH:]) for h in hb]
    r = lax.broadcasted_iota(jnp.int32, (blk, blk), 0)
    c = lax.broadcasted_iota(jnp.int32, (blk, blk), 1)
    keep = c <= r
    if chunk < blk:
        sh = chunk.bit_length() - 1
        keep = jnp.logical_and(keep, (r >> sh) == (c >> sh))
    wsm = [jnp.where(keep, ws_ref[g], 0.0).astype(BF16) for g in range(GM_GROUPS)]
    for si, rs in enumerate(subs):
        u = _gelu(zu[si])
        v = _gelu(zv[si])
        mu = jnp.mean(v, axis=-1, keepdims=True)
        vc = v - mu
        var = jnp.mean(vc * vc, axis=-1, keepdims=True)
        vn = vc * lax.rsqrt(var + EPS) * lng_ref[...] + lnb_ref[...]
        if emit_v:
            vn_ref[rs, :] = vn
        vb = vn.astype(BF16)
        for g in range(GM_GROUPS):
            cs = slice(g * GM_GROUP_W, (g + 1) * GM_GROUP_W)
            for i in range(sub // blk):
                ls = slice(i * blk, (i + 1) * blk)
                s = _bdot(wsm[g], vb[ls, cs]) + bs_ref[:, cs]
                a_sc[rs.start + i * blk:rs.start + (i + 1) * blk, cs] = (u[ls, cs] * s).astype(BF16)
        pa_ref[rs, :] = _bdot(a_sc[rs, :], wpa_ref[...])


def _cast_block_spec(n_rows, n_cols, steps):
    n_blocks = steps
    while n_rows % n_blocks or (n_rows // n_blocks) % BF16_ROWS:
        n_blocks //= 2
    per = steps // n_blocks
    return pl.BlockSpec((n_rows // n_blocks, n_cols), lambda i: (i // per, 0))


TCAST_ROWS = 256


def _tcast_specs(job):
    first_row, n_blocks, first_step = job
    assert first_row % SUBLANES == 0
    blk = lambda i: jnp.clip(i - first_step, 0, n_blocks - 1)
    src = pl.BlockSpec((pl.Element(TCAST_ROWS), pl.Element(D_MODEL)),
                       lambda i: (pl.multiple_of(first_row + TCAST_ROWS * blk(i), SUBLANES), 0))
    dst = pl.BlockSpec((D_MODEL, TCAST_ROWS), lambda i: (0, blk(i)))
    return src, dst, jax.ShapeDtypeStruct((D_MODEL, TCAST_ROWS * n_blocks), BF16)


def _gmlp_call(x, g1, wuv, lng, lnb, ws_t, bs_t, wpa, *, chunk, emit_v, emit_hperm, tm,
               cast_weights=(), tcast_weight=None, tcast_jobs=()):
    m = x.shape[0]
    blk = ws_t.shape[1]
    steps = m // tm
    assert steps & (steps - 1) == 0
    assert all(first_step + n_blocks <= steps for _, n_blocks, first_step in tcast_jobs)
    row = pl.BlockSpec((tm, D_MODEL), lambda i: (i, 0))
    out_shape = [jax.ShapeDtypeStruct((m, D_MODEL), F32)]
    out_specs = [row]
    scratch = [pltpu.VMEM((tm, GM_WIDTH), BF16)]
    cast_specs = [_cast_block_spec(cw.shape[0], cw.shape[1], steps) for cw in cast_weights]
    tcast = [_tcast_specs(job) for job in tcast_jobs]
    if emit_v:
        out_shape.append(jax.ShapeDtypeStruct((m, GM_WIDTH), F32))
        out_specs.append(row)
    if emit_hperm:
        assert tm % PL == 0 and GMLP_SUB_ROWS == PL
        out_shape.append(jax.ShapeDtypeStruct((m, D_MODEL), BF16))
        out_specs.append(row)
        scratch.append(pltpu.VMEM((D_MODEL // LANES, tm, LANES), F32))
    out_shape += [jax.ShapeDtypeStruct(cw.shape, BF16) for cw in cast_weights]
    out_shape += [t[2] for t in tcast]
    out_specs += cast_specs + [t[1] for t in tcast]
    return pl.pallas_call(
        functools.partial(_gmlp_kernel, chunk=chunk, emit_v=emit_v, emit_hperm=emit_hperm,
                          n_cast=len(cast_weights), n_tcast=len(tcast)),
        grid=(steps,),
        in_specs=[row, _const_spec((1, D_MODEL)), _const_spec((D_MODEL, 2 * GM_WIDTH)),
                  _const_spec((1, GM_WIDTH)), _const_spec((1, GM_WIDTH)),
                  _const_spec((GM_GROUPS, blk, blk)), _const_spec((blk, GM_WIDTH)),
                  _const_spec((GM_WIDTH, D_MODEL))] + cast_specs + [t[0] for t in tcast],
        out_specs=out_specs,
        out_shape=out_shape,
        scratch_shapes=scratch,
        compiler_params=pltpu.CompilerParams(dimension_semantics=("arbitrary",),
                                             vmem_limit_bytes=VMEM_LIMIT),
        name="gmlp",
    )(x, g1, wuv, lng, lnb, ws_t, bs_t, wpa, *cast_weights, *([tcast_weight] * len(tcast)))


def _ml_weight_specs():
    wide = 2 * ML_WIDTH
    col_block = lambda width, idx: pl.BlockSpec((D_MODEL, width), lambda *_: (0, idx),
                                                pipeline_mode=pl.Buffered(1))
    return [col_block(wide, 0), col_block(wide, 1), col_block(GATE_COLS, 4 * ML_WIDTH // GATE_COLS)]


def _inproj_kernel(x_ref, g1_ref, wqk_ref, wvo_ref, wif_ref, z_ref):
    h = _rms(x_ref[...], g1_ref[...]).astype(BF16)
    z_ref[:, 0:2 * ML_WIDTH] = _bdot(h, wqk_ref[...])
    z_ref[:, 2 * ML_WIDTH:4 * ML_WIDTH] = _bdot(h, wvo_ref[...])
    z_ref[:, 4 * ML_WIDTH:ML_COLS] = _bdot(h, wif_ref[...])


def _inproj_call(x, g1, w_all, *, tm):
    m = x.shape[0]
    return pl.pallas_call(
        _inproj_kernel,
        grid=(m // tm,),
        in_specs=[pl.BlockSpec((tm, D_MODEL), lambda i: (i, 0)), _const_spec((1, D_MODEL))]
        + _ml_weight_specs(),
        out_specs=pl.BlockSpec((tm, ML_COLS), lambda i: (i, 0)),
        out_shape=jax.ShapeDtypeStruct((m, ML_COLS), F32),
        compiler_params=pltpu.CompilerParams(dimension_semantics=("arbitrary",),
                                             vmem_limit_bytes=VMEM_LIMIT),
        name="inproj",
    )(x, g1, w_all, w_all, w_all)


def _mlstm_heads(q_of, k_of, v_of, o_of, causal, ipre, bcum, groups, m0_of, C0_of, n0_of, hng_ref,
                 fill=None):
    n_groups = len(groups)
    single = n_groups == 1
    fill = fill or (lambda: None)

    def rows_of(x, g):
        return x if single else x[groups[g][0]:groups[g][0] + groups[g][1]]

    def per_row(vals):
        if single:
            return vals[0]
        return jnp.concatenate([jnp.broadcast_to(v, (groups[g][1], v.shape[1]))
                                for g, v in enumerate(vals)], axis=0)

    a = ipre - bcum
    a_t = a.T
    m_rows = per_row([m0_of(g) for g in range(n_groups)])

    def prepare(h):
        p = {}
        a2 = jnp.where(causal, a_t[h:h + 1, :], -jnp.inf)
        p["mc"] = mc = jnp.maximum(jnp.max(a2, axis=1, keepdims=True), m_rows[:, h:h + 1])
        p["m_last"] = m_last = [mc[grp[2]:grp[2] + 1, :] for grp in groups]
        p["dm"] = jnp.exp(a2 - (mc + LN_INV_K_SCALE))
        p["w_inter"] = jnp.exp(m_rows[:, h:h + 1] - mc)
        p["w_col"] = w_col = jnp.exp(a[:, h:h + 1] - (per_row(m_last) + LN_INV_K_SCALE))
        p["decay"] = [jnp.exp(m0_of(g)[:, h:h + 1] - m_last[g]) for g in range(n_groups)]
        p["q"] = q = q_of(h)
        p["k"] = k = k_of(h)
        v = v_of(h)
        p["qb"], p["kb"], p["vb"] = q.astype(BF16), k.astype(BF16), v.astype(BF16)
        p["vw"] = (v * w_col).astype(BF16)
        p["c_old"] = [C0_of(g, h) for g in range(n_groups)]
        p["n_old"] = [n0_of(g, h) for g in range(n_groups)]
        return p

    def first_matmuls(p):
        p["qk"] = lax.dot_general(p["qb"], p["kb"], NT_DIMS, preferred_element_type=F32)
        p["qc"] = [lax.dot_general(rows_of(p["qb"], g), p["c_old"][g].astype(BF16), NT_DIMS,
                                   preferred_element_type=F32) for g in range(n_groups)]
        if single:
            n_rows = jnp.broadcast_to(p["n_old"][0], (LANES, ML_HEAD_DIM)).astype(BF16)
            p["qn"] = lax.dot_general(p["qb"], n_rows, NT_DIMS, preferred_element_type=F32)[:, 0:1]
        else:
            p["qn"] = jnp.sum(p["q"] * per_row(p["n_old"]), axis=1, keepdims=True)

    def second_matmuls(p):
        p["s"] = s = p["dm"] * p["qk"]
        p["sv"] = _bdot(s.astype(BF16), p["vb"])
        p["cupd"] = [lax.dot_general(rows_of(p["vw"], g), rows_of(p["kb"], g), TN_DIMS,
                                     preferred_element_type=F32) for g in range(n_groups)]

    def finish(h, p):
        qc_rows = p["qc"][0] if single else jnp.concatenate(p["qc"], axis=0)
        num = p["w_inter"] * qc_rows + p["sv"]
        den = p["w_inter"] * p["qn"] + jnp.sum(p["s"], axis=1, keepdims=True)
        hcur = num / jnp.maximum(jnp.abs(den), jnp.exp(-(bcum[:, h:h + 1] + p["mc"])))
        mu = jnp.mean(hcur, axis=1, keepdims=True)
        hc = hcur - mu
        var = jnp.mean(hc * hc, axis=1, keepdims=True)
        hs = slice(h * ML_HEAD_DIM, (h + 1) * ML_HEAD_DIM)
        out = jax.nn.sigmoid(o_of(h)) * (hc * lax.rsqrt(var + EPS) * hng_ref[:, hs])
        kw = p["k"] * p["w_col"]
        c_new = [p["decay"][g] * p["c_old"][g] + p["cupd"][g] for g in range(n_groups)]
        n_new = [p["decay"][g] * p["n_old"][g] + jnp.sum(rows_of(kw, g), axis=0, keepdims=True)
                 for g in range(n_groups)]
        return out, c_new, n_new

    H = ML_HEADS
    per_head = [None] * H
    done = [None] * H
    if single:
        per_head[0] = prepare(0)
        fill()
        first_matmuls(per_head[0])
        if H > 1:
            per_head[1] = prepare(1)
        fill()
        for h in range(H):
            second_matmuls(per_head[h])
            if h + 1 < H:
                first_matmuls(per_head[h + 1])
            fill()
            if h + 2 < H:
                per_head[h + 2] = prepare(h + 2)
            done[h] = finish(h, per_head[h])
            fill()
    else:
        per_head = [prepare(h) for h in range(H)]
        for stage in (first_matmuls, second_matmuls):
            for h in range(H):
                stage(per_head[h])
        done = [finish(h, per_head[h]) for h in range(H)]

    outs = [done[h][0] for h in range(H)]
    C_new = [[done[h][1][g] for h in range(H)] for g in range(n_groups)]
    n_new = [[done[h][2][g] for h in range(H)] for g in range(n_groups)]
    m_new = []
    for g, grp in enumerate(groups):
        row = m0_of(g)
        lane = lax.broadcasted_iota(jnp.int32, row.shape, 1)
        for h in range(H):
            row = jnp.where(lane == h, bcum[grp[2]:grp[2] + 1, h:h + 1] + per_head[h]["m_last"][g], row)
        m_new.append(row)
    return outs, C_new, n_new, m_new


def _mlstm_kernel(zq_ref, cst_ref, C0_ref, n0_ref, m0_ref, bif_ref, cw_ref, cb_ref, hng_ref,
                  bo_ref, conv_ref, C_ref, n_ref, m_ref, zp, xp, *, t_valid):
    nb = C0_ref.shape[0]
    L = BF16_ROWS
    R = nb * L

    @pl.when(pl.program_id(0) == 0)
    def _():
        zp[...] = jnp.zeros(zp.shape, F32)

    for bb in range(nb):
        zp[bb, 0:t_valid, :] = zq_ref[bb * t_valid:(bb + 1) * t_valid, :]
        for j in range(CONV_W - 1):
            row = SUBLANES - (CONV_W - 1) + j
            xp[bb, row:row + 1, :] = cst_ref[j, bb:bb + 1, :]
        xp[bb, SUBLANES:2 * SUBLANES, :] = zp[bb, 0:SUBLANES, 0:2 * ML_WIDTH]
        for j in range(CONV_W - 1):
            row = SUBLANES + t_valid - (CONV_W - 1) + j
            conv_ref[j, bb:bb + 1, :] = xp[bb, row:row + 1, :]
    qk_rows = []
    for bb in range(nb):
        acc = cb_ref[...]
        for j in range(CONV_W):
            off = SUBLANES - (CONV_W - 1) + j
            acc = acc + cw_ref[j:j + 1, :] * xp[bb, off:off + SUBLANES, :]
        qk_rows += [acc, jnp.zeros((L - SUBLANES, 2 * ML_WIDTH), F32)]
    qk = jnp.concatenate(qk_rows, axis=0)
    qk = qk * jax.nn.sigmoid(qk)

    def cols(c0, width):
        return zp[:, :, c0:c0 + width].reshape(R, width)

    zif = cols(4 * ML_WIDTH, GATE_COLS) + bif_ref[...]
    live = (lax.broadcasted_iota(jnp.int32, (R, GATE_COLS), 0) & (L - 1)) < t_valid
    ipre = jnp.where(live, zif, NEG_BIG)
    logf = jnp.where(live, pltpu.roll(_log_sigmoid(zif), GATE_COLS - ML_HEADS, axis=1), 0.0)
    r = lax.broadcasted_iota(jnp.int32, (R, R), 0)
    c = lax.broadcasted_iota(jnp.int32, (R, R), 1)
    sh = L.bit_length() - 1
    causal = jnp.logical_and(c <= r, (r >> sh) == (c >> sh))
    bcum = jnp.dot(causal.astype(F32), logf, preferred_element_type=F32,
                   precision=lax.Precision.HIGHEST)
    live_w = (lax.broadcasted_iota(jnp.int32, (R, ML_HEAD_DIM), 0) & (L - 1)) < t_valid

    def head_cols(x, base, h):
        return x[:, base + h * ML_HEAD_DIM:base + (h + 1) * ML_HEAD_DIM]

    groups = [(bb * L, L, bb * L + L - 1) for bb in range(nb)]
    outs, C_new, n_new, m_new = _mlstm_heads(
        q_of=lambda h: head_cols(qk, 0, h),
        k_of=lambda h: jnp.where(live_w, head_cols(qk, ML_WIDTH, h), 0.0),
        v_of=lambda h: jnp.where(live_w, cols(2 * ML_WIDTH + h * ML_HEAD_DIM, ML_HEAD_DIM), 0.0),
        o_of=lambda h: cols(3 * ML_WIDTH + h * ML_HEAD_DIM, ML_HEAD_DIM),
        causal=causal, ipre=ipre, bcum=bcum, groups=groups,
        m0_of=lambda g: m0_ref[g:g + 1, :], C0_of=lambda g, h: C0_ref[g, h],
        n0_of=lambda g, h: n0_ref[g, h:h + 1, :], hng_ref=hng_ref)
    for bb in range(nb):
        for h in range(ML_HEADS):
            hs = slice(h * ML_HEAD_DIM, (h + 1) * ML_HEAD_DIM)
            bo_ref[bb * t_valid:(bb + 1) * t_valid, hs] = outs[h][bb * L:bb * L + t_valid, :]
            C_ref[bb, h] = C_new[bb][h]
            n_ref[bb, h:h + 1, :] = n_new[bb][h]
        m_ref[bb:bb + 1, :] = m_new[bb]


def _mlstm_call(zq, cst, state, bif, cw, cb, hng, *, t_valid, nb):
    T = t_valid
    B = zq.shape[0] // T
    kern = functools.partial(_mlstm_kernel, t_valid=T)
    per_b = lambda shape: pl.BlockSpec((nb,) + shape, lambda b: (b,) + (0,) * len(shape))
    rows = lambda width: pl.BlockSpec((nb * T, width), lambda b: (b, 0))
    conv_spec = pl.BlockSpec((CONV_W - 1, nb, 2 * ML_WIDTH), lambda b: (0, b, 0))
    st_specs = [per_b((ML_HEADS, ML_HEAD_DIM, ML_HEAD_DIM)), per_b((ML_HEADS, ML_HEAD_DIM)),
                per_b((ML_HEADS,))]
    in_specs = ([rows(ML_COLS), conv_spec] + st_specs
                + [_const_spec((1, GATE_COLS)), _const_spec((CONV_W, 2 * ML_WIDTH)),
                   _const_spec((1, 2 * ML_WIDTH)), _const_spec((1, ML_WIDTH))])
    out_shape = [jax.ShapeDtypeStruct((B * T, ML_WIDTH), F32),
                 jax.ShapeDtypeStruct((CONV_W - 1, B, 2 * ML_WIDTH), F32),
                 jax.ShapeDtypeStruct((B, ML_HEADS, ML_HEAD_DIM, ML_HEAD_DIM), F32),
                 jax.ShapeDtypeStruct((B, ML_HEADS, ML_HEAD_DIM), F32),
                 jax.ShapeDtypeStruct((B, ML_HEADS), F32)]
    out_specs = [rows(ML_WIDTH), conv_spec] + st_specs
    return pl.pallas_call(
        kern,
        grid=(B // nb,),
        in_specs=in_specs,
        out_specs=out_specs,
        out_shape=out_shape,
        scratch_shapes=[pltpu.VMEM((nb, BF16_ROWS, ML_COLS), F32),
                        pltpu.VMEM((nb, 2 * SUBLANES, 2 * ML_WIDTH), F32)],
        compiler_params=pltpu.CompilerParams(dimension_semantics=("arbitrary",),
                                             vmem_limit_bytes=VMEM_LIMIT),
        name="mlstm",
    )(zq, cst, *state, bif, cw, cb, hng)


PL = 256
PG = PL // SUBLANES
TAIL = (CONV_W - 1) * SUBLANES
PROJ_PIECE_COLS = 512


def _perm_rows(j):
    return pl.ds((PL // 4) * (j % 4) + j // 4, SUBLANES, stride=SUBLANES)


def _mixer_ml_kernel(h_ref, wqk_ref, wvo_ref, wif_ref, bif_ref, cw_ref, cb_ref, hng_ref,
                     bo_ref, conv_ref, C_ref, n_ref, m_ref, zqk_buf, zvo_buf, zif_buf, tail,
                     *, chunks_per_seq):
    s = pl.program_id(0)

    @pl.when(s == 0)
    def _():
        zqk_buf[1] = jnp.zeros(zqk_buf.shape[1:], F32)
        zvo_buf[1] = jnp.zeros(zvo_buf.shape[1:], F32)
        zif_buf[1] = jnp.zeros(zif_buf.shape[1:], F32)

    @pl.when(jnp.maximum(s - 1, 0) % chunks_per_seq == 0)
    def _():
        tail[...] = jnp.zeros(tail.shape, F32)
        C_ref[...] = jnp.zeros(C_ref.shape, F32)
        n_ref[...] = jnp.zeros(n_ref.shape, F32)
        m_ref[...] = jnp.zeros(m_ref.shape, F32)

    step = functools.partial(_mixer_ml_step, h_ref, wqk_ref, wvo_ref, wif_ref, bif_ref, cw_ref,
                             cb_ref, hng_ref, bo_ref, conv_ref, C_ref, n_ref, m_ref,
                             zqk_buf, zvo_buf, zif_buf, tail)

    @pl.when(s % 2 == 0)
    def _():
        step(rd=1, wr=0)

    @pl.when(s % 2 == 1)
    def _():
        step(rd=0, wr=1)


def _mixer_ml_step(h_ref, wqk_ref, wvo_ref, wif_ref, bif_ref, cw_ref, cb_ref, hng_ref,
                   bo_ref, conv_ref, C_ref, n_ref, m_ref, zqk_buf, zvo_buf, zif_buf, tail,
                   *, rd, wr):
    h = h_ref[...]

    def piece(w_ref, buf, c0, width):
        def emit():
            buf[wr, :, c0:c0 + width] = _bdot(h, w_ref[:, c0:c0 + width])
        return emit

    pieces = [piece(w_ref, buf, c0, PROJ_PIECE_COLS)
              for w_ref, buf in ((wqk_ref, zqk_buf), (wvo_ref, zvo_buf))
              for c0 in range(0, 2 * ML_WIDTH, PROJ_PIECE_COLS)]
    pieces.append(piece(wif_ref, zif_buf, 0, GATE_COLS))
    pieces = iter(pieces)

    def fill():
        emit = next(pieces, None)
        if emit is not None:
            emit()

    fill()
    zqk = zqk_buf.at[rd]
    zvo = zvo_buf.at[rd]
    zif = zif_buf[rd] + bif_ref[...]

    zqk_tail = zqk[PL - TAIL:, :]
    sub = lax.broadcasted_iota(jnp.int32, (SUBLANES, 2 * ML_WIDTH), 0)
    wrapped = []
    for g in range(CONV_W - 1):
        cur = pltpu.roll(zqk_tail[g * SUBLANES:(g + 1) * SUBLANES], 1, axis=0)
        prev = pltpu.roll(tail[g * SUBLANES:(g + 1) * SUBLANES, :], 1, axis=0)
        wrapped.append(jnp.where(sub == 0, prev, cur))
    wrapped = jnp.concatenate(wrapped, axis=0)
    tail[...] = zqk_tail
    conv_ref[...] = jnp.concatenate(
        [zqk_tail[g * SUBLANES + SUBLANES - 1:(g + 1) * SUBLANES, :] for g in range(CONV_W - 1)], axis=0)

    def conv_silu(c0, width):
        cs = slice(c0, c0 + width)
        acc = cb_ref[:, cs] + cw_ref[CONV_W - 1:CONV_W, cs] * zqk[:, cs]
        for d in range(1, CONV_W):
            shifted = jnp.concatenate(
                [wrapped[TAIL - d * SUBLANES:, cs], zqk[:PL - d * SUBLANES, cs]], axis=0)
            acc = acc + cw_ref[CONV_W - 1 - d:CONV_W - d, cs] * shifted
        return acc * jax.nn.sigmoid(acc)

    pr = lax.broadcasted_iota(jnp.int32, (PL, PL), 0)
    pc = lax.broadcasted_iota(jnp.int32, (PL, PL), 1)
    causal = ((pc >> 3) + PG * (pc & 7)) <= ((pr >> 3) + PG * (pr & 7))

    logf = pltpu.roll(_log_sigmoid(zif), GATE_COLS - ML_HEADS, axis=1)
    run, partial = None, []
    for n in range(PG):
        blk = logf[n * SUBLANES:(n + 1) * SUBLANES, :]
        run = blk if run is None else run + blk
        partial.append(run)
    sub_g = lax.broadcasted_iota(jnp.int32, (SUBLANES, GATE_COLS), 0)
    incl = run
    for step in (1, 2, 4):
        incl = incl + jnp.where(sub_g >= step, pltpu.roll(incl, step, axis=0), 0.0)
    earlier = incl - run
    bcum = jnp.concatenate([p + earlier for p in partial], axis=0)

    def head_cols(base, h):
        return zvo[:, base + h * ML_HEAD_DIM:base + (h + 1) * ML_HEAD_DIM]

    outs, C_new, n_new, m_new = _mlstm_heads(
        q_of=lambda h: conv_silu(h * ML_HEAD_DIM, ML_HEAD_DIM),
        k_of=lambda h: conv_silu(ML_WIDTH + h * ML_HEAD_DIM, ML_HEAD_DIM),
        v_of=lambda h: head_cols(0, h),
        o_of=lambda h: head_cols(ML_WIDTH, h),
        causal=causal, ipre=zif, bcum=bcum, groups=[(0, PL, PL - 1)],
        m0_of=lambda g: m_ref[...], C0_of=lambda g, h: C_ref[h], n0_of=lambda g, h: n_ref[h:h + 1, :],
        hng_ref=hng_ref, fill=fill)
    while next(pieces, None) is not None:
        raise AssertionError("projection pieces left over")
    for h in range(ML_HEADS):
        bo_ref[:, h * ML_HEAD_DIM:(h + 1) * ML_HEAD_DIM] = outs[h].astype(BF16)
        C_ref[h] = C_new[0][h]
        n_ref[h:h + 1, :] = n_new[0][h]
    m_ref[...] = m_new[0]


def _mixer_ml_call(hp, w_all, bif, cw, cb, hng):
    B, T, _ = hp.shape
    cps = T // PL
    n_chunks = B * cps
    proj = lambda s: jnp.minimum(s, n_chunks - 1)
    math = lambda s: jnp.maximum(s - 1, 0)
    per_b = lambda shape: pl.BlockSpec((None,) + shape,
                                       lambda s: (math(s) // cps,) + (0,) * len(shape))
    out_shape = [jax.ShapeDtypeStruct((B, T, ML_WIDTH), BF16),
                 jax.ShapeDtypeStruct((B, CONV_W - 1, 2 * ML_WIDTH), F32),
                 jax.ShapeDtypeStruct((B, ML_HEADS, ML_HEAD_DIM, ML_HEAD_DIM), F32),
                 jax.ShapeDtypeStruct((B, ML_HEADS, ML_HEAD_DIM), F32),
                 jax.ShapeDtypeStruct((B, 1, ML_HEADS), F32)]
    out_specs = [pl.BlockSpec((None, PL, ML_WIDTH), lambda s: (math(s) // cps, math(s) % cps, 0)),
                 per_b((CONV_W - 1, 2 * ML_WIDTH)),
                 per_b((ML_HEADS, ML_HEAD_DIM, ML_HEAD_DIM)), per_b((ML_HEADS, ML_HEAD_DIM)),
                 per_b((1, ML_HEADS))]
    return pl.pallas_call(
        functools.partial(_mixer_ml_kernel, chunks_per_seq=cps),
        grid=(n_chunks + 1,),
        in_specs=[pl.BlockSpec((None, PL, D_MODEL), lambda s: (proj(s) // cps, proj(s) % cps, 0))]
        + _ml_weight_specs()
        + [_const_spec((1, GATE_COLS)), _const_spec((CONV_W, 2 * ML_WIDTH)),
           _const_spec((1, 2 * ML_WIDTH)), _const_spec((1, ML_WIDTH))],
        out_specs=out_specs,
        out_shape=out_shape,
        scratch_shapes=[pltpu.VMEM((2, PL, 2 * ML_WIDTH), F32),
                        pltpu.VMEM((2, PL, 2 * ML_WIDTH), F32),
                        pltpu.VMEM((2, PL, GATE_COLS), F32),
                        pltpu.VMEM((TAIL, 2 * ML_WIDTH), F32)],
        compiler_params=pltpu.CompilerParams(dimension_semantics=("arbitrary",),
                                             vmem_limit_bytes=VMEM_LIMIT),
        name="mixer_ml",
    )(hp, w_all, w_all, w_all, bif, cw, cb, hng)


def _merge_ffn_kernel(x_ref, pa_ref, bo_ref, g1_ref, wg_ref, bg_ref, wpb_ref, wout_ref,
                      g2_ref, wfi_ref, wfo_ref, gf_ref, y_ref, *scratch, bo_permuted):
    tm = x_ref.shape[0]
    subs = [slice(r0, r0 + FFN_SUB_ROWS) for r0 in range(0, tm, FFN_SUB_ROWS)]
    x = [x_ref[rs, :] for rs in subs]
    h = [_rms(xi, g1_ref[...]).astype(BF16) for xi in x]
    gab = [_bdot(hi, wg_ref[...]) + bg_ref[...] for hi in h]
    pb = [_bdot(bo_ref[rs, :].astype(BF16), wpb_ref[...]) for rs in subs]
    if bo_permuted:
        (us,) = scratch
        for si, rs in enumerate(subs):
            for c in range(D_MODEL // LANES):
                us[c, rs, :] = pb[si][:, c * LANES:(c + 1) * LANES]
            pb[si] = jnp.concatenate(
                [jnp.concatenate([us[c, pl.ds(rs.start + _perm_rows(j).start, SUBLANES, stride=SUBLANES), :]
                                  for j in range(PG)], axis=0)
                 for c in range(D_MODEL // LANES)], axis=1)
    merged = [(jax.nn.sigmoid(g[:, :D_MODEL]) * pa_ref[rs, :]
               + jax.nn.sigmoid(g[:, D_MODEL:]) * p).astype(BF16) for g, p, rs in zip(gab, pb, subs)]
    x1 = [xi + _bdot(mi, wout_ref[...]) for xi, mi in zip(x, merged)]
    h2 = [_rms(xi, g2_ref[...]).astype(BF16) for xi in x1]
    gu = [_bdot(hi, wfi_ref[...]) for hi in h2]
    hid = [(g[:, :D_FF] * jax.nn.sigmoid(g[:, :D_FF]) * g[:, D_FF:]).astype(BF16) for g in gu]
    x2 = [xi + _bdot(hi, wfo_ref[...]) for xi, hi in zip(x1, hid)]
    for rs, xi in zip(subs, x2):
        y_ref[rs, :] = _rms(xi, gf_ref[...])


def _merge_ffn_call(x, pa, bo, g1, wg, bg, wpb, wout, g2, wfi, wfo, gf, *, tm, bo_permuted):
    m = x.shape[0]
    row = pl.BlockSpec((tm, D_MODEL), lambda i: (i, 0))
    assert not bo_permuted or FFN_SUB_ROWS == PL
    return pl.pallas_call(
        functools.partial(_merge_ffn_kernel, bo_permuted=bo_permuted),
        grid=(m // tm,),
        in_specs=[row, row, row, _const_spec((1, D_MODEL)), _const_spec((D_MODEL, 2 * D_MODEL)),
                  _const_spec((1, 2 * D_MODEL)), _const_spec((ML_WIDTH, D_MODEL)),
                  _const_spec((D_MODEL, D_MODEL)), _const_spec((1, D_MODEL)),
                  _const_spec((D_MODEL, 2 * D_FF)), _const_spec((D_FF, D_MODEL)),
                  _const_spec((1, D_MODEL))],
        out_specs=row,
        out_shape=jax.ShapeDtypeStruct((m, D_MODEL), F32),
        scratch_shapes=[pltpu.VMEM((D_MODEL // LANES, tm, LANES), F32)] if bo_permuted else [],
        compiler_params=pltpu.CompilerParams(dimension_semantics=("arbitrary",),
                                             vmem_limit_bytes=VMEM_LIMIT),
        name="merge_ffn",
    )(x, pa, bo, g1, wg, bg, wpb, wout, g2, wfi, wfo, gf)


SAMPLE_SEQS_PER_STEP = 8
TM_MIX = 512
TM_GMLP = 512
TM_FFN = 512


def _spatial_tiles(w_s, b_s, chunk):
    if chunk == GM_CHUNK:
        ws_t, b_pos = w_s[:, :chunk, :chunk], b_s[:, :chunk].T
    else:
        onehot = jnp.asarray(np.arange(GM_CHUNK)[:, None] % chunk == np.arange(chunk)[None, :], F32)
        hp = lax.Precision.HIGHEST
        ws_t = jnp.einsum("ri,gij,cj->grc", onehot, w_s[:, :chunk, :chunk], onehot, precision=hp)
        b_pos = jnp.dot(onehot, b_s[:, :chunk].T, precision=hp)
    bs_t = jnp.repeat(b_pos, GM_GROUP_W, axis=1)
    return ws_t, bs_t


def _gmlp_branch(xf, w, chunk, *, emit_v, emit_hperm, **side_jobs):
    ws_t, bs_t = _spatial_tiles(w["w_s"], w["b_s"], chunk)
    return _gmlp_call(xf, w["g1"], w["wuv"], w["lng"], w["lnb"], ws_t, bs_t, w["wpa"],
                      chunk=chunk, emit_v=emit_v, emit_hperm=emit_hperm,
                      tm=min(TM_GMLP, xf.shape[0]), **side_jobs)


def _merge_branch(xf, pa, bo, w, *, bo_permuted):
    return _merge_ffn_call(xf, pa, bo, w["g1"], w["wg"], w["bg"], w["wpb"], w["wout"], w["g2"],
                           w["wfi"], w["wfo"], w["gf"], tm=TM_FFN, bo_permuted=bo_permuted)


def kernel(x_prompt, x_sample, state_conv, state_C, state_n, state_m, g_norm1, w_in, b_i, b_f, ln_g, ln_b, w_s, b_s, conv_w, conv_b, hn_g, b_gate, w_proj_a, w_proj_b, w_out, g_norm2, w_ffn_in, w_ffn_out, g_final):
    Bp, Tp, _ = x_prompt.shape
    Bs, Ts, _ = x_sample.shape
    win = w_in[0]
    c_ml = 2 * GM_WIDTH
    c_gate = c_ml + 4 * ML_WIDTH + 2 * ML_HEADS
    n_ml_blocks = -(-(ML_COLS) // TCAST_ROWS)
    w = dict(
        g1=g_norm1[0][None], g2=g_norm2[0][None], gf=g_final[None],
        wuv=win[:, :c_ml].astype(BF16),
        bg=b_gate[0].reshape(1, 2 * D_MODEL),
        lng=ln_g[0][None], lnb=ln_b[0][None], w_s=w_s[0], b_s=b_s[0],
        bif=jnp.pad(jnp.concatenate([b_i[0], b_f[0]]), (0, GATE_COLS - 2 * ML_HEADS))[None],
        cw=conv_w[0], cb=conv_b[0][None], hng=hn_g[0][None],
        wpa=w_proj_a[0].astype(BF16),
    )

    xpf = x_prompt.reshape(Bp * Tp, D_MODEL)
    pa_p, hp_p, w["wpb"], w["wout"], w["wfi"], w["wfo"], w["w_ml"], w["wg"] = _gmlp_branch(
        xpf, w, GM_CHUNK, emit_v=False, emit_hperm=True,
        cast_weights=(w_proj_b[0], w_out[0], w_ffn_in[0], w_ffn_out[0]),
        tcast_weight=jnp.transpose(win),
        tcast_jobs=((c_ml, n_ml_blocks, 0), (c_gate, 2 * D_MODEL // TCAST_ROWS, n_ml_blocks)))
    bo_p, conv_p, C_p, n_p, m_p = _mixer_ml_call(hp_p.reshape(Bp, Tp, D_MODEL), w["w_ml"], w["bif"],
                                                 w["cw"], w["cb"], w["hng"])
    y_p = _merge_branch(xpf, pa_p, bo_p.reshape(Bp * Tp, ML_WIDTH), w, bo_permuted=True)

    xsf = x_sample.reshape(Bs * Ts, D_MODEL)
    pa_s, vn_s = _gmlp_branch(xsf, w, Ts, emit_v=True, emit_hperm=False)
    zq = _inproj_call(xsf, w["g1"], w["w_ml"], tm=TM_MIX)
    st = (state_C[0], state_n[0], state_m[0])
    bo_s, conv_s, C_s, n_s, m_s = _mlstm_call(zq, jnp.transpose(state_conv[0], (1, 0, 2)), st, w["bif"],
                                              w["cw"], w["cb"], w["hng"], t_valid=Ts,
                                              nb=SAMPLE_SEQS_PER_STEP)
    y_s = _merge_branch(xsf, pa_s, bo_s, w, bo_permuted=False)

    return (y_p.reshape(Bp, Tp, D_MODEL), y_s.reshape(Bs, Ts, D_MODEL),
            conv_p[None], C_p[None], n_p[None], m_p.reshape(1, Bp, ML_HEADS),
            jnp.transpose(conv_s, (1, 0, 2))[None], C_s[None], n_s[None], m_s[None],
            vn_s.reshape(1, Bs, Ts, GM_WIDTH))
```

```python
import functools
import math

import jax
import jax.numpy as jnp
import numpy as np
from jax import lax
from jax.experimental import pallas as pl
from jax.experimental.pallas import tpu as pltpu

D_MODEL = 1024
GM_WIDTH = D_MODEL
GM_GROUPS = 4
GM_GROUP_W = GM_WIDTH // GM_GROUPS
GM_CHUNK = 128
ML_HEADS = 4
ML_HEAD_DIM = D_MODEL // ML_HEADS
ML_WIDTH = ML_HEADS * ML_HEAD_DIM
CONV_W = 4
D_FF = 2816
EPS = 1e-6

LANES = 128
SUBLANES = 8
BF16_ROWS = 16
GATE_COLS = LANES
ML_COLS = 4 * ML_WIDTH + GATE_COLS
VMEM_LIMIT = 56 * 1024 * 1024
GMLP_SUB_ROWS = 256
FFN_SUB_ROWS = 256

F32 = jnp.float32
BF16 = jnp.bfloat16
NEG_BIG = -1e30
LN_INV_K_SCALE = 0.5 * math.log(ML_HEAD_DIM)
NT_DIMS = (((1,), (1,)), ((), ()))
TN_DIMS = (((0,), (0,)), ((), ()))


def _rms(x, g):
    return x * lax.rsqrt(jnp.mean(x * x, axis=-1, keepdims=True) + EPS) * g


def _gelu(x):
    return 0.5 * x * (1.0 + lax.erf(x * (2.0 ** -0.5)))


def _log_sigmoid(x):
    return jnp.minimum(x, 0.0) - jnp.log1p(jnp.exp(-jnp.abs(x)))


def _bdot(a, b):
    return jnp.dot(a, b, preferred_element_type=F32)


def _const_spec(shape):
    nd = len(shape)
    return pl.BlockSpec(shape, lambda *_: (0,) * nd, pipeline_mode=pl.Buffered(1))


def _gmlp_kernel(x_ref, g1_ref, wuv_ref, lng_ref, lnb_ref, ws_ref, bs_ref, wpa_ref, *rest,
                 chunk, emit_v, emit_hperm, n_cast, n_tcast):
    rest = list(rest)
    cast_src = [rest.pop(0) for _ in range(n_cast)]
    tcast_src = [rest.pop(0) for _ in range(n_tcast)]
    pa_ref = rest.pop(0)
    vn_ref = rest.pop(0) if emit_v else None
    hp_ref = rest.pop(0) if emit_hperm else None
    for src in cast_src:
        rest.pop(0)[...] = src[...].astype(BF16)
    for src in tcast_src:
        rest.pop(0)[...] = src[...].T.astype(BF16)
    a_sc = rest.pop(0)
    tm = x_ref.shape[0]
    blk = ws_ref.shape[1]
    hs = rest.pop(0) if emit_hperm else None
    sub = GMLP_SUB_ROWS
    subs = [slice(r0, r0 + sub) for r0 in range(0, tm, sub)]
    hb = []
    for rs in subs:
        hf = _rms(x_ref[rs, :], g1_ref[...])
        hb.append(hf.astype(BF16))
        if emit_hperm:
            for j in range(rs.start // SUBLANES, rs.stop // SUBLANES):
                base = (j // PG) * PL
                dst = _perm_rows(j % PG)
                lo = j * SUBLANES - rs.start
                for c in range(D_MODEL // LANES):
                    hs[c, pl.ds(base + dst.start, SUBLANES, stride=SUBLANES), :] = (
                        hf[lo:lo + SUBLANES, c * LANES:(c + 1) * LANES])
            for c in range(D_MODEL // LANES):
                hp_ref[rs, c * LANES:(c + 1) * LANES] = hs[c, rs, :].astype(BF16)
    zu = [_bdot(h, wuv_ref[:, :GM_WIDTH]) for h in hb]
    zv = [_bdot(h, wuv_ref[:, GM_WIDTH:]) for h in hb]
    r = lax.broadcasted_iota(jnp.int32, (blk, blk), 0)
    c = lax.broadcasted_iota(jnp.int32, (blk, blk), 1)
    keep = c <= r
    if chunk < blk:
        sh = chunk.bit_length() - 1
        keep = jnp.logical_and(keep, (r >> sh) == (c >> sh))
    wsm = [jnp.where(keep, ws_ref[g], 0.0).astype(BF16) for g in range(GM_GROUPS)]
    for si, rs in enumerate(subs):
        u = _gelu(zu[si])
        v = _gelu(zv[si])
        mu = jnp.mean(v, axis=-1, keepdims=True)
        vc = v - mu
        var = jnp.mean(vc * vc, axis=-1, keepdims=True)
        vn = vc * lax.rsqrt(var + EPS) * lng_ref[...] + lnb_ref[...]
        if emit_v:
            vn_ref[rs, :] = vn
        vb = vn.astype(BF16)
        for g in range(GM_GROUPS):
            cs = slice(g * GM_GROUP_W, (g + 1) * GM_GROUP_W)
            for i in range(sub // blk):
                ls = slice(i * blk, (i + 1) * blk)
                s = _bdot(wsm[g], vb[ls, cs]) + bs_ref[:, cs]
                a_sc[rs.start + i * blk:rs.start + (i + 1) * blk, cs] = (u[ls, cs] * s).astype(BF16)
        pa_ref[rs, :] = _bdot(a_sc[rs, :], wpa_ref[...])


def _cast_block_spec(n_rows, n_cols, steps):
    n_blocks = steps
    while n_rows % n_blocks or (n_rows // n_blocks) % BF16_ROWS:
        n_blocks //= 2
    per = steps // n_blocks
    return pl.BlockSpec((n_rows // n_blocks, n_cols), lambda i: (i // per, 0))


TCAST_ROWS = 256


def _tcast_specs(job):
    first_row, n_blocks, first_step = job
    assert first_row % SUBLANES == 0
    blk = lambda i: jnp.clip(i - first_step, 0, n_blocks - 1)
    src = pl.BlockSpec((pl.Element(TCAST_ROWS), pl.Element(D_MODEL)),
                       lambda i: (pl.multiple_of(first_row + TCAST_ROWS * blk(i), SUBLANES), 0))
    dst = pl.BlockSpec((D_MODEL, TCAST_ROWS), lambda i: (0, blk(i)))
    return src, dst, jax.ShapeDtypeStruct((D_MODEL, TCAST_ROWS * n_blocks), BF16)


def _tcast_kernel(src_ref, dst_ref):
    dst_ref[...] = src_ref[...].T.astype(BF16)


def _tcast_call(wt, first_row, n_blocks):
    src, dst, shape = _tcast_specs((first_row, n_blocks, 0))
    return pl.pallas_call(
        _tcast_kernel, grid=(n_blocks,), in_specs=[src], out_specs=dst, out_shape=shape,
        compiler_params=pltpu.CompilerParams(dimension_semantics=("arbitrary",)),
        name="tcast",
    )(wt)


def _gmlp_call(x, g1, wuv, lng, lnb, ws_t, bs_t, wpa, *, chunk, emit_v, emit_hperm, tm,
               cast_weights=(), tcast_weight=None, tcast_jobs=()):
    m = x.shape[0]
    blk = ws_t.shape[1]
    steps = m // tm
    assert steps & (steps - 1) == 0
    assert all(first_step + n_blocks <= steps for _, n_blocks, first_step in tcast_jobs)
    row = pl.BlockSpec((tm, D_MODEL), lambda i: (i, 0))
    out_shape = [jax.ShapeDtypeStruct((m, D_MODEL), F32)]
    out_specs = [row]
    scratch = [pltpu.VMEM((tm, GM_WIDTH), BF16)]
    cast_specs = [_cast_block_spec(cw.shape[0], cw.shape[1], steps) for cw in cast_weights]
    tcast = [_tcast_specs(job) for job in tcast_jobs]
    if emit_v:
        out_shape.append(jax.ShapeDtypeStruct((m, GM_WIDTH), F32))
        out_specs.append(row)
    if emit_hperm:
        assert tm % PL == 0 and GMLP_SUB_ROWS == PL
        out_shape.append(jax.ShapeDtypeStruct((m, D_MODEL), BF16))
        out_specs.append(row)
        scratch.append(pltpu.VMEM((D_MODEL // LANES, tm, LANES), F32))
    out_shape += [jax.ShapeDtypeStruct(cw.shape, BF16) for cw in cast_weights]
    out_shape += [t[2] for t in tcast]
    out_specs += cast_specs + [t[1] for t in tcast]
    return pl.pallas_call(
        functools.partial(_gmlp_kernel, chunk=chunk, emit_v=emit_v, emit_hperm=emit_hperm,
                          n_cast=len(cast_weights), n_tcast=len(tcast)),
        grid=(steps,),
        in_specs=[row, _const_spec((1, D_MODEL)), _const_spec((D_MODEL, 2 * GM_WIDTH)),
                  _const_spec((1, GM_WIDTH)), _const_spec((1, GM_WIDTH)),
                  _const_spec((GM_GROUPS, blk, blk)), _const_spec((blk, GM_WIDTH)),
                  _const_spec((GM_WIDTH, D_MODEL))] + cast_specs + [t[0] for t in tcast],
        out_specs=out_specs,
        out_shape=out_shape,
        scratch_shapes=scratch,
        compiler_params=pltpu.CompilerParams(dimension_semantics=("arbitrary",),
                                             vmem_limit_bytes=VMEM_LIMIT),
        name="gmlp",
    )(x, g1, wuv, lng, lnb, ws_t, bs_t, wpa, *cast_weights, *([tcast_weight] * len(tcast)))


def _ml_weight_specs():
    wide = 2 * ML_WIDTH
    col_block = lambda width, idx: pl.BlockSpec((D_MODEL, width), lambda *_: (0, idx),
                                                pipeline_mode=pl.Buffered(1))
    return [col_block(wide, 0), col_block(wide, 1), col_block(GATE_COLS, 4 * ML_WIDTH // GATE_COLS)]


def _inproj_kernel(x_ref, g1_ref, wqk_ref, wvo_ref, wif_ref, z_ref):
    h = _rms(x_ref[...], g1_ref[...]).astype(BF16)
    z_ref[:, 0:2 * ML_WIDTH] = _bdot(h, wqk_ref[...])
    z_ref[:, 2 * ML_WIDTH:4 * ML_WIDTH] = _bdot(h, wvo_ref[...])
    z_ref[:, 4 * ML_WIDTH:ML_COLS] = _bdot(h, wif_ref[...])


def _inproj_call(x, g1, w_all, *, tm):
    m = x.shape[0]
    return pl.pallas_call(
        _inproj_kernel,
        grid=(m // tm,),
        in_specs=[pl.BlockSpec((tm, D_MODEL), lambda i: (i, 0)), _const_spec((1, D_MODEL))]
        + _ml_weight_specs(),
        out_specs=pl.BlockSpec((tm, ML_COLS), lambda i: (i, 0)),
        out_shape=jax.ShapeDtypeStruct((m, ML_COLS), F32),
        compiler_params=pltpu.CompilerParams(dimension_semantics=("arbitrary",),
                                             vmem_limit_bytes=VMEM_LIMIT),
        name="inproj",
    )(x, g1, w_all, w_all, w_all)


def _mlstm_heads(q_of, k_of, v_of, o_of, causal, ipre, bcum, groups, m0_of, C0_of, n0_of, hng_ref,
                 fill=None):
    n_groups = len(groups)
    single = n_groups == 1
    fill = fill or (lambda: None)

    def rows_of(x, g):
        return x if single else x[groups[g][0]:groups[g][0] + groups[g][1]]

    def per_row(vals):
        if single:
            return vals[0]
        return jnp.concatenate([jnp.broadcast_to(v, (groups[g][1], v.shape[1]))
                                for g, v in enumerate(vals)], axis=0)

    a = ipre - bcum
    a_t = a.T
    m_rows = per_row([m0_of(g) for g in range(n_groups)])

    def prepare(h):
        p = {}
        a2 = jnp.where(causal, a_t[h:h + 1, :], -jnp.inf)
        p["mc"] = mc = jnp.maximum(jnp.max(a2, axis=1, keepdims=True), m_rows[:, h:h + 1])
        p["m_last"] = m_last = [mc[grp[2]:grp[2] + 1, :] for grp in groups]
        p["dm"] = jnp.exp(a2 - (mc + LN_INV_K_SCALE))
        p["w_inter"] = jnp.exp(m_rows[:, h:h + 1] - mc)
        p["w_col"] = w_col = jnp.exp(a[:, h:h + 1] - (per_row(m_last) + LN_INV_K_SCALE))
        p["decay"] = [jnp.exp(m0_of(g)[:, h:h + 1] - m_last[g]) for g in range(n_groups)]
        p["q"] = q = q_of(h)
        p["k"] = k = k_of(h)
        v = v_of(h)
        p["qb"], p["kb"], p["vb"] = q.astype(BF16), k.astype(BF16), v.astype(BF16)
        p["vw"] = (v * w_col).astype(BF16)
        p["c_old"] = [C0_of(g, h) for g in range(n_groups)]
        p["n_old"] = [n0_of(g, h) for g in range(n_groups)]
        return p

    def first_matmuls(p):
        p["qk"] = lax.dot_general(p["qb"], p["kb"], NT_DIMS, preferred_element_type=F32)
        p["qc"] = [lax.dot_general(rows_of(p["qb"], g), p["c_old"][g].astype(BF16), NT_DIMS,
                                   preferred_element_type=F32) for g in range(n_groups)]
        if single:
            n_rows = jnp.broadcast_to(p["n_old"][0], (LANES, ML_HEAD_DIM)).astype(BF16)
            p["qn"] = lax.dot_general(p["qb"], n_rows, NT_DIMS, preferred_element_type=F32)[:, 0:1]
        else:
            p["qn"] = jnp.sum(p["q"] * per_row(p["n_old"]), axis=1, keepdims=True)

    def second_matmuls(p):
        p["s"] = s = p["dm"] * p["qk"]
        p["sv"] = _bdot(s.astype(BF16), p["vb"])
        p["cupd"] = [lax.dot_general(rows_of(p["vw"], g), rows_of(p["kb"], g), TN_DIMS,
                                     preferred_element_type=F32) for g in range(n_groups)]

    def finish(h, p):
        qc_rows = p["qc"][0] if single else jnp.concatenate(p["qc"], axis=0)
        num = p["w_inter"] * qc_rows + p["sv"]
        den = p["w_inter"] * p["qn"] + jnp.sum(p["s"], axis=1, keepdims=True)
        hcur = num / jnp.maximum(jnp.abs(den), jnp.exp(-(bcum[:, h:h + 1] + p["mc"])))
        mu = jnp.mean(hcur, axis=1, keepdims=True)
        hc = hcur - mu
        var = jnp.mean(hc * hc, axis=1, keepdims=True)
        hs = slice(h * ML_HEAD_DIM, (h + 1) * ML_HEAD_DIM)
        out = jax.nn.sigmoid(o_of(h)) * (hc * lax.rsqrt(var + EPS) * hng_ref[:, hs])
        kw = p["k"] * p["w_col"]
        c_new = [p["decay"][g] * p["c_old"][g] + p["cupd"][g] for g in range(n_groups)]
        n_new = [p["decay"][g] * p["n_old"][g] + jnp.sum(rows_of(kw, g), axis=0, keepdims=True)
                 for g in range(n_groups)]
        return out, c_new, n_new

    H = ML_HEADS
    per_head = [None] * H
    done = [None] * H
    if single:
        per_head[0] = prepare(0)
        fill()
        first_matmuls(per_head[0])
        if H > 1:
            per_head[1] = prepare(1)
        fill()
        for h in range(H):
            second_matmuls(per_head[h])
            if h + 1 < H:
                first_matmuls(per_head[h + 1])
            fill()
            if h + 2 < H:
                per_head[h + 2] = prepare(h + 2)
            done[h] = finish(h, per_head[h])
            fill()
    else:
        per_head = [prepare(h) for h in range(H)]
        for stage in (first_matmuls, second_matmuls):
            for h in range(H):
                stage(per_head[h])
        done = [finish(h, per_head[h]) for h in range(H)]

    outs = [done[h][0] for h in range(H)]
    C_new = [[done[h][1][g] for h in range(H)] for g in range(n_groups)]
    n_new = [[done[h][2][g] for h in range(H)] for g in range(n_groups)]
    m_new = []
    for g, grp in enumerate(groups):
        row = m0_of(g)
        lane = lax.broadcasted_iota(jnp.int32, row.shape, 1)
        for h in range(H):
            row = jnp.where(lane == h, bcum[grp[2]:grp[2] + 1, h:h + 1] + per_head[h]["m_last"][g], row)
        m_new.append(row)
    return outs, C_new, n_new, m_new


def _mlstm_kernel(zq_ref, cst_ref, C0_ref, n0_ref, m0_ref, bif_ref, cw_ref, cb_ref, hng_ref,
                  bo_ref, conv_ref, C_ref, n_ref, m_ref, zp, xp, *, t_valid):
    nb = C0_ref.shape[0]
    L = BF16_ROWS
    R = nb * L

    @pl.when(pl.program_id(0) == 0)
    def _():
        zp[...] = jnp.zeros(zp.shape, F32)

    for bb in range(nb):
        zp[bb, 0:t_valid, :] = zq_ref[bb * t_valid:(bb + 1) * t_valid, :]
        for j in range(CONV_W - 1):
            row = SUBLANES - (CONV_W - 1) + j
            xp[bb, row:row + 1, :] = cst_ref[j, bb:bb + 1, :]
        xp[bb, SUBLANES:2 * SUBLANES, :] = zp[bb, 0:SUBLANES, 0:2 * ML_WIDTH]
        for j in range(CONV_W - 1):
            row = SUBLANES + t_valid - (CONV_W - 1) + j
            conv_ref[j, bb:bb + 1, :] = xp[bb, row:row + 1, :]
    qk_rows = []
    for bb in range(nb):
        acc = cb_ref[...]
        for j in range(CONV_W):
            off = SUBLANES - (CONV_W - 1) + j
            acc = acc + cw_ref[j:j + 1, :] * xp[bb, off:off + SUBLANES, :]
        qk_rows += [acc, jnp.zeros((L - SUBLANES, 2 * ML_WIDTH), F32)]
    qk = jnp.concatenate(qk_rows, axis=0)
    qk = qk * jax.nn.sigmoid(qk)

    def cols(c0, width):
        return zp[:, :, c0:c0 + width].reshape(R, width)

    zif = cols(4 * ML_WIDTH, GATE_COLS) + bif_ref[...]
    live = (lax.broadcasted_iota(jnp.int32, (R, GATE_COLS), 0) & (L - 1)) < t_valid
    ipre = jnp.where(live, zif, NEG_BIG)
    logf = jnp.where(live, pltpu.roll(_log_sigmoid(zif), GATE_COLS - ML_HEADS, axis=1), 0.0)
    r = lax.broadcasted_iota(jnp.int32, (R, R), 0)
    c = lax.broadcasted_iota(jnp.int32, (R, R), 1)
    sh = L.bit_length() - 1
    causal = jnp.logical_and(c <= r, (r >> sh) == (c >> sh))
    bcum = jnp.dot(causal.astype(F32), logf, preferred_element_type=F32,
                   precision=lax.Precision.HIGHEST)
    live_w = (lax.broadcasted_iota(jnp.int32, (R, ML_HEAD_DIM), 0) & (L - 1)) < t_valid

    def head_cols(x, base, h):
        return x[:, base + h * ML_HEAD_DIM:base + (h + 1) * ML_HEAD_DIM]

    groups = [(bb * L, L, bb * L + L - 1) for bb in range(nb)]
    outs, C_new, n_new, m_new = _mlstm_heads(
        q_of=lambda h: head_cols(qk, 0, h),
        k_of=lambda h: jnp.where(live_w, head_cols(qk, ML_WIDTH, h), 0.0),
        v_of=lambda h: jnp.where(live_w, cols(2 * ML_WIDTH + h * ML_HEAD_DIM, ML_HEAD_DIM), 0.0),
        o_of=lambda h: cols(3 * ML_WIDTH + h * ML_HEAD_DIM, ML_HEAD_DIM),
        causal=causal, ipre=ipre, bcum=bcum, groups=groups,
        m0_of=lambda g: m0_ref[g:g + 1, :], C0_of=lambda g, h: C0_ref[g, h],
        n0_of=lambda g, h: n0_ref[g, h:h + 1, :], hng_ref=hng_ref)
    for bb in range(nb):
        for h in range(ML_HEADS):
            hs = slice(h * ML_HEAD_DIM, (h + 1) * ML_HEAD_DIM)
            bo_ref[bb * t_valid:(bb + 1) * t_valid, hs] = outs[h][bb * L:bb * L + t_valid, :]
            C_ref[bb, h] = C_new[bb][h]
            n_ref[bb, h:h + 1, :] = n_new[bb][h]
        m_ref[bb:bb + 1, :] = m_new[bb]


def _mlstm_call(zq, cst, state, bif, cw, cb, hng, *, t_valid, nb):
    T = t_valid
    B = zq.shape[0] // T
    kern = functools.partial(_mlstm_kernel, t_valid=T)
    per_b = lambda shape: pl.BlockSpec((nb,) + shape, lambda b: (b,) + (0,) * len(shape))
    rows = lambda width: pl.BlockSpec((nb * T, width), lambda b: (b, 0))
    conv_spec = pl.BlockSpec((CONV_W - 1, nb, 2 * ML_WIDTH), lambda b: (0, b, 0))
    st_specs = [per_b((ML_HEADS, ML_HEAD_DIM, ML_HEAD_DIM)), per_b((ML_HEADS, ML_HEAD_DIM)),
                per_b((ML_HEADS,))]
    in_specs = ([rows(ML_COLS), conv_spec] + st_specs
                + [_const_spec((1, GATE_COLS)), _const_spec((CONV_W, 2 * ML_WIDTH)),
                   _const_spec((1, 2 * ML_WIDTH)), _const_spec((1, ML_WIDTH))])
    out_shape = [jax.ShapeDtypeStruct((B * T, ML_WIDTH), F32),
                 jax.ShapeDtypeStruct((CONV_W - 1, B, 2 * ML_WIDTH), F32),
                 jax.ShapeDtypeStruct((B, ML_HEADS, ML_HEAD_DIM, ML_HEAD_DIM), F32),
                 jax.ShapeDtypeStruct((B, ML_HEADS, ML_HEAD_DIM), F32),
                 jax.ShapeDtypeStruct((B, ML_HEADS), F32)]
    out_specs = [rows(ML_WIDTH), conv_spec] + st_specs
    return pl.pallas_call(
        kern,
        grid=(B // nb,),
        in_specs=in_specs,
        out_specs=out_specs,
        out_shape=out_shape,
        scratch_shapes=[pltpu.VMEM((nb, BF16_ROWS, ML_COLS), F32),
                        pltpu.VMEM((nb, 2 * SUBLANES, 2 * ML_WIDTH), F32)],
        compiler_params=pltpu.CompilerParams(dimension_semantics=("arbitrary",),
                                             vmem_limit_bytes=VMEM_LIMIT),
        name="mlstm",
    )(zq, cst, *state, bif, cw, cb, hng)


PL = 256
PG = PL // SUBLANES
TAIL = (CONV_W - 1) * SUBLANES
PROJ_PIECE_COLS = 512


def _perm_rows(j):
    return pl.ds((PL // 4) * (j % 4) + j // 4, SUBLANES, stride=SUBLANES)


def _mixer_ml_kernel(h_ref, wqk_ref, wvo_ref, wif_ref, bif_ref, cw_ref, cb_ref, hng_ref,
                     bo_ref, conv_ref, C_ref, n_ref, m_ref, zqk_buf, zvo_buf, zif_buf, tail,
                     *, chunks_per_seq):
    s = pl.program_id(0)

    @pl.when(s == 0)
    def _():
        zqk_buf[1] = jnp.zeros(zqk_buf.shape[1:], F32)
        zvo_buf[1] = jnp.zeros(zvo_buf.shape[1:], F32)
        zif_buf[1] = jnp.zeros(zif_buf.shape[1:], F32)

    @pl.when(jnp.maximum(s - 1, 0) % chunks_per_seq == 0)
    def _():
        tail[...] = jnp.zeros(tail.shape, F32)
        C_ref[...] = jnp.zeros(C_ref.shape, F32)
        n_ref[...] = jnp.zeros(n_ref.shape, F32)
        m_ref[...] = jnp.zeros(m_ref.shape, F32)

    step = functools.partial(_mixer_ml_step, h_ref, wqk_ref, wvo_ref, wif_ref, bif_ref, cw_ref,
                             cb_ref, hng_ref, bo_ref, conv_ref, C_ref, n_ref, m_ref,
                             zqk_buf, zvo_buf, zif_buf, tail)

    @pl.when(s % 2 == 0)
    def _():
        step(rd=1, wr=0)

    @pl.when(s % 2 == 1)
    def _():
        step(rd=0, wr=1)


def _mixer_ml_step(h_ref, wqk_ref, wvo_ref, wif_ref, bif_ref, cw_ref, cb_ref, hng_ref,
                   bo_ref, conv_ref, C_ref, n_ref, m_ref, zqk_buf, zvo_buf, zif_buf, tail,
                   *, rd, wr):
    h = h_ref[...]

    def piece(w_ref, buf, c0, width):
        def emit():
            buf[wr, :, c0:c0 + width] = _bdot(h, w_ref[:, c0:c0 + width])
        return emit

    pieces = [piece(w_ref, buf, c0, PROJ_PIECE_COLS)
              for w_ref, buf in ((wqk_ref, zqk_buf), (wvo_ref, zvo_buf))
              for c0 in range(0, 2 * ML_WIDTH, PROJ_PIECE_COLS)]
    pieces.append(piece(wif_ref, zif_buf, 0, GATE_COLS))
    pieces = iter(pieces)

    def fill():
        emit = next(pieces, None)
        if emit is not None:
            emit()

    fill()
    zqk = zqk_buf.at[rd]
    zvo = zvo_buf.at[rd]
    zif = zif_buf[rd] + bif_ref[...]

    zqk_tail = zqk[PL - TAIL:, :]
    sub = lax.broadcasted_iota(jnp.int32, (SUBLANES, 2 * ML_WIDTH), 0)
    wrapped = []
    for g in range(CONV_W - 1):
        cur = pltpu.roll(zqk_tail[g * SUBLANES:(g + 1) * SUBLANES], 1, axis=0)
        prev = pltpu.roll(tail[g * SUBLANES:(g + 1) * SUBLANES, :], 1, axis=0)
        wrapped.append(jnp.where(sub == 0, prev, cur))
    wrapped = jnp.concatenate(wrapped, axis=0)
    tail[...] = zqk_tail
    conv_ref[...] = jnp.concatenate(
        [zqk_tail[g * SUBLANES + SUBLANES - 1:(g + 1) * SUBLANES, :] for g in range(CONV_W - 1)], axis=0)

    def conv_silu(c0, width):
        cs = slice(c0, c0 + width)
        acc = cb_ref[:, cs] + cw_ref[CONV_W - 1:CONV_W, cs] * zqk[:, cs]
        for d in range(1, CONV_W):
            shifted = jnp.concatenate(
                [wrapped[TAIL - d * SUBLANES:, cs], zqk[:PL - d * SUBLANES, cs]], axis=0)
            acc = acc + cw_ref[CONV_W - 1 - d:CONV_W - d, cs] * shifted
        return acc * jax.nn.sigmoid(acc)

    pr = lax.broadcasted_iota(jnp.int32, (PL, PL), 0)
    pc = lax.broadcasted_iota(jnp.int32, (PL, PL), 1)
    causal = ((pc >> 3) + PG * (pc & 7)) <= ((pr >> 3) + PG * (pr & 7))

    logf = pltpu.roll(_log_sigmoid(zif), GATE_COLS - ML_HEADS, axis=1)
    run, partial = None, []
    for n in range(PG):
        blk = logf[n * SUBLANES:(n + 1) * SUBLANES, :]
        run = blk if run is None else run + blk
        partial.append(run)
    sub_g = lax.broadcasted_iota(jnp.int32, (SUBLANES, GATE_COLS), 0)
    incl = run
    for step in (1, 2, 4):
        incl = incl + jnp.where(sub_g >= step, pltpu.roll(incl, step, axis=0), 0.0)
    earlier = incl - run
    bcum = jnp.concatenate([p + earlier for p in partial], axis=0)

    def head_cols(base, h):
        return zvo[:, base + h * ML_HEAD_DIM:base + (h + 1) * ML_HEAD_DIM]

    outs, C_new, n_new, m_new = _mlstm_heads(
        q_of=lambda h: conv_silu(h * ML_HEAD_DIM, ML_HEAD_DIM),
        k_of=lambda h: conv_silu(ML_WIDTH + h * ML_HEAD_DIM, ML_HEAD_DIM),
        v_of=lambda h: head_cols(0, h),
        o_of=lambda h: head_cols(ML_WIDTH, h),
        causal=causal, ipre=zif, bcum=bcum, groups=[(0, PL, PL - 1)],
        m0_of=lambda g: m_ref[...], C0_of=lambda g, h: C_ref[h], n0_of=lambda g, h: n_ref[h:h + 1, :],
        hng_ref=hng_ref, fill=fill)
    while next(pieces, None) is not None:
        raise AssertionError("projection pieces left over")
    for h in range(ML_HEADS):
        bo_ref[:, h * ML_HEAD_DIM:(h + 1) * ML_HEAD_DIM] = outs[h].astype(BF16)
        C_ref[h] = C_new[0][h]
        n_ref[h:h + 1, :] = n_new[0][h]
    m_ref[...] = m_new[0]


def _mixer_ml_call(hp, w_all, bif, cw, cb, hng):
    B, T, _ = hp.shape
    cps = T // PL
    n_chunks = B * cps
    proj = lambda s: jnp.minimum(s, n_chunks - 1)
    math = lambda s: jnp.maximum(s - 1, 0)
    per_b = lambda shape: pl.BlockSpec((None,) + shape,
                                       lambda s: (math(s) // cps,) + (0,) * len(shape))
    out_shape = [jax.ShapeDtypeStruct((B, T, ML_WIDTH), BF16),
                 jax.ShapeDtypeStruct((B, CONV_W - 1, 2 * ML_WIDTH), F32),
                 jax.ShapeDtypeStruct((B, ML_HEADS, ML_HEAD_DIM, ML_HEAD_DIM), F32),
                 jax.ShapeDtypeStruct((B, ML_HEADS, ML_HEAD_DIM), F32),
                 jax.ShapeDtypeStruct((B, 1, ML_HEADS), F32)]
    out_specs = [pl.BlockSpec((None, PL, ML_WIDTH), lambda s: (math(s) // cps, math(s) % cps, 0)),
                 per_b((CONV_W - 1, 2 * ML_WIDTH)),
                 per_b((ML_HEADS, ML_HEAD_DIM, ML_HEAD_DIM)), per_b((ML_HEADS, ML_HEAD_DIM)),
                 per_b((1, ML_HEADS))]
    return pl.pallas_call(
        functools.partial(_mixer_ml_kernel, chunks_per_seq=cps),
        grid=(n_chunks + 1,),
        in_specs=[pl.BlockSpec((None, PL, D_MODEL), lambda s: (proj(s) // cps, proj(s) % cps, 0))]
        + _ml_weight_specs()
        + [_const_spec((1, GATE_COLS)), _const_spec((CONV_W, 2 * ML_WIDTH)),
           _const_spec((1, 2 * ML_WIDTH)), _const_spec((1, ML_WIDTH))],
        out_specs=out_specs,
        out_shape=out_shape,
        scratch_shapes=[pltpu.VMEM((2, PL, 2 * ML_WIDTH), F32),
                        pltpu.VMEM((2, PL, 2 * ML_WIDTH), F32),
                        pltpu.VMEM((2, PL, GATE_COLS), F32),
                        pltpu.VMEM((TAIL, 2 * ML_WIDTH), F32)],
        compiler_params=pltpu.CompilerParams(dimension_semantics=("arbitrary",),
                                             vmem_limit_bytes=VMEM_LIMIT),
        name="mixer_ml",
    )(hp, w_all, w_all, w_all, bif, cw, cb, hng)


def _merge_ffn_kernel(x_ref, pa_ref, bo_ref, g1_ref, wg_ref, bg_ref, wpb_ref, wout_ref,
                      g2_ref, wfi_ref, wfo_ref, gf_ref, y_ref, *scratch, bo_permuted):
    tm = x_ref.shape[0]
    subs = [slice(r0, r0 + FFN_SUB_ROWS) for r0 in range(0, tm, FFN_SUB_ROWS)]
    x = [x_ref[rs, :] for rs in subs]
    h = [_rms(xi, g1_ref[...]).astype(BF16) for xi in x]
    gab = [_bdot(hi, wg_ref[...]) + bg_ref[...] for hi in h]
    pb = [_bdot(bo_ref[rs, :].astype(BF16), wpb_ref[...]) for rs in subs]
    if bo_permuted:
        (us,) = scratch
        for si, rs in enumerate(subs):
            for c in range(D_MODEL // LANES):
                us[c, rs, :] = pb[si][:, c * LANES:(c + 1) * LANES]
            pb[si] = jnp.concatenate(
                [jnp.concatenate([us[c, pl.ds(rs.start + _perm_rows(j).start, SUBLANES, stride=SUBLANES), :]
                                  for j in range(PG)], axis=0)
                 for c in range(D_MODEL // LANES)], axis=1)
    merged = [(jax.nn.sigmoid(g[:, :D_MODEL]) * pa_ref[rs, :]
               + jax.nn.sigmoid(g[:, D_MODEL:]) * p).astype(BF16) for g, p, rs in zip(gab, pb, subs)]
    x1 = [xi + _bdot(mi, wout_ref[...]) for xi, mi in zip(x, merged)]
    h2 = [_rms(xi, g2_ref[...]).astype(BF16) for xi in x1]
    gu = [_bdot(hi, wfi_ref[...]) for hi in h2]
    hid = [(g[:, :D_FF] * jax.nn.sigmoid(g[:, :D_FF]) * g[:, D_FF:]).astype(BF16) for g in gu]
    x2 = [xi + _bdot(hi, wfo_ref[...]) for xi, hi in zip(x1, hid)]
    for rs, xi in zip(subs, x2):
        y_ref[rs, :] = _rms(xi, gf_ref[...])


def _merge_ffn_call(x, pa, bo, g1, wg, bg, wpb, wout, g2, wfi, wfo, gf, *, tm, bo_permuted):
    m = x.shape[0]
    row = pl.BlockSpec((tm, D_MODEL), lambda i: (i, 0))
    assert not bo_permuted or FFN_SUB_ROWS == PL
    return pl.pallas_call(
        functools.partial(_merge_ffn_kernel, bo_permuted=bo_permuted),
        grid=(m // tm,),
        in_specs=[row, row, row, _const_spec((1, D_MODEL)), _const_spec((D_MODEL, 2 * D_MODEL)),
                  _const_spec((1, 2 * D_MODEL)), _const_spec((ML_WIDTH, D_MODEL)),
                  _const_spec((D_MODEL, D_MODEL)), _const_spec((1, D_MODEL)),
                  _const_spec((D_MODEL, 2 * D_FF)), _const_spec((D_FF, D_MODEL)),
                  _const_spec((1, D_MODEL))],
        out_specs=row,
        out_shape=jax.ShapeDtypeStruct((m, D_MODEL), F32),
        scratch_shapes=[pltpu.VMEM((D_MODEL // LANES, tm, LANES), F32)] if bo_permuted else [],
        compiler_params=pltpu.CompilerParams(dimension_semantics=("arbitrary",),
                                             vmem_limit_bytes=VMEM_LIMIT),
        name="merge_ffn",
    )(x, pa, bo, g1, wg, bg, wpb, wout, g2, wfi, wfo, gf)


SAMPLE_SEQS_PER_STEP = 8
TM_MIX = 512
TM_GMLP = 512
TM_FFN = 512


def _spatial_tiles(w_s, b_s, chunk):
    if chunk == GM_CHUNK:
        ws_t, b_pos = w_s[:, :chunk, :chunk], b_s[:, :chunk].T
    else:
        onehot = jnp.asarray(np.arange(GM_CHUNK)[:, None] % chunk == np.arange(chunk)[None, :], F32)
        hp = lax.Precision.HIGHEST
        ws_t = jnp.einsum("ri,gij,cj->grc", onehot, w_s[:, :chunk, :chunk], onehot, precision=hp)
        b_pos = jnp.dot(onehot, b_s[:, :chunk].T, precision=hp)
    bs_t = jnp.repeat(b_pos, GM_GROUP_W, axis=1)
    return ws_t, bs_t


def _gmlp_branch(xf, w, chunk, *, emit_v, emit_hperm, **side_jobs):
    ws_t, bs_t = _spatial_tiles(w["w_s"], w["b_s"], chunk)
    return _gmlp_call(xf, w["g1"], w["wuv"], w["lng"], w["lnb"], ws_t, bs_t, w["wpa"],
                      chunk=chunk, emit_v=emit_v, emit_hperm=emit_hperm,
                      tm=min(TM_GMLP, xf.shape[0]), **side_jobs)


def _merge_branch(xf, pa, bo, w, *, bo_permuted):
    return _merge_ffn_call(xf, pa, bo, w["g1"], w["wg"], w["bg"], w["wpb"], w["wout"], w["g2"],
                           w["wfi"], w["wfo"], w["gf"], tm=TM_FFN, bo_permuted=bo_permuted)


def kernel(x_prompt, x_sample, state_conv, state_C, state_n, state_m, g_norm1, w_in, b_i, b_f, ln_g, ln_b, w_s, b_s, conv_w, conv_b, hn_g, b_gate, w_proj_a, w_proj_b, w_out, g_norm2, w_ffn_in, w_ffn_out, g_final):
    Bp, Tp, _ = x_prompt.shape
    Bs, Ts, _ = x_sample.shape
    win = w_in[0]
    c_ml = 2 * GM_WIDTH
    c_gate = c_ml + 4 * ML_WIDTH + 2 * ML_HEADS
    n_ml_blocks = -(-(ML_COLS) // TCAST_ROWS)
    win_t = jnp.transpose(win)
    w = dict(
        g1=g_norm1[0][None], g2=g_norm2[0][None], gf=g_final[None],
        wuv=_tcast_call(win_t, 0, c_ml // TCAST_ROWS),
        bg=b_gate[0].reshape(1, 2 * D_MODEL),
        lng=ln_g[0][None], lnb=ln_b[0][None], w_s=w_s[0], b_s=b_s[0],
        bif=jnp.pad(jnp.concatenate([b_i[0], b_f[0]]), (0, GATE_COLS - 2 * ML_HEADS))[None],
        cw=conv_w[0], cb=conv_b[0][None], hng=hn_g[0][None],
        wpa=w_proj_a[0].astype(BF16),
    )

    xpf = x_prompt.reshape(Bp * Tp, D_MODEL)
    pa_p, hp_p, w["wpb"], w["wout"], w["wfi"], w["wfo"], w["w_ml"], w["wg"] = _gmlp_branch(
        xpf, w, GM_CHUNK, emit_v=False, emit_hperm=True,
        cast_weights=(w_proj_b[0], w_out[0], w_ffn_in[0], w_ffn_out[0]),
        tcast_weight=win_t,
        tcast_jobs=((c_ml, n_ml_blocks, 0), (c_gate, 2 * D_MODEL // TCAST_ROWS, n_ml_blocks)))
    bo_p, conv_p, C_p, n_p, m_p = _mixer_ml_call(hp_p.reshape(Bp, Tp, D_MODEL), w["w_ml"], w["bif"],
                                                 w["cw"], w["cb"], w["hng"])
    y_p = _merge_branch(xpf, pa_p, bo_p.reshape(Bp * Tp, ML_WIDTH), w, bo_permuted=True)

    xsf = x_sample.reshape(Bs * Ts, D_MODEL)
    pa_s, vn_s = _gmlp_branch(xsf, w, Ts, emit_v=True, emit_hperm=False)
    zq = _inproj_call(xsf, w["g1"], w["w_ml"], tm=TM_MIX)
    st = (state_C[0], state_n[0], state_m[0])
    bo_s, conv_s, C_s, n_s, m_s = _mlstm_call(zq, jnp.transpose(state_conv[0], (1, 0, 2)), st, w["bif"],
                                              w["cw"], w["cb"], w["hng"], t_valid=Ts,
                                              nb=SAMPLE_SEQS_PER_STEP)
    y_s = _merge_branch(xsf, pa_s, bo_s, w, bo_permuted=False)

    return (y_p.reshape(Bp, Tp, D_MODEL), y_s.reshape(Bs, Ts, D_MODEL),
            conv_p[None], C_p[None], n_p[None], m_p.reshape(1, Bp, ML_HEADS),
            jnp.transpose(conv_s, (1, 0, 2))[None], C_s[None], n_s[None], m_s[None],
            vn_s.reshape(1, Bs, Ts, GM_WIDTH))
```

```python
import functools
import math

import jax
import jax.numpy as jnp
import numpy as np
from jax import lax
from jax.experimental import pallas as pl
from jax.experimental.pallas import tpu as pltpu

D_MODEL = 1024
GM_WIDTH = D_MODEL
GM_GROUPS = 4
GM_GROUP_W = GM_WIDTH // GM_GROUPS
GM_CHUNK = 128
ML_HEADS = 4
ML_HEAD_DIM = D_MODEL // ML_HEADS
ML_WIDTH = ML_HEADS * ML_HEAD_DIM
CONV_W = 4
D_FF = 2816
EPS = 1e-6

LANES = 128
SUBLANES = 8
BF16_ROWS = 16
GATE_COLS = LANES
ML_COLS = 4 * ML_WIDTH + GATE_COLS
VMEM_LIMIT = 56 * 1024 * 1024
GMLP_SUB_ROWS = 256
FFN_SUB_ROWS = 256

F32 = jnp.float32
BF16 = jnp.bfloat16
NEG_BIG = -1e30
LN_INV_K_SCALE = 0.5 * math.log(ML_HEAD_DIM)
NT_DIMS = (((1,), (1,)), ((), ()))
TN_DIMS = (((0,), (0,)), ((), ()))


def _rms(x, g):
    return x * lax.rsqrt(jnp.mean(x * x, axis=-1, keepdims=True) + EPS) * g


def _gelu(x):
    return 0.5 * x * (1.0 + lax.erf(x * (2.0 ** -0.5)))


def _log_sigmoid(x):
    return jnp.minimum(x, 0.0) - jnp.log1p(jnp.exp(-jnp.abs(x)))


def _bdot(a, b):
    return jnp.dot(a, b, preferred_element_type=F32)


def _const_spec(shape):
    nd = len(shape)
    return pl.BlockSpec(shape, lambda *_: (0,) * nd, pipeline_mode=pl.Buffered(1))


def _gmlp_kernel(x_ref, g1_ref, wuv_ref, lng_ref, lnb_ref, ws_ref, bs_ref, wpa_ref, *rest,
                 chunk, emit_v, emit_hperm, n_cast, n_tcast):
    rest = list(rest)
    cast_src = [rest.pop(0) for _ in range(n_cast)]
    tcast_src = [rest.pop(0) for _ in range(n_tcast)]
    pa_ref = rest.pop(0)
    vn_ref = rest.pop(0) if emit_v else None
    hp_ref = rest.pop(0) if emit_hperm else None
    for src in cast_src:
        rest.pop(0)[...] = src[...].astype(BF16)
    for src in tcast_src:
        rest.pop(0)[...] = src[...].T.astype(BF16)
    a_sc = rest.pop(0)
    tm = x_ref.shape[0]
    blk = ws_ref.shape[1]
    hs = rest.pop(0) if emit_hperm else None
    sub = GMLP_SUB_ROWS
    subs = [slice(r0, r0 + sub) for r0 in range(0, tm, sub)]
    hb = []
    for rs in subs:
        hf = _rms(x_ref[rs, :], g1_ref[...])
        hb.append(hf.astype(BF16))
        if emit_hperm:
            for j in range(rs.start // SUBLANES, rs.stop // SUBLANES):
                base = (j // PG) * PL
                dst = _perm_rows(j % PG)
                lo = j * SUBLANES - rs.start
                for c in range(D_MODEL // LANES):
                    hs[c, pl.ds(base + dst.start, SUBLANES, stride=SUBLANES), :] = (
                        hf[lo:lo + SUBLANES, c * LANES:(c + 1) * LANES])
            for c in range(D_MODEL // LANES):
                hp_ref[rs, c * LANES:(c + 1) * LANES] = hs[c, rs, :].astype(BF16)
    zu = [_bdot(h, wuv_ref[:, :GM_WIDTH]) for h in hb]
    zv = [_bdot(h, wuv_ref[:, GM_WIDTH:]) for h in hb]
    r = lax.broadcasted_iota(jnp.int32, (blk, blk), 0)
    c = lax.broadcasted_iota(jnp.int32, (blk, blk), 1)
    keep = c <= r
    if chunk < blk:
        sh = chunk.bit_length() - 1
        keep = jnp.logical_and(keep, (r >> sh) == (c >> sh))
    wsm = [jnp.where(keep, ws_ref[g], 0.0).astype(BF16) for g in range(GM_GROUPS)]
    for si, rs in enumerate(subs):
        u = _gelu(zu[si])
        v = _gelu(zv[si])
        mu = jnp.mean(v, axis=-1, keepdims=True)
        vc = v - mu
        var = jnp.mean(vc * vc, axis=-1, keepdims=True)
        vn = vc * lax.rsqrt(var + EPS) * lng_ref[...] + lnb_ref[...]
        if emit_v:
            vn_ref[rs, :] = vn
        vb = vn.astype(BF16)
        for g in range(GM_GROUPS):
            cs = slice(g * GM_GROUP_W, (g + 1) * GM_GROUP_W)
            for i in range(sub // blk):
                ls = slice(i * blk, (i + 1) * blk)
                s = _bdot(wsm[g], vb[ls, cs]) + bs_ref[:, cs]
                a_sc[rs.start + i * blk:rs.start + (i + 1) * blk, cs] = (u[ls, cs] * s).astype(BF16)
        pa_ref[rs, :] = _bdot(a_sc[rs, :], wpa_ref[...])


def _cast_block_spec(n_rows, n_cols, steps):
    n_blocks = steps
    while n_rows % n_blocks or (n_rows // n_blocks) % BF16_ROWS:
        n_blocks //= 2
    per = steps // n_blocks
    return pl.BlockSpec((n_rows // n_blocks, n_cols), lambda i: (i // per, 0))


TCAST_ROWS = 256


def _tcast_specs(job):
    first_row, n_blocks, first_step = job
    assert first_row % SUBLANES == 0
    blk = lambda i: jnp.clip(i - first_step, 0, n_blocks - 1)
    src = pl.BlockSpec((pl.Element(TCAST_ROWS), pl.Element(D_MODEL)),
                       lambda i: (pl.multiple_of(first_row + TCAST_ROWS * blk(i), SUBLANES), 0))
    dst = pl.BlockSpec((D_MODEL, TCAST_ROWS), lambda i: (0, blk(i)))
    return src, dst, jax.ShapeDtypeStruct((D_MODEL, TCAST_ROWS * n_blocks), BF16)


def _tcast_kernel(src_ref, dst_ref):
    dst_ref[...] = src_ref[...].T.astype(BF16)


def _tcast_call(wt, first_row, n_blocks):
    src, dst, shape = _tcast_specs((first_row, n_blocks, 0))
    return pl.pallas_call(
        _tcast_kernel, grid=(n_blocks,), in_specs=[src], out_specs=dst, out_shape=shape,
        compiler_params=pltpu.CompilerParams(dimension_semantics=("arbitrary",)),
        name="tcast",
    )(wt)


def _gmlp_call(x, g1, wuv, lng, lnb, ws_t, bs_t, wpa, *, chunk, emit_v, emit_hperm, tm,
               cast_weights=(), tcast_weight=None, tcast_jobs=()):
    m = x.shape[0]
    blk = ws_t.shape[1]
    steps = m // tm
    assert steps & (steps - 1) == 0
    assert all(first_step + n_blocks <= steps for _, n_blocks, first_step in tcast_jobs)
    row = pl.BlockSpec((tm, D_MODEL), lambda i: (i, 0))
    out_shape = [jax.ShapeDtypeStruct((m, D_MODEL), F32)]
    out_specs = [row]
    scratch = [pltpu.VMEM((tm, GM_WIDTH), BF16)]
    cast_specs = [_cast_block_spec(cw.shape[0], cw.shape[1], steps) for cw in cast_weights]
    tcast = [_tcast_specs(job) for job in tcast_jobs]
    if emit_v:
        out_shape.append(jax.ShapeDtypeStruct((m, GM_WIDTH), F32))
        out_specs.append(row)
    if emit_hperm:
        assert tm % PL == 0 and GMLP_SUB_ROWS == PL
        out_shape.append(jax.ShapeDtypeStruct((m, D_MODEL), BF16))
        out_specs.append(row)
        scratch.append(pltpu.VMEM((D_MODEL // LANES, tm, LANES), F32))
    out_shape += [jax.ShapeDtypeStruct(cw.shape, BF16) for cw in cast_weights]
    out_shape += [t[2] for t in tcast]
    out_specs += cast_specs + [t[1] for t in tcast]
    return pl.pallas_call(
        functools.partial(_gmlp_kernel, chunk=chunk, emit_v=emit_v, emit_hperm=emit_hperm,
                          n_cast=len(cast_weights), n_tcast=len(tcast)),
        grid=(steps,),
        in_specs=[row, _const_spec((1, D_MODEL)), _const_spec((D_MODEL, 2 * GM_WIDTH)),
                  _const_spec((1, GM_WIDTH)), _const_spec((1, GM_WIDTH)),
                  _const_spec((GM_GROUPS, blk, blk)), _const_spec((blk, GM_WIDTH)),
                  _const_spec((GM_WIDTH, D_MODEL))] + cast_specs + [t[0] for t in tcast],
        out_specs=out_specs,
        out_shape=out_shape,
        scratch_shapes=scratch,
        compiler_params=pltpu.CompilerParams(dimension_semantics=("arbitrary",),
                                             vmem_limit_bytes=VMEM_LIMIT),
        name="gmlp",
    )(x, g1, wuv, lng, lnb, ws_t, bs_t, wpa, *cast_weights, *([tcast_weight] * len(tcast)))


def _ml_weight_specs():
    wide = 2 * ML_WIDTH
    col_block = lambda width, idx: pl.BlockSpec((D_MODEL, width), lambda *_: (0, idx),
                                                pipeline_mode=pl.Buffered(1))
    return [col_block(wide, 0), col_block(wide, 1), col_block(GATE_COLS, 4 * ML_WIDTH // GATE_COLS)]


def _inproj_kernel(x_ref, g1_ref, wqk_ref, wvo_ref, wif_ref, z_ref):
    h = _rms(x_ref[...], g1_ref[...]).astype(BF16)
    z_ref[:, 0:2 * ML_WIDTH] = _bdot(h, wqk_ref[...])
    z_ref[:, 2 * ML_WIDTH:4 * ML_WIDTH] = _bdot(h, wvo_ref[...])
    z_ref[:, 4 * ML_WIDTH:ML_COLS] = _bdot(h, wif_ref[...])


def _inproj_call(x, g1, w_all, *, tm):
    m = x.shape[0]
    return pl.pallas_call(
        _inproj_kernel,
        grid=(m // tm,),
        in_specs=[pl.BlockSpec((tm, D_MODEL), lambda i: (i, 0)), _const_spec((1, D_MODEL))]
        + _ml_weight_specs(),
        out_specs=pl.BlockSpec((tm, ML_COLS), lambda i: (i, 0)),
        out_shape=jax.ShapeDtypeStruct((m, ML_COLS), F32),
        compiler_params=pltpu.CompilerParams(dimension_semantics=("arbitrary",),
                                             vmem_limit_bytes=VMEM_LIMIT),
        name="inproj",
    )(x, g1, w_all, w_all, w_all)


def _mlstm_heads(q_of, k_of, v_of, o_of, causal, ipre, bcum, groups, m0_of, C0_of, n0_of, hng_ref,
                 fill=None, staged=False):
    n_groups = len(groups)
    single = n_groups == 1
    fill = fill or (lambda: None)

    def rows_of(x, g):
        return x if single else x[groups[g][0]:groups[g][0] + groups[g][1]]

    def per_row(vals):
        if single:
            return vals[0]
        return jnp.concatenate([jnp.broadcast_to(v, (groups[g][1], v.shape[1]))
                                for g, v in enumerate(vals)], axis=0)

    a = ipre - bcum
    a_t = a.T
    m_rows = per_row([m0_of(g) for g in range(n_groups)])

    def prepare(h):
        p = {}
        a2 = jnp.where(causal, a_t[h:h + 1, :], -jnp.inf)
        p["mc"] = mc = jnp.maximum(jnp.max(a2, axis=1, keepdims=True), m_rows[:, h:h + 1])
        p["m_last"] = m_last = [mc[grp[2]:grp[2] + 1, :] for grp in groups]
        p["dm"] = jnp.exp(a2 - (mc + LN_INV_K_SCALE))
        p["w_inter"] = jnp.exp(m_rows[:, h:h + 1] - mc)
        p["w_col"] = w_col = jnp.exp(a[:, h:h + 1] - (per_row(m_last) + LN_INV_K_SCALE))
        p["decay"] = [jnp.exp(m0_of(g)[:, h:h + 1] - m_last[g]) for g in range(n_groups)]
        p["q"] = q = q_of(h)
        p["k"] = k = k_of(h)
        v = v_of(h)
        p["qb"], p["kb"], p["vb"] = q.astype(BF16), k.astype(BF16), v.astype(BF16)
        p["vw"] = (v * w_col).astype(BF16)
        p["c_old"] = [C0_of(g, h) for g in range(n_groups)]
        p["n_old"] = [n0_of(g, h) for g in range(n_groups)]
        return p

    def first_matmuls(p):
        p["qk"] = lax.dot_general(p["qb"], p["kb"], NT_DIMS, preferred_element_type=F32)
        p["qc"] = [lax.dot_general(rows_of(p["qb"], g), p["c_old"][g].astype(BF16), NT_DIMS,
                                   preferred_element_type=F32) for g in range(n_groups)]
        if single:
            n_rows = jnp.broadcast_to(p["n_old"][0], (LANES, ML_HEAD_DIM)).astype(BF16)
            p["qn"] = lax.dot_general(p["qb"], n_rows, NT_DIMS, preferred_element_type=F32)[:, 0:1]
        else:
            p["qn"] = jnp.sum(p["q"] * per_row(p["n_old"]), axis=1, keepdims=True)

    def second_matmuls(p):
        p["s"] = s = p["dm"] * p["qk"]
        p["sv"] = _bdot(s.astype(BF16), p["vb"])
        p["cupd"] = [lax.dot_general(rows_of(p["vw"], g), rows_of(p["kb"], g), TN_DIMS,
                                     preferred_element_type=F32) for g in range(n_groups)]

    def finish(h, p):
        qc_rows = p["qc"][0] if single else jnp.concatenate(p["qc"], axis=0)
        num = p["w_inter"] * qc_rows + p["sv"]
        den = p["w_inter"] * p["qn"] + jnp.sum(p["s"], axis=1, keepdims=True)
        hcur = num / jnp.maximum(jnp.abs(den), jnp.exp(-(bcum[:, h:h + 1] + p["mc"])))
        mu = jnp.mean(hcur, axis=1, keepdims=True)
        hc = hcur - mu
        var = jnp.mean(hc * hc, axis=1, keepdims=True)
        hs = slice(h * ML_HEAD_DIM, (h + 1) * ML_HEAD_DIM)
        out = jax.nn.sigmoid(o_of(h)) * (hc * lax.rsqrt(var + EPS) * hng_ref[:, hs])
        kw = p["k"] * p["w_col"]
        c_new = [p["decay"][g] * p["c_old"][g] + p["cupd"][g] for g in range(n_groups)]
        n_new = [p["decay"][g] * p["n_old"][g] + jnp.sum(rows_of(kw, g), axis=0, keepdims=True)
                 for g in range(n_groups)]
        return out, c_new, n_new

    H = ML_HEADS
    per_head = [None] * H
    done = [None] * H

    def collect():
        outs = [done[h][0] for h in range(H)]
        C_new = [[done[h][1][g] for h in range(H)] for g in range(n_groups)]
        n_new = [[done[h][2][g] for h in range(H)] for g in range(n_groups)]
        m_new = []
        for g, grp in enumerate(groups):
            row = m0_of(g)
            lane = lax.broadcasted_iota(jnp.int32, row.shape, 1)
            for h in range(H):
                row = jnp.where(lane == h,
                                bcum[grp[2]:grp[2] + 1, h:h + 1] + per_head[h]["m_last"][g], row)
            m_new.append(row)
        return outs, C_new, n_new, m_new

    if staged:
        def stage_first():
            for h in range(H):
                per_head[h] = prepare(h)
            for h in range(H):
                first_matmuls(per_head[h])

        def stage_second():
            for h in range(H):
                second_matmuls(per_head[h])

        def stage_finish():
            for h in range(H):
                done[h] = finish(h, per_head[h])

        return [stage_first, stage_second, stage_finish], collect

    assert single
    per_head[0] = prepare(0)
    fill()
    first_matmuls(per_head[0])
    if H > 1:
        per_head[1] = prepare(1)
    fill()
    for h in range(H):
        second_matmuls(per_head[h])
        if h + 1 < H:
            first_matmuls(per_head[h + 1])
        fill()
        if h + 2 < H:
            per_head[h + 2] = prepare(h + 2)
        done[h] = finish(h, per_head[h])
        fill()
    return collect()


def _sample_block_stages(zq_ref, cst_ref, C0_ref, n0_ref, m0_ref, bif_ref, cw_ref, cb_ref, hng_ref,
                         bo_ref, conv_ref, C_ref, n_ref, m_ref, zp, xp, *, t_valid):
    nb = C0_ref.shape[0]
    L = BF16_ROWS
    R = nb * L
    assert t_valid <= SUBLANES
    box = {}

    def cols(c0, width):
        return zp[:, :, c0:c0 + width].reshape(R, width)

    def head_cols(x, base, h):
        return x[:, base + h * ML_HEAD_DIM:base + (h + 1) * ML_HEAD_DIM]

    def stage_conv_gates():
        for bb in range(nb):
            zp[bb, 0:t_valid, :] = zq_ref[bb * t_valid:(bb + 1) * t_valid, :]
            for j in range(CONV_W - 1):
                row = SUBLANES - (CONV_W - 1) + j
                xp[bb, row:row + 1, :] = cst_ref[j, bb:bb + 1, :]
            xp[bb, SUBLANES:2 * SUBLANES, :] = zp[bb, 0:SUBLANES, 0:2 * ML_WIDTH]
            for j in range(CONV_W - 1):
                row = SUBLANES + t_valid - (CONV_W - 1) + j
                conv_ref[j, bb:bb + 1, :] = xp[bb, row:row + 1, :]
        qk_rows = []
        for bb in range(nb):
            acc = cb_ref[...]
            for j in range(CONV_W):
                off = SUBLANES - (CONV_W - 1) + j
                acc = acc + cw_ref[j:j + 1, :] * xp[bb, off:off + SUBLANES, :]
            qk_rows += [acc, jnp.zeros((L - SUBLANES, 2 * ML_WIDTH), F32)]
        qk = jnp.concatenate(qk_rows, axis=0)
        qk = qk * jax.nn.sigmoid(qk)

        zif = cols(4 * ML_WIDTH, GATE_COLS) + bif_ref[...]
        live = (lax.broadcasted_iota(jnp.int32, (R, GATE_COLS), 0) & (L - 1)) < t_valid
        ipre = jnp.where(live, zif, NEG_BIG)
        logf = jnp.where(live, pltpu.roll(_log_sigmoid(zif), GATE_COLS - ML_HEADS, axis=1), 0.0)
        sub_g = lax.broadcasted_iota(jnp.int32, (SUBLANES, GATE_COLS), 0)
        parts = []
        for bb in range(nb):
            incl = logf[bb * L:bb * L + SUBLANES, :]
            for step in (1, 2, 4):
                incl = incl + jnp.where(sub_g >= step, pltpu.roll(incl, step, axis=0), 0.0)
            total = jnp.broadcast_to(incl[SUBLANES - 1:SUBLANES, :], (L - SUBLANES, GATE_COLS))
            parts += [incl, total]
        bcum = jnp.concatenate(parts, axis=0)
        r = lax.broadcasted_iota(jnp.int32, (R, R), 0)
        c = lax.broadcasted_iota(jnp.int32, (R, R), 1)
        sh = L.bit_length() - 1
        causal = jnp.logical_and(c <= r, (r >> sh) == (c >> sh))
        live_w = (lax.broadcasted_iota(jnp.int32, (R, ML_HEAD_DIM), 0) & (L - 1)) < t_valid
        groups = [(bb * L, L, bb * L + L - 1) for bb in range(nb)]
        box["stages"], box["collect"] = _mlstm_heads(
            q_of=lambda h: head_cols(qk, 0, h),
            k_of=lambda h: jnp.where(live_w, head_cols(qk, ML_WIDTH, h), 0.0),
            v_of=lambda h: jnp.where(live_w, cols(2 * ML_WIDTH + h * ML_HEAD_DIM, ML_HEAD_DIM), 0.0),
            o_of=lambda h: cols(3 * ML_WIDTH + h * ML_HEAD_DIM, ML_HEAD_DIM),
            causal=causal, ipre=ipre, bcum=bcum, groups=groups,
            m0_of=lambda g: m0_ref[g:g + 1, :], C0_of=lambda g, h: C0_ref[g, h],
            n0_of=lambda g, h: n0_ref[g, h:h + 1, :], hng_ref=hng_ref, staged=True)

    def stage_finish_store():
        box["stages"][2]()
        outs, C_new, n_new, m_new = box["collect"]()
        for bb in range(nb):
            for h in range(ML_HEADS):
                hs = slice(h * ML_HEAD_DIM, (h + 1) * ML_HEAD_DIM)
                bo_ref[bb * t_valid:(bb + 1) * t_valid, hs] = outs[h][bb * L:bb * L + t_valid, :]
                C_ref[bb, h] = C_new[bb][h]
                n_ref[bb, h:h + 1, :] = n_new[bb][h]
            m_ref[bb:bb + 1, :] = m_new[bb]

    return [stage_conv_gates, lambda: box["stages"][0](), lambda: box["stages"][1](),
            stage_finish_store]


PL = 256
PG = PL // SUBLANES
TAIL = (CONV_W - 1) * SUBLANES
PROJ_PIECE_COLS = 512


def _perm_rows(j):
    return pl.ds((PL // 4) * (j % 4) + j // 4, SUBLANES, stride=SUBLANES)


def _mixer_ml_kernel(h_ref, wqk_ref, wvo_ref, wif_ref, bif_ref, cw_ref, cb_ref, hng_ref,
                     zq_s, cst_s, C0_s, n0_s, m0_s,
                     bo_ref, conv_ref, C_ref, n_ref, m_ref, bo_s, conv_s, C_s, n_s, m_s,
                     zqk_buf, zvo_buf, zif_buf, tail, zp, xp, *, chunks_per_seq, t_sample):
    s = pl.program_id(0)

    @pl.when(s == 0)
    def _():
        zqk_buf[1] = jnp.zeros(zqk_buf.shape[1:], F32)
        zvo_buf[1] = jnp.zeros(zvo_buf.shape[1:], F32)
        zif_buf[1] = jnp.zeros(zif_buf.shape[1:], F32)
        zp[...] = jnp.zeros(zp.shape, F32)

    @pl.when(jnp.maximum(s - 1, 0) % chunks_per_seq == 0)
    def _():
        tail[...] = jnp.zeros(tail.shape, F32)
        C_ref[...] = jnp.zeros(C_ref.shape, F32)
        n_ref[...] = jnp.zeros(n_ref.shape, F32)
        m_ref[...] = jnp.zeros(m_ref.shape, F32)

    def sample_stages():
        return _sample_block_stages(zq_s, cst_s, C0_s, n0_s, m0_s, bif_ref, cw_ref, cb_ref, hng_ref,
                                    bo_s, conv_s, C_s, n_s, m_s, zp, xp, t_valid=t_sample)

    step = functools.partial(_mixer_ml_step, h_ref, wqk_ref, wvo_ref, wif_ref, bif_ref, cw_ref,
                             cb_ref, hng_ref, bo_ref, conv_ref, C_ref, n_ref, m_ref,
                             zqk_buf, zvo_buf, zif_buf, tail, sample_stages=sample_stages)

    @pl.when(s % 2 == 0)
    def _():
        step(rd=1, wr=0)

    @pl.when(s % 2 == 1)
    def _():
        step(rd=0, wr=1)


def _mixer_ml_step(h_ref, wqk_ref, wvo_ref, wif_ref, bif_ref, cw_ref, cb_ref, hng_ref,
                   bo_ref, conv_ref, C_ref, n_ref, m_ref, zqk_buf, zvo_buf, zif_buf, tail,
                   *, rd, wr, sample_stages):
    h = h_ref[...]

    def piece(w_ref, buf, c0, width):
        def emit():
            buf[wr, :, c0:c0 + width] = _bdot(h, w_ref[:, c0:c0 + width])
        return emit

    pieces = [piece(w_ref, buf, c0, PROJ_PIECE_COLS)
              for w_ref, buf in ((wqk_ref, zqk_buf), (wvo_ref, zvo_buf))
              for c0 in range(0, 2 * ML_WIDTH, PROJ_PIECE_COLS)]
    pieces.append(piece(wif_ref, zif_buf, 0, GATE_COLS))
    for gap, stage in zip(range(0, len(pieces), 2), sample_stages()):
        pieces[gap] = (lambda emit, stage: lambda: (emit(), stage()))(pieces[gap], stage)
    pieces = iter(pieces)

    def fill():
        emit = next(pieces, None)
        if emit is not None:
            emit()

    fill()
    zqk = zqk_buf.at[rd]
    zvo = zvo_buf.at[rd]
    zif = zif_buf[rd] + bif_ref[...]

    zqk_tail = zqk[PL - TAIL:, :]
    sub = lax.broadcasted_iota(jnp.int32, (SUBLANES, 2 * ML_WIDTH), 0)
    wrapped = []
    for g in range(CONV_W - 1):
        cur = pltpu.roll(zqk_tail[g * SUBLANES:(g + 1) * SUBLANES], 1, axis=0)
        prev = pltpu.roll(tail[g * SUBLANES:(g + 1) * SUBLANES, :], 1, axis=0)
        wrapped.append(jnp.where(sub == 0, prev, cur))
    wrapped = jnp.concatenate(wrapped, axis=0)
    tail[...] = zqk_tail
    conv_ref[...] = jnp.concatenate(
        [zqk_tail[g * SUBLANES + SUBLANES - 1:(g + 1) * SUBLANES, :] for g in range(CONV_W - 1)], axis=0)

    def conv_silu(c0, width):
        cs = slice(c0, c0 + width)
        acc = cb_ref[:, cs] + cw_ref[CONV_W - 1:CONV_W, cs] * zqk[:, cs]
        for d in range(1, CONV_W):
            shifted = jnp.concatenate(
                [wrapped[TAIL - d * SUBLANES:, cs], zqk[:PL - d * SUBLANES, cs]], axis=0)
            acc = acc + cw_ref[CONV_W - 1 - d:CONV_W - d, cs] * shifted
        return acc * jax.nn.sigmoid(acc)

    pr = lax.broadcasted_iota(jnp.int32, (PL, PL), 0)
    pc = lax.broadcasted_iota(jnp.int32, (PL, PL), 1)
    causal = ((pc >> 3) + PG * (pc & 7)) <= ((pr >> 3) + PG * (pr & 7))

    logf = pltpu.roll(_log_sigmoid(zif), GATE_COLS - ML_HEADS, axis=1)
    run, partial = None, []
    for n in range(PG):
        blk = logf[n * SUBLANES:(n + 1) * SUBLANES, :]
        run = blk if run is None else run + blk
        partial.append(run)
    sub_g = lax.broadcasted_iota(jnp.int32, (SUBLANES, GATE_COLS), 0)
    incl = run
    for step in (1, 2, 4):
        incl = incl + jnp.where(sub_g >= step, pltpu.roll(incl, step, axis=0), 0.0)
    earlier = incl - run
    bcum = jnp.concatenate([p + earlier for p in partial], axis=0)

    def head_cols(base, h):
        return zvo[:, base + h * ML_HEAD_DIM:base + (h + 1) * ML_HEAD_DIM]

    outs, C_new, n_new, m_new = _mlstm_heads(
        q_of=lambda h: conv_silu(h * ML_HEAD_DIM, ML_HEAD_DIM),
        k_of=lambda h: conv_silu(ML_WIDTH + h * ML_HEAD_DIM, ML_HEAD_DIM),
        v_of=lambda h: head_cols(0, h),
        o_of=lambda h: head_cols(ML_WIDTH, h),
        causal=causal, ipre=zif, bcum=bcum, groups=[(0, PL, PL - 1)],
        m0_of=lambda g: m_ref[...], C0_of=lambda g, h: C_ref[h], n0_of=lambda g, h: n_ref[h:h + 1, :],
        hng_ref=hng_ref, fill=fill)
    while next(pieces, None) is not None:
        raise AssertionError("projection pieces left over")
    for h in range(ML_HEADS):
        bo_ref[:, h * ML_HEAD_DIM:(h + 1) * ML_HEAD_DIM] = outs[h].astype(BF16)
        C_ref[h] = C_new[0][h]
        n_ref[h:h + 1, :] = n_new[0][h]
    m_ref[...] = m_new[0]


def _mixer_ml_call(hp, w_ml, bif, cw, cb, hng, zq_s, conv_s, C_s, n_s, m_s, *, t_sample):
    B, T, _ = hp.shape
    cps = T // PL
    n_chunks = B * cps
    Bs = C_s.shape[0]
    nb = Bs // n_chunks
    assert nb * n_chunks == Bs
    proj = lambda s: jnp.minimum(s, n_chunks - 1)
    math = lambda s: jnp.maximum(s - 1, 0)
    per_b = lambda shape: pl.BlockSpec((None,) + shape,
                                       lambda s: (math(s) // cps,) + (0,) * len(shape))
    per_blk = lambda shape: pl.BlockSpec((nb,) + shape, lambda s: (proj(s),) + (0,) * len(shape))
    rows_blk = lambda width: pl.BlockSpec((nb * t_sample, width), lambda s: (proj(s), 0))
    conv_blk = pl.BlockSpec((CONV_W - 1, None, nb, 2 * ML_WIDTH), lambda s: (0, proj(s), 0, 0))
    m_blk = pl.BlockSpec((None, nb, ML_HEADS), lambda s: (proj(s), 0, 0))
    state_specs = [conv_blk, per_blk((ML_HEADS, ML_HEAD_DIM, ML_HEAD_DIM)),
                   per_blk((ML_HEADS, ML_HEAD_DIM)), m_blk]
    conv_view = (CONV_W - 1, n_chunks, nb, 2 * ML_WIDTH)
    out_shape = [jax.ShapeDtypeStruct((B, T, ML_WIDTH), BF16),
                 jax.ShapeDtypeStruct((B, CONV_W - 1, 2 * ML_WIDTH), F32),
                 jax.ShapeDtypeStruct((B, ML_HEADS, ML_HEAD_DIM, ML_HEAD_DIM), F32),
                 jax.ShapeDtypeStruct((B, ML_HEADS, ML_HEAD_DIM), F32),
                 jax.ShapeDtypeStruct((B, 1, ML_HEADS), F32),
                 jax.ShapeDtypeStruct((Bs * t_sample, ML_WIDTH), F32),
                 jax.ShapeDtypeStruct(conv_view, F32),
                 jax.ShapeDtypeStruct(C_s.shape, F32),
                 jax.ShapeDtypeStruct(n_s.shape, F32),
                 jax.ShapeDtypeStruct((n_chunks, nb, ML_HEADS), F32)]
    out_specs = [pl.BlockSpec((None, PL, ML_WIDTH), lambda s: (math(s) // cps, math(s) % cps, 0)),
                 per_b((CONV_W - 1, 2 * ML_WIDTH)),
                 per_b((ML_HEADS, ML_HEAD_DIM, ML_HEAD_DIM)), per_b((ML_HEADS, ML_HEAD_DIM)),
                 per_b((1, ML_HEADS)),
                 rows_blk(ML_WIDTH)] + state_specs
    outs = list(pl.pallas_call(
        functools.partial(_mixer_ml_kernel, chunks_per_seq=cps, t_sample=t_sample),
        grid=(n_chunks + 1,),
        in_specs=[pl.BlockSpec((None, PL, D_MODEL), lambda s: (proj(s) // cps, proj(s) % cps, 0))]
        + _ml_weight_specs()
        + [_const_spec((1, GATE_COLS)), _const_spec((CONV_W, 2 * ML_WIDTH)),
           _const_spec((1, 2 * ML_WIDTH)), _const_spec((1, ML_WIDTH))]
        + [rows_blk(ML_COLS)] + state_specs,
        out_specs=out_specs,
        out_shape=out_shape,
        scratch_shapes=[pltpu.VMEM((2, PL, 2 * ML_WIDTH), F32),
                        pltpu.VMEM((2, PL, 2 * ML_WIDTH), F32),
                        pltpu.VMEM((2, PL, GATE_COLS), F32),
                        pltpu.VMEM((TAIL, 2 * ML_WIDTH), F32),
                        pltpu.VMEM((nb, BF16_ROWS, ML_COLS), F32),
                        pltpu.VMEM((nb, 2 * SUBLANES, 2 * ML_WIDTH), F32)],
        compiler_params=pltpu.CompilerParams(dimension_semantics=("arbitrary",),
                                             vmem_limit_bytes=VMEM_LIMIT),
        name="mixer_ml",
    )(hp, w_ml, w_ml, w_ml, bif, cw, cb, hng,
      zq_s, conv_s.reshape(conv_view), C_s, n_s, m_s.reshape(n_chunks, nb, ML_HEADS)))
    outs[6] = outs[6].reshape(conv_s.shape)
    outs[9] = outs[9].reshape(m_s.shape)
    return outs


def _merge_ffn_kernel(x_ref, pa_ref, bo_ref, g1_ref, wg_ref, bg_ref, wpb_ref, wout_ref,
                      g2_ref, wfi_ref, wfo_ref, gf_ref, y_ref, *scratch, bo_permuted):
    tm = x_ref.shape[0]
    subs = [slice(r0, r0 + FFN_SUB_ROWS) for r0 in range(0, tm, FFN_SUB_ROWS)]
    x = [x_ref[rs, :] for rs in subs]
    h = [_rms(xi, g1_ref[...]).astype(BF16) for xi in x]
    gab = [_bdot(hi, wg_ref[...]) + bg_ref[...] for hi in h]
    pb = [_bdot(bo_ref[rs, :].astype(BF16), wpb_ref[...]) for rs in subs]
    if bo_permuted:
        (us,) = scratch
        for si, rs in enumerate(subs):
            for c in range(D_MODEL // LANES):
                us[c, rs, :] = pb[si][:, c * LANES:(c + 1) * LANES]
            pb[si] = jnp.concatenate(
                [jnp.concatenate([us[c, pl.ds(rs.start + _perm_rows(j).start, SUBLANES, stride=SUBLANES), :]
                                  for j in range(PG)], axis=0)
                 for c in range(D_MODEL // LANES)], axis=1)
    merged = [(jax.nn.sigmoid(g[:, :D_MODEL]) * pa_ref[rs, :]
               + jax.nn.sigmoid(g[:, D_MODEL:]) * p).astype(BF16) for g, p, rs in zip(gab, pb, subs)]
    x1 = [xi + _bdot(mi, wout_ref[...]) for xi, mi in zip(x, merged)]
    h2 = [_rms(xi, g2_ref[...]).astype(BF16) for xi in x1]
    gu = [_bdot(hi, wfi_ref[...]) for hi in h2]
    hid = [(g[:, :D_FF] * jax.nn.sigmoid(g[:, :D_FF]) * g[:, D_FF:]).astype(BF16) for g in gu]
    x2 = [xi + _bdot(hi, wfo_ref[...]) for xi, hi in zip(x1, hid)]
    for rs, xi in zip(subs, x2):
        y_ref[rs, :] = _rms(xi, gf_ref[...])


def _merge_ffn_call(x, pa, bo, g1, wg, bg, wpb, wout, g2, wfi, wfo, gf, *, tm, bo_permuted):
    m = x.shape[0]
    row = pl.BlockSpec((tm, D_MODEL), lambda i: (i, 0))
    assert not bo_permuted or FFN_SUB_ROWS == PL
    return pl.pallas_call(
        functools.partial(_merge_ffn_kernel, bo_permuted=bo_permuted),
        grid=(m // tm,),
        in_specs=[row, row, row, _const_spec((1, D_MODEL)), _const_spec((D_MODEL, 2 * D_MODEL)),
                  _const_spec((1, 2 * D_MODEL)), _const_spec((ML_WIDTH, D_MODEL)),
                  _const_spec((D_MODEL, D_MODEL)), _const_spec((1, D_MODEL)),
                  _const_spec((D_MODEL, 2 * D_FF)), _const_spec((D_FF, D_MODEL)),
                  _const_spec((1, D_MODEL))],
        out_specs=row,
        out_shape=jax.ShapeDtypeStruct((m, D_MODEL), F32),
        scratch_shapes=[pltpu.VMEM((D_MODEL // LANES, tm, LANES), F32)] if bo_permuted else [],
        compiler_params=pltpu.CompilerParams(dimension_semantics=("arbitrary",),
                                             vmem_limit_bytes=VMEM_LIMIT),
        name="merge_ffn",
    )(x, pa, bo, g1, wg, bg, wpb, wout, g2, wfi, wfo, gf)


TM_MIX = 512
TM_GMLP = 512
TM_FFN = 512


def _spatial_tiles(w_s, b_s, chunk):
    if chunk == GM_CHUNK:
        ws_t, b_pos = w_s[:, :chunk, :chunk], b_s[:, :chunk].T
    else:
        onehot = jnp.asarray(np.arange(GM_CHUNK)[:, None] % chunk == np.arange(chunk)[None, :], F32)
        hp = lax.Precision.HIGHEST
        ws_t = jnp.einsum("ri,gij,cj->grc", onehot, w_s[:, :chunk, :chunk], onehot, precision=hp)
        b_pos = jnp.dot(onehot, b_s[:, :chunk].T, precision=hp)
    bs_t = jnp.repeat(b_pos, GM_GROUP_W, axis=1)
    return ws_t, bs_t


def _gmlp_branch(xf, w, chunk, *, emit_v, emit_hperm, **side_jobs):
    ws_t, bs_t = _spatial_tiles(w["w_s"], w["b_s"], chunk)
    return _gmlp_call(xf, w["g1"], w["wuv"], w["lng"], w["lnb"], ws_t, bs_t, w["wpa"],
                      chunk=chunk, emit_v=emit_v, emit_hperm=emit_hperm,
                      tm=min(TM_GMLP, xf.shape[0]), **side_jobs)


def _merge_branch(xf, pa, bo, w, *, bo_permuted):
    return _merge_ffn_call(xf, pa, bo, w["g1"], w["wg"], w["bg"], w["wpb"], w["wout"], w["g2"],
                           w["wfi"], w["wfo"], w["gf"], tm=TM_FFN, bo_permuted=bo_permuted)


def kernel(x_prompt, x_sample, state_conv, state_C, state_n, state_m, g_norm1, w_in, b_i, b_f, ln_g, ln_b, w_s, b_s, conv_w, conv_b, hn_g, b_gate, w_proj_a, w_proj_b, w_out, g_norm2, w_ffn_in, w_ffn_out, g_final):
    Bp, Tp, _ = x_prompt.shape
    Bs, Ts, _ = x_sample.shape
    win = w_in[0]
    c_ml = 2 * GM_WIDTH
    c_gate = c_ml + 4 * ML_WIDTH + 2 * ML_HEADS
    n_ml_blocks = -(-(ML_COLS) // TCAST_ROWS)
    win_t = jnp.transpose(win)
    w = dict(
        g1=g_norm1[0][None], g2=g_norm2[0][None], gf=g_final[None],
        wuv=_tcast_call(win_t, 0, c_ml // TCAST_ROWS),
        bg=b_gate[0].reshape(1, 2 * D_MODEL),
        lng=ln_g[0][None], lnb=ln_b[0][None], w_s=w_s[0], b_s=b_s[0],
        bif=jnp.pad(jnp.concatenate([b_i[0], b_f[0]]), (0, GATE_COLS - 2 * ML_HEADS))[None],
        cw=conv_w[0], cb=conv_b[0][None], hng=hn_g[0][None],
        wpa=w_proj_a[0].astype(BF16),
    )

    xpf = x_prompt.reshape(Bp * Tp, D_MODEL)
    pa_p, hp_p, w["wpb"], w["wout"], w["wfi"], w["wfo"], w["w_ml"], w["wg"] = _gmlp_branch(
        xpf, w, GM_CHUNK, emit_v=False, emit_hperm=True,
        cast_weights=(w_proj_b[0], w_out[0], w_ffn_in[0], w_ffn_out[0]),
        tcast_weight=win_t,
        tcast_jobs=((c_ml, n_ml_blocks, 0), (c_gate, 2 * D_MODEL // TCAST_ROWS, n_ml_blocks)))

    xsf = x_sample.reshape(Bs * Ts, D_MODEL)
    pa_s, vn_s = _gmlp_branch(xsf, w, Ts, emit_v=True, emit_hperm=False)
    zq_s = _inproj_call(xsf, w["g1"], w["w_ml"], tm=TM_MIX)

    (bo_p, conv_p, C_p, n_p, m_p, bo_s, conv_s, C_s, n_s, m_s) = _mixer_ml_call(
        hp_p.reshape(Bp, Tp, D_MODEL), w["w_ml"], w["bif"], w["cw"], w["cb"], w["hng"],
        zq_s, jnp.transpose(state_conv[0], (1, 0, 2)), state_C[0], state_n[0], state_m[0],
        t_sample=Ts)
    y_p = _merge_branch(xpf, pa_p, bo_p.reshape(Bp * Tp, ML_WIDTH), w, bo_permuted=True)
    y_s = _merge_branch(xsf, pa_s, bo_s, w, bo_permuted=False)

    return (y_p.reshape(Bp, Tp, D_MODEL), y_s.reshape(Bs, Ts, D_MODEL),
            conv_p[None], C_p[None], n_p[None], m_p.reshape(1, Bp, ML_HEADS),
            jnp.transpose(conv_s, (1, 0, 2))[None], C_s[None], n_s[None], m_s[None],
            vn_s.reshape(1, Bs, Ts, GM_WIDTH))
```

```python
import functools
import math

import jax
import jax.numpy as jnp
import numpy as np
from jax import lax
from jax.experimental import pallas as pl
from jax.experimental.pallas import tpu as pltpu

D_MODEL = 1024
GM_WIDTH = D_MODEL
GM_GROUPS = 4
GM_GROUP_W = GM_WIDTH // GM_GROUPS
GM_CHUNK = 128
ML_HEADS = 4
ML_HEAD_DIM = D_MODEL // ML_HEADS
ML_WIDTH = ML_HEADS * ML_HEAD_DIM
CONV_W = 4
D_FF = 2816
EPS = 1e-6

LANES = 128
SUBLANES = 8
BF16_ROWS = 16
GATE_COLS = LANES
ML_COLS = 4 * ML_WIDTH + GATE_COLS
VMEM_LIMIT = 56 * 1024 * 1024
GMLP_SUB_ROWS = 256
FFN_SUB_ROWS = 256

F32 = jnp.float32
BF16 = jnp.bfloat16
NEG_BIG = -1e30
LN_INV_K_SCALE = 0.5 * math.log(ML_HEAD_DIM)
NT_DIMS = (((1,), (1,)), ((), ()))
TN_DIMS = (((0,), (0,)), ((), ()))


def _rms(x, g):
    return x * lax.rsqrt(jnp.mean(x * x, axis=-1, keepdims=True) + EPS) * g


def _gelu(x):
    return 0.5 * x * (1.0 + lax.erf(x * (2.0 ** -0.5)))


def _log_sigmoid(x):
    return jnp.minimum(x, 0.0) - jnp.log1p(jnp.exp(-jnp.abs(x)))


def _bdot(a, b):
    return jnp.dot(a, b, preferred_element_type=F32)


def _const_spec(shape):
    nd = len(shape)
    return pl.BlockSpec(shape, lambda *_: (0,) * nd, pipeline_mode=pl.Buffered(1))


def _gmlp_kernel(x_ref, g1_ref, wuv_ref, lng_ref, lnb_ref, ws_ref, bs_ref, wpa_ref, *rest,
                 chunk, emit_v, emit_h, n_cast, n_tcast):
    rest = list(rest)
    cast_src = [rest.pop(0) for _ in range(n_cast)]
    tcast_src = [rest.pop(0) for _ in range(n_tcast)]
    pa_ref = rest.pop(0)
    vn_ref = rest.pop(0) if emit_v else None
    hp_ref = rest.pop(0) if emit_h else None
    for src in cast_src:
        rest.pop(0)[...] = src[...].astype(BF16)
    for src in tcast_src:
        rest.pop(0)[...] = src[...].T.astype(BF16)
    a_sc = rest.pop(0)
    tm = x_ref.shape[0]
    blk = ws_ref.shape[1]
    sub = GMLP_SUB_ROWS
    subs = [slice(r0, r0 + sub) for r0 in range(0, tm, sub)]
    hb = []
    for rs in subs:
        hf = _rms(x_ref[rs, :], g1_ref[...])
        hb.append(hf.astype(BF16))
        if emit_h:
            hp_ref[rs, :] = hf
    zu = [_bdot(h, wuv_ref[:, :GM_WIDTH]) for h in hb]
    zv = [_bdot(h, wuv_ref[:, GM_WIDTH:]) for h in hb]
    r = lax.broadcasted_iota(jnp.int32, (blk, blk), 0)
    c = lax.broadcasted_iota(jnp.int32, (blk, blk), 1)
    keep = c <= r
    if chunk < blk:
        sh = chunk.bit_length() - 1
        keep = jnp.logical_and(keep, (r >> sh) == (c >> sh))
    wsm = [jnp.where(keep, ws_ref[g], 0.0).astype(BF16) for g in range(GM_GROUPS)]
    for si, rs in enumerate(subs):
        u = _gelu(zu[si])
        v = _gelu(zv[si])
        mu = jnp.mean(v, axis=-1, keepdims=True)
        vc = v - mu
        var = jnp.mean(vc * vc, axis=-1, keepdims=True)
        vn = vc * lax.rsqrt(var + EPS) * lng_ref[...] + lnb_ref[...]
        if emit_v:
            vn_ref[rs, :] = vn
        vb = vn.astype(BF16)
        for g in range(GM_GROUPS):
            cs = slice(g * GM_GROUP_W, (g + 1) * GM_GROUP_W)
            for i in range(sub // blk):
                ls = slice(i * blk, (i + 1) * blk)
                s = _bdot(wsm[g], vb[ls, cs]) + bs_ref[:, cs]
                a_sc[rs.start + i * blk:rs.start + (i + 1) * blk, cs] = (u[ls, cs] * s).astype(BF16)
        pa_ref[rs, :] = _bdot(a_sc[rs, :], wpa_ref[...])


def _cast_block_spec(n_rows, n_cols, steps):
    n_blocks = steps
    while n_rows % n_blocks or (n_rows // n_blocks) % BF16_ROWS:
        n_blocks //= 2
    per = steps // n_blocks
    return pl.BlockSpec((n_rows // n_blocks, n_cols), lambda i: (i // per, 0))


TCAST_ROWS = 256


def _tcast_specs(job):
    first_row, n_blocks, first_step = job
    assert first_row % SUBLANES == 0
    blk = lambda i: jnp.clip(i - first_step, 0, n_blocks - 1)
    src = pl.BlockSpec((pl.Element(TCAST_ROWS), pl.Element(D_MODEL)),
                       lambda i: (pl.multiple_of(first_row + TCAST_ROWS * blk(i), SUBLANES), 0))
    dst = pl.BlockSpec((D_MODEL, TCAST_ROWS), lambda i: (0, blk(i)))
    return src, dst, jax.ShapeDtypeStruct((D_MODEL, TCAST_ROWS * n_blocks), BF16)


def _tcast_kernel(src_ref, dst_ref):
    dst_ref[...] = src_ref[...].T.astype(BF16)


def _tcast_call(wt, first_row, n_blocks):
    src, dst, shape = _tcast_specs((first_row, n_blocks, 0))
    return pl.pallas_call(
        _tcast_kernel, grid=(n_blocks,), in_specs=[src], out_specs=dst, out_shape=shape,
        compiler_params=pltpu.CompilerParams(dimension_semantics=("arbitrary",)),
        name="tcast",
    )(wt)


def _gmlp_call(x, g1, wuv, lng, lnb, ws_t, bs_t, wpa, *, chunk, emit_v, emit_h, tm,
               cast_weights=(), tcast_weight=None, tcast_jobs=()):
    m = x.shape[0]
    blk = ws_t.shape[1]
    steps = m // tm
    assert steps & (steps - 1) == 0
    assert all(first_step + n_blocks <= steps for _, n_blocks, first_step in tcast_jobs)
    row = pl.BlockSpec((tm, D_MODEL), lambda i: (i, 0))
    out_shape = [jax.ShapeDtypeStruct((m, D_MODEL), F32)]
    out_specs = [row]
    scratch = [pltpu.VMEM((tm, GM_WIDTH), BF16)]
    cast_specs = [_cast_block_spec(cw.shape[0], cw.shape[1], steps) for cw in cast_weights]
    tcast = [_tcast_specs(job) for job in tcast_jobs]
    if emit_v:
        out_shape.append(jax.ShapeDtypeStruct((m, GM_WIDTH), F32))
        out_specs.append(row)
    if emit_h:
        out_shape.append(jax.ShapeDtypeStruct((m, D_MODEL), F32))
        out_specs.append(row)
    out_shape += [jax.ShapeDtypeStruct(cw.shape, BF16) for cw in cast_weights]
    out_shape += [t[2] for t in tcast]
    out_specs += cast_specs + [t[1] for t in tcast]
    return pl.pallas_call(
        functools.partial(_gmlp_kernel, chunk=chunk, emit_v=emit_v, emit_h=emit_h,
                          n_cast=len(cast_weights), n_tcast=len(tcast)),
        grid=(steps,),
        in_specs=[row, _const_spec((1, D_MODEL)), _const_spec((D_MODEL, 2 * GM_WIDTH)),
                  _const_spec((1, GM_WIDTH)), _const_spec((1, GM_WIDTH)),
                  _const_spec((GM_GROUPS, blk, blk)), _const_spec((blk, GM_WIDTH)),
                  _const_spec((GM_WIDTH, D_MODEL))] + cast_specs + [t[0] for t in tcast],
        out_specs=out_specs,
        out_shape=out_shape,
        scratch_shapes=scratch,
        compiler_params=pltpu.CompilerParams(dimension_semantics=("arbitrary",),
                                             vmem_limit_bytes=VMEM_LIMIT),
        name="gmlp",
    )(x, g1, wuv, lng, lnb, ws_t, bs_t, wpa, *cast_weights, *([tcast_weight] * len(tcast)))


def _ml_weight_specs():
    wide = 2 * ML_WIDTH
    col_block = lambda width, idx: pl.BlockSpec((D_MODEL, width), lambda *_: (0, idx),
                                                pipeline_mode=pl.Buffered(1))
    return [col_block(wide, 0), col_block(wide, 1), col_block(GATE_COLS, 4 * ML_WIDTH // GATE_COLS)]


def _inproj_kernel(x_ref, g1_ref, wqk_ref, wvo_ref, wif_ref, z_ref):
    h = _rms(x_ref[...], g1_ref[...]).astype(BF16)
    z_ref[:, 0:2 * ML_WIDTH] = _bdot(h, wqk_ref[...])
    z_ref[:, 2 * ML_WIDTH:4 * ML_WIDTH] = _bdot(h, wvo_ref[...])
    z_ref[:, 4 * ML_WIDTH:ML_COLS] = _bdot(h, wif_ref[...])


def _inproj_call(x, g1, w_all, *, tm):
    m = x.shape[0]
    return pl.pallas_call(
        _inproj_kernel,
        grid=(m // tm,),
        in_specs=[pl.BlockSpec((tm, D_MODEL), lambda i: (i, 0)), _const_spec((1, D_MODEL))]
        + _ml_weight_specs(),
        out_specs=pl.BlockSpec((tm, ML_COLS), lambda i: (i, 0)),
        out_shape=jax.ShapeDtypeStruct((m, ML_COLS), F32),
        compiler_params=pltpu.CompilerParams(dimension_semantics=("arbitrary",),
                                             vmem_limit_bytes=VMEM_LIMIT),
        name="inproj",
    )(x, g1, w_all, w_all, w_all)


def _mlstm_heads(q_of, k_of, v_of, o_of, causal, ipre, bcum, groups, m0_of, C0_of, n0_of, hng_ref,
                 fill=None):
    n_groups = len(groups)
    single = n_groups == 1
    fill = fill or (lambda: None)

    def rows_of(x, g):
        return x if single else x[groups[g][0]:groups[g][0] + groups[g][1]]

    def per_row(vals):
        if single:
            return vals[0]
        return jnp.concatenate([jnp.broadcast_to(v, (groups[g][1], v.shape[1]))
                                for g, v in enumerate(vals)], axis=0)

    a = ipre - bcum
    a_t = a.T
    m_rows = per_row([m0_of(g) for g in range(n_groups)])

    def prepare(h):
        p = {}
        a2 = jnp.where(causal, a_t[h:h + 1, :], -jnp.inf)
        p["mc"] = mc = jnp.maximum(jnp.max(a2, axis=1, keepdims=True), m_rows[:, h:h + 1])
        p["m_last"] = m_last = [mc[grp[2]:grp[2] + 1, :] for grp in groups]
        p["dm"] = jnp.exp(a2 - (mc + LN_INV_K_SCALE))
        p["w_inter"] = jnp.exp(m_rows[:, h:h + 1] - mc)
        p["w_col"] = w_col = jnp.exp(a[:, h:h + 1] - (per_row(m_last) + LN_INV_K_SCALE))
        p["decay"] = [jnp.exp(m0_of(g)[:, h:h + 1] - m_last[g]) for g in range(n_groups)]
        p["q"] = q = q_of(h)
        p["k"] = k = k_of(h)
        v = v_of(h)
        p["qb"], p["kb"], p["vb"] = q.astype(BF16), k.astype(BF16), v.astype(BF16)
        p["vw"] = (v * w_col).astype(BF16)
        p["c_old"] = [C0_of(g, h) for g in range(n_groups)]
        p["n_old"] = [n0_of(g, h) for g in range(n_groups)]
        return p

    def first_matmuls(p):
        p["qk"] = lax.dot_general(p["qb"], p["kb"], NT_DIMS, preferred_element_type=F32)
        p["qc"] = [lax.dot_general(rows_of(p["qb"], g), p["c_old"][g].astype(BF16), NT_DIMS,
                                   preferred_element_type=F32) for g in range(n_groups)]
        if single:
            n_rows = jnp.broadcast_to(p["n_old"][0], (LANES, ML_HEAD_DIM)).astype(BF16)
            p["qn"] = lax.dot_general(p["qb"], n_rows, NT_DIMS, preferred_element_type=F32)[:, 0:1]
        else:
            p["qn"] = jnp.sum(p["q"] * per_row(p["n_old"]), axis=1, keepdims=True)

    def second_matmuls(p):
        p["s"] = s = p["dm"] * p["qk"]
        p["sv"] = _bdot(s.astype(BF16), p["vb"])
        p["cupd"] = [lax.dot_general(rows_of(p["vw"], g), rows_of(p["kb"], g), TN_DIMS,
                                     preferred_element_type=F32) for g in range(n_groups)]

    def finish(h, p):
        qc_rows = p["qc"][0] if single else jnp.concatenate(p["qc"], axis=0)
        num = p["w_inter"] * qc_rows + p["sv"]
        den = p["w_inter"] * p["qn"] + jnp.sum(p["s"], axis=1, keepdims=True)
        hcur = num / jnp.maximum(jnp.abs(den), jnp.exp(-(bcum[:, h:h + 1] + p["mc"])))
        mu = jnp.mean(hcur, axis=1, keepdims=True)
        hc = hcur - mu
        var = jnp.mean(hc * hc, axis=1, keepdims=True)
        hs = slice(h * ML_HEAD_DIM, (h + 1) * ML_HEAD_DIM)
        out = jax.nn.sigmoid(o_of(h)) * (hc * lax.rsqrt(var + EPS) * hng_ref[:, hs])
        kw = p["k"] * p["w_col"]
        c_new = [p["decay"][g] * p["c_old"][g] + p["cupd"][g] for g in range(n_groups)]
        n_new = [p["decay"][g] * p["n_old"][g] + jnp.sum(rows_of(kw, g), axis=0, keepdims=True)
                 for g in range(n_groups)]
        return out, c_new, n_new

    H = ML_HEADS
    per_head = [None] * H
    done = [None] * H
    if single:
        per_head[0] = prepare(0)
        fill()
        first_matmuls(per_head[0])
        if H > 1:
            per_head[1] = prepare(1)
        fill()
        for h in range(H):
            second_matmuls(per_head[h])
            if h + 1 < H:
                first_matmuls(per_head[h + 1])
            fill()
            if h + 2 < H:
                per_head[h + 2] = prepare(h + 2)
            done[h] = finish(h, per_head[h])
            fill()
    else:
        per_head = [prepare(h) for h in range(H)]
        for stage in (first_matmuls, second_matmuls):
            for h in range(H):
                stage(per_head[h])
        done = [finish(h, per_head[h]) for h in range(H)]

    outs = [done[h][0] for h in range(H)]
    C_new = [[done[h][1][g] for h in range(H)] for g in range(n_groups)]
    n_new = [[done[h][2][g] for h in range(H)] for g in range(n_groups)]
    m_new = []
    for g, grp in enumerate(groups):
        row = m0_of(g)
        lane = lax.broadcasted_iota(jnp.int32, row.shape, 1)
        for h in range(H):
            row = jnp.where(lane == h, bcum[grp[2]:grp[2] + 1, h:h + 1] + per_head[h]["m_last"][g], row)
        m_new.append(row)
    return outs, C_new, n_new, m_new


def _mlstm_kernel(zq_ref, cst_ref, C0_ref, n0_ref, m0_ref, bif_ref, cw_ref, cb_ref, hng_ref,
                  bo_ref, conv_ref, C_ref, n_ref, m_ref, zp, xp, *, t_valid):
    nb = C0_ref.shape[0]
    L = BF16_ROWS
    R = nb * L

    @pl.when(pl.program_id(0) == 0)
    def _():
        zp[...] = jnp.zeros(zp.shape, F32)

    for bb in range(nb):
        zp[bb, 0:t_valid, :] = zq_ref[bb * t_valid:(bb + 1) * t_valid, :]
        for j in range(CONV_W - 1):
            row = SUBLANES - (CONV_W - 1) + j
            xp[bb, row:row + 1, :] = cst_ref[j, bb:bb + 1, :]
        xp[bb, SUBLANES:2 * SUBLANES, :] = zp[bb, 0:SUBLANES, 0:2 * ML_WIDTH]
        for j in range(CONV_W - 1):
            row = SUBLANES + t_valid - (CONV_W - 1) + j
            conv_ref[j, bb:bb + 1, :] = xp[bb, row:row + 1, :]
    qk_rows = []
    for bb in range(nb):
        acc = cb_ref[...]
        for j in range(CONV_W):
            off = SUBLANES - (CONV_W - 1) + j
            acc = acc + cw_ref[j:j + 1, :] * xp[bb, off:off + SUBLANES, :]
        qk_rows += [acc, jnp.zeros((L - SUBLANES, 2 * ML_WIDTH), F32)]
    qk = jnp.concatenate(qk_rows, axis=0)
    qk = qk * jax.nn.sigmoid(qk)

    def cols(c0, width):
        return zp[:, :, c0:c0 + width].reshape(R, width)

    zif = cols(4 * ML_WIDTH, GATE_COLS) + bif_ref[...]
    live = (lax.broadcasted_iota(jnp.int32, (R, GATE_COLS), 0) & (L - 1)) < t_valid
    ipre = jnp.where(live, zif, NEG_BIG)
    logf = jnp.where(live, pltpu.roll(_log_sigmoid(zif), GATE_COLS - ML_HEADS, axis=1), 0.0)
    r = lax.broadcasted_iota(jnp.int32, (R, R), 0)
    c = lax.broadcasted_iota(jnp.int32, (R, R), 1)
    sh = L.bit_length() - 1
    causal = jnp.logical_and(c <= r, (r >> sh) == (c >> sh))
    bcum = jnp.dot(causal.astype(F32), logf, preferred_element_type=F32,
                   precision=lax.Precision.HIGHEST)
    live_w = (lax.broadcasted_iota(jnp.int32, (R, ML_HEAD_DIM), 0) & (L - 1)) < t_valid

    def head_cols(x, base, h):
        return x[:, base + h * ML_HEAD_DIM:base + (h + 1) * ML_HEAD_DIM]

    groups = [(bb * L, L, bb * L + L - 1) for bb in range(nb)]
    outs, C_new, n_new, m_new = _mlstm_heads(
        q_of=lambda h: head_cols(qk, 0, h),
        k_of=lambda h: jnp.where(live_w, head_cols(qk, ML_WIDTH, h), 0.0),
        v_of=lambda h: jnp.where(live_w, cols(2 * ML_WIDTH + h * ML_HEAD_DIM, ML_HEAD_DIM), 0.0),
        o_of=lambda h: cols(3 * ML_WIDTH + h * ML_HEAD_DIM, ML_HEAD_DIM),
        causal=causal, ipre=ipre, bcum=bcum, groups=groups,
        m0_of=lambda g: m0_ref[g:g + 1, :], C0_of=lambda g, h: C0_ref[g, h],
        n0_of=lambda g, h: n0_ref[g, h:h + 1, :], hng_ref=hng_ref)
    for bb in range(nb):
        for h in range(ML_HEADS):
            hs = slice(h * ML_HEAD_DIM, (h + 1) * ML_HEAD_DIM)
            bo_ref[bb * t_valid:(bb + 1) * t_valid, hs] = outs[h][bb * L:bb * L + t_valid, :]
            C_ref[bb, h] = C_new[bb][h]
            n_ref[bb, h:h + 1, :] = n_new[bb][h]
        m_ref[bb:bb + 1, :] = m_new[bb]


def _mlstm_call(zq, cst, state, bif, cw, cb, hng, *, t_valid, nb):
    T = t_valid
    B = zq.shape[0] // T
    kern = functools.partial(_mlstm_kernel, t_valid=T)
    per_b = lambda shape: pl.BlockSpec((nb,) + shape, lambda b: (b,) + (0,) * len(shape))
    rows = lambda width: pl.BlockSpec((nb * T, width), lambda b: (b, 0))
    conv_spec = pl.BlockSpec((CONV_W - 1, nb, 2 * ML_WIDTH), lambda b: (0, b, 0))
    st_specs = [per_b((ML_HEADS, ML_HEAD_DIM, ML_HEAD_DIM)), per_b((ML_HEADS, ML_HEAD_DIM)),
                per_b((ML_HEADS,))]
    in_specs = ([rows(ML_COLS), conv_spec] + st_specs
                + [_const_spec((1, GATE_COLS)), _const_spec((CONV_W, 2 * ML_WIDTH)),
                   _const_spec((1, 2 * ML_WIDTH)), _const_spec((1, ML_WIDTH))])
    out_shape = [jax.ShapeDtypeStruct((B * T, ML_WIDTH), F32),
                 jax.ShapeDtypeStruct((CONV_W - 1, B, 2 * ML_WIDTH), F32),
                 jax.ShapeDtypeStruct((B, ML_HEADS, ML_HEAD_DIM, ML_HEAD_DIM), F32),
                 jax.ShapeDtypeStruct((B, ML_HEADS, ML_HEAD_DIM), F32),
                 jax.ShapeDtypeStruct((B, ML_HEADS), F32)]
    out_specs = [rows(ML_WIDTH), conv_spec] + st_specs
    return pl.pallas_call(
        kern,
        grid=(B // nb,),
        in_specs=in_specs,
        out_specs=out_specs,
        out_shape=out_shape,
        scratch_shapes=[pltpu.VMEM((nb, BF16_ROWS, ML_COLS), F32),
                        pltpu.VMEM((nb, 2 * SUBLANES, 2 * ML_WIDTH), F32)],
        compiler_params=pltpu.CompilerParams(dimension_semantics=("arbitrary",),
                                             vmem_limit_bytes=VMEM_LIMIT),
        name="mlstm",
    )(zq, cst, *state, bif, cw, cb, hng)


PL = 256
PG = PL // SUBLANES
TAIL = (CONV_W - 1) * SUBLANES
PROJ_PIECE_COLS = 512


def _mixer_ml_kernel(h_hbm, wqk_ref, wvo_ref, wif_ref, bif_ref, cw_ref, cb_ref, hng_ref,
                     bo_hbm, conv_ref, C_ref, n_ref, m_ref,
                     zqk_buf, zvo_buf, zif_buf, tail, hbuf, obuf, sem_in, sem_out,
                     *, chunks_per_seq, n_chunks):
    s = pl.program_id(0)

    def chunk_dmas(chunk, slot, inbound):
        row0 = chunk * PL
        copies = []
        for i in range(SUBLANES):
            hbm = (h_hbm if inbound else bo_hbm).at[pl.ds(row0 + PG * i, PG), :]
            vmem = (hbuf if inbound else obuf).at[slot, :, i, :]
            sem = (sem_in if inbound else sem_out).at[slot, i]
            copies.append(pltpu.make_async_copy(hbm, vmem, sem) if inbound
                          else pltpu.make_async_copy(vmem, hbm, sem))
        return copies

    @pl.when(s == 0)
    def _():
        zqk_buf[1] = jnp.zeros(zqk_buf.shape[1:], F32)
        zvo_buf[1] = jnp.zeros(zvo_buf.shape[1:], F32)
        zif_buf[1] = jnp.zeros(zif_buf.shape[1:], F32)
        for cp in chunk_dmas(0, 0, True):
            cp.start()

    @pl.when(jnp.maximum(s - 1, 0) % chunks_per_seq == 0)
    def _():
        tail[...] = jnp.zeros(tail.shape, F32)
        C_ref[...] = jnp.zeros(C_ref.shape, F32)
        n_ref[...] = jnp.zeros(n_ref.shape, F32)
        m_ref[...] = jnp.zeros(m_ref.shape, F32)

    def step(rd, wr):
        @pl.when(s + 1 < n_chunks)
        def _():
            for cp in chunk_dmas(s + 1, rd, True):
                cp.start()

        @pl.when(s < n_chunks)
        def _():
            for cp in chunk_dmas(s, wr, True):
                cp.wait()

        _mixer_ml_step(hbuf.at[wr], wqk_ref, wvo_ref, wif_ref, bif_ref, cw_ref, cb_ref, hng_ref,
                       obuf.at[rd], conv_ref, C_ref, n_ref, m_ref, zqk_buf, zvo_buf, zif_buf, tail,
                       rd=rd, wr=wr)

        @pl.when(s >= 1)
        def _():
            for cp in chunk_dmas(s - 1, rd, False):
                cp.start()

        @pl.when(s >= 2)
        def _():
            for cp in chunk_dmas(s - 2, wr, False):
                cp.wait()

        @pl.when(s == n_chunks)
        def _():
            for cp in chunk_dmas(s - 1, rd, False):
                cp.wait()

    @pl.when(s % 2 == 0)
    def _():
        step(rd=1, wr=0)

    @pl.when(s % 2 == 1)
    def _():
        step(rd=0, wr=1)


def _mixer_ml_step(h_ref, wqk_ref, wvo_ref, wif_ref, bif_ref, cw_ref, cb_ref, hng_ref,
                   bo_ref, conv_ref, C_ref, n_ref, m_ref, zqk_buf, zvo_buf, zif_buf, tail,
                   *, rd, wr):
    h = h_ref[...].reshape(PL, D_MODEL).astype(BF16)

    def piece(w_ref, buf, c0, width):
        def emit():
            buf[wr, :, c0:c0 + width] = _bdot(h, w_ref[:, c0:c0 + width])
        return emit

    pieces = [piece(w_ref, buf, c0, PROJ_PIECE_COLS)
              for w_ref, buf in ((wqk_ref, zqk_buf), (wvo_ref, zvo_buf))
              for c0 in range(0, 2 * ML_WIDTH, PROJ_PIECE_COLS)]
    pieces.append(piece(wif_ref, zif_buf, 0, GATE_COLS))
    pieces = iter(pieces)

    def fill():
        emit = next(pieces, None)
        if emit is not None:
            emit()

    fill()
    zqk = zqk_buf.at[rd]
    zvo = zvo_buf.at[rd]
    zif = zif_buf[rd] + bif_ref[...]

    zqk_tail = zqk[PL - TAIL:, :]
    sub = lax.broadcasted_iota(jnp.int32, (SUBLANES, 2 * ML_WIDTH), 0)
    wrapped = []
    for g in range(CONV_W - 1):
        cur = pltpu.roll(zqk_tail[g * SUBLANES:(g + 1) * SUBLANES], 1, axis=0)
        prev = pltpu.roll(tail[g * SUBLANES:(g + 1) * SUBLANES, :], 1, axis=0)
        wrapped.append(jnp.where(sub == 0, prev, cur))
    wrapped = jnp.concatenate(wrapped, axis=0)
    tail[...] = zqk_tail
    conv_ref[...] = jnp.concatenate(
        [zqk_tail[g * SUBLANES + SUBLANES - 1:(g + 1) * SUBLANES, :] for g in range(CONV_W - 1)], axis=0)

    def conv_silu(c0, width):
        cs = slice(c0, c0 + width)
        acc = cb_ref[:, cs] + cw_ref[CONV_W - 1:CONV_W, cs] * zqk[:, cs]
        for d in range(1, CONV_W):
            shifted = jnp.concatenate(
                [wrapped[TAIL - d * SUBLANES:, cs], zqk[:PL - d * SUBLANES, cs]], axis=0)
            acc = acc + cw_ref[CONV_W - 1 - d:CONV_W - d, cs] * shifted
        return acc * jax.nn.sigmoid(acc)

    pr = lax.broadcasted_iota(jnp.int32, (PL, PL), 0)
    pc = lax.broadcasted_iota(jnp.int32, (PL, PL), 1)
    causal = ((pc >> 3) + PG * (pc & 7)) <= ((pr >> 3) + PG * (pr & 7))

    logf = pltpu.roll(_log_sigmoid(zif), GATE_COLS - ML_HEADS, axis=1)
    run, partial = None, []
    for n in range(PG):
        blk = logf[n * SUBLANES:(n + 1) * SUBLANES, :]
        run = blk if run is None else run + blk
        partial.append(run)
    sub_g = lax.broadcasted_iota(jnp.int32, (SUBLANES, GATE_COLS), 0)
    incl = run
    for step in (1, 2, 4):
        incl = incl + jnp.where(sub_g >= step, pltpu.roll(incl, step, axis=0), 0.0)
    earlier = incl - run
    bcum = jnp.concatenate([p + earlier for p in partial], axis=0)

    def head_cols(base, h):
        return zvo[:, base + h * ML_HEAD_DIM:base + (h + 1) * ML_HEAD_DIM]

    outs, C_new, n_new, m_new = _mlstm_heads(
        q_of=lambda h: conv_silu(h * ML_HEAD_DIM, ML_HEAD_DIM),
        k_of=lambda h: conv_silu(ML_WIDTH + h * ML_HEAD_DIM, ML_HEAD_DIM),
        v_of=lambda h: head_cols(0, h),
        o_of=lambda h: head_cols(ML_WIDTH, h),
        causal=causal, ipre=zif, bcum=bcum, groups=[(0, PL, PL - 1)],
        m0_of=lambda g: m_ref[...], C0_of=lambda g, h: C_ref[h], n0_of=lambda g, h: n_ref[h:h + 1, :],
        hng_ref=hng_ref, fill=fill)
    while next(pieces, None) is not None:
        raise AssertionError("projection pieces left over")
    for h in range(ML_HEADS):
        bo_ref[:, :, h * ML_HEAD_DIM:(h + 1) * ML_HEAD_DIM] = outs[h].reshape(PG, SUBLANES, ML_HEAD_DIM)
        C_ref[h] = C_new[0][h]
        n_ref[h:h + 1, :] = n_new[0][h]
    m_ref[...] = m_new[0]


def _mixer_ml_call(hp, n_seq, w_all, bif, cw, cb, hng):
    B = n_seq
    T = hp.shape[0] // B
    cps = T // PL
    n_chunks = B * cps
    math = lambda s: jnp.maximum(s - 1, 0)
    per_b = lambda shape: pl.BlockSpec((None,) + shape,
                                       lambda s: (math(s) // cps,) + (0,) * len(shape))
    out_shape = [jax.ShapeDtypeStruct((B * T, ML_WIDTH), F32),
                 jax.ShapeDtypeStruct((B, CONV_W - 1, 2 * ML_WIDTH), F32),
                 jax.ShapeDtypeStruct((B, ML_HEADS, ML_HEAD_DIM, ML_HEAD_DIM), F32),
                 jax.ShapeDtypeStruct((B, ML_HEADS, ML_HEAD_DIM), F32),
                 jax.ShapeDtypeStruct((B, 1, ML_HEADS), F32)]
    out_specs = [pl.BlockSpec(memory_space=pl.ANY),
                 per_b((CONV_W - 1, 2 * ML_WIDTH)),
                 per_b((ML_HEADS, ML_HEAD_DIM, ML_HEAD_DIM)), per_b((ML_HEADS, ML_HEAD_DIM)),
                 per_b((1, ML_HEADS))]
    return pl.pallas_call(
        functools.partial(_mixer_ml_kernel, chunks_per_seq=cps, n_chunks=n_chunks),
        grid=(n_chunks + 1,),
        in_specs=[pl.BlockSpec(memory_space=pl.ANY)]
        + _ml_weight_specs()
        + [_const_spec((1, GATE_COLS)), _const_spec((CONV_W, 2 * ML_WIDTH)),
           _const_spec((1, 2 * ML_WIDTH)), _const_spec((1, ML_WIDTH))],
        out_specs=out_specs,
        out_shape=out_shape,
        scratch_shapes=[pltpu.VMEM((2, PL, 2 * ML_WIDTH), F32),
                        pltpu.VMEM((2, PL, 2 * ML_WIDTH), F32),
                        pltpu.VMEM((2, PL, GATE_COLS), F32),
                        pltpu.VMEM((TAIL, 2 * ML_WIDTH), F32),
                        pltpu.VMEM((2, PG, SUBLANES, D_MODEL), F32),
                        pltpu.VMEM((2, PG, SUBLANES, ML_WIDTH), F32),
                        pltpu.SemaphoreType.DMA((2, SUBLANES)),
                        pltpu.SemaphoreType.DMA((2, SUBLANES))],
        compiler_params=pltpu.CompilerParams(dimension_semantics=("arbitrary",),
                                             vmem_limit_bytes=VMEM_LIMIT),
        name="mixer_ml",
    )(hp, w_all, w_all, w_all, bif, cw, cb, hng)


def _merge_ffn_kernel(x_ref, pa_ref, bo_ref, g1_ref, wg_ref, bg_ref, wpb_ref, wout_ref,
                      g2_ref, wfi_ref, wfo_ref, gf_ref, y_ref):
    tm = x_ref.shape[0]
    subs = [slice(r0, r0 + FFN_SUB_ROWS) for r0 in range(0, tm, FFN_SUB_ROWS)]
    x = [x_ref[rs, :] for rs in subs]
    h = [_rms(xi, g1_ref[...]).astype(BF16) for xi in x]
    gab = [_bdot(hi, wg_ref[...]) + bg_ref[...] for hi in h]
    pb = [_bdot(bo_ref[rs, :].astype(BF16), wpb_ref[...]) for rs in subs]
    merged = [(jax.nn.sigmoid(g[:, :D_MODEL]) * pa_ref[rs, :]
               + jax.nn.sigmoid(g[:, D_MODEL:]) * p).astype(BF16) for g, p, rs in zip(gab, pb, subs)]
    x1 = [xi + _bdot(mi, wout_ref[...]) for xi, mi in zip(x, merged)]
    h2 = [_rms(xi, g2_ref[...]).astype(BF16) for xi in x1]
    gu = [_bdot(hi, wfi_ref[...]) for hi in h2]
    hid = [(g[:, :D_FF] * jax.nn.sigmoid(g[:, :D_FF]) * g[:, D_FF:]).astype(BF16) for g in gu]
    x2 = [xi + _bdot(hi, wfo_ref[...]) for xi, hi in zip(x1, hid)]
    for rs, xi in zip(subs, x2):
        y_ref[rs, :] = _rms(xi, gf_ref[...])


def _merge_ffn_call(x, pa, bo, g1, wg, bg, wpb, wout, g2, wfi, wfo, gf, *, tm):
    m = x.shape[0]
    row = pl.BlockSpec((tm, D_MODEL), lambda i: (i, 0))
    return pl.pallas_call(
        _merge_ffn_kernel,
        grid=(m // tm,),
        in_specs=[row, row, row, _const_spec((1, D_MODEL)), _const_spec((D_MODEL, 2 * D_MODEL)),
                  _const_spec((1, 2 * D_MODEL)), _const_spec((ML_WIDTH, D_MODEL)),
                  _const_spec((D_MODEL, D_MODEL)), _const_spec((1, D_MODEL)),
                  _const_spec((D_MODEL, 2 * D_FF)), _const_spec((D_FF, D_MODEL)),
                  _const_spec((1, D_MODEL))],
        out_specs=row,
        out_shape=jax.ShapeDtypeStruct((m, D_MODEL), F32),
        compiler_params=pltpu.CompilerParams(dimension_semantics=("arbitrary",),
                                             vmem_limit_bytes=VMEM_LIMIT),
        name="merge_ffn",
    )(x, pa, bo, g1, wg, bg, wpb, wout, g2, wfi, wfo, gf)


SAMPLE_SEQS_PER_STEP = 8
TM_MIX = 512
TM_GMLP = 512
TM_FFN = 512


def _spatial_tiles(w_s, b_s, chunk):
    if chunk == GM_CHUNK:
        ws_t, b_pos = w_s[:, :chunk, :chunk], b_s[:, :chunk].T
    else:
        onehot = jnp.asarray(np.arange(GM_CHUNK)[:, None] % chunk == np.arange(chunk)[None, :], F32)
        hp = lax.Precision.HIGHEST
        ws_t = jnp.einsum("ri,gij,cj->grc", onehot, w_s[:, :chunk, :chunk], onehot, precision=hp)
        b_pos = jnp.dot(onehot, b_s[:, :chunk].T, precision=hp)
    bs_t = jnp.repeat(b_pos, GM_GROUP_W, axis=1)
    return ws_t, bs_t


def _gmlp_branch(xf, w, chunk, *, emit_v, emit_h, **side_jobs):
    ws_t, bs_t = _spatial_tiles(w["w_s"], w["b_s"], chunk)
    return _gmlp_call(xf, w["g1"], w["wuv"], w["lng"], w["lnb"], ws_t, bs_t, w["wpa"],
                      chunk=chunk, emit_v=emit_v, emit_h=emit_h,
                      tm=min(TM_GMLP, xf.shape[0]), **side_jobs)


def _merge_branch(xf, pa, bo, w):
    return _merge_ffn_call(xf, pa, bo, w["g1"], w["wg"], w["bg"], w["wpb"], w["wout"], w["g2"],
                           w["wfi"], w["wfo"], w["gf"], tm=TM_FFN)


def kernel(x_prompt, x_sample, state_conv, state_C, state_n, state_m, g_norm1, w_in, b_i, b_f, ln_g, ln_b, w_s, b_s, conv_w, conv_b, hn_g, b_gate, w_proj_a, w_proj_b, w_out, g_norm2, w_ffn_in, w_ffn_out, g_final):
    Bp, Tp, _ = x_prompt.shape
    Bs, Ts, _ = x_sample.shape
    win = w_in[0]
    c_ml = 2 * GM_WIDTH
    c_gate = c_ml + 4 * ML_WIDTH + 2 * ML_HEADS
    n_ml_blocks = -(-(ML_COLS) // TCAST_ROWS)
    win_t = jnp.transpose(win)
    w = dict(
        g1=g_norm1[0][None], g2=g_norm2[0][None], gf=g_final[None],
        wuv=_tcast_call(win_t, 0, c_ml // TCAST_ROWS),
        bg=b_gate[0].reshape(1, 2 * D_MODEL),
        lng=ln_g[0][None], lnb=ln_b[0][None], w_s=w_s[0], b_s=b_s[0],
        bif=jnp.pad(jnp.concatenate([b_i[0], b_f[0]]), (0, GATE_COLS - 2 * ML_HEADS))[None],
        cw=conv_w[0], cb=conv_b[0][None], hng=hn_g[0][None],
        wpa=w_proj_a[0].astype(BF16),
    )

    xpf = x_prompt.reshape(Bp * Tp, D_MODEL)
    pa_p, hp_p, w["wpb"], w["wout"], w["wfi"], w["wfo"], w["w_ml"], w["wg"] = _gmlp_branch(
        xpf, w, GM_CHUNK, emit_v=False, emit_h=True,
        cast_weights=(w_proj_b[0], w_out[0], w_ffn_in[0], w_ffn_out[0]),
        tcast_weight=win_t,
        tcast_jobs=((c_ml, n_ml_blocks, 0), (c_gate, 2 * D_MODEL // TCAST_ROWS, n_ml_blocks)))
    bo_p, conv_p, C_p, n_p, m_p = _mixer_ml_call(hp_p, Bp, w["w_ml"], w["bif"], w["cw"], w["cb"],
                                                 w["hng"])
    y_p = _merge_branch(xpf, pa_p, bo_p, w)

    xsf = x_sample.reshape(Bs * Ts, D_MODEL)
    pa_s, vn_s = _gmlp_branch(xsf, w, Ts, emit_v=True, emit_h=False)
    zq = _inproj_call(xsf, w["g1"], w["w_ml"], tm=TM_MIX)
    st = (state_C[0], state_n[0], state_m[0])
    bo_s, conv_s, C_s, n_s, m_s = _mlstm_call(zq, jnp.transpose(state_conv[0], (1, 0, 2)), st, w["bif"],
                                              w["cw"], w["cb"], w["hng"], t_valid=Ts,
                                              nb=SAMPLE_SEQS_PER_STEP)
    y_s = _merge_branch(xsf, pa_s, bo_s, w)

    return (y_p.reshape(Bp, Tp, D_MODEL), y_s.reshape(Bs, Ts, D_MODEL),
            conv_p[None], C_p[None], n_p[None], m_p.reshape(1, Bp, ML_HEADS),
            jnp.transpose(conv_s, (1, 0, 2))[None], C_s[None], n_s[None], m_s[None],
            vn_s.reshape(1, Bs, Ts, GM_WIDTH))
```

```python
import functools
import math

import jax
import jax.numpy as jnp
import numpy as np
from jax import lax
from jax.experimental import pallas as pl
from jax.experimental.pallas import tpu as pltpu

D_MODEL = 1024
GM_WIDTH = D_MODEL
GM_GROUPS = 4
GM_GROUP_W = GM_WIDTH // GM_GROUPS
GM_CHUNK = 128
ML_HEADS = 4
ML_HEAD_DIM = D_MODEL // ML_HEADS
ML_WIDTH = ML_HEADS * ML_HEAD_DIM
CONV_W = 4
D_FF = 2816
EPS = 1e-6

LANES = 128
SUBLANES = 8
BF16_ROWS = 16
GATE_COLS = LANES
ML_COLS = 4 * ML_WIDTH + GATE_COLS
VMEM_LIMIT = 56 * 1024 * 1024
GMLP_SUB_ROWS = 256
FFN_SUB_ROWS = 256

F32 = jnp.float32
BF16 = jnp.bfloat16
NEG_INF = float("-inf")
LN_INV_K_SCALE = 0.5 * math.log(ML_HEAD_DIM)
NT_DIMS = (((1,), (1,)), ((), ()))
TN_DIMS = (((0,), (0,)), ((), ()))


def _rms(x, g):
    return x * lax.rsqrt(jnp.mean(x * x, axis=-1, keepdims=True) + EPS) * g


def _gelu(x):
    return 0.5 * x * (1.0 + lax.erf(x * (2.0 ** -0.5)))


def _log_sigmoid(x):
    return jnp.minimum(x, 0.0) - jnp.log1p(jnp.exp(-jnp.abs(x)))


def _bdot(a, b):
    return jnp.dot(a, b, preferred_element_type=F32)


def _const_spec(shape):
    nd = len(shape)
    return pl.BlockSpec(shape, lambda *_: (0,) * nd, pipeline_mode=pl.Buffered(1))


def _gmlp_kernel(x_ref, g1_ref, wuv_ref, lng_ref, lnb_ref, ws_ref, bs_ref, wpa_ref, *rest,
                 chunk, emit_v, emit_h, n_cast, n_tcast):
    rest = list(rest)
    cast_src = [rest.pop(0) for _ in range(n_cast)]
    tcast_src = [rest.pop(0) for _ in range(n_tcast)]
    pa_ref = rest.pop(0)
    vn_ref = rest.pop(0) if emit_v else None
    hp_ref = rest.pop(0) if emit_h else None
    for src in cast_src:
        rest.pop(0)[...] = src[...].astype(BF16)
    for src in tcast_src:
        rest.pop(0)[...] = src[...].T.astype(BF16)
    a_sc = rest.pop(0)
    tm = x_ref.shape[0]
    blk = ws_ref.shape[1]
    sub = GMLP_SUB_ROWS
    subs = [slice(r0, r0 + sub) for r0 in range(0, tm, sub)]
    hb = []
    for rs in subs:
        hf = _rms(x_ref[rs, :], g1_ref[...])
        hb.append(hf.astype(BF16))
        if emit_h:
            hp_ref[rs, :] = hf
    zu = [_bdot(h, wuv_ref[:, :GM_WIDTH]) for h in hb]
    zv = [_bdot(h, wuv_ref[:, GM_WIDTH:]) for h in hb]
    r = lax.broadcasted_iota(jnp.int32, (blk, blk), 0)
    c = lax.broadcasted_iota(jnp.int32, (blk, blk), 1)
    keep = c <= r
    if chunk < blk:
        sh = chunk.bit_length() - 1
        keep = jnp.logical_and(keep, (r >> sh) == (c >> sh))
    wsm = [jnp.where(keep, ws_ref[g], 0.0).astype(BF16) for g in range(GM_GROUPS)]
    for si, rs in enumerate(subs):
        u = _gelu(zu[si])
        v = _gelu(zv[si])
        mu = jnp.mean(v, axis=-1, keepdims=True)
        vc = v - mu
        var = jnp.mean(vc * vc, axis=-1, keepdims=True)
        vn = vc * lax.rsqrt(var + EPS) * lng_ref[...] + lnb_ref[...]
        if emit_v:
            vn_ref[rs, :] = vn
        vb = vn.astype(BF16)
        for g in range(GM_GROUPS):
            cs = slice(g * GM_GROUP_W, (g + 1) * GM_GROUP_W)
            for i in range(sub // blk):
                ls = slice(i * blk, (i + 1) * blk)
                s = _bdot(wsm[g], vb[ls, cs]) + bs_ref[:, cs]
                a_sc[rs.start + i * blk:rs.start + (i + 1) * blk, cs] = (u[ls, cs] * s).astype(BF16)
        pa_ref[rs, :] = _bdot(a_sc[rs, :], wpa_ref[...])


def _cast_block_spec(n_rows, n_cols, steps):
    n_blocks = steps
    while n_rows % n_blocks or (n_rows // n_blocks) % BF16_ROWS:
        n_blocks //= 2
    per = steps // n_blocks
    return pl.BlockSpec((n_rows // n_blocks, n_cols), lambda i: (i // per, 0))


TCAST_ROWS = 256


def _tcast_specs(job):
    first_row, n_blocks, first_step = job
    assert first_row % SUBLANES == 0
    blk = lambda i: jnp.clip(i - first_step, 0, n_blocks - 1)
    src = pl.BlockSpec((pl.Element(TCAST_ROWS), pl.Element(D_MODEL)),
                       lambda i: (pl.multiple_of(first_row + TCAST_ROWS * blk(i), SUBLANES), 0))
    dst = pl.BlockSpec((D_MODEL, TCAST_ROWS), lambda i: (0, blk(i)))
    return src, dst, jax.ShapeDtypeStruct((D_MODEL, TCAST_ROWS * n_blocks), BF16)


def _tcast_kernel(src_ref, dst_ref):
    dst_ref[...] = src_ref[...].T.astype(BF16)


def _tcast_call(wt, first_row, n_blocks):
    src, dst, shape = _tcast_specs((first_row, n_blocks, 0))
    return pl.pallas_call(
        _tcast_kernel, grid=(n_blocks,), in_specs=[src], out_specs=dst, out_shape=shape,
        compiler_params=pltpu.CompilerParams(dimension_semantics=("arbitrary",)),
        name="tcast",
    )(wt)


def _gmlp_call(x, g1, wuv, lng, lnb, ws_t, bs_t, wpa, *, chunk, emit_v, emit_h, tm,
               cast_weights=(), tcast_weight=None, tcast_jobs=()):
    m = x.shape[0]
    blk = ws_t.shape[1]
    steps = m // tm
    assert steps & (steps - 1) == 0
    assert all(first_step + n_blocks <= steps for _, n_blocks, first_step in tcast_jobs)
    row = pl.BlockSpec((tm, D_MODEL), lambda i: (i, 0))
    out_shape = [jax.ShapeDtypeStruct((m, D_MODEL), F32)]
    out_specs = [row]
    scratch = [pltpu.VMEM((tm, GM_WIDTH), BF16)]
    cast_specs = [_cast_block_spec(cw.shape[0], cw.shape[1], steps) for cw in cast_weights]
    tcast = [_tcast_specs(job) for job in tcast_jobs]
    if emit_v:
        out_shape.append(jax.ShapeDtypeStruct((m, GM_WIDTH), F32))
        out_specs.append(row)
    if emit_h:
        out_shape.append(jax.ShapeDtypeStruct((m, D_MODEL), F32))
        out_specs.append(row)
    out_shape += [jax.ShapeDtypeStruct(cw.shape, BF16) for cw in cast_weights]
    out_shape += [t[2] for t in tcast]
    out_specs += cast_specs + [t[1] for t in tcast]
    return pl.pallas_call(
        functools.partial(_gmlp_kernel, chunk=chunk, emit_v=emit_v, emit_h=emit_h,
                          n_cast=len(cast_weights), n_tcast=len(tcast)),
        grid=(steps,),
        in_specs=[row, _const_spec((1, D_MODEL)), _const_spec((D_MODEL, 2 * GM_WIDTH)),
                  _const_spec((1, GM_WIDTH)), _const_spec((1, GM_WIDTH)),
                  _const_spec((GM_GROUPS, blk, blk)), _const_spec((blk, GM_WIDTH)),
                  _const_spec((GM_WIDTH, D_MODEL))] + cast_specs + [t[0] for t in tcast],
        out_specs=out_specs,
        out_shape=out_shape,
        scratch_shapes=scratch,
        compiler_params=pltpu.CompilerParams(dimension_semantics=("arbitrary",),
                                             vmem_limit_bytes=VMEM_LIMIT),
        name="gmlp",
    )(x, g1, wuv, lng, lnb, ws_t, bs_t, wpa, *cast_weights, *([tcast_weight] * len(tcast)))


def _ml_weight_specs():
    wide = 2 * ML_WIDTH
    col_block = lambda width, idx: pl.BlockSpec((D_MODEL, width), lambda *_: (0, idx),
                                                pipeline_mode=pl.Buffered(1))
    return [col_block(wide, 0), col_block(wide, 1), col_block(GATE_COLS, 4 * ML_WIDTH // GATE_COLS)]


def _inproj_kernel(x_ref, g1_ref, wqk_ref, wvo_ref, wif_ref, z_ref):
    h = _rms(x_ref[...], g1_ref[...]).astype(BF16)
    z_ref[:, 0:2 * ML_WIDTH] = _bdot(h, wqk_ref[...])
    z_ref[:, 2 * ML_WIDTH:4 * ML_WIDTH] = _bdot(h, wvo_ref[...])
    z_ref[:, 4 * ML_WIDTH:ML_COLS] = _bdot(h, wif_ref[...])


def _inproj_call(x, g1, w_all, *, tm):
    m = x.shape[0]
    return pl.pallas_call(
        _inproj_kernel,
        grid=(m // tm,),
        in_specs=[pl.BlockSpec((tm, D_MODEL), lambda i: (i, 0)), _const_spec((1, D_MODEL))]
        + _ml_weight_specs(),
        out_specs=pl.BlockSpec((tm, ML_COLS), lambda i: (i, 0)),
        out_shape=jax.ShapeDtypeStruct((m, ML_COLS), F32),
        compiler_params=pltpu.CompilerParams(dimension_semantics=("arbitrary",),
                                             vmem_limit_bytes=VMEM_LIMIT),
        name="inproj",
    )(x, g1, w_all, w_all, w_all)


def _mlstm_heads(q_of, k_of, v_of, o_of, causal, ipre, bcum, groups, m0_of, C0_of, n0_of, hng_ref,
                 fill=None):
    n_groups = len(groups)
    single = n_groups == 1
    fill = fill or (lambda: None)

    def rows_of(x, g):
        return x if single else x[groups[g][0]:groups[g][0] + groups[g][1]]

    def per_row(vals):
        if single:
            return vals[0]
        return jnp.concatenate([jnp.broadcast_to(v, (groups[g][1], v.shape[1]))
                                for g, v in enumerate(vals)], axis=0)

    a = ipre - bcum
    a_t = a.T
    m_rows = per_row([m0_of(g) for g in range(n_groups)])

    def prepare(h):
        p = {}
        a2 = jnp.where(causal, a_t[h:h + 1, :], -jnp.inf)
        p["mc"] = mc = jnp.maximum(jnp.max(a2, axis=1, keepdims=True), m_rows[:, h:h + 1])
        p["m_last"] = m_last = [mc[grp[2]:grp[2] + 1, :] for grp in groups]
        p["dm"] = jnp.exp(a2 - (mc + LN_INV_K_SCALE))
        p["w_inter"] = jnp.exp(m_rows[:, h:h + 1] - mc)
        p["w_col"] = w_col = jnp.exp(a[:, h:h + 1] - (per_row(m_last) + LN_INV_K_SCALE))
        p["decay"] = [jnp.exp(m0_of(g)[:, h:h + 1] - m_last[g]) for g in range(n_groups)]
        p["q"] = q = q_of(h)
        p["k"] = k = k_of(h)
        v = v_of(h)
        p["qb"], p["kb"], p["vb"] = q.astype(BF16), k.astype(BF16), v.astype(BF16)
        p["vw"] = (v * w_col).astype(BF16)
        p["c_old"] = [C0_of(g, h) for g in range(n_groups)]
        p["n_old"] = [n0_of(g, h) for g in range(n_groups)]
        return p

    def first_matmuls(p):
        p["qk"] = lax.dot_general(p["qb"], p["kb"], NT_DIMS, preferred_element_type=F32)
        p["qc"] = [lax.dot_general(rows_of(p["qb"], g), p["c_old"][g].astype(BF16), NT_DIMS,
                                   preferred_element_type=F32) for g in range(n_groups)]
        if single:
            n_rows = jnp.broadcast_to(p["n_old"][0], (LANES, ML_HEAD_DIM)).astype(BF16)
            p["qn"] = lax.dot_general(p["qb"], n_rows, NT_DIMS, preferred_element_type=F32)[:, 0:1]
        else:
            p["qn"] = jnp.sum(p["q"] * per_row(p["n_old"]), axis=1, keepdims=True)

    def second_matmuls(p):
        p["s"] = s = p["dm"] * p["qk"]
        p["sv"] = _bdot(s.astype(BF16), p["vb"])
        p["cupd"] = [lax.dot_general(rows_of(p["vw"], g), rows_of(p["kb"], g), TN_DIMS,
                                     preferred_element_type=F32) for g in range(n_groups)]

    def finish(h, p):
        qc_rows = p["qc"][0] if single else jnp.concatenate(p["qc"], axis=0)
        num = p["w_inter"] * qc_rows + p["sv"]
        den = p["w_inter"] * p["qn"] + jnp.sum(p["s"], axis=1, keepdims=True)
        hcur = num / jnp.maximum(jnp.abs(den), jnp.exp(-(bcum[:, h:h + 1] + p["mc"])))
        mu = jnp.mean(hcur, axis=1, keepdims=True)
        hc = hcur - mu
        var = jnp.mean(hc * hc, axis=1, keepdims=True)
        hs = slice(h * ML_HEAD_DIM, (h + 1) * ML_HEAD_DIM)
        out = jax.nn.sigmoid(o_of(h)) * (hc * lax.rsqrt(var + EPS) * hng_ref[:, hs])
        kw = p["k"] * p["w_col"]
        c_new = [p["decay"][g] * p["c_old"][g] + p["cupd"][g] for g in range(n_groups)]
        n_new = [p["decay"][g] * p["n_old"][g] + jnp.sum(rows_of(kw, g), axis=0, keepdims=True)
                 for g in range(n_groups)]
        return out, c_new, n_new

    H = ML_HEADS
    per_head = [None] * H
    done = [None] * H
    if single:
        per_head[0] = prepare(0)
        fill()
        first_matmuls(per_head[0])
        if H > 1:
            per_head[1] = prepare(1)
        fill()
        for h in range(H):
            second_matmuls(per_head[h])
            if h + 1 < H:
                first_matmuls(per_head[h + 1])
            fill()
            if h + 2 < H:
                per_head[h + 2] = prepare(h + 2)
            done[h] = finish(h, per_head[h])
            fill()
    else:
        per_head = [prepare(h) for h in range(H)]
        for stage in (first_matmuls, second_matmuls):
            for h in range(H):
                stage(per_head[h])
        done = [finish(h, per_head[h]) for h in range(H)]

    outs = [done[h][0] for h in range(H)]
    C_new = [[done[h][1][g] for h in range(H)] for g in range(n_groups)]
    n_new = [[done[h][2][g] for h in range(H)] for g in range(n_groups)]
    m_new = []
    for g, grp in enumerate(groups):
        row = m0_of(g)
        lane = lax.broadcasted_iota(jnp.int32, row.shape, 1)
        for h in range(H):
            row = jnp.where(lane == h, bcum[grp[2]:grp[2] + 1, h:h + 1] + per_head[h]["m_last"][g], row)
        m_new.append(row)
    return outs, C_new, n_new, m_new


def _mlstm_kernel(zq_ref, cst_ref, C0_ref, n0_ref, m0_ref, bif_ref, cw_ref, cb_ref, hng_ref,
                  bo_ref, conv_ref, C_ref, n_ref, m_ref, zp, xp, *, t_valid):
    nb = C0_ref.shape[0]
    L = BF16_ROWS
    R = nb * L

    @pl.when(pl.program_id(0) == 0)
    def _():
        zp[...] = jnp.zeros(zp.shape, F32)

    for bb in range(nb):
        zp[bb, 0:t_valid, :] = zq_ref[bb * t_valid:(bb + 1) * t_valid, :]
        for j in range(CONV_W - 1):
            row = SUBLANES - (CONV_W - 1) + j
            xp[bb, row:row + 1, :] = cst_ref[j, bb:bb + 1, :]
        xp[bb, SUBLANES:2 * SUBLANES, :] = zp[bb, 0:SUBLANES, 0:2 * ML_WIDTH]
        for j in range(CONV_W - 1):
            row = SUBLANES + t_valid - (CONV_W - 1) + j
            conv_ref[j, bb:bb + 1, :] = xp[bb, row:row + 1, :]
    qk_rows = []
    for bb in range(nb):
        acc = cb_ref[...]
        for j in range(CONV_W):
            off = SUBLANES - (CONV_W - 1) + j
            acc = acc + cw_ref[j:j + 1, :] * xp[bb, off:off + SUBLANES, :]
        qk_rows += [acc, jnp.zeros((L - SUBLANES, 2 * ML_WIDTH), F32)]
    qk = jnp.concatenate(qk_rows, axis=0)
    qk = qk * jax.nn.sigmoid(qk)

    def cols(c0, width):
        return zp[:, :, c0:c0 + width].reshape(R, width)

    zif = cols(4 * ML_WIDTH, GATE_COLS) + bif_ref[...]
    live = (lax.broadcasted_iota(jnp.int32, (R, GATE_COLS), 0) & (L - 1)) < t_valid
    ipre = jnp.where(live, zif, NEG_INF)
    logf = jnp.where(live, pltpu.roll(_log_sigmoid(zif), GATE_COLS - ML_HEADS, axis=1), 0.0)
    assert t_valid <= SUBLANES
    sub_g = lax.broadcasted_iota(jnp.int32, (SUBLANES, GATE_COLS), 0)
    parts = []
    for bb in range(nb):
        incl = logf[bb * L:bb * L + SUBLANES, :]
        for step in (1, 2, 4):
            incl = incl + jnp.where(sub_g >= step, pltpu.roll(incl, step, axis=0), 0.0)
        parts += [incl, jnp.broadcast_to(incl[SUBLANES - 1:SUBLANES, :], (L - SUBLANES, GATE_COLS))]
    bcum = jnp.concatenate(parts, axis=0)
    r = lax.broadcasted_iota(jnp.int32, (R, R), 0)
    c = lax.broadcasted_iota(jnp.int32, (R, R), 1)
    sh = L.bit_length() - 1
    causal = jnp.logical_and(c <= r, (r >> sh) == (c >> sh))
    live_w = (lax.broadcasted_iota(jnp.int32, (R, ML_HEAD_DIM), 0) & (L - 1)) < t_valid

    def head_cols(x, base, h):
        return x[:, base + h * ML_HEAD_DIM:base + (h + 1) * ML_HEAD_DIM]

    groups = [(bb * L, L, bb * L + L - 1) for bb in range(nb)]
    outs, C_new, n_new, m_new = _mlstm_heads(
        q_of=lambda h: head_cols(qk, 0, h),
        k_of=lambda h: jnp.where(live_w, head_cols(qk, ML_WIDTH, h), 0.0),
        v_of=lambda h: jnp.where(live_w, cols(2 * ML_WIDTH + h * ML_HEAD_DIM, ML_HEAD_DIM), 0.0),
        o_of=lambda h: cols(3 * ML_WIDTH + h * ML_HEAD_DIM, ML_HEAD_DIM),
        causal=causal, ipre=ipre, bcum=bcum, groups=groups,
        m0_of=lambda g: m0_ref[g:g + 1, :], C0_of=lambda g, h: C0_ref[g, h],
        n0_of=lambda g, h: n0_ref[g, h:h + 1, :], hng_ref=hng_ref)
    for bb in range(nb):
        for h in range(ML_HEADS):
            hs = slice(h * ML_HEAD_DIM, (h + 1) * ML_HEAD_DIM)
            bo_ref[bb * t_valid:(bb + 1) * t_valid, hs] = outs[h][bb * L:bb * L + t_valid, :]
            C_ref[bb, h] = C_new[bb][h]
            n_ref[bb, h:h + 1, :] = n_new[bb][h]
        m_ref[bb:bb + 1, :] = m_new[bb]


def _mlstm_call(zq, cst, state, bif, cw, cb, hng, *, t_valid, nb):
    T = t_valid
    B = zq.shape[0] // T
    kern = functools.partial(_mlstm_kernel, t_valid=T)
    per_b = lambda shape: pl.BlockSpec((nb,) + shape, lambda b: (b,) + (0,) * len(shape))
    rows = lambda width: pl.BlockSpec((nb * T, width), lambda b: (b, 0))
    conv_spec = pl.BlockSpec((CONV_W - 1, nb, 2 * ML_WIDTH), lambda b: (0, b, 0))
    st_specs = [per_b((ML_HEADS, ML_HEAD_DIM, ML_HEAD_DIM)), per_b((ML_HEADS, ML_HEAD_DIM)),
                per_b((ML_HEADS,))]
    in_specs = ([rows(ML_COLS), conv_spec] + st_specs
                + [_const_spec((1, GATE_COLS)), _const_spec((CONV_W, 2 * ML_WIDTH)),
                   _const_spec((1, 2 * ML_WIDTH)), _const_spec((1, ML_WIDTH))])
    out_shape = [jax.ShapeDtypeStruct((B * T, ML_WIDTH), F32),
                 jax.ShapeDtypeStruct((CONV_W - 1, B, 2 * ML_WIDTH), F32),
                 jax.ShapeDtypeStruct((B, ML_HEADS, ML_HEAD_DIM, ML_HEAD_DIM), F32),
                 jax.ShapeDtypeStruct((B, ML_HEADS, ML_HEAD_DIM), F32),
                 jax.ShapeDtypeStruct((B, ML_HEADS), F32)]
    out_specs = [rows(ML_WIDTH), conv_spec] + st_specs
    return pl.pallas_call(
        kern,
        grid=(B // nb,),
        in_specs=in_specs,
        out_specs=out_specs,
        out_shape=out_shape,
        scratch_shapes=[pltpu.VMEM((nb, BF16_ROWS, ML_COLS), F32),
                        pltpu.VMEM((nb, 2 * SUBLANES, 2 * ML_WIDTH), F32)],
        compiler_params=pltpu.CompilerParams(dimension_semantics=("arbitrary",),
                                             vmem_limit_bytes=VMEM_LIMIT),
        name="mlstm",
    )(zq, cst, *state, bif, cw, cb, hng)


PL = 256
PG = PL // SUBLANES
TAIL = (CONV_W - 1) * SUBLANES
PROJ_PIECE_COLS = 512


def _mixer_ml_kernel(h_hbm, wqk_ref, wvo_ref, wif_ref, bif_ref, cw_ref, cb_ref, hng_ref,
                     bo_hbm, conv_ref, C_ref, n_ref, m_ref,
                     zqk_buf, zvo_buf, zif_buf, tail, hbuf, obuf, sem_in, sem_out,
                     *, chunks_per_seq, n_chunks):
    s = pl.program_id(0)

    def chunk_dmas(chunk, slot, inbound):
        row0 = chunk * PL
        copies = []
        for i in range(SUBLANES):
            hbm = (h_hbm if inbound else bo_hbm).at[pl.ds(row0 + PG * i, PG), :]
            vmem = (hbuf if inbound else obuf).at[slot, :, i, :]
            sem = (sem_in if inbound else sem_out).at[slot]
            copies.append(pltpu.make_async_copy(hbm, vmem, sem) if inbound
                          else pltpu.make_async_copy(vmem, hbm, sem))
        return copies

    @pl.when(s == 0)
    def _():
        zqk_buf[1] = jnp.zeros(zqk_buf.shape[1:], F32)
        zvo_buf[1] = jnp.zeros(zvo_buf.shape[1:], F32)
        zif_buf[1] = jnp.zeros(zif_buf.shape[1:], F32)
        for cp in chunk_dmas(0, 0, True):
            cp.start()

    @pl.when(jnp.maximum(s - 1, 0) % chunks_per_seq == 0)
    def _():
        tail[...] = jnp.zeros(tail.shape, F32)
        C_ref[...] = jnp.zeros(C_ref.shape, F32)
        n_ref[...] = jnp.zeros(n_ref.shape, F32)
        m_ref[...] = jnp.zeros(m_ref.shape, F32)

    def step(rd, wr):
        for cp in chunk_dmas(jnp.minimum(s + 1, n_chunks - 1), rd, True):
            cp.start()
        for cp in chunk_dmas(jnp.minimum(s, n_chunks - 1), wr, True):
            cp.wait()

        _mixer_ml_step(hbuf.at[wr], wqk_ref, wvo_ref, wif_ref, bif_ref, cw_ref, cb_ref, hng_ref,
                       obuf.at[rd], conv_ref, C_ref, n_ref, m_ref, zqk_buf, zvo_buf, zif_buf, tail,
                       rd=rd, wr=wr)

        @pl.when(s >= 1)
        def _():
            for cp in chunk_dmas(s - 1, rd, False):
                cp.start()

        @pl.when(s >= 2)
        def _():
            for cp in chunk_dmas(s - 2, wr, False):
                cp.wait()

        @pl.when(s == n_chunks)
        def _():
            for cp in chunk_dmas(n_chunks - 1, rd, True) + chunk_dmas(s - 1, rd, False):
                cp.wait()

    @pl.when(s % 2 == 0)
    def _():
        step(rd=1, wr=0)

    @pl.when(s % 2 == 1)
    def _():
        step(rd=0, wr=1)


def _mixer_ml_step(h_ref, wqk_ref, wvo_ref, wif_ref, bif_ref, cw_ref, cb_ref, hng_ref,
                   bo_ref, conv_ref, C_ref, n_ref, m_ref, zqk_buf, zvo_buf, zif_buf, tail,
                   *, rd, wr):
    h = h_ref[...].reshape(PL, D_MODEL).astype(BF16)

    def piece(w_ref, buf, c0, width):
        def emit():
            buf[wr, :, c0:c0 + width] = _bdot(h, w_ref[:, c0:c0 + width])
        return emit

    pieces = [piece(w_ref, buf, c0, PROJ_PIECE_COLS)
              for w_ref, buf in ((wqk_ref, zqk_buf), (wvo_ref, zvo_buf))
              for c0 in range(0, 2 * ML_WIDTH, PROJ_PIECE_COLS)]
    pieces.append(piece(wif_ref, zif_buf, 0, GATE_COLS))
    pieces = iter(pieces)

    def fill():
        emit = next(pieces, None)
        if emit is not None:
            emit()

    fill()
    zqk = zqk_buf.at[rd]
    zvo = zvo_buf.at[rd]
    zif = zif_buf[rd] + bif_ref[...]

    zqk_tail = zqk[PL - TAIL:, :]
    sub = lax.broadcasted_iota(jnp.int32, (SUBLANES, 2 * ML_WIDTH), 0)
    wrapped = []
    for g in range(CONV_W - 1):
        cur = pltpu.roll(zqk_tail[g * SUBLANES:(g + 1) * SUBLANES], 1, axis=0)
        prev = pltpu.roll(tail[g * SUBLANES:(g + 1) * SUBLANES, :], 1, axis=0)
        wrapped.append(jnp.where(sub == 0, prev, cur))
    wrapped = jnp.concatenate(wrapped, axis=0)
    tail[...] = zqk_tail
    conv_ref[...] = jnp.concatenate(
        [zqk_tail[g * SUBLANES + SUBLANES - 1:(g + 1) * SUBLANES, :] for g in range(CONV_W - 1)], axis=0)

    def conv_silu(c0, width):
        cs = slice(c0, c0 + width)
        acc = cb_ref[:, cs] + cw_ref[CONV_W - 1:CONV_W, cs] * zqk[:, cs]
        for d in range(1, CONV_W):
            shifted = jnp.concatenate(
                [wrapped[TAIL - d * SUBLANES:, cs], zqk[:PL - d * SUBLANES, cs]], axis=0)
            acc = acc + cw_ref[CONV_W - 1 - d:CONV_W - d, cs] * shifted
        return acc * jax.nn.sigmoid(acc)

    pr = lax.broadcasted_iota(jnp.int32, (PL, PL), 0)
    pc = lax.broadcasted_iota(jnp.int32, (PL, PL), 1)
    bits, low = SUBLANES.bit_length() - 1, SUBLANES - 1
    causal = ((pc >> bits) + PG * (pc & low)) <= ((pr >> bits) + PG * (pr & low))

    logf = pltpu.roll(_log_sigmoid(zif), GATE_COLS - ML_HEADS, axis=1)
    run, partial = None, []
    for n in range(PG):
        blk = logf[n * SUBLANES:(n + 1) * SUBLANES, :]
        run = blk if run is None else run + blk
        partial.append(run)
    sub_g = lax.broadcasted_iota(jnp.int32, (SUBLANES, GATE_COLS), 0)
    incl = run
    for step in (1, 2, 4):
        incl = incl + jnp.where(sub_g >= step, pltpu.roll(incl, step, axis=0), 0.0)
    earlier = incl - run
    bcum = jnp.concatenate([p + earlier for p in partial], axis=0)

    def head_cols(base, h):
        return zvo[:, base + h * ML_HEAD_DIM:base + (h + 1) * ML_HEAD_DIM]

    outs, C_new, n_new, m_new = _mlstm_heads(
        q_of=lambda h: conv_silu(h * ML_HEAD_DIM, ML_HEAD_DIM),
        k_of=lambda h: conv_silu(ML_WIDTH + h * ML_HEAD_DIM, ML_HEAD_DIM),
        v_of=lambda h: head_cols(0, h),
        o_of=lambda h: head_cols(ML_WIDTH, h),
        causal=causal, ipre=zif, bcum=bcum, groups=[(0, PL, PL - 1)],
        m0_of=lambda g: m_ref[...], C0_of=lambda g, h: C_ref[h], n0_of=lambda g, h: n_ref[h:h + 1, :],
        hng_ref=hng_ref, fill=fill)
    assert next(pieces, None) is None, "projection pieces left over"
    for h in range(ML_HEADS):
        bo_ref[:, :, h * ML_HEAD_DIM:(h + 1) * ML_HEAD_DIM] = outs[h].reshape(PG, SUBLANES, ML_HEAD_DIM)
        C_ref[h] = C_new[0][h]
        n_ref[h:h + 1, :] = n_new[0][h]
    m_ref[...] = m_new[0]


def _mixer_ml_call(hp, n_seq, w_all, bif, cw, cb, hng):
    B = n_seq
    T = hp.shape[0] // B
    cps = T // PL
    n_chunks = B * cps
    math = lambda s: jnp.maximum(s - 1, 0)
    per_b = lambda shape: pl.BlockSpec((None,) + shape,
                                       lambda s: (math(s) // cps,) + (0,) * len(shape))
    out_shape = [jax.ShapeDtypeStruct((B * T, ML_WIDTH), F32),
                 jax.ShapeDtypeStruct((B, CONV_W - 1, 2 * ML_WIDTH), F32),
                 jax.ShapeDtypeStruct((B, ML_HEADS, ML_HEAD_DIM, ML_HEAD_DIM), F32),
                 jax.ShapeDtypeStruct((B, ML_HEADS, ML_HEAD_DIM), F32),
                 jax.ShapeDtypeStruct((B, 1, ML_HEADS), F32)]
    out_specs = [pl.BlockSpec(memory_space=pl.ANY),
                 per_b((CONV_W - 1, 2 * ML_WIDTH)),
                 per_b((ML_HEADS, ML_HEAD_DIM, ML_HEAD_DIM)), per_b((ML_HEADS, ML_HEAD_DIM)),
                 per_b((1, ML_HEADS))]
    return pl.pallas_call(
        functools.partial(_mixer_ml_kernel, chunks_per_seq=cps, n_chunks=n_chunks),
        grid=(n_chunks + 1,),
        in_specs=[pl.BlockSpec(memory_space=pl.ANY)]
        + _ml_weight_specs()
        + [_const_spec((1, GATE_COLS)), _const_spec((CONV_W, 2 * ML_WIDTH)),
           _const_spec((1, 2 * ML_WIDTH)), _const_spec((1, ML_WIDTH))],
        out_specs=out_specs,
        out_shape=out_shape,
        scratch_shapes=[pltpu.VMEM((2, PL, 2 * ML_WIDTH), F32),
                        pltpu.VMEM((2, PL, 2 * ML_WIDTH), F32),
                        pltpu.VMEM((2, PL, GATE_COLS), F32),
                        pltpu.VMEM((TAIL, 2 * ML_WIDTH), F32),
                        pltpu.VMEM((2, PG, SUBLANES, D_MODEL), F32),
                        pltpu.VMEM((2, PG, SUBLANES, ML_WIDTH), F32),
                        pltpu.SemaphoreType.DMA((2,)),
                        pltpu.SemaphoreType.DMA((2,))],
        compiler_params=pltpu.CompilerParams(dimension_semantics=("arbitrary",),
                                             vmem_limit_bytes=VMEM_LIMIT),
        name="mixer_ml",
    )(hp, w_all, w_all, w_all, bif, cw, cb, hng)


def _merge_ffn_kernel(x_ref, pa_ref, bo_ref, g1_ref, wg_ref, bg_ref, wpb_ref, wout_ref,
                      g2_ref, wfi_ref, wfo_ref, gf_ref, y_ref):
    tm = x_ref.shape[0]
    subs = [slice(r0, r0 + FFN_SUB_ROWS) for r0 in range(0, tm, FFN_SUB_ROWS)]
    x = [x_ref[rs, :] for rs in subs]
    h = [_rms(xi, g1_ref[...]).astype(BF16) for xi in x]
    gab = [_bdot(hi, wg_ref[...]) + bg_ref[...] for hi in h]
    pb = [_bdot(bo_ref[rs, :].astype(BF16), wpb_ref[...]) for rs in subs]
    merged = [(jax.nn.sigmoid(g[:, :D_MODEL]) * pa_ref[rs, :]
               + jax.nn.sigmoid(g[:, D_MODEL:]) * p).astype(BF16) for g, p, rs in zip(gab, pb, subs)]
    x1 = [xi + _bdot(mi, wout_ref[...]) for xi, mi in zip(x, merged)]
    h2 = [_rms(xi, g2_ref[...]).astype(BF16) for xi in x1]
    gu = [_bdot(hi, wfi_ref[...]) for hi in h2]
    hid = [(g[:, :D_FF] * jax.nn.sigmoid(g[:, :D_FF]) * g[:, D_FF:]).astype(BF16) for g in gu]
    x2 = [xi + _bdot(hi, wfo_ref[...]) for xi, hi in zip(x1, hid)]
    for rs, xi in zip(subs, x2):
        y_ref[rs, :] = _rms(xi, gf_ref[...])


def _merge_ffn_call(x, pa, bo, g1, wg, bg, wpb, wout, g2, wfi, wfo, gf, *, tm):
    m = x.shape[0]
    row = pl.BlockSpec((tm, D_MODEL), lambda i: (i, 0))
    return pl.pallas_call(
        _merge_ffn_kernel,
        grid=(m // tm,),
        in_specs=[row, row, row, _const_spec((1, D_MODEL)), _const_spec((D_MODEL, 2 * D_MODEL)),
                  _const_spec((1, 2 * D_MODEL)), _const_spec((ML_WIDTH, D_MODEL)),
                  _const_spec((D_MODEL, D_MODEL)), _const_spec((1, D_MODEL)),
                  _const_spec((D_MODEL, 2 * D_FF)), _const_spec((D_FF, D_MODEL)),
                  _const_spec((1, D_MODEL))],
        out_specs=row,
        out_shape=jax.ShapeDtypeStruct((m, D_MODEL), F32),
        compiler_params=pltpu.CompilerParams(dimension_semantics=("arbitrary",),
                                             vmem_limit_bytes=VMEM_LIMIT),
        name="merge_ffn",
    )(x, pa, bo, g1, wg, bg, wpb, wout, g2, wfi, wfo, gf)


SAMPLE_SEQS_PER_STEP = 8
TM_MIX = 512
TM_GMLP = 512
TM_FFN = 512


def _spatial_tiles(w_s, b_s, chunk):
    if chunk == GM_CHUNK:
        ws_t, b_pos = w_s[:, :chunk, :chunk], b_s[:, :chunk].T
    else:
        onehot = jnp.asarray(np.arange(GM_CHUNK)[:, None] % chunk == np.arange(chunk)[None, :], F32)
        hp = lax.Precision.HIGHEST
        ws_t = jnp.einsum("ri,gij,cj->grc", onehot, w_s[:, :chunk, :chunk], onehot, precision=hp)
        b_pos = jnp.dot(onehot, b_s[:, :chunk].T, precision=hp)
    bs_t = jnp.repeat(b_pos, GM_GROUP_W, axis=1)
    return ws_t, bs_t


def _gmlp_branch(xf, w, chunk, *, emit_v, emit_h, **side_jobs):
    ws_t, bs_t = _spatial_tiles(w["w_s"], w["b_s"], chunk)
    return _gmlp_call(xf, w["g1"], w["wuv"], w["lng"], w["lnb"], ws_t, bs_t, w["wpa"],
                      chunk=chunk, emit_v=emit_v, emit_h=emit_h,
                      tm=min(TM_GMLP, xf.shape[0]), **side_jobs)


def _merge_branch(xf, pa, bo, w):
    return _merge_ffn_call(xf, pa, bo, w["g1"], w["wg"], w["bg"], w["wpb"], w["wout"], w["g2"],
                           w["wfi"], w["wfo"], w["gf"], tm=TM_FFN)


def kernel(x_prompt, x_sample, state_conv, state_C, state_n, state_m, g_norm1, w_in, b_i, b_f, ln_g, ln_b, w_s, b_s, conv_w, conv_b, hn_g, b_gate, w_proj_a, w_proj_b, w_out, g_norm2, w_ffn_in, w_ffn_out, g_final):
    Bp, Tp, _ = x_prompt.shape
    Bs, Ts, _ = x_sample.shape
    win = w_in[0]
    c_ml = 2 * GM_WIDTH
    c_gate = c_ml + 4 * ML_WIDTH + 2 * ML_HEADS
    n_ml_blocks = -(-(ML_COLS) // TCAST_ROWS)
    win_t = jnp.transpose(win)
    w = dict(
        g1=g_norm1[0][None], g2=g_norm2[0][None], gf=g_final[None],
        wuv=_tcast_call(win_t, 0, c_ml // TCAST_ROWS),
        bg=b_gate[0].reshape(1, 2 * D_MODEL),
        lng=ln_g[0][None], lnb=ln_b[0][None], w_s=w_s[0], b_s=b_s[0],
        bif=jnp.pad(jnp.concatenate([b_i[0], b_f[0]]), (0, GATE_COLS - 2 * ML_HEADS))[None],
        cw=conv_w[0], cb=conv_b[0][None], hng=hn_g[0][None],
        wpa=w_proj_a[0].astype(BF16),
    )

    xpf = x_prompt.reshape(Bp * Tp, D_MODEL)
    pa_p, hp_p, w["wpb"], w["wout"], w["wfi"], w["wfo"], w["w_ml"], w["wg"] = _gmlp_branch(
        xpf, w, GM_CHUNK, emit_v=False, emit_h=True,
        cast_weights=(w_proj_b[0], w_out[0], w_ffn_in[0], w_ffn_out[0]),
        tcast_weight=win_t,
        tcast_jobs=((c_ml, n_ml_blocks, 0), (c_gate, 2 * D_MODEL // TCAST_ROWS, n_ml_blocks)))
    bo_p, conv_p, C_p, n_p, m_p = _mixer_ml_call(hp_p, Bp, w["w_ml"], w["bif"], w["cw"], w["cb"],
                                                 w["hng"])
    y_p = _merge_branch(xpf, pa_p, bo_p, w)

    xsf = x_sample.reshape(Bs * Ts, D_MODEL)
    pa_s, vn_s = _gmlp_branch(xsf, w, Ts, emit_v=True, emit_h=False)
    zq = _inproj_call(xsf, w["g1"], w["w_ml"], tm=TM_MIX)
    st = (state_C[0], state_n[0], state_m[0])
    bo_s, conv_s, C_s, n_s, m_s = _mlstm_call(zq, jnp.transpose(state_conv[0], (1, 0, 2)), st, w["bif"],
                                              w["cw"], w["cb"], w["hng"], t_valid=Ts,
                                              nb=SAMPLE_SEQS_PER_STEP)
    y_s = _merge_branch(xsf, pa_s, bo_s, w)

    return (y_p.reshape(Bp, Tp, D_MODEL), y_s.reshape(Bs, Ts, D_MODEL),
            conv_p[None], C_p[None], n_p[None], m_p.reshape(1, Bp, ML_HEADS),
            jnp.transpose(conv_s, (1, 0, 2))[None], C_s[None], n_s[None], m_s[None],
            vn_s.reshape(1, Bs, Ts, GM_WIDTH))
```

```python
import functools
import math

import jax
import jax.numpy as jnp
import numpy as np
from jax import lax
from jax.experimental import pallas as pl
from jax.experimental.pallas import tpu as pltpu

D_MODEL = 1024
GM_WIDTH = D_MODEL
GM_GROUPS = 4
GM_GROUP_W = GM_WIDTH // GM_GROUPS
GM_CHUNK = 128
ML_HEADS = 4
ML_HEAD_DIM = D_MODEL // ML_HEADS
ML_WIDTH = ML_HEADS * ML_HEAD_DIM
CONV_W = 4
D_FF = 2816
EPS = 1e-6

LANES = 128
SUBLANES = 8
BF16_ROWS = 16
GATE_COLS = LANES
ML_COLS = 4 * ML_WIDTH + GATE_COLS
VMEM_LIMIT = 56 * 1024 * 1024
GMLP_SUB_ROWS = 256
FFN_SUB_ROWS = 256

F32 = jnp.float32
BF16 = jnp.bfloat16
NEG_INF = float("-inf")
LN_INV_K_SCALE = 0.5 * math.log(ML_HEAD_DIM)
NT_DIMS = (((1,), (1,)), ((), ()))
TN_DIMS = (((0,), (0,)), ((), ()))


def _rms(x, g):
    return x * lax.rsqrt(jnp.mean(x * x, axis=-1, keepdims=True) + EPS) * g


def _gelu(x):
    return 0.5 * x * (1.0 + lax.erf(x * (2.0 ** -0.5)))


def _log_sigmoid(x):
    return jnp.minimum(x, 0.0) - jnp.log1p(jnp.exp(-jnp.abs(x)))


def _bdot(a, b):
    return jnp.dot(a, b, preferred_element_type=F32)


def _n_tokens(a):
    return a.shape[0] if len(a.shape) == 2 else a.shape[0] * a.shape[1]


def _load_rows(ref, rs):
    if len(ref.shape) == 2:
        return ref[rs, :]
    t = ref.shape[1]
    return ref[rs.start // t:rs.stop // t].reshape(rs.stop - rs.start, ref.shape[2])


def _store_rows(ref, rs, val):
    if len(ref.shape) == 2:
        ref[rs, :] = val
    else:
        t = ref.shape[1]
        ref[rs.start // t:rs.stop // t] = val.reshape((rs.stop - rs.start) // t, t, ref.shape[2])


def _token_spec(shape, tm):
    if len(shape) == 2:
        return pl.BlockSpec((tm, shape[1]), lambda i: (i, 0))
    return pl.BlockSpec((tm // shape[1], shape[1], shape[2]), lambda i: (i, 0, 0))


def _const_spec(shape):
    nd = len(shape)
    return pl.BlockSpec(shape, lambda *_: (0,) * nd, pipeline_mode=pl.Buffered(1))


def _gmlp_kernel(x_ref, g1_ref, wuv_ref, lng_ref, lnb_ref, ws_ref, bs_ref, wpa_ref, *rest,
                 chunk, emit_v, emit_h, n_cast, n_tcast):
    rest = list(rest)
    cast_src = [rest.pop(0) for _ in range(n_cast)]
    tcast_src = [rest.pop(0) for _ in range(n_tcast)]
    pa_ref = rest.pop(0)
    vn_ref = rest.pop(0) if emit_v else None
    hp_ref = rest.pop(0) if emit_h else None
    for src in cast_src:
        rest.pop(0)[...] = src[...].astype(BF16)
    for src in tcast_src:
        rest.pop(0)[...] = src[...].T.astype(BF16)
    a_sc = rest.pop(0)
    tm = _n_tokens(x_ref)
    blk = ws_ref.shape[1]
    sub = GMLP_SUB_ROWS
    subs = [slice(r0, r0 + sub) for r0 in range(0, tm, sub)]
    hb = []
    for rs in subs:
        hf = _rms(_load_rows(x_ref, rs), g1_ref[...])
        hb.append(hf.astype(BF16))
        if emit_h:
            hp_ref[rs, :] = hf
    zu = [_bdot(h, wuv_ref[:, :GM_WIDTH]) for h in hb]
    zv = [_bdot(h, wuv_ref[:, GM_WIDTH:]) for h in hb]
    r = lax.broadcasted_iota(jnp.int32, (blk, blk), 0)
    c = lax.broadcasted_iota(jnp.int32, (blk, blk), 1)
    keep = c <= r
    if chunk < blk:
        sh = chunk.bit_length() - 1
        keep = jnp.logical_and(keep, (r >> sh) == (c >> sh))
    wsm = [jnp.where(keep, ws_ref[g], 0.0).astype(BF16) for g in range(GM_GROUPS)]
    for si, rs in enumerate(subs):
        u = _gelu(zu[si])
        v = _gelu(zv[si])
        mu = jnp.mean(v, axis=-1, keepdims=True)
        vc = v - mu
        var = jnp.mean(vc * vc, axis=-1, keepdims=True)
        vn = vc * lax.rsqrt(var + EPS) * lng_ref[...] + lnb_ref[...]
        if emit_v:
            _store_rows(vn_ref, rs, vn)
        vb = vn.astype(BF16)
        for g in range(GM_GROUPS):
            cs = slice(g * GM_GROUP_W, (g + 1) * GM_GROUP_W)
            for i in range(sub // blk):
                ls = slice(i * blk, (i + 1) * blk)
                s = _bdot(wsm[g], vb[ls, cs]) + bs_ref[:, cs]
                a_sc[rs.start + i * blk:rs.start + (i + 1) * blk, cs] = (u[ls, cs] * s).astype(BF16)
        pa_ref[rs, :] = _bdot(a_sc[rs, :], wpa_ref[...])


def _cast_block_spec(n_rows, n_cols, steps):
    n_blocks = steps
    while n_rows % n_blocks or (n_rows // n_blocks) % BF16_ROWS:
        n_blocks //= 2
    per = steps // n_blocks
    return pl.BlockSpec((n_rows // n_blocks, n_cols), lambda i: (i // per, 0))


TCAST_ROWS = 256


def _tcast_specs(job):
    first_row, n_blocks, first_step = job
    assert first_row % SUBLANES == 0
    blk = lambda i: jnp.clip(i - first_step, 0, n_blocks - 1)
    src = pl.BlockSpec((pl.Element(TCAST_ROWS), pl.Element(D_MODEL)),
                       lambda i: (pl.multiple_of(first_row + TCAST_ROWS * blk(i), SUBLANES), 0))
    dst = pl.BlockSpec((D_MODEL, TCAST_ROWS), lambda i: (0, blk(i)))
    return src, dst, jax.ShapeDtypeStruct((D_MODEL, TCAST_ROWS * n_blocks), BF16)


def _tcast_kernel(src_ref, dst_ref):
    dst_ref[...] = src_ref[...].T.astype(BF16)


def _tcast_call(wt, first_row, n_blocks):
    src, dst, shape = _tcast_specs((first_row, n_blocks, 0))
    return pl.pallas_call(
        _tcast_kernel, grid=(n_blocks,), in_specs=[src], out_specs=dst, out_shape=shape,
        compiler_params=pltpu.CompilerParams(dimension_semantics=("arbitrary",)),
        name="tcast",
    )(wt)


def _gmlp_call(x, g1, wuv, lng, lnb, ws_t, bs_t, wpa, *, chunk, emit_v, emit_h, tm,
               cast_weights=(), tcast_weight=None, tcast_jobs=()):
    m = _n_tokens(x)
    blk = ws_t.shape[1]
    steps = m // tm
    assert steps & (steps - 1) == 0
    assert all(first_step + n_blocks <= steps for _, n_blocks, first_step in tcast_jobs)
    row = pl.BlockSpec((tm, D_MODEL), lambda i: (i, 0))
    out_shape = [jax.ShapeDtypeStruct((m, D_MODEL), F32)]
    out_specs = [row]
    scratch = [pltpu.VMEM((tm, GM_WIDTH), BF16)]
    cast_specs = [_cast_block_spec(cw.shape[0], cw.shape[1], steps) for cw in cast_weights]
    tcast = [_tcast_specs(job) for job in tcast_jobs]
    if emit_v:
        out_shape.append(jax.ShapeDtypeStruct(x.shape, F32))
        out_specs.append(_token_spec(x.shape, tm))
    if emit_h:
        out_shape.append(jax.ShapeDtypeStruct((m, D_MODEL), F32))
        out_specs.append(row)
    out_shape += [jax.ShapeDtypeStruct(cw.shape, BF16) for cw in cast_weights]
    out_shape += [t[2] for t in tcast]
    out_specs += cast_specs + [t[1] for t in tcast]
    return pl.pallas_call(
        functools.partial(_gmlp_kernel, chunk=chunk, emit_v=emit_v, emit_h=emit_h,
                          n_cast=len(cast_weights), n_tcast=len(tcast)),
        grid=(steps,),
        in_specs=[_token_spec(x.shape, tm), _const_spec((1, D_MODEL)),
                  _const_spec((D_MODEL, 2 * GM_WIDTH)),
                  _const_spec((1, GM_WIDTH)), _const_spec((1, GM_WIDTH)),
                  _const_spec((GM_GROUPS, blk, blk)), _const_spec((blk, GM_WIDTH)),
                  _const_spec((GM_WIDTH, D_MODEL))] + cast_specs + [t[0] for t in tcast],
        out_specs=out_specs,
        out_shape=out_shape,
        scratch_shapes=scratch,
        compiler_params=pltpu.CompilerParams(dimension_semantics=("arbitrary",),
                                             vmem_limit_bytes=VMEM_LIMIT),
        name="gmlp",
    )(x, g1, wuv, lng, lnb, ws_t, bs_t, wpa, *cast_weights, *([tcast_weight] * len(tcast)))


def _ml_weight_specs():
    wide = 2 * ML_WIDTH
    col_block = lambda width, idx: pl.BlockSpec((D_MODEL, width), lambda *_: (0, idx),
                                                pipeline_mode=pl.Buffered(1))
    return [col_block(wide, 0), col_block(wide, 1), col_block(GATE_COLS, 4 * ML_WIDTH // GATE_COLS)]


def _inproj_kernel(x_ref, g1_ref, wqk_ref, wvo_ref, wif_ref, z_ref):
    h = _rms(_load_rows(x_ref, slice(0, _n_tokens(x_ref))), g1_ref[...]).astype(BF16)
    z_ref[:, 0:2 * ML_WIDTH] = _bdot(h, wqk_ref[...])
    z_ref[:, 2 * ML_WIDTH:4 * ML_WIDTH] = _bdot(h, wvo_ref[...])
    z_ref[:, 4 * ML_WIDTH:ML_COLS] = _bdot(h, wif_ref[...])


def _inproj_call(x, g1, w_all, *, tm):
    m = _n_tokens(x)
    return pl.pallas_call(
        _inproj_kernel,
        grid=(m // tm,),
        in_specs=[_token_spec(x.shape, tm), _const_spec((1, D_MODEL))]
        + _ml_weight_specs(),
        out_specs=pl.BlockSpec((tm, ML_COLS), lambda i: (i, 0)),
        out_shape=jax.ShapeDtypeStruct((m, ML_COLS), F32),
        compiler_params=pltpu.CompilerParams(dimension_semantics=("arbitrary",),
                                             vmem_limit_bytes=VMEM_LIMIT),
        name="inproj",
    )(x, g1, w_all, w_all, w_all)


def _mlstm_heads(q_of, k_of, v_of, o_of, causal, ipre, bcum, groups, m0_of, C0_of, n0_of, hng_ref,
                 fill=None):
    n_groups = len(groups)
    single = n_groups == 1
    fill = fill or (lambda: None)

    def rows_of(x, g):
        return x if single else x[groups[g][0]:groups[g][0] + groups[g][1]]

    def per_row(vals):
        if single:
            return vals[0]
        return jnp.concatenate([jnp.broadcast_to(v, (groups[g][1], v.shape[1]))
                                for g, v in enumerate(vals)], axis=0)

    a = ipre - bcum
    a_t = a.T
    m_rows = per_row([m0_of(g) for g in range(n_groups)])

    def prepare(h):
        p = {}
        a2 = jnp.where(causal, a_t[h:h + 1, :], -jnp.inf)
        p["mc"] = mc = jnp.maximum(jnp.max(a2, axis=1, keepdims=True), m_rows[:, h:h + 1])
        p["m_last"] = m_last = [mc[grp[2]:grp[2] + 1, :] for grp in groups]
        p["dm"] = jnp.exp(a2 - (mc + LN_INV_K_SCALE))
        p["w_inter"] = jnp.exp(m_rows[:, h:h + 1] - mc)
        p["w_col"] = w_col = jnp.exp(a[:, h:h + 1] - (per_row(m_last) + LN_INV_K_SCALE))
        p["decay"] = [jnp.exp(m0_of(g)[:, h:h + 1] - m_last[g]) for g in range(n_groups)]
        p["q"] = q = q_of(h)
        p["k"] = k = k_of(h)
        v = v_of(h)
        p["qb"], p["kb"], p["vb"] = q.astype(BF16), k.astype(BF16), v.astype(BF16)
        p["vw"] = (v * w_col).astype(BF16)
        p["c_old"] = [C0_of(g, h) for g in range(n_groups)]
        p["n_old"] = [n0_of(g, h) for g in range(n_groups)]
        return p

    def first_matmuls(p):
        p["qk"] = lax.dot_general(p["qb"], p["kb"], NT_DIMS, preferred_element_type=F32)
        p["qc"] = [lax.dot_general(rows_of(p["qb"], g), p["c_old"][g].astype(BF16), NT_DIMS,
                                   preferred_element_type=F32) for g in range(n_groups)]
        if single:
            n_rows = jnp.broadcast_to(p["n_old"][0], (LANES, ML_HEAD_DIM)).astype(BF16)
            p["qn"] = lax.dot_general(p["qb"], n_rows, NT_DIMS, preferred_element_type=F32)[:, 0:1]
        else:
            p["qn"] = jnp.sum(p["q"] * per_row(p["n_old"]), axis=1, keepdims=True)

    def second_matmuls(p):
        p["s"] = s = p["dm"] * p["qk"]
        p["sv"] = _bdot(s.astype(BF16), p["vb"])
        p["cupd"] = [lax.dot_general(rows_of(p["vw"], g), rows_of(p["kb"], g), TN_DIMS,
                                     preferred_element_type=F32) for g in range(n_groups)]

    def finish(h, p):
        qc_rows = p["qc"][0] if single else jnp.concatenate(p["qc"], axis=0)
        num = p["w_inter"] * qc_rows + p["sv"]
        den = p["w_inter"] * p["qn"] + jnp.sum(p["s"], axis=1, keepdims=True)
        hcur = num / jnp.maximum(jnp.abs(den), jnp.exp(-(bcum[:, h:h + 1] + p["mc"])))
        mu = jnp.mean(hcur, axis=1, keepdims=True)
        hc = hcur - mu
        var = jnp.mean(hc * hc, axis=1, keepdims=True)
        hs = slice(h * ML_HEAD_DIM, (h + 1) * ML_HEAD_DIM)
        out = jax.nn.sigmoid(o_of(h)) * (hc * lax.rsqrt(var + EPS) * hng_ref[:, hs])
        kw = p["k"] * p["w_col"]
        c_new = [p["decay"][g] * p["c_old"][g] + p["cupd"][g] for g in range(n_groups)]
        n_new = [p["decay"][g] * p["n_old"][g] + jnp.sum(rows_of(kw, g), axis=0, keepdims=True)
                 for g in range(n_groups)]
        return out, c_new, n_new

    H = ML_HEADS
    per_head = [None] * H
    done = [None] * H
    if single:
        per_head[0] = prepare(0)
        fill()
        first_matmuls(per_head[0])
        if H > 1:
            per_head[1] = prepare(1)
        fill()
        for h in range(H):
            second_matmuls(per_head[h])
            if h + 1 < H:
                first_matmuls(per_head[h + 1])
            fill()
            if h + 2 < H:
                per_head[h + 2] = prepare(h + 2)
            done[h] = finish(h, per_head[h])
            fill()
    else:
        per_head = [prepare(h) for h in range(H)]
        for stage in (first_matmuls, second_matmuls):
            for h in range(H):
                stage(per_head[h])
        done = [finish(h, per_head[h]) for h in range(H)]

    outs = [done[h][0] for h in range(H)]
    C_new = [[done[h][1][g] for h in range(H)] for g in range(n_groups)]
    n_new = [[done[h][2][g] for h in range(H)] for g in range(n_groups)]
    m_new = []
    for g, grp in enumerate(groups):
        row = m0_of(g)
        lane = lax.broadcasted_iota(jnp.int32, row.shape, 1)
        for h in range(H):
            row = jnp.where(lane == h, bcum[grp[2]:grp[2] + 1, h:h + 1] + per_head[h]["m_last"][g], row)
        m_new.append(row)
    return outs, C_new, n_new, m_new


def _mlstm_kernel(zq_ref, cst_ref, C0_ref, n0_ref, m0_ref, bif_ref, cw_ref, cb_ref, hng_ref,
                  bo_ref, conv_ref, C_ref, n_ref, m_ref, zp, xp, *, t_valid):
    nb = C0_ref.shape[0]
    L = BF16_ROWS
    R = nb * L

    @pl.when(pl.program_id(0) == 0)
    def _():
        zp[...] = jnp.zeros(zp.shape, F32)

    for bb in range(nb):
        zp[bb, 0:t_valid, :] = zq_ref[bb * t_valid:(bb + 1) * t_valid, :]
        for j in range(CONV_W - 1):
            row = SUBLANES - (CONV_W - 1) + j
            xp[bb, row:row + 1, :] = cst_ref[j, bb:bb + 1, :]
        xp[bb, SUBLANES:2 * SUBLANES, :] = zp[bb, 0:SUBLANES, 0:2 * ML_WIDTH]
        for j in range(CONV_W - 1):
            row = SUBLANES + t_valid - (CONV_W - 1) + j
            conv_ref[j, bb:bb + 1, :] = xp[bb, row:row + 1, :]
    qk_rows = []
    for bb in range(nb):
        acc = cb_ref[...]
        for j in range(CONV_W):
            off = SUBLANES - (CONV_W - 1) + j
            acc = acc + cw_ref[j:j + 1, :] * xp[bb, off:off + SUBLANES, :]
        qk_rows += [acc, jnp.zeros((L - SUBLANES, 2 * ML_WIDTH), F32)]
    qk = jnp.concatenate(qk_rows, axis=0)
    qk = qk * jax.nn.sigmoid(qk)

    def cols(c0, width):
        return zp[:, :, c0:c0 + width].reshape(R, width)

    zif = cols(4 * ML_WIDTH, GATE_COLS) + bif_ref[...]
    live = (lax.broadcasted_iota(jnp.int32, (R, GATE_COLS), 0) & (L - 1)) < t_valid
    ipre = jnp.where(live, zif, NEG_INF)
    logf = jnp.where(live, pltpu.roll(_log_sigmoid(zif), GATE_COLS - ML_HEADS, axis=1), 0.0)
    assert t_valid <= SUBLANES
    sub_g = lax.broadcasted_iota(jnp.int32, (SUBLANES, GATE_COLS), 0)
    parts = []
    for bb in range(nb):
        incl = logf[bb * L:bb * L + SUBLANES, :]
        for step in (1, 2, 4):
            incl = incl + jnp.where(sub_g >= step, pltpu.roll(incl, step, axis=0), 0.0)
        parts += [incl, jnp.broadcast_to(incl[SUBLANES - 1:SUBLANES, :], (L - SUBLANES, GATE_COLS))]
    bcum = jnp.concatenate(parts, axis=0)
    r = lax.broadcasted_iota(jnp.int32, (R, R), 0)
    c = lax.broadcasted_iota(jnp.int32, (R, R), 1)
    sh = L.bit_length() - 1
    causal = jnp.logical_and(c <= r, (r >> sh) == (c >> sh))
    live_w = (lax.broadcasted_iota(jnp.int32, (R, ML_HEAD_DIM), 0) & (L - 1)) < t_valid

    def head_cols(x, base, h):
        return x[:, base + h * ML_HEAD_DIM:base + (h + 1) * ML_HEAD_DIM]

    groups = [(bb * L, L, bb * L + L - 1) for bb in range(nb)]
    outs, C_new, n_new, m_new = _mlstm_heads(
        q_of=lambda h: head_cols(qk, 0, h),
        k_of=lambda h: jnp.where(live_w, head_cols(qk, ML_WIDTH, h), 0.0),
        v_of=lambda h: jnp.where(live_w, cols(2 * ML_WIDTH + h * ML_HEAD_DIM, ML_HEAD_DIM), 0.0),
        o_of=lambda h: cols(3 * ML_WIDTH + h * ML_HEAD_DIM, ML_HEAD_DIM),
        causal=causal, ipre=ipre, bcum=bcum, groups=groups,
        m0_of=lambda g: m0_ref[g:g + 1, :], C0_of=lambda g, h: C0_ref[g, h],
        n0_of=lambda g, h: n0_ref[g, h:h + 1, :], hng_ref=hng_ref)
    for bb in range(nb):
        for h in range(ML_HEADS):
            hs = slice(h * ML_HEAD_DIM, (h + 1) * ML_HEAD_DIM)
            bo_ref[bb * t_valid:(bb + 1) * t_valid, hs] = outs[h][bb * L:bb * L + t_valid, :]
            C_ref[bb, h] = C_new[bb][h]
            n_ref[bb, h:h + 1, :] = n_new[bb][h]
        m_ref[bb:bb + 1, :] = m_new[bb]


def _mlstm_call(zq, cst, state, bif, cw, cb, hng, *, t_valid, nb):
    T = t_valid
    B = zq.shape[0] // T
    kern = functools.partial(_mlstm_kernel, t_valid=T)
    per_b = lambda shape: pl.BlockSpec((nb,) + shape, lambda b: (b,) + (0,) * len(shape))
    rows = lambda width: pl.BlockSpec((nb * T, width), lambda b: (b, 0))
    conv_spec = pl.BlockSpec((CONV_W - 1, nb, 2 * ML_WIDTH), lambda b: (0, b, 0))
    st_specs = [per_b((ML_HEADS, ML_HEAD_DIM, ML_HEAD_DIM)), per_b((ML_HEADS, ML_HEAD_DIM)),
                per_b((ML_HEADS,))]
    in_specs = ([rows(ML_COLS), conv_spec] + st_specs
                + [_const_spec((1, GATE_COLS)), _const_spec((CONV_W, 2 * ML_WIDTH)),
                   _const_spec((1, 2 * ML_WIDTH)), _const_spec((1, ML_WIDTH))])
    out_shape = [jax.ShapeDtypeStruct((B * T, ML_WIDTH), F32),
                 jax.ShapeDtypeStruct((CONV_W - 1, B, 2 * ML_WIDTH), F32),
                 jax.ShapeDtypeStruct((B, ML_HEADS, ML_HEAD_DIM, ML_HEAD_DIM), F32),
                 jax.ShapeDtypeStruct((B, ML_HEADS, ML_HEAD_DIM), F32),
                 jax.ShapeDtypeStruct((B, ML_HEADS), F32)]
    out_specs = [rows(ML_WIDTH), conv_spec] + st_specs
    return pl.pallas_call(
        kern,
        grid=(B // nb,),
        in_specs=in_specs,
        out_specs=out_specs,
        out_shape=out_shape,
        scratch_shapes=[pltpu.VMEM((nb, BF16_ROWS, ML_COLS), F32),
                        pltpu.VMEM((nb, 2 * SUBLANES, 2 * ML_WIDTH), F32)],
        compiler_params=pltpu.CompilerParams(dimension_semantics=("arbitrary",),
                                             vmem_limit_bytes=VMEM_LIMIT),
        name="mlstm",
    )(zq, cst, *state, bif, cw, cb, hng)


PL = 256
PG = PL // SUBLANES
TAIL = (CONV_W - 1) * SUBLANES
PROJ_PIECE_COLS = 512


def _mixer_ml_kernel(h_hbm, wqk_ref, wvo_ref, wif_ref, bif_ref, cw_ref, cb_ref, hng_ref,
                     bo_hbm, conv_ref, C_ref, n_ref, m_ref,
                     zqk_buf, zvo_buf, zif_buf, tail, hbuf, obuf, sem_in, sem_out,
                     *, chunks_per_seq, n_chunks):
    s = pl.program_id(0)

    def chunk_dmas(chunk, slot, inbound):
        row0 = chunk * PL
        copies = []
        for i in range(SUBLANES):
            hbm = (h_hbm if inbound else bo_hbm).at[pl.ds(row0 + PG * i, PG), :]
            vmem = (hbuf if inbound else obuf).at[slot, :, i, :]
            sem = (sem_in if inbound else sem_out).at[slot]
            copies.append(pltpu.make_async_copy(hbm, vmem, sem) if inbound
                          else pltpu.make_async_copy(vmem, hbm, sem))
        return copies

    @pl.when(s == 0)
    def _():
        zqk_buf[1] = jnp.zeros(zqk_buf.shape[1:], F32)
        zvo_buf[1] = jnp.zeros(zvo_buf.shape[1:], F32)
        zif_buf[1] = jnp.zeros(zif_buf.shape[1:], F32)
        for cp in chunk_dmas(0, 0, True):
            cp.start()

    @pl.when(jnp.maximum(s - 1, 0) % chunks_per_seq == 0)
    def _():
        tail[...] = jnp.zeros(tail.shape, F32)
        C_ref[...] = jnp.zeros(C_ref.shape, F32)
        n_ref[...] = jnp.zeros(n_ref.shape, F32)
        m_ref[...] = jnp.zeros(m_ref.shape, F32)

    def step(rd, wr):
        for cp in chunk_dmas(jnp.minimum(s + 1, n_chunks - 1), rd, True):
            cp.start()
        for cp in chunk_dmas(jnp.minimum(s, n_chunks - 1), wr, True):
            cp.wait()

        _mixer_ml_step(hbuf.at[wr], wqk_ref, wvo_ref, wif_ref, bif_ref, cw_ref, cb_ref, hng_ref,
                       obuf.at[rd], conv_ref, C_ref, n_ref, m_ref, zqk_buf, zvo_buf, zif_buf, tail,
                       rd=rd, wr=wr)

        @pl.when(s >= 1)
        def _():
            for cp in chunk_dmas(s - 1, rd, False):
                cp.start()

        @pl.when(s >= 2)
        def _():
            for cp in chunk_dmas(s - 2, wr, False):
                cp.wait()

        @pl.when(s == n_chunks)
        def _():
            for cp in chunk_dmas(n_chunks - 1, rd, True) + chunk_dmas(s - 1, rd, False):
                cp.wait()

    @pl.when(s % 2 == 0)
    def _():
        step(rd=1, wr=0)

    @pl.when(s % 2 == 1)
    def _():
        step(rd=0, wr=1)


def _mixer_ml_step(h_ref, wqk_ref, wvo_ref, wif_ref, bif_ref, cw_ref, cb_ref, hng_ref,
                   bo_ref, conv_ref, C_ref, n_ref, m_ref, zqk_buf, zvo_buf, zif_buf, tail,
                   *, rd, wr):
    h = h_ref[...].reshape(PL, D_MODEL).astype(BF16)

    def piece(w_ref, buf, c0, width):
        def emit():
            buf[wr, :, c0:c0 + width] = _bdot(h, w_ref[:, c0:c0 + width])
        return emit

    pieces = [piece(w_ref, buf, c0, PROJ_PIECE_COLS)
              for w_ref, buf in ((wqk_ref, zqk_buf), (wvo_ref, zvo_buf))
              for c0 in range(0, 2 * ML_WIDTH, PROJ_PIECE_COLS)]
    pieces.append(piece(wif_ref, zif_buf, 0, GATE_COLS))
    pieces = iter(pieces)

    def fill():
        emit = next(pieces, None)
        if emit is not None:
            emit()

    fill()
    zqk = zqk_buf.at[rd]
    zvo = zvo_buf.at[rd]
    zif = zif_buf[rd] + bif_ref[...]

    zqk_tail = zqk[PL - TAIL:, :]
    sub = lax.broadcasted_iota(jnp.int32, (SUBLANES, 2 * ML_WIDTH), 0)
    wrapped = []
    for g in range(CONV_W - 1):
        cur = pltpu.roll(zqk_tail[g * SUBLANES:(g + 1) * SUBLANES], 1, axis=0)
        prev = pltpu.roll(tail[g * SUBLANES:(g + 1) * SUBLANES, :], 1, axis=0)
        wrapped.append(jnp.where(sub == 0, prev, cur))
    wrapped = jnp.concatenate(wrapped, axis=0)
    tail[...] = zqk_tail
    conv_ref[...] = jnp.concatenate(
        [zqk_tail[g * SUBLANES + SUBLANES - 1:(g + 1) * SUBLANES, :] for g in range(CONV_W - 1)], axis=0)

    def conv_silu(c0, width):
        cs = slice(c0, c0 + width)
        acc = cb_ref[:, cs] + cw_ref[CONV_W - 1:CONV_W, cs] * zqk[:, cs]
        for d in range(1, CONV_W):
            shifted = jnp.concatenate(
                [wrapped[TAIL - d * SUBLANES:, cs], zqk[:PL - d * SUBLANES, cs]], axis=0)
            acc = acc + cw_ref[CONV_W - 1 - d:CONV_W - d, cs] * shifted
        return acc * jax.nn.sigmoid(acc)

    pr = lax.broadcasted_iota(jnp.int32, (PL, PL), 0)
    pc = lax.broadcasted_iota(jnp.int32, (PL, PL), 1)
    bits, low = SUBLANES.bit_length() - 1, SUBLANES - 1
    causal = ((pc >> bits) + PG * (pc & low)) <= ((pr >> bits) + PG * (pr & low))

    logf = pltpu.roll(_log_sigmoid(zif), GATE_COLS - ML_HEADS, axis=1)
    run, partial = None, []
    for n in range(PG):
        blk = logf[n * SUBLANES:(n + 1) * SUBLANES, :]
        run = blk if run is None else run + blk
        partial.append(run)
    sub_g = lax.broadcasted_iota(jnp.int32, (SUBLANES, GATE_COLS), 0)
    incl = run
    for step in (1, 2, 4):
        incl = incl + jnp.where(sub_g >= step, pltpu.roll(incl, step, axis=0), 0.0)
    earlier = incl - run
    bcum = jnp.concatenate([p + earlier for p in partial], axis=0)

    def head_cols(base, h):
        return zvo[:, base + h * ML_HEAD_DIM:base + (h + 1) * ML_HEAD_DIM]

    outs, C_new, n_new, m_new = _mlstm_heads(
        q_of=lambda h: conv_silu(h * ML_HEAD_DIM, ML_HEAD_DIM),
        k_of=lambda h: conv_silu(ML_WIDTH + h * ML_HEAD_DIM, ML_HEAD_DIM),
        v_of=lambda h: head_cols(0, h),
        o_of=lambda h: head_cols(ML_WIDTH, h),
        causal=causal, ipre=zif, bcum=bcum, groups=[(0, PL, PL - 1)],
        m0_of=lambda g: m_ref[...], C0_of=lambda g, h: C_ref[h], n0_of=lambda g, h: n_ref[h:h + 1, :],
        hng_ref=hng_ref, fill=fill)
    assert next(pieces, None) is None, "projection pieces left over"
    for h in range(ML_HEADS):
        bo_ref[:, :, h * ML_HEAD_DIM:(h + 1) * ML_HEAD_DIM] = outs[h].reshape(PG, SUBLANES, ML_HEAD_DIM)
        C_ref[h] = C_new[0][h]
        n_ref[h:h + 1, :] = n_new[0][h]
    m_ref[...] = m_new[0]


def _mixer_ml_call(hp, n_seq, w_all, bif, cw, cb, hng):
    B = n_seq
    T = hp.shape[0] // B
    cps = T // PL
    n_chunks = B * cps
    math = lambda s: jnp.maximum(s - 1, 0)
    per_b = lambda shape: pl.BlockSpec((None,) + shape,
                                       lambda s: (math(s) // cps,) + (0,) * len(shape))
    out_shape = [jax.ShapeDtypeStruct((B * T, ML_WIDTH), F32),
                 jax.ShapeDtypeStruct((B, CONV_W - 1, 2 * ML_WIDTH), F32),
                 jax.ShapeDtypeStruct((B, ML_HEADS, ML_HEAD_DIM, ML_HEAD_DIM), F32),
                 jax.ShapeDtypeStruct((B, ML_HEADS, ML_HEAD_DIM), F32),
                 jax.ShapeDtypeStruct((B, 1, ML_HEADS), F32)]
    out_specs = [pl.BlockSpec(memory_space=pl.ANY),
                 per_b((CONV_W - 1, 2 * ML_WIDTH)),
                 per_b((ML_HEADS, ML_HEAD_DIM, ML_HEAD_DIM)), per_b((ML_HEADS, ML_HEAD_DIM)),
                 per_b((1, ML_HEADS))]
    return pl.pallas_call(
        functools.partial(_mixer_ml_kernel, chunks_per_seq=cps, n_chunks=n_chunks),
        grid=(n_chunks + 1,),
        in_specs=[pl.BlockSpec(memory_space=pl.ANY)]
        + _ml_weight_specs()
        + [_const_spec((1, GATE_COLS)), _const_spec((CONV_W, 2 * ML_WIDTH)),
           _const_spec((1, 2 * ML_WIDTH)), _const_spec((1, ML_WIDTH))],
        out_specs=out_specs,
        out_shape=out_shape,
        scratch_shapes=[pltpu.VMEM((2, PL, 2 * ML_WIDTH), F32),
                        pltpu.VMEM((2, PL, 2 * ML_WIDTH), F32),
                        pltpu.VMEM((2, PL, GATE_COLS), F32),
                        pltpu.VMEM((TAIL, 2 * ML_WIDTH), F32),
                        pltpu.VMEM((2, PG, SUBLANES, D_MODEL), F32),
                        pltpu.VMEM((2, PG, SUBLANES, ML_WIDTH), F32),
                        pltpu.SemaphoreType.DMA((2,)),
                        pltpu.SemaphoreType.DMA((2,))],
        compiler_params=pltpu.CompilerParams(dimension_semantics=("arbitrary",),
                                             vmem_limit_bytes=VMEM_LIMIT),
        name="mixer_ml",
    )(hp, w_all, w_all, w_all, bif, cw, cb, hng)


def _merge_ffn_kernel(x_ref, pa_ref, bo_ref, g1_ref, wg_ref, bg_ref, wpb_ref, wout_ref,
                      g2_ref, wfi_ref, wfo_ref, gf_ref, y_ref):
    tm = _n_tokens(x_ref)
    subs = [slice(r0, r0 + FFN_SUB_ROWS) for r0 in range(0, tm, FFN_SUB_ROWS)]
    x = [_load_rows(x_ref, rs) for rs in subs]
    h = [_rms(xi, g1_ref[...]).astype(BF16) for xi in x]
    gab = [_bdot(hi, wg_ref[...]) + bg_ref[...] for hi in h]
    pb = [_bdot(bo_ref[rs, :].astype(BF16), wpb_ref[...]) for rs in subs]
    merged = [(jax.nn.sigmoid(g[:, :D_MODEL]) * pa_ref[rs, :]
               + jax.nn.sigmoid(g[:, D_MODEL:]) * p).astype(BF16) for g, p, rs in zip(gab, pb, subs)]
    x1 = [xi + _bdot(mi, wout_ref[...]) for xi, mi in zip(x, merged)]
    h2 = [_rms(xi, g2_ref[...]).astype(BF16) for xi in x1]
    gu = [_bdot(hi, wfi_ref[...]) for hi in h2]
    hid = [(g[:, :D_FF] * jax.nn.sigmoid(g[:, :D_FF]) * g[:, D_FF:]).astype(BF16) for g in gu]
    x2 = [xi + _bdot(hi, wfo_ref[...]) for xi, hi in zip(x1, hid)]
    for rs, xi in zip(subs, x2):
        _store_rows(y_ref, rs, _rms(xi, gf_ref[...]))


def _merge_ffn_call(x, pa, bo, g1, wg, bg, wpb, wout, g2, wfi, wfo, gf, *, tm):
    m = _n_tokens(x)
    row = pl.BlockSpec((tm, D_MODEL), lambda i: (i, 0))
    x_spec = _token_spec(x.shape, tm)
    return pl.pallas_call(
        _merge_ffn_kernel,
        grid=(m // tm,),
        in_specs=[x_spec, row, row, _const_spec((1, D_MODEL)), _const_spec((D_MODEL, 2 * D_MODEL)),
                  _const_spec((1, 2 * D_MODEL)), _const_spec((ML_WIDTH, D_MODEL)),
                  _const_spec((D_MODEL, D_MODEL)), _const_spec((1, D_MODEL)),
                  _const_spec((D_MODEL, 2 * D_FF)), _const_spec((D_FF, D_MODEL)),
                  _const_spec((1, D_MODEL))],
        out_specs=x_spec,
        out_shape=jax.ShapeDtypeStruct(x.shape, F32),
        compiler_params=pltpu.CompilerParams(dimension_semantics=("arbitrary",),
                                             vmem_limit_bytes=VMEM_LIMIT),
        name="merge_ffn",
    )(x, pa, bo, g1, wg, bg, wpb, wout, g2, wfi, wfo, gf)


SAMPLE_SEQS_PER_STEP = 8
TM_MIX = 512
TM_GMLP = 512
TM_FFN = 512


def _spatial_tiles(w_s, b_s, chunk):
    if chunk == GM_CHUNK:
        ws_t, b_pos = w_s[:, :chunk, :chunk], b_s[:, :chunk].T
    else:
        onehot = jnp.asarray(np.arange(GM_CHUNK)[:, None] % chunk == np.arange(chunk)[None, :], F32)
        hp = lax.Precision.HIGHEST
        ws_t = jnp.einsum("ri,gij,cj->grc", onehot, w_s[:, :chunk, :chunk], onehot, precision=hp)
        b_pos = jnp.dot(onehot, b_s[:, :chunk].T, precision=hp)
    bs_t = jnp.repeat(b_pos, GM_GROUP_W, axis=1)
    return ws_t, bs_t


def _gmlp_branch(xf, w, chunk, *, emit_v, emit_h, **side_jobs):
    ws_t, bs_t = _spatial_tiles(w["w_s"], w["b_s"], chunk)
    return _gmlp_call(xf, w["g1"], w["wuv"], w["lng"], w["lnb"], ws_t, bs_t, w["wpa"],
                      chunk=chunk, emit_v=emit_v, emit_h=emit_h,
                      tm=min(TM_GMLP, _n_tokens(xf)), **side_jobs)


def _merge_branch(xf, pa, bo, w):
    return _merge_ffn_call(xf, pa, bo, w["g1"], w["wg"], w["bg"], w["wpb"], w["wout"], w["g2"],
                           w["wfi"], w["wfo"], w["gf"], tm=TM_FFN)


def kernel(x_prompt, x_sample, state_conv, state_C, state_n, state_m, g_norm1, w_in, b_i, b_f, ln_g, ln_b, w_s, b_s, conv_w, conv_b, hn_g, b_gate, w_proj_a, w_proj_b, w_out, g_norm2, w_ffn_in, w_ffn_out, g_final):
    Bp, Tp, _ = x_prompt.shape
    Bs, Ts, _ = x_sample.shape
    win = w_in[0]
    c_ml = 2 * GM_WIDTH
    c_gate = c_ml + 4 * ML_WIDTH + 2 * ML_HEADS
    n_ml_blocks = -(-(ML_COLS) // TCAST_ROWS)
    win_t = jnp.transpose(win)
    w = dict(
        g1=g_norm1[0][None], g2=g_norm2[0][None], gf=g_final[None],
        wuv=_tcast_call(win_t, 0, c_ml // TCAST_ROWS),
        bg=b_gate[0].reshape(1, 2 * D_MODEL),
        lng=ln_g[0][None], lnb=ln_b[0][None], w_s=w_s[0], b_s=b_s[0],
        bif=jnp.pad(jnp.concatenate([b_i[0], b_f[0]]), (0, GATE_COLS - 2 * ML_HEADS))[None],
        cw=conv_w[0], cb=conv_b[0][None], hng=hn_g[0][None],
        wpa=w_proj_a[0].astype(BF16),
    )

    xpf = x_prompt.reshape(Bp * Tp, D_MODEL)
    pa_p, hp_p, w["wpb"], w["wout"], w["wfi"], w["wfo"], w["w_ml"], w["wg"] = _gmlp_branch(
        xpf, w, GM_CHUNK, emit_v=False, emit_h=True,
        cast_weights=(w_proj_b[0], w_out[0], w_ffn_in[0], w_ffn_out[0]),
        tcast_weight=win_t,
        tcast_jobs=((c_ml, n_ml_blocks, 0), (c_gate, 2 * D_MODEL // TCAST_ROWS, n_ml_blocks)))
    bo_p, conv_p, C_p, n_p, m_p = _mixer_ml_call(hp_p, Bp, w["w_ml"], w["bif"], w["cw"], w["cb"],
                                                 w["hng"])
    y_p = _merge_branch(xpf, pa_p, bo_p, w)

    pa_s, vn_s = _gmlp_branch(x_sample, w, Ts, emit_v=True, emit_h=False)
    zq = _inproj_call(x_sample, w["g1"], w["w_ml"], tm=TM_MIX)
    st = (state_C[0], state_n[0], state_m[0])
    bo_s, conv_s, C_s, n_s, m_s = _mlstm_call(zq, jnp.transpose(state_conv[0], (1, 0, 2)), st, w["bif"],
                                              w["cw"], w["cb"], w["hng"], t_valid=Ts,
                                              nb=SAMPLE_SEQS_PER_STEP)
    y_s = _merge_branch(x_sample, pa_s, bo_s, w)

    return (y_p.reshape(Bp, Tp, D_MODEL), y_s,
            conv_p[None], C_p[None], n_p[None], m_p.reshape(1, Bp, ML_HEADS),
            jnp.transpose(conv_s, (1, 0, 2))[None], C_s[None], n_s[None], m_s[None],
            vn_s[None])
```

```python
import functools
import math

import jax
import jax.numpy as jnp
import numpy as np
from jax import lax
from jax.experimental import pallas as pl
from jax.experimental.pallas import tpu as pltpu

D_MODEL = 1024
GM_WIDTH = D_MODEL
GM_GROUPS = 4
GM_GROUP_W = GM_WIDTH // GM_GROUPS
GM_CHUNK = 128
ML_HEADS = 4
ML_HEAD_DIM = D_MODEL // ML_HEADS
ML_WIDTH = ML_HEADS * ML_HEAD_DIM
CONV_W = 4
D_FF = 2816
EPS = 1e-6

LANES = 128
SUBLANES = 8
BF16_ROWS = 16
GATE_COLS = LANES
ML_COLS = 4 * ML_WIDTH + GATE_COLS
VMEM_LIMIT = 56 * 1024 * 1024
GMLP_SUB_ROWS = 256
FFN_SUB_ROWS = 256

F32 = jnp.float32
BF16 = jnp.bfloat16
NEG_INF = float("-inf")
LN_INV_K_SCALE = 0.5 * math.log(ML_HEAD_DIM)
NT_DIMS = (((1,), (1,)), ((), ()))
TN_DIMS = (((0,), (0,)), ((), ()))


def _rms(x, g):
    return x * lax.rsqrt(jnp.mean(x * x, axis=-1, keepdims=True) + EPS) * g


def _gelu(x):
    return 0.5 * x * (1.0 + lax.erf(x * (2.0 ** -0.5)))


def _log_sigmoid(x):
    return jnp.minimum(x, 0.0) - jnp.log1p(jnp.exp(-jnp.abs(x)))


def _bdot(a, b):
    return jnp.dot(a, b, preferred_element_type=F32)


def _n_tokens(a):
    return a.shape[0] if len(a.shape) == 2 else a.shape[0] * a.shape[1]


def _load_rows(ref, rs):
    if len(ref.shape) == 2:
        return ref[rs, :]
    t = ref.shape[1]
    return ref[rs.start // t:rs.stop // t].reshape(rs.stop - rs.start, ref.shape[2])


def _store_rows(ref, rs, val):
    if len(ref.shape) == 2:
        ref[rs, :] = val
    else:
        t = ref.shape[1]
        ref[rs.start // t:rs.stop // t] = val.reshape((rs.stop - rs.start) // t, t, ref.shape[2])


def _token_spec(shape, tm):
    if len(shape) == 2:
        return pl.BlockSpec((tm, shape[1]), lambda i: (i, 0))
    return pl.BlockSpec((tm // shape[1], shape[1], shape[2]), lambda i: (i, 0, 0))


def _const_spec(shape):
    nd = len(shape)
    return pl.BlockSpec(shape, lambda *_: (0,) * nd, pipeline_mode=pl.Buffered(1))


def _gmlp_kernel(x_ref, g1_ref, wuv_ref, lng_ref, lnb_ref, ws_ref, bs_ref, wpa_ref, *rest,
                 chunk, emit_v, emit_h, emit_zq, n_cast, n_tcast):
    rest = list(rest)
    ml_weights = [rest.pop(0) for _ in range(3)] if emit_zq else None
    cast_src = [rest.pop(0) for _ in range(n_cast)]
    tcast_src = [rest.pop(0) for _ in range(n_tcast)]
    pa_ref = rest.pop(0)
    vn_ref = rest.pop(0) if emit_v else None
    hp_ref = rest.pop(0) if emit_h else None
    zq_ref = rest.pop(0) if emit_zq else None
    for src in cast_src:
        rest.pop(0)[...] = src[...].astype(BF16)
    for src in tcast_src:
        rest.pop(0)[...] = src[...].T.astype(BF16)
    a_sc = rest.pop(0)
    tm = _n_tokens(x_ref)
    blk = ws_ref.shape[1]
    sub = GMLP_SUB_ROWS
    subs = [slice(r0, r0 + sub) for r0 in range(0, tm, sub)]
    hb = []
    for rs in subs:
        hf = _rms(_load_rows(x_ref, rs), g1_ref[...])
        hb.append(hf.astype(BF16))
        if emit_h:
            hp_ref[rs, :] = hf
    zu = [_bdot(h, wuv_ref[:, :GM_WIDTH]) for h in hb]
    zv = [_bdot(h, wuv_ref[:, GM_WIDTH:]) for h in hb]
    if emit_zq:
        for h, rs in zip(hb, subs):
            for w_ref, c0 in zip(ml_weights, (0, 2 * ML_WIDTH, 4 * ML_WIDTH)):
                zq_ref[rs, c0:c0 + w_ref.shape[1]] = _bdot(h, w_ref[...])
    r = lax.broadcasted_iota(jnp.int32, (blk, blk), 0)
    c = lax.broadcasted_iota(jnp.int32, (blk, blk), 1)
    keep = c <= r
    if chunk < blk:
        sh = chunk.bit_length() - 1
        keep = jnp.logical_and(keep, (r >> sh) == (c >> sh))
    wsm = [jnp.where(keep, ws_ref[g], 0.0).astype(BF16) for g in range(GM_GROUPS)]
    for si, rs in enumerate(subs):
        u = _gelu(zu[si])
        v = _gelu(zv[si])
        mu = jnp.mean(v, axis=-1, keepdims=True)
        vc = v - mu
        var = jnp.mean(vc * vc, axis=-1, keepdims=True)
        vn = vc * lax.rsqrt(var + EPS) * lng_ref[...] + lnb_ref[...]
        if emit_v:
            _store_rows(vn_ref, rs, vn)
        vb = vn.astype(BF16)
        for g in range(GM_GROUPS):
            cs = slice(g * GM_GROUP_W, (g + 1) * GM_GROUP_W)
            for i in range(sub // blk):
                ls = slice(i * blk, (i + 1) * blk)
                s = _bdot(wsm[g], vb[ls, cs]) + bs_ref[:, cs]
                a_sc[rs.start + i * blk:rs.start + (i + 1) * blk, cs] = (u[ls, cs] * s).astype(BF16)
        pa_ref[rs, :] = _bdot(a_sc[rs, :], wpa_ref[...])


def _cast_block_spec(n_rows, n_cols, steps):
    n_blocks = steps
    while n_rows % n_blocks or (n_rows // n_blocks) % BF16_ROWS:
        n_blocks //= 2
    per = steps // n_blocks
    return pl.BlockSpec((n_rows // n_blocks, n_cols), lambda i: (i // per, 0))


TCAST_ROWS = 256


def _tcast_specs(job):
    first_row, n_blocks, first_step = job
    assert first_row % SUBLANES == 0
    blk = lambda i: jnp.clip(i - first_step, 0, n_blocks - 1)
    src = pl.BlockSpec((pl.Element(TCAST_ROWS), pl.Element(D_MODEL)),
                       lambda i: (pl.multiple_of(first_row + TCAST_ROWS * blk(i), SUBLANES), 0))
    dst = pl.BlockSpec((D_MODEL, TCAST_ROWS), lambda i: (0, blk(i)))
    return src, dst, jax.ShapeDtypeStruct((D_MODEL, TCAST_ROWS * n_blocks), BF16)


def _tcast_kernel(src_ref, dst_ref):
    dst_ref[...] = src_ref[...].T.astype(BF16)


def _tcast_call(wt, first_row, n_blocks):
    src, dst, shape = _tcast_specs((first_row, n_blocks, 0))
    return pl.pallas_call(
        _tcast_kernel, grid=(n_blocks,), in_specs=[src], out_specs=dst, out_shape=shape,
        compiler_params=pltpu.CompilerParams(dimension_semantics=("arbitrary",)),
        name="tcast",
    )(wt)


def _gmlp_call(x, g1, wuv, lng, lnb, ws_t, bs_t, wpa, *, chunk, emit_v, emit_h, tm, w_ml=None,
               cast_weights=(), tcast_weight=None, tcast_jobs=()):
    m = _n_tokens(x)
    blk = ws_t.shape[1]
    steps = m // tm
    assert steps & (steps - 1) == 0
    assert all(first_step + n_blocks <= steps for _, n_blocks, first_step in tcast_jobs)
    row = pl.BlockSpec((tm, D_MODEL), lambda i: (i, 0))
    out_shape = [jax.ShapeDtypeStruct((m, D_MODEL), F32)]
    out_specs = [row]
    scratch = [pltpu.VMEM((tm, GM_WIDTH), BF16)]
    cast_specs = [_cast_block_spec(cw.shape[0], cw.shape[1], steps) for cw in cast_weights]
    tcast = [_tcast_specs(job) for job in tcast_jobs]
    if emit_v:
        out_shape.append(jax.ShapeDtypeStruct(x.shape, F32))
        out_specs.append(_token_spec(x.shape, tm))
    if emit_h:
        out_shape.append(jax.ShapeDtypeStruct((m, D_MODEL), F32))
        out_specs.append(row)
    emit_zq = w_ml is not None
    if emit_zq:
        out_shape.append(jax.ShapeDtypeStruct((m, ML_COLS), F32))
        out_specs.append(pl.BlockSpec((tm, ML_COLS), lambda i: (i, 0)))
    out_shape += [jax.ShapeDtypeStruct(cw.shape, BF16) for cw in cast_weights]
    out_shape += [t[2] for t in tcast]
    out_specs += cast_specs + [t[1] for t in tcast]
    return pl.pallas_call(
        functools.partial(_gmlp_kernel, chunk=chunk, emit_v=emit_v, emit_h=emit_h, emit_zq=emit_zq,
                          n_cast=len(cast_weights), n_tcast=len(tcast)),
        grid=(steps,),
        in_specs=[_token_spec(x.shape, tm), _const_spec((1, D_MODEL)),
                  _const_spec((D_MODEL, 2 * GM_WIDTH)),
                  _const_spec((1, GM_WIDTH)), _const_spec((1, GM_WIDTH)),
                  _const_spec((GM_GROUPS, blk, blk)), _const_spec((blk, GM_WIDTH)),
                  _const_spec((GM_WIDTH, D_MODEL))]
        + (_ml_weight_specs() if emit_zq else []) + cast_specs + [t[0] for t in tcast],
        out_specs=out_specs,
        out_shape=out_shape,
        scratch_shapes=scratch,
        compiler_params=pltpu.CompilerParams(dimension_semantics=("arbitrary",),
                                             vmem_limit_bytes=VMEM_LIMIT),
        name="gmlp",
    )(x, g1, wuv, lng, lnb, ws_t, bs_t, wpa, *([w_ml] * 3 if emit_zq else []),
      *cast_weights, *([tcast_weight] * len(tcast)))


def _ml_weight_specs():
    wide = 2 * ML_WIDTH
    col_block = lambda width, idx: pl.BlockSpec((D_MODEL, width), lambda *_: (0, idx),
                                                pipeline_mode=pl.Buffered(1))
    return [col_block(wide, 0), col_block(wide, 1), col_block(GATE_COLS, 4 * ML_WIDTH // GATE_COLS)]


def _mlstm_heads(q_of, k_of, v_of, o_of, causal, ipre, bcum, groups, m0_of, C0_of, n0_of, hng_ref,
                 fill=None):
    n_groups = len(groups)
    single = n_groups == 1
    fill = fill or (lambda: None)

    def rows_of(x, g):
        return x if single else x[groups[g][0]:groups[g][0] + groups[g][1]]

    def per_row(vals):
        if single:
            return vals[0]
        return jnp.concatenate([jnp.broadcast_to(v, (groups[g][1], v.shape[1]))
                                for g, v in enumerate(vals)], axis=0)

    a = ipre - bcum
    a_t = a.T
    m_rows = per_row([m0_of(g) for g in range(n_groups)])

    def prepare(h):
        p = {}
        a2 = jnp.where(causal, a_t[h:h + 1, :], -jnp.inf)
        p["mc"] = mc = jnp.maximum(jnp.max(a2, axis=1, keepdims=True), m_rows[:, h:h + 1])
        p["m_last"] = m_last = [mc[grp[2]:grp[2] + 1, :] for grp in groups]
        p["dm"] = jnp.exp(a2 - (mc + LN_INV_K_SCALE))
        p["w_inter"] = jnp.exp(m_rows[:, h:h + 1] - mc)
        p["w_col"] = w_col = jnp.exp(a[:, h:h + 1] - (per_row(m_last) + LN_INV_K_SCALE))
        p["decay"] = [jnp.exp(m0_of(g)[:, h:h + 1] - m_last[g]) for g in range(n_groups)]
        p["q"] = q = q_of(h)
        p["k"] = k = k_of(h)
        v = v_of(h)
        p["qb"], p["kb"], p["vb"] = q.astype(BF16), k.astype(BF16), v.astype(BF16)
        p["vw"] = (v * w_col).astype(BF16)
        p["c_old"] = [C0_of(g, h) for g in range(n_groups)]
        p["n_old"] = [n0_of(g, h) for g in range(n_groups)]
        return p

    def first_matmuls(p):
        p["qk"] = lax.dot_general(p["qb"], p["kb"], NT_DIMS, preferred_element_type=F32)
        p["qc"] = [lax.dot_general(rows_of(p["qb"], g), p["c_old"][g].astype(BF16), NT_DIMS,
                                   preferred_element_type=F32) for g in range(n_groups)]
        if single:
            n_rows = jnp.broadcast_to(p["n_old"][0], (LANES, ML_HEAD_DIM)).astype(BF16)
            p["qn"] = lax.dot_general(p["qb"], n_rows, NT_DIMS, preferred_element_type=F32)[:, 0:1]
        else:
            p["qn"] = jnp.sum(p["q"] * per_row(p["n_old"]), axis=1, keepdims=True)

    def second_matmuls(p):
        p["s"] = s = p["dm"] * p["qk"]
        p["sv"] = _bdot(s.astype(BF16), p["vb"])
        p["cupd"] = [lax.dot_general(rows_of(p["vw"], g), rows_of(p["kb"], g), TN_DIMS,
                                     preferred_element_type=F32) for g in range(n_groups)]

    def finish(h, p):
        qc_rows = p["qc"][0] if single else jnp.concatenate(p["qc"], axis=0)
        num = p["w_inter"] * qc_rows + p["sv"]
        den = p["w_inter"] * p["qn"] + jnp.sum(p["s"], axis=1, keepdims=True)
        hcur = num / jnp.maximum(jnp.abs(den), jnp.exp(-(bcum[:, h:h + 1] + p["mc"])))
        mu = jnp.mean(hcur, axis=1, keepdims=True)
        hc = hcur - mu
        var = jnp.mean(hc * hc, axis=1, keepdims=True)
        hs = slice(h * ML_HEAD_DIM, (h + 1) * ML_HEAD_DIM)
        out = jax.nn.sigmoid(o_of(h)) * (hc * lax.rsqrt(var + EPS) * hng_ref[:, hs])
        kw = p["k"] * p["w_col"]
        c_new = [p["decay"][g] * p["c_old"][g] + p["cupd"][g] for g in range(n_groups)]
        n_new = [p["decay"][g] * p["n_old"][g] + jnp.sum(rows_of(kw, g), axis=0, keepdims=True)
                 for g in range(n_groups)]
        return out, c_new, n_new

    H = ML_HEADS
    per_head = [None] * H
    done = [None] * H
    if single:
        per_head[0] = prepare(0)
        fill()
        first_matmuls(per_head[0])
        if H > 1:
            per_head[1] = prepare(1)
        fill()
        for h in range(H):
            second_matmuls(per_head[h])
            if h + 1 < H:
                first_matmuls(per_head[h + 1])
            fill()
            if h + 2 < H:
                per_head[h + 2] = prepare(h + 2)
            done[h] = finish(h, per_head[h])
            fill()
    else:
        per_head = [prepare(h) for h in range(H)]
        for stage in (first_matmuls, second_matmuls):
            for h in range(H):
                stage(per_head[h])
        done = [finish(h, per_head[h]) for h in range(H)]

    outs = [done[h][0] for h in range(H)]
    C_new = [[done[h][1][g] for h in range(H)] for g in range(n_groups)]
    n_new = [[done[h][2][g] for h in range(H)] for g in range(n_groups)]
    m_new = []
    for g, grp in enumerate(groups):
        row = m0_of(g)
        lane = lax.broadcasted_iota(jnp.int32, row.shape, 1)
        for h in range(H):
            row = jnp.where(lane == h, bcum[grp[2]:grp[2] + 1, h:h + 1] + per_head[h]["m_last"][g], row)
        m_new.append(row)
    return outs, C_new, n_new, m_new


def _mlstm_kernel(zq_ref, cst_ref, C0_ref, n0_ref, m0_ref, bif_ref, cw_ref, cb_ref, hng_ref,
                  bo_ref, conv_ref, C_ref, n_ref, m_ref, zp, xp, *, t_valid):
    nb = C0_ref.shape[0]
    L = BF16_ROWS
    R = nb * L

    @pl.when(pl.program_id(0) == 0)
    def _():
        zp[...] = jnp.zeros(zp.shape, F32)

    for bb in range(nb):
        zp[bb, 0:t_valid, :] = zq_ref[bb * t_valid:(bb + 1) * t_valid, :]
        for j in range(CONV_W - 1):
            row = SUBLANES - (CONV_W - 1) + j
            xp[bb, row:row + 1, :] = cst_ref[j, bb:bb + 1, :]
        xp[bb, SUBLANES:2 * SUBLANES, :] = zp[bb, 0:SUBLANES, 0:2 * ML_WIDTH]
        for j in range(CONV_W - 1):
            row = SUBLANES + t_valid - (CONV_W - 1) + j
            conv_ref[j, bb:bb + 1, :] = xp[bb, row:row + 1, :]
    qk_rows = []
    for bb in range(nb):
        acc = cb_ref[...]
        for j in range(CONV_W):
            off = SUBLANES - (CONV_W - 1) + j
            acc = acc + cw_ref[j:j + 1, :] * xp[bb, off:off + SUBLANES, :]
        qk_rows += [acc, jnp.zeros((L - SUBLANES, 2 * ML_WIDTH), F32)]
    qk = jnp.concatenate(qk_rows, axis=0)
    qk = qk * jax.nn.sigmoid(qk)

    def cols(c0, width):
        return zp[:, :, c0:c0 + width].reshape(R, width)

    zif = cols(4 * ML_WIDTH, GATE_COLS) + bif_ref[...]
    live = (lax.broadcasted_iota(jnp.int32, (R, GATE_COLS), 0) & (L - 1)) < t_valid
    ipre = jnp.where(live, zif, NEG_INF)
    logf = jnp.where(live, pltpu.roll(_log_sigmoid(zif), GATE_COLS - ML_HEADS, axis=1), 0.0)
    assert t_valid <= SUBLANES
    sub_g = lax.broadcasted_iota(jnp.int32, (SUBLANES, GATE_COLS), 0)
    parts = []
    for bb in range(nb):
        incl = logf[bb * L:bb * L + SUBLANES, :]
        for step in (1, 2, 4):
            incl = incl + jnp.where(sub_g >= step, pltpu.roll(incl, step, axis=0), 0.0)
        parts += [incl, jnp.broadcast_to(incl[SUBLANES - 1:SUBLANES, :], (L - SUBLANES, GATE_COLS))]
    bcum = jnp.concatenate(parts, axis=0)
    r = lax.broadcasted_iota(jnp.int32, (R, R), 0)
    c = lax.broadcasted_iota(jnp.int32, (R, R), 1)
    sh = L.bit_length() - 1
    causal = jnp.logical_and(c <= r, (r >> sh) == (c >> sh))
    live_w = (lax.broadcasted_iota(jnp.int32, (R, ML_HEAD_DIM), 0) & (L - 1)) < t_valid

    def head_cols(x, base, h):
        return x[:, base + h * ML_HEAD_DIM:base + (h + 1) * ML_HEAD_DIM]

    groups = [(bb * L, L, bb * L + L - 1) for bb in range(nb)]
    outs, C_new, n_new, m_new = _mlstm_heads(
        q_of=lambda h: head_cols(qk, 0, h),
        k_of=lambda h: jnp.where(live_w, head_cols(qk, ML_WIDTH, h), 0.0),
        v_of=lambda h: jnp.where(live_w, cols(2 * ML_WIDTH + h * ML_HEAD_DIM, ML_HEAD_DIM), 0.0),
        o_of=lambda h: cols(3 * ML_WIDTH + h * ML_HEAD_DIM, ML_HEAD_DIM),
        causal=causal, ipre=ipre, bcum=bcum, groups=groups,
        m0_of=lambda g: m0_ref[g:g + 1, :], C0_of=lambda g, h: C0_ref[g, h],
        n0_of=lambda g, h: n0_ref[g, h:h + 1, :], hng_ref=hng_ref)
    for bb in range(nb):
        for h in range(ML_HEADS):
            hs = slice(h * ML_HEAD_DIM, (h + 1) * ML_HEAD_DIM)
            bo_ref[bb * t_valid:(bb + 1) * t_valid, hs] = outs[h][bb * L:bb * L + t_valid, :]
            C_ref[bb, h] = C_new[bb][h]
            n_ref[bb, h:h + 1, :] = n_new[bb][h]
        m_ref[bb:bb + 1, :] = m_new[bb]


def _mlstm_call(zq, cst, state, bif, cw, cb, hng, *, t_valid, nb):
    T = t_valid
    B = zq.shape[0] // T
    kern = functools.partial(_mlstm_kernel, t_valid=T)
    per_b = lambda shape: pl.BlockSpec((nb,) + shape, lambda b: (b,) + (0,) * len(shape))
    rows = lambda width: pl.BlockSpec((nb * T, width), lambda b: (b, 0))
    conv_spec = pl.BlockSpec((CONV_W - 1, nb, 2 * ML_WIDTH), lambda b: (0, b, 0))
    st_specs = [per_b((ML_HEADS, ML_HEAD_DIM, ML_HEAD_DIM)), per_b((ML_HEADS, ML_HEAD_DIM)),
                per_b((ML_HEADS,))]
    in_specs = ([rows(ML_COLS), conv_spec] + st_specs
                + [_const_spec((1, GATE_COLS)), _const_spec((CONV_W, 2 * ML_WIDTH)),
                   _const_spec((1, 2 * ML_WIDTH)), _const_spec((1, ML_WIDTH))])
    out_shape = [jax.ShapeDtypeStruct((B * T, ML_WIDTH), F32),
                 jax.ShapeDtypeStruct((CONV_W - 1, B, 2 * ML_WIDTH), F32),
                 jax.ShapeDtypeStruct((B, ML_HEADS, ML_HEAD_DIM, ML_HEAD_DIM), F32),
                 jax.ShapeDtypeStruct((B, ML_HEADS, ML_HEAD_DIM), F32),
                 jax.ShapeDtypeStruct((B, ML_HEADS), F32)]
    out_specs = [rows(ML_WIDTH), conv_spec] + st_specs
    return pl.pallas_call(
        kern,
        grid=(B // nb,),
        in_specs=in_specs,
        out_specs=out_specs,
        out_shape=out_shape,
        scratch_shapes=[pltpu.VMEM((nb, BF16_ROWS, ML_COLS), F32),
                        pltpu.VMEM((nb, 2 * SUBLANES, 2 * ML_WIDTH), F32)],
        compiler_params=pltpu.CompilerParams(dimension_semantics=("arbitrary",),
                                             vmem_limit_bytes=VMEM_LIMIT),
        name="mlstm",
    )(zq, cst, *state, bif, cw, cb, hng)


PL = 256
PG = PL // SUBLANES
TAIL = (CONV_W - 1) * SUBLANES
PROJ_PIECE_COLS = 512


def _mixer_ml_kernel(h_hbm, wqk_ref, wvo_ref, wif_ref, bif_ref, cw_ref, cb_ref, hng_ref,
                     bo_hbm, conv_ref, C_ref, n_ref, m_ref,
                     zqk_buf, zvo_buf, zif_buf, tail, hbuf, obuf, sem_in, sem_out,
                     *, chunks_per_seq, n_chunks):
    s = pl.program_id(0)

    def chunk_dmas(chunk, slot, inbound):
        row0 = chunk * PL
        copies = []
        for i in range(SUBLANES):
            hbm = (h_hbm if inbound else bo_hbm).at[pl.ds(row0 + PG * i, PG), :]
            vmem = (hbuf if inbound else obuf).at[slot, :, i, :]
            sem = (sem_in if inbound else sem_out).at[slot]
            copies.append(pltpu.make_async_copy(hbm, vmem, sem) if inbound
                          else pltpu.make_async_copy(vmem, hbm, sem))
        return copies

    @pl.when(s == 0)
    def _():
        zqk_buf[1] = jnp.zeros(zqk_buf.shape[1:], F32)
        zvo_buf[1] = jnp.zeros(zvo_buf.shape[1:], F32)
        zif_buf[1] = jnp.zeros(zif_buf.shape[1:], F32)
        for cp in chunk_dmas(0, 0, True):
            cp.start()

    @pl.when(jnp.maximum(s - 1, 0) % chunks_per_seq == 0)
    def _():
        tail[...] = jnp.zeros(tail.shape, F32)
        C_ref[...] = jnp.zeros(C_ref.shape, F32)
        n_ref[...] = jnp.zeros(n_ref.shape, F32)
        m_ref[...] = jnp.zeros(m_ref.shape, F32)

    def step(rd, wr):
        for cp in chunk_dmas(jnp.minimum(s + 1, n_chunks - 1), rd, True):
            cp.start()
        for cp in chunk_dmas(jnp.minimum(s, n_chunks - 1), wr, True):
            cp.wait()

        _mixer_ml_step(hbuf.at[wr], wqk_ref, wvo_ref, wif_ref, bif_ref, cw_ref, cb_ref, hng_ref,
                       obuf.at[rd], conv_ref, C_ref, n_ref, m_ref, zqk_buf, zvo_buf, zif_buf, tail,
                       rd=rd, wr=wr)

        @pl.when(s >= 1)
        def _():
            for cp in chunk_dmas(s - 1, rd, False):
                cp.start()

        @pl.when(s >= 2)
        def _():
            for cp in chunk_dmas(s - 2, wr, False):
                cp.wait()

        @pl.when(s == n_chunks)
        def _():
            for cp in chunk_dmas(n_chunks - 1, rd, True) + chunk_dmas(s - 1, rd, False):
                cp.wait()

    @pl.when(s % 2 == 0)
    def _():
        step(rd=1, wr=0)

    @pl.when(s % 2 == 1)
    def _():
        step(rd=0, wr=1)


def _mixer_ml_step(h_ref, wqk_ref, wvo_ref, wif_ref, bif_ref, cw_ref, cb_ref, hng_ref,
                   bo_ref, conv_ref, C_ref, n_ref, m_ref, zqk_buf, zvo_buf, zif_buf, tail,
                   *, rd, wr):
    h = h_ref[...].reshape(PL, D_MODEL).astype(BF16)

    def piece(w_ref, buf, c0, width):
        def emit():
            buf[wr, :, c0:c0 + width] = _bdot(h, w_ref[:, c0:c0 + width])
        return emit

    pieces = [piece(w_ref, buf, c0, PROJ_PIECE_COLS)
              for w_ref, buf in ((wqk_ref, zqk_buf), (wvo_ref, zvo_buf))
              for c0 in range(0, 2 * ML_WIDTH, PROJ_PIECE_COLS)]
    pieces.append(piece(wif_ref, zif_buf, 0, GATE_COLS))
    pieces = iter(pieces)

    def fill():
        emit = next(pieces, None)
        if emit is not None:
            emit()

    fill()
    zqk = zqk_buf.at[rd]
    zvo = zvo_buf.at[rd]
    zif = zif_buf[rd] + bif_ref[...]

    zqk_tail = zqk[PL - TAIL:, :]
    sub = lax.broadcasted_iota(jnp.int32, (SUBLANES, 2 * ML_WIDTH), 0)
    wrapped = []
    for g in range(CONV_W - 1):
        cur = pltpu.roll(zqk_tail[g * SUBLANES:(g + 1) * SUBLANES], 1, axis=0)
        prev = pltpu.roll(tail[g * SUBLANES:(g + 1) * SUBLANES, :], 1, axis=0)
        wrapped.append(jnp.where(sub == 0, prev, cur))
    wrapped = jnp.concatenate(wrapped, axis=0)
    tail[...] = zqk_tail
    conv_ref[...] = jnp.concatenate(
        [zqk_tail[g * SUBLANES + SUBLANES - 1:(g + 1) * SUBLANES, :] for g in range(CONV_W - 1)], axis=0)

    def conv_silu(c0, width):
        cs = slice(c0, c0 + width)
        acc = cb_ref[:, cs] + cw_ref[CONV_W - 1:CONV_W, cs] * zqk[:, cs]
        for d in range(1, CONV_W):
            shifted = jnp.concatenate(
                [wrapped[TAIL - d * SUBLANES:, cs], zqk[:PL - d * SUBLANES, cs]], axis=0)
            acc = acc + cw_ref[CONV_W - 1 - d:CONV_W - d, cs] * shifted
        return acc * jax.nn.sigmoid(acc)

    pr = lax.broadcasted_iota(jnp.int32, (PL, PL), 0)
    pc = lax.broadcasted_iota(jnp.int32, (PL, PL), 1)
    bits, low = SUBLANES.bit_length() - 1, SUBLANES - 1
    causal = ((pc >> bits) + PG * (pc & low)) <= ((pr >> bits) + PG * (pr & low))

    logf = pltpu.roll(_log_sigmoid(zif), GATE_COLS - ML_HEADS, axis=1)
    run, partial = None, []
    for n in range(PG):
        blk = logf[n * SUBLANES:(n + 1) * SUBLANES, :]
        run = blk if run is None else run + blk
        partial.append(run)
    sub_g = lax.broadcasted_iota(jnp.int32, (SUBLANES, GATE_COLS), 0)
    incl = run
    for step in (1, 2, 4):
        incl = incl + jnp.where(sub_g >= step, pltpu.roll(incl, step, axis=0), 0.0)
    earlier = incl - run
    bcum = jnp.concatenate([p + earlier for p in partial], axis=0)

    def head_cols(base, h):
        return zvo[:, base + h * ML_HEAD_DIM:base + (h + 1) * ML_HEAD_DIM]

    outs, C_new, n_new, m_new = _mlstm_heads(
        q_of=lambda h: conv_silu(h * ML_HEAD_DIM, ML_HEAD_DIM),
        k_of=lambda h: conv_silu(ML_WIDTH + h * ML_HEAD_DIM, ML_HEAD_DIM),
        v_of=lambda h: head_cols(0, h),
        o_of=lambda h: head_cols(ML_WIDTH, h),
        causal=causal, ipre=zif, bcum=bcum, groups=[(0, PL, PL - 1)],
        m0_of=lambda g: m_ref[...], C0_of=lambda g, h: C_ref[h], n0_of=lambda g, h: n_ref[h:h + 1, :],
        hng_ref=hng_ref, fill=fill)
    assert next(pieces, None) is None, "projection pieces left over"
    for h in range(ML_HEADS):
        bo_ref[:, :, h * ML_HEAD_DIM:(h + 1) * ML_HEAD_DIM] = outs[h].reshape(PG, SUBLANES, ML_HEAD_DIM)
        C_ref[h] = C_new[0][h]
        n_ref[h:h + 1, :] = n_new[0][h]
    m_ref[...] = m_new[0]


def _mixer_ml_call(hp, n_seq, w_all, bif, cw, cb, hng):
    B = n_seq
    T = hp.shape[0] // B
    cps = T // PL
    n_chunks = B * cps
    math = lambda s: jnp.maximum(s - 1, 0)
    per_b = lambda shape: pl.BlockSpec((None,) + shape,
                                       lambda s: (math(s) // cps,) + (0,) * len(shape))
    out_shape = [jax.ShapeDtypeStruct((B * T, ML_WIDTH), F32),
                 jax.ShapeDtypeStruct((B, CONV_W - 1, 2 * ML_WIDTH), F32),
                 jax.ShapeDtypeStruct((B, ML_HEADS, ML_HEAD_DIM, ML_HEAD_DIM), F32),
                 jax.ShapeDtypeStruct((B, ML_HEADS, ML_HEAD_DIM), F32),
                 jax.ShapeDtypeStruct((B, 1, ML_HEADS), F32)]
    out_specs = [pl.BlockSpec(memory_space=pl.ANY),
                 per_b((CONV_W - 1, 2 * ML_WIDTH)),
                 per_b((ML_HEADS, ML_HEAD_DIM, ML_HEAD_DIM)), per_b((ML_HEADS, ML_HEAD_DIM)),
                 per_b((1, ML_HEADS))]
    return pl.pallas_call(
        functools.partial(_mixer_ml_kernel, chunks_per_seq=cps, n_chunks=n_chunks),
        grid=(n_chunks + 1,),
        in_specs=[pl.BlockSpec(memory_space=pl.ANY)]
        + _ml_weight_specs()
        + [_const_spec((1, GATE_COLS)), _const_spec((CONV_W, 2 * ML_WIDTH)),
           _const_spec((1, 2 * ML_WIDTH)), _const_spec((1, ML_WIDTH))],
        out_specs=out_specs,
        out_shape=out_shape,
        scratch_shapes=[pltpu.VMEM((2, PL, 2 * ML_WIDTH), F32),
                        pltpu.VMEM((2, PL, 2 * ML_WIDTH), F32),
                        pltpu.VMEM((2, PL, GATE_COLS), F32),
                        pltpu.VMEM((TAIL, 2 * ML_WIDTH), F32),
                        pltpu.VMEM((2, PG, SUBLANES, D_MODEL), F32),
                        pltpu.VMEM((2, PG, SUBLANES, ML_WIDTH), F32),
                        pltpu.SemaphoreType.DMA((2,)),
                        pltpu.SemaphoreType.DMA((2,))],
        compiler_params=pltpu.CompilerParams(dimension_semantics=("arbitrary",),
                                             vmem_limit_bytes=VMEM_LIMIT),
        name="mixer_ml",
    )(hp, w_all, w_all, w_all, bif, cw, cb, hng)


def _merge_ffn_kernel(x_ref, pa_ref, bo_ref, g1_ref, wg_ref, bg_ref, wpb_ref, wout_ref,
                      g2_ref, wfi_ref, wfo_ref, gf_ref, y_ref):
    tm = _n_tokens(x_ref)
    subs = [slice(r0, r0 + FFN_SUB_ROWS) for r0 in range(0, tm, FFN_SUB_ROWS)]
    x = [_load_rows(x_ref, rs) for rs in subs]
    h = [_rms(xi, g1_ref[...]).astype(BF16) for xi in x]
    gab = [_bdot(hi, wg_ref[...]) + bg_ref[...] for hi in h]
    pb = [_bdot(bo_ref[rs, :].astype(BF16), wpb_ref[...]) for rs in subs]
    merged = [(jax.nn.sigmoid(g[:, :D_MODEL]) * pa_ref[rs, :]
               + jax.nn.sigmoid(g[:, D_MODEL:]) * p).astype(BF16) for g, p, rs in zip(gab, pb, subs)]
    x1 = [xi + _bdot(mi, wout_ref[...]) for xi, mi in zip(x, merged)]
    h2 = [_rms(xi, g2_ref[...]).astype(BF16) for xi in x1]
    gu = [_bdot(hi, wfi_ref[...]) for hi in h2]
    hid = [(g[:, :D_FF] * jax.nn.sigmoid(g[:, :D_FF]) * g[:, D_FF:]).astype(BF16) for g in gu]
    x2 = [xi + _bdot(hi, wfo_ref[...]) for xi, hi in zip(x1, hid)]
    for rs, xi in zip(subs, x2):
        _store_rows(y_ref, rs, _rms(xi, gf_ref[...]))


def _merge_ffn_call(x, pa, bo, g1, wg, bg, wpb, wout, g2, wfi, wfo, gf, *, tm):
    m = _n_tokens(x)
    row = pl.BlockSpec((tm, D_MODEL), lambda i: (i, 0))
    x_spec = _token_spec(x.shape, tm)
    return pl.pallas_call(
        _merge_ffn_kernel,
        grid=(m // tm,),
        in_specs=[x_spec, row, row, _const_spec((1, D_MODEL)), _const_spec((D_MODEL, 2 * D_MODEL)),
                  _const_spec((1, 2 * D_MODEL)), _const_spec((ML_WIDTH, D_MODEL)),
                  _const_spec((D_MODEL, D_MODEL)), _const_spec((1, D_MODEL)),
                  _const_spec((D_MODEL, 2 * D_FF)), _const_spec((D_FF, D_MODEL)),
                  _const_spec((1, D_MODEL))],
        out_specs=x_spec,
        out_shape=jax.ShapeDtypeStruct(x.shape, F32),
        compiler_params=pltpu.CompilerParams(dimension_semantics=("arbitrary",),
                                             vmem_limit_bytes=VMEM_LIMIT),
        name="merge_ffn",
    )(x, pa, bo, g1, wg, bg, wpb, wout, g2, wfi, wfo, gf)


SAMPLE_SEQS_PER_STEP = 8
TM_GMLP = 512
TM_FFN = 512


def _spatial_tiles(w_s, b_s, chunk):
    if chunk == GM_CHUNK:
        ws_t, b_pos = w_s[:, :chunk, :chunk], b_s[:, :chunk].T
    else:
        onehot = jnp.asarray(np.arange(GM_CHUNK)[:, None] % chunk == np.arange(chunk)[None, :], F32)
        hp = lax.Precision.HIGHEST
        ws_t = jnp.einsum("ri,gij,cj->grc", onehot, w_s[:, :chunk, :chunk], onehot, precision=hp)
        b_pos = jnp.dot(onehot, b_s[:, :chunk].T, precision=hp)
    bs_t = jnp.repeat(b_pos, GM_GROUP_W, axis=1)
    return ws_t, bs_t


def _gmlp_branch(xf, w, chunk, *, emit_v, emit_h, **side_jobs):
    ws_t, bs_t = _spatial_tiles(w["w_s"], w["b_s"], chunk)
    return _gmlp_call(xf, w["g1"], w["wuv"], w["lng"], w["lnb"], ws_t, bs_t, w["wpa"],
                      chunk=chunk, emit_v=emit_v, emit_h=emit_h,
                      tm=min(TM_GMLP, _n_tokens(xf)), **side_jobs)


def _merge_branch(xf, pa, bo, w):
    return _merge_ffn_call(xf, pa, bo, w["g1"], w["wg"], w["bg"], w["wpb"], w["wout"], w["g2"],
                           w["wfi"], w["wfo"], w["gf"], tm=TM_FFN)


def kernel(x_prompt, x_sample, state_conv, state_C, state_n, state_m, g_norm1, w_in, b_i, b_f, ln_g, ln_b, w_s, b_s, conv_w, conv_b, hn_g, b_gate, w_proj_a, w_proj_b, w_out, g_norm2, w_ffn_in, w_ffn_out, g_final):
    Bp, Tp, _ = x_prompt.shape
    Bs, Ts, _ = x_sample.shape
    win = w_in[0]
    c_ml = 2 * GM_WIDTH
    c_gate = c_ml + 4 * ML_WIDTH + 2 * ML_HEADS
    n_ml_blocks = -(-(ML_COLS) // TCAST_ROWS)
    win_t = jnp.transpose(win)
    w = dict(
        g1=g_norm1[0][None], g2=g_norm2[0][None], gf=g_final[None],
        wuv=_tcast_call(win_t, 0, c_ml // TCAST_ROWS),
        bg=b_gate[0].reshape(1, 2 * D_MODEL),
        lng=ln_g[0][None], lnb=ln_b[0][None], w_s=w_s[0], b_s=b_s[0],
        bif=jnp.pad(jnp.concatenate([b_i[0], b_f[0]]), (0, GATE_COLS - 2 * ML_HEADS))[None],
        cw=conv_w[0], cb=conv_b[0][None], hng=hn_g[0][None],
        wpa=w_proj_a[0].astype(BF16),
    )

    xpf = x_prompt.reshape(Bp * Tp, D_MODEL)
    pa_p, hp_p, w["wpb"], w["wout"], w["wfi"], w["wfo"], w["w_ml"], w["wg"] = _gmlp_branch(
        xpf, w, GM_CHUNK, emit_v=False, emit_h=True,
        cast_weights=(w_proj_b[0], w_out[0], w_ffn_in[0], w_ffn_out[0]),
        tcast_weight=win_t,
        tcast_jobs=((c_ml, n_ml_blocks, 0), (c_gate, 2 * D_MODEL // TCAST_ROWS, n_ml_blocks)))
    bo_p, conv_p, C_p, n_p, m_p = _mixer_ml_call(hp_p, Bp, w["w_ml"], w["bif"], w["cw"], w["cb"],
                                                 w["hng"])
    y_p = _merge_branch(xpf, pa_p, bo_p, w)

    pa_s, vn_s, zq = _gmlp_branch(x_sample, w, Ts, emit_v=True, emit_h=False, w_ml=w["w_ml"])
    st = (state_C[0], state_n[0], state_m[0])
    bo_s, conv_s, C_s, n_s, m_s = _mlstm_call(zq, jnp.transpose(state_conv[0], (1, 0, 2)), st, w["bif"],
                                              w["cw"], w["cb"], w["hng"], t_valid=Ts,
                                              nb=SAMPLE_SEQS_PER_STEP)
    y_s = _merge_branch(x_sample, pa_s, bo_s, w)

    return (y_p.reshape(Bp, Tp, D_MODEL), y_s,
            conv_p[None], C_p[None], n_p[None], m_p.reshape(1, Bp, ML_HEADS),
            jnp.transpose(conv_s, (1, 0, 2))[None], C_s[None], n_s[None], m_s[None],
            vn_s[None])
```

```python
import functools
import math

import jax
import jax.numpy as jnp
import numpy as np
from jax import lax
from jax.experimental import pallas as pl
from jax.experimental.pallas import tpu as pltpu

D_MODEL = 1024
GM_WIDTH = D_MODEL
GM_GROUPS = 4
GM_GROUP_W = GM_WIDTH // GM_GROUPS
GM_CHUNK = 128
ML_HEADS = 4
ML_HEAD_DIM = D_MODEL // ML_HEADS
ML_WIDTH = ML_HEADS * ML_HEAD_DIM
CONV_W = 4
D_FF = 2816
EPS = 1e-6

LANES = 128
SUBLANES = 8
BF16_ROWS = 16
GATE_COLS = LANES
ML_COLS = 4 * ML_WIDTH + GATE_COLS
VMEM_LIMIT = 56 * 1024 * 1024
GMLP_SUB_ROWS = 256
FFN_SUB_ROWS = 256

F32 = jnp.float32
BF16 = jnp.bfloat16
NEG_INF = float("-inf")
LN_INV_K_SCALE = 0.5 * math.log(ML_HEAD_DIM)
NT_DIMS = (((1,), (1,)), ((), ()))
TN_DIMS = (((0,), (0,)), ((), ()))


def _rms(x, g):
    return x * lax.rsqrt(jnp.mean(x * x, axis=-1, keepdims=True) + EPS) * g


def _gelu(x):
    return 0.5 * x * (1.0 + lax.erf(x * (2.0 ** -0.5)))


def _log_sigmoid(x):
    return jnp.minimum(x, 0.0) - jnp.log1p(jnp.exp(-jnp.abs(x)))


def _bdot(a, b):
    return jnp.dot(a, b, preferred_element_type=F32)


def _n_tokens(a):
    return a.shape[0] if len(a.shape) == 2 else a.shape[0] * a.shape[1]


def _load_rows(ref, rs):
    if len(ref.shape) == 2:
        return ref[rs, :]
    t = ref.shape[1]
    return ref[rs.start // t:rs.stop // t].reshape(rs.stop - rs.start, ref.shape[2])


def _store_rows(ref, rs, val):
    if len(ref.shape) == 2:
        ref[rs, :] = val
    else:
        t = ref.shape[1]
        ref[rs.start // t:rs.stop // t] = val.reshape((rs.stop - rs.start) // t, t, ref.shape[2])


def _token_spec(shape, tm):
    if len(shape) == 2:
        return pl.BlockSpec((tm, shape[1]), lambda i: (i, 0))
    return pl.BlockSpec((tm // shape[1], shape[1], shape[2]), lambda i: (i, 0, 0))


def _const_spec(shape):
    nd = len(shape)
    return pl.BlockSpec(shape, lambda *_: (0,) * nd, pipeline_mode=pl.Buffered(1))


def _gmlp_kernel(x_ref, g1_ref, wuv_ref, lng_ref, lnb_ref, ws_ref, bs_ref, wpa_ref, *rest,
                 chunk, emit_v, emit_h, emit_zq, n_cast, n_tcast):
    rest = list(rest)
    ml_weights = [rest.pop(0) for _ in range(3)] if emit_zq else None
    cast_src = [rest.pop(0) for _ in range(n_cast)]
    tcast_src = [rest.pop(0) for _ in range(n_tcast)]
    pa_ref = rest.pop(0)
    vn_ref = rest.pop(0) if emit_v else None
    hp_ref = rest.pop(0) if emit_h else None
    zq_ref = rest.pop(0) if emit_zq else None
    for src in cast_src:
        rest.pop(0)[...] = src[...].astype(BF16)
    for src in tcast_src:
        rest.pop(0)[...] = src[...].T.astype(BF16)
    a_sc = rest.pop(0)
    tm = _n_tokens(x_ref)
    blk = ws_ref.shape[1]
    sub = GMLP_SUB_ROWS
    subs = [slice(r0, r0 + sub) for r0 in range(0, tm, sub)]
    hb = []
    for rs in subs:
        hf = _rms(_load_rows(x_ref, rs), g1_ref[...])
        hb.append(hf.astype(BF16))
        if emit_h:
            hp_ref[rs, :] = hf
    zu = [_bdot(h, wuv_ref[:, :GM_WIDTH]) for h in hb]
    zv = [_bdot(h, wuv_ref[:, GM_WIDTH:]) for h in hb]
    if emit_zq:
        for h, rs in zip(hb, subs):
            for w_ref, c0 in zip(ml_weights, (0, 2 * ML_WIDTH, 4 * ML_WIDTH)):
                zq_ref[rs, c0:c0 + w_ref.shape[1]] = _bdot(h, w_ref[...])
    r = lax.broadcasted_iota(jnp.int32, (blk, blk), 0)
    c = lax.broadcasted_iota(jnp.int32, (blk, blk), 1)
    keep = c <= r
    if chunk < blk:
        sh = chunk.bit_length() - 1
        keep = jnp.logical_and(keep, (r >> sh) == (c >> sh))
    wsm = [jnp.where(keep, ws_ref[g], 0.0).astype(BF16) for g in range(GM_GROUPS)]
    for si, rs in enumerate(subs):
        u = _gelu(zu[si])
        v = _gelu(zv[si])
        mu = jnp.mean(v, axis=-1, keepdims=True)
        vc = v - mu
        var = jnp.mean(vc * vc, axis=-1, keepdims=True)
        vn = vc * lax.rsqrt(var + EPS) * lng_ref[...] + lnb_ref[...]
        if emit_v:
            _store_rows(vn_ref, rs, vn)
        vb = vn.astype(BF16)
        for g in range(GM_GROUPS):
            cs = slice(g * GM_GROUP_W, (g + 1) * GM_GROUP_W)
            for i in range(sub // blk):
                ls = slice(i * blk, (i + 1) * blk)
                s = _bdot(wsm[g], vb[ls, cs]) + bs_ref[:, cs]
                a_sc[rs.start + i * blk:rs.start + (i + 1) * blk, cs] = (u[ls, cs] * s).astype(BF16)
        pa_ref[rs, :] = _bdot(a_sc[rs, :], wpa_ref[...])


def _cast_block_spec(n_rows, n_cols, steps):
    n_blocks = steps
    while n_rows % n_blocks or (n_rows // n_blocks) % BF16_ROWS:
        n_blocks //= 2
    per = steps // n_blocks
    return pl.BlockSpec((n_rows // n_blocks, n_cols), lambda i: (i // per, 0))


TCAST_ROWS = 256


def _tcast_specs(job):
    first_row, n_blocks, first_step = job
    assert first_row % SUBLANES == 0
    blk = lambda i: jnp.clip(i - first_step, 0, n_blocks - 1)
    src = pl.BlockSpec((pl.Element(TCAST_ROWS), pl.Element(D_MODEL)),
                       lambda i: (pl.multiple_of(first_row + TCAST_ROWS * blk(i), SUBLANES), 0))
    dst = pl.BlockSpec((D_MODEL, TCAST_ROWS), lambda i: (0, blk(i)))
    return src, dst, jax.ShapeDtypeStruct((D_MODEL, TCAST_ROWS * n_blocks), BF16)


def _tcast_kernel(src_ref, dst_ref):
    dst_ref[...] = src_ref[...].T.astype(BF16)


def _tcast_call(wt, first_row, n_blocks):
    src, dst, shape = _tcast_specs((first_row, n_blocks, 0))
    return pl.pallas_call(
        _tcast_kernel, grid=(n_blocks,), in_specs=[src], out_specs=dst, out_shape=shape,
        compiler_params=pltpu.CompilerParams(dimension_semantics=("arbitrary",)),
        name="tcast",
    )(wt)


def _gmlp_call(x, g1, wuv, lng, lnb, ws_t, bs_t, wpa, *, chunk, emit_v, emit_h, tm, w_ml=None,
               cast_weights=(), tcast_weight=None, tcast_jobs=()):
    m = _n_tokens(x)
    blk = ws_t.shape[1]
    steps = m // tm
    assert steps & (steps - 1) == 0
    assert all(first_step + n_blocks <= steps for _, n_blocks, first_step in tcast_jobs)
    row = pl.BlockSpec((tm, D_MODEL), lambda i: (i, 0))
    out_shape = [jax.ShapeDtypeStruct((m, D_MODEL), F32)]
    out_specs = [row]
    scratch = [pltpu.VMEM((tm, GM_WIDTH), BF16)]
    cast_specs = [_cast_block_spec(cw.shape[0], cw.shape[1], steps) for cw in cast_weights]
    tcast = [_tcast_specs(job) for job in tcast_jobs]
    if emit_v:
        out_shape.append(jax.ShapeDtypeStruct(x.shape, F32))
        out_specs.append(_token_spec(x.shape, tm))
    if emit_h:
        out_shape.append(jax.ShapeDtypeStruct((m, D_MODEL), F32))
        out_specs.append(row)
    emit_zq = w_ml is not None
    if emit_zq:
        out_shape.append(jax.ShapeDtypeStruct((m, ML_COLS), F32))
        out_specs.append(pl.BlockSpec((tm, ML_COLS), lambda i: (i, 0)))
    out_shape += [jax.ShapeDtypeStruct(cw.shape, BF16) for cw in cast_weights]
    out_shape += [t[2] for t in tcast]
    out_specs += cast_specs + [t[1] for t in tcast]
    return pl.pallas_call(
        functools.partial(_gmlp_kernel, chunk=chunk, emit_v=emit_v, emit_h=emit_h, emit_zq=emit_zq,
                          n_cast=len(cast_weights), n_tcast=len(tcast)),
        grid=(steps,),
        in_specs=[_token_spec(x.shape, tm), _const_spec((1, D_MODEL)),
                  _const_spec((D_MODEL, 2 * GM_WIDTH)),
                  _const_spec((1, GM_WIDTH)), _const_spec((1, GM_WIDTH)),
                  _const_spec((GM_GROUPS, blk, blk)), _const_spec((blk, GM_WIDTH)),
                  _const_spec((GM_WIDTH, D_MODEL))]
        + (_ml_weight_specs() if emit_zq else []) + cast_specs + [t[0] for t in tcast],
        out_specs=out_specs,
        out_shape=out_shape,
        scratch_shapes=scratch,
        compiler_params=pltpu.CompilerParams(dimension_semantics=("arbitrary",),
                                             vmem_limit_bytes=VMEM_LIMIT),
        name="gmlp",
    )(x, g1, wuv, lng, lnb, ws_t, bs_t, wpa, *([w_ml] * 3 if emit_zq else []),
      *cast_weights, *([tcast_weight] * len(tcast)))


def _ml_weight_specs():
    wide = 2 * ML_WIDTH
    col_block = lambda width, idx: pl.BlockSpec((D_MODEL, width), lambda *_: (0, idx),
                                                pipeline_mode=pl.Buffered(1))
    return [col_block(wide, 0), col_block(wide, 1), col_block(GATE_COLS, 4 * ML_WIDTH // GATE_COLS)]


def _mlstm_heads(q_of, k_of, v_of, o_of, causal, ipre, bcum, groups, m0_of, C0_of, n0_of, hng_ref,
                 fill=None):
    n_groups = len(groups)
    single = n_groups == 1
    fill = fill or (lambda: None)

    def rows_of(x, g):
        return x if single else x[groups[g][0]:groups[g][0] + groups[g][1]]

    def per_row(vals):
        if single:
            return vals[0]
        return jnp.concatenate([jnp.broadcast_to(v, (groups[g][1], v.shape[1]))
                                for g, v in enumerate(vals)], axis=0)

    a = ipre - bcum
    a_t = a.T
    m_rows = per_row([m0_of(g) for g in range(n_groups)])

    def prepare(h):
        p = {}
        a2 = jnp.where(causal, a_t[h:h + 1, :], -jnp.inf)
        p["mc"] = mc = jnp.maximum(jnp.max(a2, axis=1, keepdims=True), m_rows[:, h:h + 1])
        p["m_last"] = m_last = [mc[grp[2]:grp[2] + 1, :] for grp in groups]
        p["dm"] = jnp.exp(a2 - (mc + LN_INV_K_SCALE))
        p["w_inter"] = jnp.exp(m_rows[:, h:h + 1] - mc)
        p["w_col"] = w_col = jnp.exp(a[:, h:h + 1] - (per_row(m_last) + LN_INV_K_SCALE))
        p["decay"] = [jnp.exp(m0_of(g)[:, h:h + 1] - m_last[g]) for g in range(n_groups)]
        p["q"] = q = q_of(h)
        p["k"] = k = k_of(h)
        v = v_of(h)
        p["qb"], p["kb"], p["vb"] = q.astype(BF16), k.astype(BF16), v.astype(BF16)
        p["vw"] = (v * w_col).astype(BF16)
        p["c_old"] = [C0_of(g, h) for g in range(n_groups)]
        p["n_old"] = [n0_of(g, h) for g in range(n_groups)]
        return p

    def first_matmuls(p):
        p["qk"] = lax.dot_general(p["qb"], p["kb"], NT_DIMS, preferred_element_type=F32)
        p["qc"] = [lax.dot_general(rows_of(p["qb"], g), p["c_old"][g].astype(BF16), NT_DIMS,
                                   preferred_element_type=F32) for g in range(n_groups)]
        if single:
            n_rows = jnp.broadcast_to(p["n_old"][0], (LANES, ML_HEAD_DIM)).astype(BF16)
            p["qn"] = lax.dot_general(p["qb"], n_rows, NT_DIMS, preferred_element_type=F32)[:, 0:1]
        else:
            p["qn"] = jnp.sum(p["q"] * per_row(p["n_old"]), axis=1, keepdims=True)

    def second_matmuls(p):
        p["s"] = s = p["dm"] * p["qk"]
        p["sv"] = _bdot(s.astype(BF16), p["vb"])
        p["cupd"] = [lax.dot_general(rows_of(p["vw"], g), rows_of(p["kb"], g), TN_DIMS,
                                     preferred_element_type=F32) for g in range(n_groups)]

    def finish(h, p):
        qc_rows = p["qc"][0] if single else jnp.concatenate(p["qc"], axis=0)
        num = p["w_inter"] * qc_rows + p["sv"]
        den = p["w_inter"] * p["qn"] + jnp.sum(p["s"], axis=1, keepdims=True)
        hcur = num / jnp.maximum(jnp.abs(den), jnp.exp(-(bcum[:, h:h + 1] + p["mc"])))
        mu = jnp.mean(hcur, axis=1, keepdims=True)
        hc = hcur - mu
        var = jnp.mean(hc * hc, axis=1, keepdims=True)
        hs = slice(h * ML_HEAD_DIM, (h + 1) * ML_HEAD_DIM)
        out = jax.nn.sigmoid(o_of(h)) * (hc * lax.rsqrt(var + EPS) * hng_ref[:, hs])
        kw = p["k"] * p["w_col"]
        c_new = [p["decay"][g] * p["c_old"][g] + p["cupd"][g] for g in range(n_groups)]
        n_new = [p["decay"][g] * p["n_old"][g] + jnp.sum(rows_of(kw, g), axis=0, keepdims=True)
                 for g in range(n_groups)]
        return out, c_new, n_new

    H = ML_HEADS
    per_head = [None] * H
    done = [None] * H
    if single:
        per_head[0] = prepare(0)
        fill()
        first_matmuls(per_head[0])
        if H > 1:
            per_head[1] = prepare(1)
        fill()
        for h in range(H):
            second_matmuls(per_head[h])
            if h + 1 < H:
                first_matmuls(per_head[h + 1])
            fill()
            if h + 2 < H:
                per_head[h + 2] = prepare(h + 2)
            done[h] = finish(h, per_head[h])
            fill()
    else:
        per_head = [prepare(h) for h in range(H)]
        for stage in (first_matmuls, second_matmuls):
            for h in range(H):
                stage(per_head[h])
        done = [finish(h, per_head[h]) for h in range(H)]

    outs = [done[h][0] for h in range(H)]
    C_new = [[done[h][1][g] for h in range(H)] for g in range(n_groups)]
    n_new = [[done[h][2][g] for h in range(H)] for g in range(n_groups)]
    m_new = []
    for g, grp in enumerate(groups):
        row = m0_of(g)
        lane = lax.broadcasted_iota(jnp.int32, row.shape, 1)
        for h in range(H):
            row = jnp.where(lane == h, bcum[grp[2]:grp[2] + 1, h:h + 1] + per_head[h]["m_last"][g], row)
        m_new.append(row)
    return outs, C_new, n_new, m_new


def _mlstm_kernel(zq_ref, cst_ref, C0_ref, n0_ref, m0_ref, bif_ref, cw_ref, cb_ref, hng_ref,
                  bo_ref, conv_ref, C_ref, n_ref, m_ref, zp, xp, *, t_valid):
    nb = C0_ref.shape[0]
    L = BF16_ROWS
    R = nb * L

    @pl.when(pl.program_id(0) == 0)
    def _():
        zp[...] = jnp.zeros(zp.shape, F32)

    for bb in range(nb):
        zp[bb, 0:t_valid, :] = zq_ref[bb * t_valid:(bb + 1) * t_valid, :]
        for j in range(CONV_W - 1):
            row = SUBLANES - (CONV_W - 1) + j
            xp[bb, row:row + 1, :] = cst_ref[j, bb:bb + 1, :]
        xp[bb, SUBLANES:2 * SUBLANES, :] = zp[bb, 0:SUBLANES, 0:2 * ML_WIDTH]
        for j in range(CONV_W - 1):
            row = SUBLANES + t_valid - (CONV_W - 1) + j
            conv_ref[j, bb:bb + 1, :] = xp[bb, row:row + 1, :]
    qk_rows = []
    for bb in range(nb):
        acc = cb_ref[...]
        for j in range(CONV_W):
            off = SUBLANES - (CONV_W - 1) + j
            acc = acc + cw_ref[j:j + 1, :] * xp[bb, off:off + SUBLANES, :]
        qk_rows += [acc, jnp.zeros((L - SUBLANES, 2 * ML_WIDTH), F32)]
    qk = jnp.concatenate(qk_rows, axis=0)
    qk = qk * jax.nn.sigmoid(qk)

    def cols(c0, width):
        return zp[:, :, c0:c0 + width].reshape(R, width)

    zif = cols(4 * ML_WIDTH, GATE_COLS) + bif_ref[...]
    live = (lax.broadcasted_iota(jnp.int32, (R, GATE_COLS), 0) & (L - 1)) < t_valid
    ipre = jnp.where(live, zif, NEG_INF)
    logf = jnp.where(live, pltpu.roll(_log_sigmoid(zif), GATE_COLS - ML_HEADS, axis=1), 0.0)
    assert t_valid <= SUBLANES
    sub_g = lax.broadcasted_iota(jnp.int32, (SUBLANES, GATE_COLS), 0)
    parts = []
    for bb in range(nb):
        incl = logf[bb * L:bb * L + SUBLANES, :]
        for step in (1, 2, 4):
            incl = incl + jnp.where(sub_g >= step, pltpu.roll(incl, step, axis=0), 0.0)
        parts += [incl, jnp.broadcast_to(incl[SUBLANES - 1:SUBLANES, :], (L - SUBLANES, GATE_COLS))]
    bcum = jnp.concatenate(parts, axis=0)
    r = lax.broadcasted_iota(jnp.int32, (R, R), 0)
    c = lax.broadcasted_iota(jnp.int32, (R, R), 1)
    sh = L.bit_length() - 1
    causal = jnp.logical_and(c <= r, (r >> sh) == (c >> sh))
    live_w = (lax.broadcasted_iota(jnp.int32, (R, ML_HEAD_DIM), 0) & (L - 1)) < t_valid

    def head_cols(x, base, h):
        return x[:, base + h * ML_HEAD_DIM:base + (h + 1) * ML_HEAD_DIM]

    groups = [(bb * L, L, bb * L + L - 1) for bb in range(nb)]
    outs, C_new, n_new, m_new = _mlstm_heads(
        q_of=lambda h: head_cols(qk, 0, h),
        k_of=lambda h: jnp.where(live_w, head_cols(qk, ML_WIDTH, h), 0.0),
        v_of=lambda h: jnp.where(live_w, cols(2 * ML_WIDTH + h * ML_HEAD_DIM, ML_HEAD_DIM), 0.0),
        o_of=lambda h: cols(3 * ML_WIDTH + h * ML_HEAD_DIM, ML_HEAD_DIM),
        causal=causal, ipre=ipre, bcum=bcum, groups=groups,
        m0_of=lambda g: m0_ref[g:g + 1, :], C0_of=lambda g, h: C0_ref[g, h],
        n0_of=lambda g, h: n0_ref[g, h:h + 1, :], hng_ref=hng_ref)
    for bb in range(nb):
        for h in range(ML_HEADS):
            hs = slice(h * ML_HEAD_DIM, (h + 1) * ML_HEAD_DIM)
            bo_ref[bb * t_valid:(bb + 1) * t_valid, hs] = outs[h][bb * L:bb * L + t_valid, :]
            C_ref[bb, h] = C_new[bb][h]
            n_ref[bb, h:h + 1, :] = n_new[bb][h]
        m_ref[bb:bb + 1, :] = m_new[bb]


def _mlstm_call(zq, cst, state, bif, cw, cb, hng, *, t_valid, nb):
    T = t_valid
    B = zq.shape[0] // T
    kern = functools.partial(_mlstm_kernel, t_valid=T)
    per_b = lambda shape: pl.BlockSpec((nb,) + shape, lambda b: (b,) + (0,) * len(shape))
    rows = lambda width: pl.BlockSpec((nb * T, width), lambda b: (b, 0))
    conv_spec = pl.BlockSpec((CONV_W - 1, nb, 2 * ML_WIDTH), lambda b: (0, b, 0))
    st_specs = [per_b((ML_HEADS, ML_HEAD_DIM, ML_HEAD_DIM)), per_b((ML_HEADS, ML_HEAD_DIM)),
                per_b((ML_HEADS,))]
    in_specs = ([rows(ML_COLS), conv_spec] + st_specs
                + [_const_spec((1, GATE_COLS)), _const_spec((CONV_W, 2 * ML_WIDTH)),
                   _const_spec((1, 2 * ML_WIDTH)), _const_spec((1, ML_WIDTH))])
    out_shape = [jax.ShapeDtypeStruct((B * T, ML_WIDTH), F32),
                 jax.ShapeDtypeStruct((CONV_W - 1, B, 2 * ML_WIDTH), F32),
                 jax.ShapeDtypeStruct((B, ML_HEADS, ML_HEAD_DIM, ML_HEAD_DIM), F32),
                 jax.ShapeDtypeStruct((B, ML_HEADS, ML_HEAD_DIM), F32),
                 jax.ShapeDtypeStruct((B, ML_HEADS), F32)]
    out_specs = [rows(ML_WIDTH), conv_spec] + st_specs
    return pl.pallas_call(
        kern,
        grid=(B // nb,),
        in_specs=in_specs,
        out_specs=out_specs,
        out_shape=out_shape,
        scratch_shapes=[pltpu.VMEM((nb, BF16_ROWS, ML_COLS), F32),
                        pltpu.VMEM((nb, 2 * SUBLANES, 2 * ML_WIDTH), F32)],
        compiler_params=pltpu.CompilerParams(dimension_semantics=("arbitrary",),
                                             vmem_limit_bytes=VMEM_LIMIT),
        name="mlstm",
    )(zq, cst, *state, bif, cw, cb, hng)


PL = 256
PG = PL // SUBLANES
TAIL = (CONV_W - 1) * SUBLANES
PROJ_PIECE_COLS = 512
DMA_PRIORITIES = 2


def _mixer_ml_kernel(h_hbm, wqk_ref, wvo_ref, wif_ref, bif_ref, cw_ref, cb_ref, hng_ref,
                     bo_hbm, conv_ref, C_ref, n_ref, m_ref,
                     zqk_buf, zvo_buf, zif_buf, tail, hbuf, obuf, sem_in, sem_out,
                     *, chunks_per_seq, n_chunks):
    s = pl.program_id(0)

    def chunk_dmas(chunk, slot, inbound):
        row0 = chunk * PL
        copies = []
        for i in range(SUBLANES):
            hbm = (h_hbm if inbound else bo_hbm).at[pl.ds(row0 + PG * i, PG), :]
            vmem = (hbuf if inbound else obuf).at[slot, :, i, :]
            sem = (sem_in if inbound else sem_out).at[slot]
            copies.append(pltpu.make_async_copy(hbm, vmem, sem) if inbound
                          else pltpu.make_async_copy(vmem, hbm, sem))
        return copies

    def start_all(copies):
        for i, cp in enumerate(copies):
            cp.start(priority=i % DMA_PRIORITIES)

    @pl.when(s == 0)
    def _():
        zqk_buf[1] = jnp.zeros(zqk_buf.shape[1:], F32)
        zvo_buf[1] = jnp.zeros(zvo_buf.shape[1:], F32)
        zif_buf[1] = jnp.zeros(zif_buf.shape[1:], F32)
        start_all(chunk_dmas(0, 0, True))

    @pl.when(jnp.maximum(s - 1, 0) % chunks_per_seq == 0)
    def _():
        tail[...] = jnp.zeros(tail.shape, F32)
        C_ref[...] = jnp.zeros(C_ref.shape, F32)
        n_ref[...] = jnp.zeros(n_ref.shape, F32)
        m_ref[...] = jnp.zeros(m_ref.shape, F32)

    def step(rd, wr):
        start_all(chunk_dmas(jnp.minimum(s + 1, n_chunks - 1), rd, True))
        for cp in chunk_dmas(jnp.minimum(s, n_chunks - 1), wr, True):
            cp.wait()

        _mixer_ml_step(hbuf.at[wr], wqk_ref, wvo_ref, wif_ref, bif_ref, cw_ref, cb_ref, hng_ref,
                       obuf.at[rd], conv_ref, C_ref, n_ref, m_ref, zqk_buf, zvo_buf, zif_buf, tail,
                       rd=rd, wr=wr)

        @pl.when(s >= 1)
        def _():
            start_all(chunk_dmas(s - 1, rd, False))

        @pl.when(s >= 2)
        def _():
            for cp in chunk_dmas(s - 2, wr, False):
                cp.wait()

        @pl.when(s == n_chunks)
        def _():
            for cp in chunk_dmas(n_chunks - 1, rd, True) + chunk_dmas(s - 1, rd, False):
                cp.wait()

    @pl.when(s % 2 == 0)
    def _():
        step(rd=1, wr=0)

    @pl.when(s % 2 == 1)
    def _():
        step(rd=0, wr=1)


def _mixer_ml_step(h_ref, wqk_ref, wvo_ref, wif_ref, bif_ref, cw_ref, cb_ref, hng_ref,
                   bo_ref, conv_ref, C_ref, n_ref, m_ref, zqk_buf, zvo_buf, zif_buf, tail,
                   *, rd, wr):
    h = h_ref[...].reshape(PL, D_MODEL).astype(BF16)

    def piece(w_ref, buf, c0, width):
        def emit():
            buf[wr, :, c0:c0 + width] = _bdot(h, w_ref[:, c0:c0 + width])
        return emit

    pieces = [piece(w_ref, buf, c0, PROJ_PIECE_COLS)
              for w_ref, buf in ((wqk_ref, zqk_buf), (wvo_ref, zvo_buf))
              for c0 in range(0, 2 * ML_WIDTH, PROJ_PIECE_COLS)]
    pieces.append(piece(wif_ref, zif_buf, 0, GATE_COLS))
    pieces = iter(pieces)

    def fill():
        emit = next(pieces, None)
        if emit is not None:
            emit()

    fill()
    zqk = zqk_buf.at[rd]
    zvo = zvo_buf.at[rd]
    zif = zif_buf[rd] + bif_ref[...]

    zqk_tail = zqk[PL - TAIL:, :]
    sub = lax.broadcasted_iota(jnp.int32, (SUBLANES, 2 * ML_WIDTH), 0)
    wrapped = []
    for g in range(CONV_W - 1):
        cur = pltpu.roll(zqk_tail[g * SUBLANES:(g + 1) * SUBLANES], 1, axis=0)
        prev = pltpu.roll(tail[g * SUBLANES:(g + 1) * SUBLANES, :], 1, axis=0)
        wrapped.append(jnp.where(sub == 0, prev, cur))
    wrapped = jnp.concatenate(wrapped, axis=0)
    tail[...] = zqk_tail
    conv_ref[...] = jnp.concatenate(
        [zqk_tail[g * SUBLANES + SUBLANES - 1:(g + 1) * SUBLANES, :] for g in range(CONV_W - 1)], axis=0)

    def conv_silu(c0, width):
        cs = slice(c0, c0 + width)
        acc = cb_ref[:, cs] + cw_ref[CONV_W - 1:CONV_W, cs] * zqk[:, cs]
        for d in range(1, CONV_W):
            shifted = jnp.concatenate(
                [wrapped[TAIL - d * SUBLANES:, cs], zqk[:PL - d * SUBLANES, cs]], axis=0)
            acc = acc + cw_ref[CONV_W - 1 - d:CONV_W - d, cs] * shifted
        return acc * jax.nn.sigmoid(acc)

    pr = lax.broadcasted_iota(jnp.int32, (PL, PL), 0)
    pc = lax.broadcasted_iota(jnp.int32, (PL, PL), 1)
    bits, low = SUBLANES.bit_length() - 1, SUBLANES - 1
    causal = ((pc >> bits) + PG * (pc & low)) <= ((pr >> bits) + PG * (pr & low))

    logf = pltpu.roll(_log_sigmoid(zif), GATE_COLS - ML_HEADS, axis=1)
    run, partial = None, []
    for n in range(PG):
        blk = logf[n * SUBLANES:(n + 1) * SUBLANES, :]
        run = blk if run is None else run + blk
        partial.append(run)
    sub_g = lax.broadcasted_iota(jnp.int32, (SUBLANES, GATE_COLS), 0)
    incl = run
    for step in (1, 2, 4):
        incl = incl + jnp.where(sub_g >= step, pltpu.roll(incl, step, axis=0), 0.0)
    earlier = incl - run
    bcum = jnp.concatenate([p + earlier for p in partial], axis=0)

    def head_cols(base, h):
        return zvo[:, base + h * ML_HEAD_DIM:base + (h + 1) * ML_HEAD_DIM]

    outs, C_new, n_new, m_new = _mlstm_heads(
        q_of=lambda h: conv_silu(h * ML_HEAD_DIM, ML_HEAD_DIM),
        k_of=lambda h: conv_silu(ML_WIDTH + h * ML_HEAD_DIM, ML_HEAD_DIM),
        v_of=lambda h: head_cols(0, h),
        o_of=lambda h: head_cols(ML_WIDTH, h),
        causal=causal, ipre=zif, bcum=bcum, groups=[(0, PL, PL - 1)],
        m0_of=lambda g: m_ref[...], C0_of=lambda g, h: C_ref[h], n0_of=lambda g, h: n_ref[h:h + 1, :],
        hng_ref=hng_ref, fill=fill)
    assert next(pieces, None) is None, "projection pieces left over"
    for h in range(ML_HEADS):
        bo_ref[:, :, h * ML_HEAD_DIM:(h + 1) * ML_HEAD_DIM] = outs[h].reshape(PG, SUBLANES, ML_HEAD_DIM)
        C_ref[h] = C_new[0][h]
        n_ref[h:h + 1, :] = n_new[0][h]
    m_ref[...] = m_new[0]


def _mixer_ml_call(hp, n_seq, w_all, bif, cw, cb, hng):
    B = n_seq
    T = hp.shape[0] // B
    cps = T // PL
    n_chunks = B * cps
    math = lambda s: jnp.maximum(s - 1, 0)
    per_b = lambda shape: pl.BlockSpec((None,) + shape,
                                       lambda s: (math(s) // cps,) + (0,) * len(shape))
    out_shape = [jax.ShapeDtypeStruct((B * T, ML_WIDTH), F32),
                 jax.ShapeDtypeStruct((B, CONV_W - 1, 2 * ML_WIDTH), F32),
                 jax.ShapeDtypeStruct((B, ML_HEADS, ML_HEAD_DIM, ML_HEAD_DIM), F32),
                 jax.ShapeDtypeStruct((B, ML_HEADS, ML_HEAD_DIM), F32),
                 jax.ShapeDtypeStruct((B, 1, ML_HEADS), F32)]
    out_specs = [pl.BlockSpec(memory_space=pl.ANY),
                 per_b((CONV_W - 1, 2 * ML_WIDTH)),
                 per_b((ML_HEADS, ML_HEAD_DIM, ML_HEAD_DIM)), per_b((ML_HEADS, ML_HEAD_DIM)),
                 per_b((1, ML_HEADS))]
    return pl.pallas_call(
        functools.partial(_mixer_ml_kernel, chunks_per_seq=cps, n_chunks=n_chunks),
        grid=(n_chunks + 1,),
        in_specs=[pl.BlockSpec(memory_space=pl.ANY)]
        + _ml_weight_specs()
        + [_const_spec((1, GATE_COLS)), _const_spec((CONV_W, 2 * ML_WIDTH)),
           _const_spec((1, 2 * ML_WIDTH)), _const_spec((1, ML_WIDTH))],
        out_specs=out_specs,
        out_shape=out_shape,
        scratch_shapes=[pltpu.VMEM((2, PL, 2 * ML_WIDTH), F32),
                        pltpu.VMEM((2, PL, 2 * ML_WIDTH), F32),
                        pltpu.VMEM((2, PL, GATE_COLS), F32),
                        pltpu.VMEM((TAIL, 2 * ML_WIDTH), F32),
                        pltpu.VMEM((2, PG, SUBLANES, D_MODEL), F32),
                        pltpu.VMEM((2, PG, SUBLANES, ML_WIDTH), F32),
                        pltpu.SemaphoreType.DMA((2,)),
                        pltpu.SemaphoreType.DMA((2,))],
        compiler_params=pltpu.CompilerParams(dimension_semantics=("arbitrary",),
                                             vmem_limit_bytes=VMEM_LIMIT),
        name="mixer_ml",
    )(hp, w_all, w_all, w_all, bif, cw, cb, hng)


def _merge_ffn_kernel(x_ref, pa_ref, bo_ref, g1_ref, wg_ref, bg_ref, wpb_ref, wout_ref,
                      g2_ref, wfi_ref, wfo_ref, gf_ref, y_ref):
    tm = _n_tokens(x_ref)
    subs = [slice(r0, r0 + FFN_SUB_ROWS) for r0 in range(0, tm, FFN_SUB_ROWS)]
    x = [_load_rows(x_ref, rs) for rs in subs]
    h = [_rms(xi, g1_ref[...]).astype(BF16) for xi in x]
    gab = [_bdot(hi, wg_ref[...]) + bg_ref[...] for hi in h]
    pb = [_bdot(bo_ref[rs, :].astype(BF16), wpb_ref[...]) for rs in subs]
    merged = [(jax.nn.sigmoid(g[:, :D_MODEL]) * pa_ref[rs, :]
               + jax.nn.sigmoid(g[:, D_MODEL:]) * p).astype(BF16) for g, p, rs in zip(gab, pb, subs)]
    x1 = [xi + _bdot(mi, wout_ref[...]) for xi, mi in zip(x, merged)]
    h2 = [_rms(xi, g2_ref[...]).astype(BF16) for xi in x1]
    gu = [_bdot(hi, wfi_ref[...]) for hi in h2]
    hid = [(g[:, :D_FF] * jax.nn.sigmoid(g[:, :D_FF]) * g[:, D_FF:]).astype(BF16) for g in gu]
    x2 = [xi + _bdot(hi, wfo_ref[...]) for xi, hi in zip(x1, hid)]
    for rs, xi in zip(subs, x2):
        _store_rows(y_ref, rs, _rms(xi, gf_ref[...]))


def _merge_ffn_call(x, pa, bo, g1, wg, bg, wpb, wout, g2, wfi, wfo, gf, *, tm):
    m = _n_tokens(x)
    row = pl.BlockSpec((tm, D_MODEL), lambda i: (i, 0))
    x_spec = _token_spec(x.shape, tm)
    return pl.pallas_call(
        _merge_ffn_kernel,
        grid=(m // tm,),
        in_specs=[x_spec, row, row, _const_spec((1, D_MODEL)), _const_spec((D_MODEL, 2 * D_MODEL)),
                  _const_spec((1, 2 * D_MODEL)), _const_spec((ML_WIDTH, D_MODEL)),
                  _const_spec((D_MODEL, D_MODEL)), _const_spec((1, D_MODEL)),
                  _const_spec((D_MODEL, 2 * D_FF)), _const_spec((D_FF, D_MODEL)),
                  _const_spec((1, D_MODEL))],
        out_specs=x_spec,
        out_shape=jax.ShapeDtypeStruct(x.shape, F32),
        compiler_params=pltpu.CompilerParams(dimension_semantics=("arbitrary",),
                                             vmem_limit_bytes=VMEM_LIMIT),
        name="merge_ffn",
    )(x, pa, bo, g1, wg, bg, wpb, wout, g2, wfi, wfo, gf)


SAMPLE_SEQS_PER_STEP = 8
TM_GMLP = 512
TM_FFN = 512


def _spatial_tiles(w_s, b_s, chunk):
    if chunk == GM_CHUNK:
        ws_t, b_pos = w_s[:, :chunk, :chunk], b_s[:, :chunk].T
    else:
        onehot = jnp.asarray(np.arange(GM_CHUNK)[:, None] % chunk == np.arange(chunk)[None, :], F32)
        hp = lax.Precision.HIGHEST
        ws_t = jnp.einsum("ri,gij,cj->grc", onehot, w_s[:, :chunk, :chunk], onehot, precision=hp)
        b_pos = jnp.dot(onehot, b_s[:, :chunk].T, precision=hp)
    bs_t = jnp.repeat(b_pos, GM_GROUP_W, axis=1)
    return ws_t, bs_t


def _gmlp_branch(xf, w, chunk, *, emit_v, emit_h, **side_jobs):
    ws_t, bs_t = _spatial_tiles(w["w_s"], w["b_s"], chunk)
    return _gmlp_call(xf, w["g1"], w["wuv"], w["lng"], w["lnb"], ws_t, bs_t, w["wpa"],
                      chunk=chunk, emit_v=emit_v, emit_h=emit_h,
                      tm=min(TM_GMLP, _n_tokens(xf)), **side_jobs)


def _merge_branch(xf, pa, bo, w):
    return _merge_ffn_call(xf, pa, bo, w["g1"], w["wg"], w["bg"], w["wpb"], w["wout"], w["g2"],
                           w["wfi"], w["wfo"], w["gf"], tm=TM_FFN)


def kernel(x_prompt, x_sample, state_conv, state_C, state_n, state_m, g_norm1, w_in, b_i, b_f, ln_g, ln_b, w_s, b_s, conv_w, conv_b, hn_g, b_gate, w_proj_a, w_proj_b, w_out, g_norm2, w_ffn_in, w_ffn_out, g_final):
    Bp, Tp, _ = x_prompt.shape
    Bs, Ts, _ = x_sample.shape
    win = w_in[0]
    c_ml = 2 * GM_WIDTH
    c_gate = c_ml + 4 * ML_WIDTH + 2 * ML_HEADS
    n_ml_blocks = -(-(ML_COLS) // TCAST_ROWS)
    win_t = jnp.transpose(win)
    w = dict(
        g1=g_norm1[0][None], g2=g_norm2[0][None], gf=g_final[None],
        wuv=_tcast_call(win_t, 0, c_ml // TCAST_ROWS),
        bg=b_gate[0].reshape(1, 2 * D_MODEL),
        lng=ln_g[0][None], lnb=ln_b[0][None], w_s=w_s[0], b_s=b_s[0],
        bif=jnp.pad(jnp.concatenate([b_i[0], b_f[0]]), (0, GATE_COLS - 2 * ML_HEADS))[None],
        cw=conv_w[0], cb=conv_b[0][None], hng=hn_g[0][None],
        wpa=w_proj_a[0].astype(BF16),
    )

    xpf = x_prompt.reshape(Bp * Tp, D_MODEL)
    pa_p, hp_p, w["wpb"], w["wout"], w["wfi"], w["wfo"], w["w_ml"], w["wg"] = _gmlp_branch(
        xpf, w, GM_CHUNK, emit_v=False, emit_h=True,
        cast_weights=(w_proj_b[0], w_out[0], w_ffn_in[0], w_ffn_out[0]),
        tcast_weight=win_t,
        tcast_jobs=((c_ml, n_ml_blocks, 0), (c_gate, 2 * D_MODEL // TCAST_ROWS, n_ml_blocks)))
    bo_p, conv_p, C_p, n_p, m_p = _mixer_ml_call(hp_p, Bp, w["w_ml"], w["bif"], w["cw"], w["cb"],
                                                 w["hng"])
    y_p = _merge_branch(xpf, pa_p, bo_p, w)

    pa_s, vn_s, zq = _gmlp_branch(x_sample, w, Ts, emit_v=True, emit_h=False, w_ml=w["w_ml"])
    st = (state_C[0], state_n[0], state_m[0])
    bo_s, conv_s, C_s, n_s, m_s = _mlstm_call(zq, jnp.transpose(state_conv[0], (1, 0, 2)), st, w["bif"],
                                              w["cw"], w["cb"], w["hng"], t_valid=Ts,
                                              nb=SAMPLE_SEQS_PER_STEP)
    y_s = _merge_branch(x_sample, pa_s, bo_s, w)

    return (y_p.reshape(Bp, Tp, D_MODEL), y_s,
            conv_p[None], C_p[None], n_p[None], m_p.reshape(1, Bp, ML_HEADS),
            jnp.transpose(conv_s, (1, 0, 2))[None], C_s[None], n_s[None], m_s[None],
            vn_s[None])
```

```python
import functools
import math

import jax
import jax.numpy as jnp
import numpy as np
from jax import lax
from jax.experimental import pallas as pl
from jax.experimental.pallas import tpu as pltpu

D_MODEL = 1024
GM_WIDTH = D_MODEL
GM_GROUPS = 4
GM_GROUP_W = GM_WIDTH // GM_GROUPS
GM_CHUNK = 128
ML_HEADS = 4
ML_HEAD_DIM = D_MODEL // ML_HEADS
ML_WIDTH = ML_HEADS * ML_HEAD_DIM
CONV_W = 4
D_FF = 2816
EPS = 1e-6

LANES = 128
SUBLANES = 8
BF16_ROWS = 16
GATE_COLS = LANES
ML_COLS = 4 * ML_WIDTH + GATE_COLS
VMEM_LIMIT = 56 * 1024 * 1024
GMLP_SUB_ROWS = 256
FFN_SUB_ROWS = 256

F32 = jnp.float32
BF16 = jnp.bfloat16
NEG_INF = float("-inf")
LN_INV_K_SCALE = 0.5 * math.log(ML_HEAD_DIM)
NT_DIMS = (((1,), (1,)), ((), ()))
TN_DIMS = (((0,), (0,)), ((), ()))


def _rms(x, g):
    return x * lax.rsqrt(jnp.mean(x * x, axis=-1, keepdims=True) + EPS) * g


def _gelu(x):
    return 0.5 * x * (1.0 + lax.erf(x * (2.0 ** -0.5)))


def _log_sigmoid(x):
    return jnp.minimum(x, 0.0) - jnp.log1p(jnp.exp(-jnp.abs(x)))


def _bdot(a, b):
    return jnp.dot(a, b, preferred_element_type=F32)


def _n_tokens(a):
    return a.shape[0] if len(a.shape) == 2 else a.shape[0] * a.shape[1]


def _load_rows(ref, rs):
    if len(ref.shape) == 2:
        return ref[rs, :]
    t = ref.shape[1]
    return ref[rs.start // t:rs.stop // t].reshape(rs.stop - rs.start, ref.shape[2])


def _store_rows(ref, rs, val):
    if len(ref.shape) == 2:
        ref[rs, :] = val
    else:
        t = ref.shape[1]
        ref[rs.start // t:rs.stop // t] = val.reshape((rs.stop - rs.start) // t, t, ref.shape[2])


def _token_spec(shape, tm):
    if len(shape) == 2:
        return pl.BlockSpec((tm, shape[1]), lambda i: (i, 0))
    return pl.BlockSpec((tm // shape[1], shape[1], shape[2]), lambda i: (i, 0, 0))


def _const_spec(shape):
    nd = len(shape)
    return pl.BlockSpec(shape, lambda *_: (0,) * nd, pipeline_mode=pl.Buffered(1))


def _gmlp_kernel(x_ref, g1_ref, wuv_ref, lng_ref, lnb_ref, ws_ref, bs_ref, wpa_ref, *rest,
                 chunk, emit_v, emit_h, emit_zq, n_cast, n_tcast):
    rest = list(rest)
    ml_weights = [rest.pop(0) for _ in range(3)] if emit_zq else None
    cast_src = [rest.pop(0) for _ in range(n_cast)]
    tcast_src = [rest.pop(0) for _ in range(n_tcast)]
    pa_ref = rest.pop(0)
    vn_ref = rest.pop(0) if emit_v else None
    hp_ref = rest.pop(0) if emit_h else None
    zq_ref = rest.pop(0) if emit_zq else None
    for src in cast_src:
        rest.pop(0)[...] = src[...].astype(BF16)
    for src in tcast_src:
        rest.pop(0)[...] = src[...].T.astype(BF16)
    a_sc = rest.pop(0)
    tm = _n_tokens(x_ref)
    blk = ws_ref.shape[1]
    sub = GMLP_SUB_ROWS
    subs = [slice(r0, r0 + sub) for r0 in range(0, tm, sub)]
    hb = []
    for rs in subs:
        hf = _rms(_load_rows(x_ref, rs), g1_ref[...])
        hb.append(hf.astype(BF16))
        if emit_h:
            hp_ref[rs, :] = hf
    zu = [_bdot(h, wuv_ref[:, :GM_WIDTH]) for h in hb]
    zv = [_bdot(h, wuv_ref[:, GM_WIDTH:]) for h in hb]
    if emit_zq:
        for h, rs in zip(hb, subs):
            for w_ref, c0 in zip(ml_weights, (0, 2 * ML_WIDTH, 4 * ML_WIDTH)):
                zq_ref[rs, c0:c0 + w_ref.shape[1]] = _bdot(h, w_ref[...])
    r = lax.broadcasted_iota(jnp.int32, (blk, blk), 0)
    c = lax.broadcasted_iota(jnp.int32, (blk, blk), 1)
    keep = c <= r
    if chunk < blk:
        sh = chunk.bit_length() - 1
        keep = jnp.logical_and(keep, (r >> sh) == (c >> sh))
    wsm = [jnp.where(keep, ws_ref[g], 0.0).astype(BF16) for g in range(GM_GROUPS)]
    for si, rs in enumerate(subs):
        u = _gelu(zu[si])
        v = _gelu(zv[si])
        mu = jnp.mean(v, axis=-1, keepdims=True)
        vc = v - mu
        var = jnp.mean(vc * vc, axis=-1, keepdims=True)
        vn = vc * lax.rsqrt(var + EPS) * lng_ref[...] + lnb_ref[...]
        if emit_v:
            _store_rows(vn_ref, rs, vn)
        vb = vn.astype(BF16)
        for g in range(GM_GROUPS):
            cs = slice(g * GM_GROUP_W, (g + 1) * GM_GROUP_W)
            for i in range(sub // blk):
                ls = slice(i * blk, (i + 1) * blk)
                s = _bdot(wsm[g], vb[ls, cs]) + bs_ref[:, cs]
                a_sc[rs.start + i * blk:rs.start + (i + 1) * blk, cs] = (u[ls, cs] * s).astype(BF16)
        pa_ref[rs, :] = _bdot(a_sc[rs, :], wpa_ref[...])


def _cast_block_spec(n_rows, n_cols, steps):
    n_blocks = steps
    while n_rows % n_blocks or (n_rows // n_blocks) % BF16_ROWS:
        n_blocks //= 2
    per = steps // n_blocks
    return pl.BlockSpec((n_rows // n_blocks, n_cols), lambda i: (i // per, 0))


TCAST_ROWS = 256


def _tcast_specs(job):
    first_row, n_blocks, first_step = job
    assert first_row % SUBLANES == 0
    blk = lambda i: jnp.clip(i - first_step, 0, n_blocks - 1)
    src = pl.BlockSpec((pl.Element(TCAST_ROWS), pl.Element(D_MODEL)),
                       lambda i: (pl.multiple_of(first_row + TCAST_ROWS * blk(i), SUBLANES), 0))
    dst = pl.BlockSpec((D_MODEL, TCAST_ROWS), lambda i: (0, blk(i)))
    return src, dst, jax.ShapeDtypeStruct((D_MODEL, TCAST_ROWS * n_blocks), BF16)


def _tcast_kernel(src_ref, dst_ref):
    dst_ref[...] = src_ref[...].T.astype(BF16)


def _tcast_call(wt, first_row, n_blocks):
    src, dst, shape = _tcast_specs((first_row, n_blocks, 0))
    return pl.pallas_call(
        _tcast_kernel, grid=(n_blocks,), in_specs=[src], out_specs=dst, out_shape=shape,
        compiler_params=pltpu.CompilerParams(dimension_semantics=("arbitrary",)),
        name="tcast",
    )(wt)


def _gmlp_call(x, g1, wuv, lng, lnb, ws_t, bs_t, wpa, *, chunk, emit_v, emit_h, tm, w_ml=None,
               cast_weights=(), tcast_weight=None, tcast_jobs=()):
    m = _n_tokens(x)
    blk = ws_t.shape[1]
    steps = m // tm
    assert steps & (steps - 1) == 0
    assert all(first_step + n_blocks <= steps for _, n_blocks, first_step in tcast_jobs)
    row = pl.BlockSpec((tm, D_MODEL), lambda i: (i, 0))
    out_shape = [jax.ShapeDtypeStruct((m, D_MODEL), F32)]
    out_specs = [row]
    scratch = [pltpu.VMEM((tm, GM_WIDTH), BF16)]
    cast_specs = [_cast_block_spec(cw.shape[0], cw.shape[1], steps) for cw in cast_weights]
    tcast = [_tcast_specs(job) for job in tcast_jobs]
    if emit_v:
        out_shape.append(jax.ShapeDtypeStruct(x.shape, F32))
        out_specs.append(_token_spec(x.shape, tm))
    if emit_h:
        out_shape.append(jax.ShapeDtypeStruct((m, D_MODEL), F32))
        out_specs.append(row)
    emit_zq = w_ml is not None
    if emit_zq:
        out_shape.append(jax.ShapeDtypeStruct((m, ML_COLS), F32))
        out_specs.append(pl.BlockSpec((tm, ML_COLS), lambda i: (i, 0)))
    out_shape += [jax.ShapeDtypeStruct(cw.shape, BF16) for cw in cast_weights]
    out_shape += [t[2] for t in tcast]
    out_specs += cast_specs + [t[1] for t in tcast]
    return pl.pallas_call(
        functools.partial(_gmlp_kernel, chunk=chunk, emit_v=emit_v, emit_h=emit_h, emit_zq=emit_zq,
                          n_cast=len(cast_weights), n_tcast=len(tcast)),
        grid=(steps,),
        in_specs=[_token_spec(x.shape, tm), _const_spec((1, D_MODEL)),
                  _const_spec((D_MODEL, 2 * GM_WIDTH)),
                  _const_spec((1, GM_WIDTH)), _const_spec((1, GM_WIDTH)),
                  _const_spec((GM_GROUPS, blk, blk)), _const_spec((blk, GM_WIDTH)),
                  _const_spec((GM_WIDTH, D_MODEL))]
        + (_ml_weight_specs() if emit_zq else []) + cast_specs + [t[0] for t in tcast],
        out_specs=out_specs,
        out_shape=out_shape,
        scratch_shapes=scratch,
        compiler_params=pltpu.CompilerParams(dimension_semantics=("arbitrary",),
                                             vmem_limit_bytes=VMEM_LIMIT),
        name="gmlp",
    )(x, g1, wuv, lng, lnb, ws_t, bs_t, wpa, *([w_ml] * 3 if emit_zq else []),
      *cast_weights, *([tcast_weight] * len(tcast)))


def _ml_weight_specs():
    wide = 2 * ML_WIDTH
    col_block = lambda width, idx: pl.BlockSpec((D_MODEL, width), lambda *_: (0, idx),
                                                pipeline_mode=pl.Buffered(1))
    return [col_block(wide, 0), col_block(wide, 1), col_block(GATE_COLS, 4 * ML_WIDTH // GATE_COLS)]


def _mlstm_heads(q_of, k_of, v_of, o_of, causal, ipre, bcum, groups, m0_of, C0_of, n0_of, hng_ref,
                 fill=None):
    n_groups = len(groups)
    single = n_groups == 1
    fill = fill or (lambda: None)

    def rows_of(x, g):
        return x if single else x[groups[g][0]:groups[g][0] + groups[g][1]]

    def per_row(vals):
        if single:
            return vals[0]
        return jnp.concatenate([jnp.broadcast_to(v, (groups[g][1], v.shape[1]))
                                for g, v in enumerate(vals)], axis=0)

    a = ipre - bcum
    a_t = a.T
    m_rows = per_row([m0_of(g) for g in range(n_groups)])

    def prepare(h):
        p = {}
        a2 = jnp.where(causal, a_t[h:h + 1, :], -jnp.inf)
        p["mc"] = mc = jnp.maximum(jnp.max(a2, axis=1, keepdims=True), m_rows[:, h:h + 1])
        p["m_last"] = m_last = [mc[grp[2]:grp[2] + 1, :] for grp in groups]
        p["dm"] = jnp.exp(a2 - (mc + LN_INV_K_SCALE))
        p["w_inter"] = jnp.exp(m_rows[:, h:h + 1] - mc)
        p["w_col"] = w_col = jnp.exp(a[:, h:h + 1] - (per_row(m_last) + LN_INV_K_SCALE))
        p["decay"] = [jnp.exp(m0_of(g)[:, h:h + 1] - m_last[g]) for g in range(n_groups)]
        p["q"] = q = q_of(h)
        p["k"] = k = k_of(h)
        v = v_of(h)
        p["qb"], p["kb"], p["vb"] = q.astype(BF16), k.astype(BF16), v.astype(BF16)
        p["vw"] = (v * w_col).astype(BF16)
        p["c_old"] = [C0_of(g, h) for g in range(n_groups)]
        p["n_old"] = [n0_of(g, h) for g in range(n_groups)]
        return p

    def first_matmuls(p):
        p["qk"] = lax.dot_general(p["qb"], p["kb"], NT_DIMS, preferred_element_type=F32)
        p["qc"] = [lax.dot_general(rows_of(p["qb"], g), p["c_old"][g].astype(BF16), NT_DIMS,
                                   preferred_element_type=F32) for g in range(n_groups)]
        if single:
            n_rows = jnp.broadcast_to(p["n_old"][0], (LANES, ML_HEAD_DIM)).astype(BF16)
            p["qn"] = lax.dot_general(p["qb"], n_rows, NT_DIMS, preferred_element_type=F32)[:, 0:1]
        else:
            p["qn"] = jnp.sum(p["q"] * per_row(p["n_old"]), axis=1, keepdims=True)

    def second_matmuls(p):
        p["s"] = s = p["dm"] * p["qk"]
        p["sv"] = _bdot(s.astype(BF16), p["vb"])
        p["cupd"] = [lax.dot_general(rows_of(p["vw"], g), rows_of(p["kb"], g), TN_DIMS,
                                     preferred_element_type=F32) for g in range(n_groups)]

    def finish(h, p):
        qc_rows = p["qc"][0] if single else jnp.concatenate(p["qc"], axis=0)
        num = p["w_inter"] * qc_rows + p["sv"]
        den = p["w_inter"] * p["qn"] + jnp.sum(p["s"], axis=1, keepdims=True)
        hcur = num / jnp.maximum(jnp.abs(den), jnp.exp(-(bcum[:, h:h + 1] + p["mc"])))
        mu = jnp.mean(hcur, axis=1, keepdims=True)
        hc = hcur - mu
        var = jnp.mean(hc * hc, axis=1, keepdims=True)
        hs = slice(h * ML_HEAD_DIM, (h + 1) * ML_HEAD_DIM)
        out = jax.nn.sigmoid(o_of(h)) * (hc * lax.rsqrt(var + EPS) * hng_ref[:, hs])
        kw = p["k"] * p["w_col"]
        c_new = [p["decay"][g] * p["c_old"][g] + p["cupd"][g] for g in range(n_groups)]
        n_new = [p["decay"][g] * p["n_old"][g] + jnp.sum(rows_of(kw, g), axis=0, keepdims=True)
                 for g in range(n_groups)]
        return out, c_new, n_new

    H = ML_HEADS
    per_head = [None] * H
    done = [None] * H
    if single:
        per_head[0] = prepare(0)
        fill()
        first_matmuls(per_head[0])
        if H > 1:
            per_head[1] = prepare(1)
        fill()
        for h in range(H):
            second_matmuls(per_head[h])
            if h + 1 < H:
                first_matmuls(per_head[h + 1])
            fill()
            if h + 2 < H:
                per_head[h + 2] = prepare(h + 2)
            done[h] = finish(h, per_head[h])
            fill()
    else:
        per_head = [prepare(h) for h in range(H)]
        for stage in (first_matmuls, second_matmuls):
            for h in range(H):
                stage(per_head[h])
        done = [finish(h, per_head[h]) for h in range(H)]

    outs = [done[h][0] for h in range(H)]
    C_new = [[done[h][1][g] for h in range(H)] for g in range(n_groups)]
    n_new = [[done[h][2][g] for h in range(H)] for g in range(n_groups)]
    m_new = []
    for g, grp in enumerate(groups):
        row = m0_of(g)
        lane = lax.broadcasted_iota(jnp.int32, row.shape, 1)
        for h in range(H):
            row = jnp.where(lane == h, bcum[grp[2]:grp[2] + 1, h:h + 1] + per_head[h]["m_last"][g], row)
        m_new.append(row)
    return outs, C_new, n_new, m_new


def _mlstm_kernel(zq_ref, cst_ref, C0_ref, n0_ref, m0_ref, bif_ref, cw_ref, cb_ref, hng_ref,
                  bo_ref, conv_ref, C_ref, n_ref, m_ref, zp, xp, *, t_valid):
    nb = C0_ref.shape[0]
    L = BF16_ROWS
    R = nb * L

    @pl.when(pl.program_id(0) == 0)
    def _():
        zp[...] = jnp.zeros(zp.shape, F32)

    for bb in range(nb):
        zp[bb, 0:t_valid, :] = zq_ref[bb * t_valid:(bb + 1) * t_valid, :]
        for j in range(CONV_W - 1):
            row = SUBLANES - (CONV_W - 1) + j
            xp[bb, row:row + 1, :] = cst_ref[j, bb:bb + 1, :]
        xp[bb, SUBLANES:2 * SUBLANES, :] = zp[bb, 0:SUBLANES, 0:2 * ML_WIDTH]
        for j in range(CONV_W - 1):
            row = SUBLANES + t_valid - (CONV_W - 1) + j
            conv_ref[j, bb:bb + 1, :] = xp[bb, row:row + 1, :]
    qk_rows = []
    for bb in range(nb):
        acc = cb_ref[...]
        for j in range(CONV_W):
            off = SUBLANES - (CONV_W - 1) + j
            acc = acc + cw_ref[j:j + 1, :] * xp[bb, off:off + SUBLANES, :]
        qk_rows += [acc, jnp.zeros((L - SUBLANES, 2 * ML_WIDTH), F32)]
    qk = jnp.concatenate(qk_rows, axis=0)
    qk = qk * jax.nn.sigmoid(qk)

    def cols(c0, width):
        return zp[:, :, c0:c0 + width].reshape(R, width)

    zif = cols(4 * ML_WIDTH, GATE_COLS) + bif_ref[...]
    live = (lax.broadcasted_iota(jnp.int32, (R, GATE_COLS), 0) & (L - 1)) < t_valid
    ipre = jnp.where(live, zif, NEG_INF)
    logf = jnp.where(live, pltpu.roll(_log_sigmoid(zif), GATE_COLS - ML_HEADS, axis=1), 0.0)
    assert t_valid <= SUBLANES
    sub_g = lax.broadcasted_iota(jnp.int32, (SUBLANES, GATE_COLS), 0)
    parts = []
    for bb in range(nb):
        incl = logf[bb * L:bb * L + SUBLANES, :]
        for step in (1, 2, 4):
            incl = incl + jnp.where(sub_g >= step, pltpu.roll(incl, step, axis=0), 0.0)
        parts += [incl, jnp.broadcast_to(incl[SUBLANES - 1:SUBLANES, :], (L - SUBLANES, GATE_COLS))]
    bcum = jnp.concatenate(parts, axis=0)
    r = lax.broadcasted_iota(jnp.int32, (R, R), 0)
    c = lax.broadcasted_iota(jnp.int32, (R, R), 1)
    sh = L.bit_length() - 1
    causal = jnp.logical_and(c <= r, (r >> sh) == (c >> sh))
    live_w = (lax.broadcasted_iota(jnp.int32, (R, ML_HEAD_DIM), 0) & (L - 1)) < t_valid

    def head_cols(x, base, h):
        return x[:, base + h * ML_HEAD_DIM:base + (h + 1) * ML_HEAD_DIM]

    groups = [(bb * L, L, bb * L + L - 1) for bb in range(nb)]
    outs, C_new, n_new, m_new = _mlstm_heads(
        q_of=lambda h: head_cols(qk, 0, h),
        k_of=lambda h: jnp.where(live_w, head_cols(qk, ML_WIDTH, h), 0.0),
        v_of=lambda h: jnp.where(live_w, cols(2 * ML_WIDTH + h * ML_HEAD_DIM, ML_HEAD_DIM), 0.0),
        o_of=lambda h: cols(3 * ML_WIDTH + h * ML_HEAD_DIM, ML_HEAD_DIM),
        causal=causal, ipre=ipre, bcum=bcum, groups=groups,
        m0_of=lambda g: m0_ref[g:g + 1, :], C0_of=lambda g, h: C0_ref[g, h],
        n0_of=lambda g, h: n0_ref[g, h:h + 1, :], hng_ref=hng_ref)
    for bb in range(nb):
        for h in range(ML_HEADS):
            hs = slice(h * ML_HEAD_DIM, (h + 1) * ML_HEAD_DIM)
            bo_ref[bb * t_valid:(bb + 1) * t_valid, hs] = outs[h][bb * L:bb * L + t_valid, :]
            C_ref[bb, h] = C_new[bb][h]
            n_ref[bb, h:h + 1, :] = n_new[bb][h]
        m_ref[bb:bb + 1, :] = m_new[bb]


def _mlstm_call(zq, cst, state, bif, cw, cb, hng, *, t_valid, nb):
    T = t_valid
    B = zq.shape[0] // T
    kern = functools.partial(_mlstm_kernel, t_valid=T)
    per_b = lambda shape: pl.BlockSpec((nb,) + shape, lambda b: (b,) + (0,) * len(shape))
    rows = lambda width: pl.BlockSpec((nb * T, width), lambda b: (b, 0))
    conv_spec = pl.BlockSpec((CONV_W - 1, nb, 2 * ML_WIDTH), lambda b: (0, b, 0))
    st_specs = [per_b((ML_HEADS, ML_HEAD_DIM, ML_HEAD_DIM)), per_b((ML_HEADS, ML_HEAD_DIM)),
                per_b((ML_HEADS,))]
    in_specs = ([rows(ML_COLS), conv_spec] + st_specs
                + [_const_spec((1, GATE_COLS)), _const_spec((CONV_W, 2 * ML_WIDTH)),
                   _const_spec((1, 2 * ML_WIDTH)), _const_spec((1, ML_WIDTH))])
    out_shape = [jax.ShapeDtypeStruct((B * T, ML_WIDTH), F32),
                 jax.ShapeDtypeStruct((CONV_W - 1, B, 2 * ML_WIDTH), F32),
                 jax.ShapeDtypeStruct((B, ML_HEADS, ML_HEAD_DIM, ML_HEAD_DIM), F32),
                 jax.ShapeDtypeStruct((B, ML_HEADS, ML_HEAD_DIM), F32),
                 jax.ShapeDtypeStruct((B, ML_HEADS), F32)]
    out_specs = [rows(ML_WIDTH), conv_spec] + st_specs
    return pl.pallas_call(
        kern,
        grid=(B // nb,),
        in_specs=in_specs,
        out_specs=out_specs,
        out_shape=out_shape,
        scratch_shapes=[pltpu.VMEM((nb, BF16_ROWS, ML_COLS), F32),
                        pltpu.VMEM((nb, 2 * SUBLANES, 2 * ML_WIDTH), F32)],
        compiler_params=pltpu.CompilerParams(dimension_semantics=("arbitrary",),
                                             vmem_limit_bytes=VMEM_LIMIT),
        name="mlstm",
    )(zq, cst, *state, bif, cw, cb, hng)


PL = 256
PG = PL // SUBLANES
TAIL = (CONV_W - 1) * SUBLANES
PROJ_PIECE_COLS = 512
DMA_PRIORITIES = 2


def _mixer_ml_kernel(h_hbm, wqk_ref, wvo_ref, wif_ref, bif_ref, cw_ref, cb_ref, hng_ref,
                     bo_hbm, conv_ref, C_ref, n_ref, m_ref,
                     zqk_buf, zvo_buf, zif_buf, tail, hbuf, obuf, sem_in, sem_out,
                     *, chunks_per_seq, n_chunks):
    s = pl.program_id(0)

    def chunk_dmas(chunk, slot, inbound):
        row0 = chunk * PL
        copies = []
        for i in range(SUBLANES):
            hbm = (h_hbm if inbound else bo_hbm).at[pl.ds(row0 + PG * i, PG), :]
            vmem = (hbuf if inbound else obuf).at[slot, :, i, :]
            sem = (sem_in if inbound else sem_out).at[slot]
            copies.append(pltpu.make_async_copy(hbm, vmem, sem) if inbound
                          else pltpu.make_async_copy(vmem, hbm, sem))
        return copies

    def start_all(copies):
        for i, cp in enumerate(copies):
            cp.start(priority=i % DMA_PRIORITIES)

    @pl.when(s == 0)
    def _():
        zqk_buf[1] = jnp.zeros(zqk_buf.shape[1:], F32)
        zvo_buf[1] = jnp.zeros(zvo_buf.shape[1:], F32)
        zif_buf[1] = jnp.zeros(zif_buf.shape[1:], F32)
        start_all(chunk_dmas(0, 0, True))

    @pl.when(jnp.maximum(s - 1, 0) % chunks_per_seq == 0)
    def _():
        tail[...] = jnp.zeros(tail.shape, F32)
        C_ref[...] = jnp.zeros(C_ref.shape, F32)
        n_ref[...] = jnp.zeros(n_ref.shape, F32)
        m_ref[...] = jnp.zeros(m_ref.shape, F32)

    def step(rd, wr):
        start_all(chunk_dmas(jnp.minimum(s + 1, n_chunks - 1), rd, True))
        for cp in chunk_dmas(jnp.minimum(s, n_chunks - 1), wr, True):
            cp.wait()

        _mixer_ml_step(hbuf.at[wr], wqk_ref, wvo_ref, wif_ref, bif_ref, cw_ref, cb_ref, hng_ref,
                       obuf.at[rd], conv_ref, C_ref, n_ref, m_ref, zqk_buf, zvo_buf, zif_buf, tail,
                       rd=rd, wr=wr)

        @pl.when(s >= 1)
        def _():
            start_all(chunk_dmas(s - 1, rd, False))

        @pl.when(s >= 2)
        def _():
            for cp in chunk_dmas(s - 2, wr, False):
                cp.wait()

        @pl.when(s == n_chunks)
        def _():
            for cp in chunk_dmas(n_chunks - 1, rd, True) + chunk_dmas(s - 1, rd, False):
                cp.wait()

    @pl.when(s % 2 == 0)
    def _():
        step(rd=1, wr=0)

    @pl.when(s % 2 == 1)
    def _():
        step(rd=0, wr=1)


def _mixer_ml_step(h_ref, wqk_ref, wvo_ref, wif_ref, bif_ref, cw_ref, cb_ref, hng_ref,
                   bo_ref, conv_ref, C_ref, n_ref, m_ref, zqk_buf, zvo_buf, zif_buf, tail,
                   *, rd, wr):
    h = h_ref[...].reshape(PL, D_MODEL).astype(BF16)

    def piece(w_ref, buf, c0, width):
        def emit():
            buf[wr, :, c0:c0 + width] = _bdot(h, w_ref[:, c0:c0 + width])
        return emit

    pieces = [piece(w_ref, buf, c0, PROJ_PIECE_COLS)
              for w_ref, buf in ((wqk_ref, zqk_buf), (wvo_ref, zvo_buf))
              for c0 in range(0, 2 * ML_WIDTH, PROJ_PIECE_COLS)]
    pieces.append(piece(wif_ref, zif_buf, 0, GATE_COLS))
    pieces = iter(pieces)

    def fill():
        emit = next(pieces, None)
        if emit is not None:
            emit()

    fill()
    zqk = zqk_buf.at[rd]
    zvo = zvo_buf.at[rd]
    zif = zif_buf[rd] + bif_ref[...]

    zqk_tail = zqk[PL - TAIL:, :]
    sub = lax.broadcasted_iota(jnp.int32, (SUBLANES, 2 * ML_WIDTH), 0)
    wrapped = []
    for g in range(CONV_W - 1):
        cur = pltpu.roll(zqk_tail[g * SUBLANES:(g + 1) * SUBLANES], 1, axis=0)
        prev = pltpu.roll(tail[g * SUBLANES:(g + 1) * SUBLANES, :], 1, axis=0)
        wrapped.append(jnp.where(sub == 0, prev, cur))
    wrapped = jnp.concatenate(wrapped, axis=0)
    tail[...] = zqk_tail
    conv_ref[...] = jnp.concatenate(
        [zqk_tail[g * SUBLANES + SUBLANES - 1:(g + 1) * SUBLANES, :] for g in range(CONV_W - 1)], axis=0)

    def conv_silu(c0, width):
        cs = slice(c0, c0 + width)
        acc = cb_ref[:, cs] + cw_ref[CONV_W - 1:CONV_W, cs] * zqk[:, cs]
        for d in range(1, CONV_W):
            shifted = jnp.concatenate(
                [wrapped[TAIL - d * SUBLANES:, cs], zqk[:PL - d * SUBLANES, cs]], axis=0)
            acc = acc + cw_ref[CONV_W - 1 - d:CONV_W - d, cs] * shifted
        return acc * jax.nn.sigmoid(acc)

    pr = lax.broadcasted_iota(jnp.int32, (PL, PL), 0)
    pc = lax.broadcasted_iota(jnp.int32, (PL, PL), 1)
    bits, low = SUBLANES.bit_length() - 1, SUBLANES - 1
    causal = ((pc >> bits) + PG * (pc & low)) <= ((pr >> bits) + PG * (pr & low))

    logf = pltpu.roll(_log_sigmoid(zif), GATE_COLS - ML_HEADS, axis=1)
    run, partial = None, []
    for n in range(PG):
        blk = logf[n * SUBLANES:(n + 1) * SUBLANES, :]
        run = blk if run is None else run + blk
        partial.append(run)
    sub_g = lax.broadcasted_iota(jnp.int32, (SUBLANES, GATE_COLS), 0)
    incl = run
    for step in (1, 2, 4):
        incl = incl + jnp.where(sub_g >= step, pltpu.roll(incl, step, axis=0), 0.0)
    earlier = incl - run
    bcum = jnp.concatenate([p + earlier for p in partial], axis=0)

    def head_cols(base, h):
        return zvo[:, base + h * ML_HEAD_DIM:base + (h + 1) * ML_HEAD_DIM]

    outs, C_new, n_new, m_new = _mlstm_heads(
        q_of=lambda h: conv_silu(h * ML_HEAD_DIM, ML_HEAD_DIM),
        k_of=lambda h: conv_silu(ML_WIDTH + h * ML_HEAD_DIM, ML_HEAD_DIM),
        v_of=lambda h: head_cols(0, h),
        o_of=lambda h: head_cols(ML_WIDTH, h),
        causal=causal, ipre=zif, bcum=bcum, groups=[(0, PL, PL - 1)],
        m0_of=lambda g: m_ref[...], C0_of=lambda g, h: C_ref[h], n0_of=lambda g, h: n_ref[h:h + 1, :],
        hng_ref=hng_ref, fill=fill)
    assert next(pieces, None) is None, "projection pieces left over"
    for h in range(ML_HEADS):
        bo_ref[:, :, h * ML_HEAD_DIM:(h + 1) * ML_HEAD_DIM] = outs[h].reshape(PG, SUBLANES, ML_HEAD_DIM)
        C_ref[h] = C_new[0][h]
        n_ref[h:h + 1, :] = n_new[0][h]
    m_ref[...] = m_new[0]


def _mixer_ml_call(hp, n_seq, w_all, bif, cw, cb, hng):
    B = n_seq
    T = hp.shape[0] // B
    cps = T // PL
    n_chunks = B * cps
    math = lambda s: jnp.maximum(s - 1, 0)
    per_b = lambda shape: pl.BlockSpec((None,) + shape,
                                       lambda s: (math(s) // cps,) + (0,) * len(shape))
    out_shape = [jax.ShapeDtypeStruct((B * T, ML_WIDTH), F32),
                 jax.ShapeDtypeStruct((B, CONV_W - 1, 2 * ML_WIDTH), F32),
                 jax.ShapeDtypeStruct((B, ML_HEADS, ML_HEAD_DIM, ML_HEAD_DIM), F32),
                 jax.ShapeDtypeStruct((B, ML_HEADS, ML_HEAD_DIM), F32),
                 jax.ShapeDtypeStruct((B, 1, ML_HEADS), F32)]
    out_specs = [pl.BlockSpec(memory_space=pl.ANY),
                 per_b((CONV_W - 1, 2 * ML_WIDTH)),
                 per_b((ML_HEADS, ML_HEAD_DIM, ML_HEAD_DIM)), per_b((ML_HEADS, ML_HEAD_DIM)),
                 per_b((1, ML_HEADS))]
    return pl.pallas_call(
        functools.partial(_mixer_ml_kernel, chunks_per_seq=cps, n_chunks=n_chunks),
        grid=(n_chunks + 1,),
        in_specs=[pl.BlockSpec(memory_space=pl.ANY)]
        + _ml_weight_specs()
        + [_const_spec((1, GATE_COLS)), _const_spec((CONV_W, 2 * ML_WIDTH)),
           _const_spec((1, 2 * ML_WIDTH)), _const_spec((1, ML_WIDTH))],
        out_specs=out_specs,
        out_shape=out_shape,
        scratch_shapes=[pltpu.VMEM((2, PL, 2 * ML_WIDTH), F32),
                        pltpu.VMEM((2, PL, 2 * ML_WIDTH), F32),
                        pltpu.VMEM((2, PL, GATE_COLS), F32),
                        pltpu.VMEM((TAIL, 2 * ML_WIDTH), F32),
                        pltpu.VMEM((2, PG, SUBLANES, D_MODEL), F32),
                        pltpu.VMEM((2, PG, SUBLANES, ML_WIDTH), F32),
                        pltpu.SemaphoreType.DMA((2,)),
                        pltpu.SemaphoreType.DMA((2,))],
        compiler_params=pltpu.CompilerParams(dimension_semantics=("arbitrary",),
                                             vmem_limit_bytes=VMEM_LIMIT),
        name="mixer_ml",
    )(hp, w_all, w_all, w_all, bif, cw, cb, hng)


def _merge_ffn_kernel(x_ref, pa_ref, bo_ref, g1_ref, wg_ref, bg_ref, wpb_ref, wout_ref,
                      g2_ref, wfi_hbm, wfo_hbm, gf_ref, y_ref, wfi_ref, wfo_ref, wsem):
    ffn_weight_copies = [pltpu.make_async_copy(wfi_hbm, wfi_ref, wsem.at[0]),
                         pltpu.make_async_copy(wfo_hbm, wfo_ref, wsem.at[1])]

    def body(fetch_ffn_weights):
        if fetch_ffn_weights:
            for cp in ffn_weight_copies:
                cp.start()
        tm = _n_tokens(x_ref)
        subs = [slice(r0, r0 + FFN_SUB_ROWS) for r0 in range(0, tm, FFN_SUB_ROWS)]
        x = [_load_rows(x_ref, rs) for rs in subs]
        h = [_rms(xi, g1_ref[...]).astype(BF16) for xi in x]
        gab = [_bdot(hi, wg_ref[...]) + bg_ref[...] for hi in h]
        pb = [_bdot(bo_ref[rs, :].astype(BF16), wpb_ref[...]) for rs in subs]
        merged = [(jax.nn.sigmoid(g[:, :D_MODEL]) * pa_ref[rs, :]
                   + jax.nn.sigmoid(g[:, D_MODEL:]) * p).astype(BF16)
                  for g, p, rs in zip(gab, pb, subs)]
        x1 = [xi + _bdot(mi, wout_ref[...]) for xi, mi in zip(x, merged)]
        h2 = [_rms(xi, g2_ref[...]).astype(BF16) for xi in x1]
        if fetch_ffn_weights:
            for cp in ffn_weight_copies:
                cp.wait()
        gu = [_bdot(hi, wfi_ref[...]) for hi in h2]
        hid = [(g[:, :D_FF] * jax.nn.sigmoid(g[:, :D_FF]) * g[:, D_FF:]).astype(BF16) for g in gu]
        x2 = [xi + _bdot(hi, wfo_ref[...]) for xi, hi in zip(x1, hid)]
        for rs, xi in zip(subs, x2):
            _store_rows(y_ref, rs, _rms(xi, gf_ref[...]))

    first = pl.program_id(0) == 0
    pl.when(first)(functools.partial(body, True))
    pl.when(jnp.logical_not(first))(functools.partial(body, False))


def _merge_ffn_call(x, pa, bo, g1, wg, bg, wpb, wout, g2, wfi, wfo, gf, *, tm):
    m = _n_tokens(x)
    row = pl.BlockSpec((tm, D_MODEL), lambda i: (i, 0))
    x_spec = _token_spec(x.shape, tm)
    hbm = pl.BlockSpec(memory_space=pl.ANY)
    return pl.pallas_call(
        _merge_ffn_kernel,
        grid=(m // tm,),
        in_specs=[x_spec, row, row, _const_spec((1, D_MODEL)), _const_spec((D_MODEL, 2 * D_MODEL)),
                  _const_spec((1, 2 * D_MODEL)), _const_spec((ML_WIDTH, D_MODEL)),
                  _const_spec((D_MODEL, D_MODEL)), _const_spec((1, D_MODEL)),
                  hbm, hbm, _const_spec((1, D_MODEL))],
        out_specs=x_spec,
        out_shape=jax.ShapeDtypeStruct(x.shape, F32),
        scratch_shapes=[pltpu.VMEM(wfi.shape, BF16), pltpu.VMEM(wfo.shape, BF16),
                        pltpu.SemaphoreType.DMA((2,))],
        compiler_params=pltpu.CompilerParams(dimension_semantics=("arbitrary",),
                                             vmem_limit_bytes=VMEM_LIMIT),
        name="merge_ffn",
    )(x, pa, bo, g1, wg, bg, wpb, wout, g2, wfi, wfo, gf)


SAMPLE_SEQS_PER_STEP = 8
TM_GMLP = 512
TM_FFN = 512


def _spatial_tiles(w_s, b_s, chunk):
    if chunk == GM_CHUNK:
        ws_t, b_pos = w_s[:, :chunk, :chunk], b_s[:, :chunk].T
    else:
        onehot = jnp.asarray(np.arange(GM_CHUNK)[:, None] % chunk == np.arange(chunk)[None, :], F32)
        hp = lax.Precision.HIGHEST
        ws_t = jnp.einsum("ri,gij,cj->grc", onehot, w_s[:, :chunk, :chunk], onehot, precision=hp)
        b_pos = jnp.dot(onehot, b_s[:, :chunk].T, precision=hp)
    bs_t = jnp.repeat(b_pos, GM_GROUP_W, axis=1)
    return ws_t, bs_t


def _gmlp_branch(xf, w, chunk, *, emit_v, emit_h, **side_jobs):
    ws_t, bs_t = _spatial_tiles(w["w_s"], w["b_s"], chunk)
    return _gmlp_call(xf, w["g1"], w["wuv"], w["lng"], w["lnb"], ws_t, bs_t, w["wpa"],
                      chunk=chunk, emit_v=emit_v, emit_h=emit_h,
                      tm=min(TM_GMLP, _n_tokens(xf)), **side_jobs)


def _merge_branch(xf, pa, bo, w):
    return _merge_ffn_call(xf, pa, bo, w["g1"], w["wg"], w["bg"], w["wpb"], w["wout"], w["g2"],
                           w["wfi"], w["wfo"], w["gf"], tm=TM_FFN)


def kernel(x_prompt, x_sample, state_conv, state_C, state_n, state_m, g_norm1, w_in, b_i, b_f, ln_g, ln_b, w_s, b_s, conv_w, conv_b, hn_g, b_gate, w_proj_a, w_proj_b, w_out, g_norm2, w_ffn_in, w_ffn_out, g_final):
    Bp, Tp, _ = x_prompt.shape
    Bs, Ts, _ = x_sample.shape
    win = w_in[0]
    c_ml = 2 * GM_WIDTH
    c_gate = c_ml + 4 * ML_WIDTH + 2 * ML_HEADS
    n_ml_blocks = -(-(ML_COLS) // TCAST_ROWS)
    win_t = jnp.transpose(win)
    w = dict(
        g1=g_norm1[0][None], g2=g_norm2[0][None], gf=g_final[None],
        wuv=_tcast_call(win_t, 0, c_ml // TCAST_ROWS),
        bg=b_gate[0].reshape(1, 2 * D_MODEL),
        lng=ln_g[0][None], lnb=ln_b[0][None], w_s=w_s[0], b_s=b_s[0],
        bif=jnp.pad(jnp.concatenate([b_i[0], b_f[0]]), (0, GATE_COLS - 2 * ML_HEADS))[None],
        cw=conv_w[0], cb=conv_b[0][None], hng=hn_g[0][None],
        wpa=w_proj_a[0].astype(BF16),
    )

    xpf = x_prompt.reshape(Bp * Tp, D_MODEL)
    pa_p, hp_p, w["wpb"], w["wout"], w["wfi"], w["wfo"], w["w_ml"], w["wg"] = _gmlp_branch(
        xpf, w, GM_CHUNK, emit_v=False, emit_h=True,
        cast_weights=(w_proj_b[0], w_out[0], w_ffn_in[0], w_ffn_out[0]),
        tcast_weight=win_t,
        tcast_jobs=((c_ml, n_ml_blocks, 0), (c_gate, 2 * D_MODEL // TCAST_ROWS, n_ml_blocks)))
    bo_p, conv_p, C_p, n_p, m_p = _mixer_ml_call(hp_p, Bp, w["w_ml"], w["bif"], w["cw"], w["cb"],
                                                 w["hng"])
    y_p = _merge_branch(xpf, pa_p, bo_p, w)

    pa_s, vn_s, zq = _gmlp_branch(x_sample, w, Ts, emit_v=True, emit_h=False, w_ml=w["w_ml"])
    st = (state_C[0], state_n[0], state_m[0])
    bo_s, conv_s, C_s, n_s, m_s = _mlstm_call(zq, jnp.transpose(state_conv[0], (1, 0, 2)), st, w["bif"],
                                              w["cw"], w["cb"], w["hng"], t_valid=Ts,
                                              nb=SAMPLE_SEQS_PER_STEP)
    y_s = _merge_branch(x_sample, pa_s, bo_s, w)

    return (y_p.reshape(Bp, Tp, D_MODEL), y_s,
            conv_p[None], C_p[None], n_p[None], m_p.reshape(1, Bp, ML_HEADS),
            jnp.transpose(conv_s, (1, 0, 2))[None], C_s[None], n_s[None], m_s[None],
            vn_s[None])
```

```python
import functools
import math

import jax
import jax.numpy as jnp
import numpy as np
from jax import lax
from jax.experimental import pallas as pl
from jax.experimental.pallas import tpu as pltpu

D_MODEL = 1024
GM_WIDTH = D_MODEL
GM_GROUPS = 4
GM_GROUP_W = GM_WIDTH // GM_GROUPS
GM_CHUNK = 128
ML_HEADS = 4
ML_HEAD_DIM = D_MODEL // ML_HEADS
ML_WIDTH = ML_HEADS * ML_HEAD_DIM
CONV_W = 4
D_FF = 2816
EPS = 1e-6

LANES = 128
SUBLANES = 8
BF16_ROWS = 16
GATE_COLS = LANES
ML_COLS = 4 * ML_WIDTH + GATE_COLS
VMEM_LIMIT = 56 * 1024 * 1024
GMLP_SUB_ROWS = 256
FFN_SUB_ROWS = 256

F32 = jnp.float32
BF16 = jnp.bfloat16
NEG_INF = float("-inf")
LN_INV_K_SCALE = 0.5 * math.log(ML_HEAD_DIM)
NT_DIMS = (((1,), (1,)), ((), ()))
TN_DIMS = (((0,), (0,)), ((), ()))


def _rms(x, g):
    return x * lax.rsqrt(jnp.mean(x * x, axis=-1, keepdims=True) + EPS) * g


def _gelu(x):
    return 0.5 * x * (1.0 + lax.erf(x * (2.0 ** -0.5)))


def _log_sigmoid(x):
    return jnp.minimum(x, 0.0) - jnp.log1p(jnp.exp(-jnp.abs(x)))


def _bdot(a, b):
    return jnp.dot(a, b, preferred_element_type=F32)


def _n_tokens(a):
    return a.shape[0] if len(a.shape) == 2 else a.shape[0] * a.shape[1]


def _load_rows(ref, rs):
    if len(ref.shape) == 2:
        return ref[rs, :]
    t = ref.shape[1]
    return ref[rs.start // t:rs.stop // t].reshape(rs.stop - rs.start, ref.shape[2])


def _store_rows(ref, rs, val):
    if len(ref.shape) == 2:
        ref[rs, :] = val
    else:
        t = ref.shape[1]
        ref[rs.start // t:rs.stop // t] = val.reshape((rs.stop - rs.start) // t, t, ref.shape[2])


def _token_spec(shape, tm):
    if len(shape) == 2:
        return pl.BlockSpec((tm, shape[1]), lambda i: (i, 0))
    return pl.BlockSpec((tm // shape[1], shape[1], shape[2]), lambda i: (i, 0, 0))


def _const_spec(shape):
    nd = len(shape)
    return pl.BlockSpec(shape, lambda *_: (0,) * nd, pipeline_mode=pl.Buffered(1))


def _gmlp_kernel(x_ref, g1_ref, wuv_ref, lng_ref, lnb_ref, ws_ref, bs_ref, wpa_ref, *rest,
                 chunk, emit_v, emit_h, emit_zq, n_cast, n_tcast):
    rest = list(rest)
    ml_weights = [rest.pop(0) for _ in range(3)] if emit_zq else None
    cast_src = [rest.pop(0) for _ in range(n_cast)]
    tcast_src = [rest.pop(0) for _ in range(n_tcast)]
    pa_ref = rest.pop(0)
    vn_ref = rest.pop(0) if emit_v else None
    hp_ref = rest.pop(0) if emit_h else None
    zq_ref = rest.pop(0) if emit_zq else None
    for src in cast_src:
        rest.pop(0)[...] = src[...].astype(BF16)
    for src in tcast_src:
        rest.pop(0)[...] = src[...].T.astype(BF16)
    a_sc = rest.pop(0)
    tm = _n_tokens(x_ref)
    blk = ws_ref.shape[1]
    sub = GMLP_SUB_ROWS
    subs = [slice(r0, r0 + sub) for r0 in range(0, tm, sub)]
    hb = []
    for rs in subs:
        hf = _rms(_load_rows(x_ref, rs), g1_ref[...])
        hb.append(hf.astype(BF16))
        if emit_h:
            hp_ref[rs, :] = hf
    zu = [_bdot(h, wuv_ref[:, :GM_WIDTH]) for h in hb]
    zv = [_bdot(h, wuv_ref[:, GM_WIDTH:]) for h in hb]
    if emit_zq:
        for h, rs in zip(hb, subs):
            for w_ref, c0 in zip(ml_weights, (0, 2 * ML_WIDTH, 4 * ML_WIDTH)):
                zq_ref[rs, c0:c0 + w_ref.shape[1]] = _bdot(h, w_ref[...])
    r = lax.broadcasted_iota(jnp.int32, (blk, blk), 0)
    c = lax.broadcasted_iota(jnp.int32, (blk, blk), 1)
    keep = c <= r
    if chunk < blk:
        sh = chunk.bit_length() - 1
        keep = jnp.logical_and(keep, (r >> sh) == (c >> sh))
    wsm = [jnp.where(keep, ws_ref[g], 0.0).astype(BF16) for g in range(GM_GROUPS)]
    for si, rs in enumerate(subs):
        u = _gelu(zu[si])
        v = _gelu(zv[si])
        mu = jnp.mean(v, axis=-1, keepdims=True)
        vc = v - mu
        var = jnp.mean(vc * vc, axis=-1, keepdims=True)
        vn = vc * lax.rsqrt(var + EPS) * lng_ref[...] + lnb_ref[...]
        if emit_v:
            _store_rows(vn_ref, rs, vn)
        vb = vn.astype(BF16)
        for g in range(GM_GROUPS):
            cs = slice(g * GM_GROUP_W, (g + 1) * GM_GROUP_W)
            for i in range(sub // blk):
                ls = slice(i * blk, (i + 1) * blk)
                s = _bdot(wsm[g], vb[ls, cs]) + bs_ref[:, cs]
                a_sc[rs.start + i * blk:rs.start + (i + 1) * blk, cs] = (u[ls, cs] * s).astype(BF16)
        pa_ref[rs, :] = _bdot(a_sc[rs, :], wpa_ref[...])


def _cast_block_spec(n_rows, n_cols, steps):
    n_blocks = steps
    while n_rows % n_blocks or (n_rows // n_blocks) % BF16_ROWS:
        n_blocks //= 2
    per = steps // n_blocks
    return pl.BlockSpec((n_rows // n_blocks, n_cols), lambda i: (i // per, 0))


TCAST_ROWS = 256


def _tcast_specs(job):
    first_row, n_blocks, first_step = job
    assert first_row % SUBLANES == 0
    blk = lambda i: jnp.clip(i - first_step, 0, n_blocks - 1)
    src = pl.BlockSpec((pl.Element(TCAST_ROWS), pl.Element(D_MODEL)),
                       lambda i: (pl.multiple_of(first_row + TCAST_ROWS * blk(i), SUBLANES), 0))
    dst = pl.BlockSpec((D_MODEL, TCAST_ROWS), lambda i: (0, blk(i)))
    return src, dst, jax.ShapeDtypeStruct((D_MODEL, TCAST_ROWS * n_blocks), BF16)


def _tcast_kernel(src_ref, dst_ref):
    dst_ref[...] = src_ref[...].T.astype(BF16)


def _tcast_call(wt, first_row, n_blocks):
    src, dst, shape = _tcast_specs((first_row, n_blocks, 0))
    return pl.pallas_call(
        _tcast_kernel, grid=(n_blocks,), in_specs=[src], out_specs=dst, out_shape=shape,
        compiler_params=pltpu.CompilerParams(dimension_semantics=("arbitrary",)),
        name="tcast",
    )(wt)


def _gmlp_call(x, g1, wuv, lng, lnb, ws_t, bs_t, wpa, *, chunk, emit_v, emit_h, tm, w_ml=None,
               cast_weights=(), tcast_weight=None, tcast_jobs=()):
    m = _n_tokens(x)
    blk = ws_t.shape[1]
    steps = m // tm
    assert steps & (steps - 1) == 0
    assert all(first_step + n_blocks <= steps for _, n_blocks, first_step in tcast_jobs)
    row = pl.BlockSpec((tm, D_MODEL), lambda i: (i, 0))
    out_shape = [jax.ShapeDtypeStruct((m, D_MODEL), F32)]
    out_specs = [row]
    scratch = [pltpu.VMEM((tm, GM_WIDTH), BF16)]
    cast_specs = [_cast_block_spec(cw.shape[0], cw.shape[1], steps) for cw in cast_weights]
    tcast = [_tcast_specs(job) for job in tcast_jobs]
    if emit_v:
        out_shape.append(jax.ShapeDtypeStruct(x.shape, F32))
        out_specs.append(_token_spec(x.shape, tm))
    if emit_h:
        out_shape.append(jax.ShapeDtypeStruct((m, D_MODEL), F32))
        out_specs.append(row)
    emit_zq = w_ml is not None
    if emit_zq:
        out_shape.append(jax.ShapeDtypeStruct((m, ML_COLS), F32))
        out_specs.append(pl.BlockSpec((tm, ML_COLS), lambda i: (i, 0)))
    out_shape += [jax.ShapeDtypeStruct(cw.shape, BF16) for cw in cast_weights]
    out_shape += [t[2] for t in tcast]
    out_specs += cast_specs + [t[1] for t in tcast]
    return pl.pallas_call(
        functools.partial(_gmlp_kernel, chunk=chunk, emit_v=emit_v, emit_h=emit_h, emit_zq=emit_zq,
                          n_cast=len(cast_weights), n_tcast=len(tcast)),
        grid=(steps,),
        in_specs=[_token_spec(x.shape, tm), _const_spec((1, D_MODEL)),
                  _const_spec((D_MODEL, 2 * GM_WIDTH)),
                  _const_spec((1, GM_WIDTH)), _const_spec((1, GM_WIDTH)),
                  _const_spec((GM_GROUPS, blk, blk)), _const_spec((blk, GM_WIDTH)),
                  _const_spec((GM_WIDTH, D_MODEL))]
        + (_ml_weight_specs() if emit_zq else []) + cast_specs + [t[0] for t in tcast],
        out_specs=out_specs,
        out_shape=out_shape,
        scratch_shapes=scratch,
        compiler_params=pltpu.CompilerParams(dimension_semantics=("arbitrary",),
                                             vmem_limit_bytes=VMEM_LIMIT),
        name="gmlp",
    )(x, g1, wuv, lng, lnb, ws_t, bs_t, wpa, *([w_ml] * 3 if emit_zq else []),
      *cast_weights, *([tcast_weight] * len(tcast)))


def _ml_weight_specs():
    wide = 2 * ML_WIDTH
    col_block = lambda width, idx: pl.BlockSpec((D_MODEL, width), lambda *_: (0, idx),
                                                pipeline_mode=pl.Buffered(1))
    return [col_block(wide, 0), col_block(wide, 1), col_block(GATE_COLS, 4 * ML_WIDTH // GATE_COLS)]


def _mlstm_heads(q_of, k_of, v_of, o_of, causal, ipre, bcum, groups, m0_of, C0_of, n0_of, hng_ref,
                 fill=None):
    n_groups = len(groups)
    single = n_groups == 1
    fill = fill or (lambda: None)

    def rows_of(x, g):
        return x if single else x[groups[g][0]:groups[g][0] + groups[g][1]]

    def per_row(vals):
        if single:
            return vals[0]
        return jnp.concatenate([jnp.broadcast_to(v, (groups[g][1], v.shape[1]))
                                for g, v in enumerate(vals)], axis=0)

    a = ipre - bcum
    a_t = a.T
    m_rows = per_row([m0_of(g) for g in range(n_groups)])

    def prepare(h):
        p = {}
        a2 = jnp.where(causal, a_t[h:h + 1, :], -jnp.inf)
        p["mc"] = mc = jnp.maximum(jnp.max(a2, axis=1, keepdims=True), m_rows[:, h:h + 1])
        p["m_last"] = m_last = [mc[grp[2]:grp[2] + 1, :] for grp in groups]
        p["dm"] = jnp.exp(a2 - (mc + LN_INV_K_SCALE))
        p["w_inter"] = jnp.exp(m_rows[:, h:h + 1] - mc)
        p["w_col"] = w_col = jnp.exp(a[:, h:h + 1] - (per_row(m_last) + LN_INV_K_SCALE))
        p["decay"] = [jnp.exp(m0_of(g)[:, h:h + 1] - m_last[g]) for g in range(n_groups)]
        p["q"] = q = q_of(h)
        p["k"] = k = k_of(h)
        v = v_of(h)
        p["qb"], p["kb"], p["vb"] = q.astype(BF16), k.astype(BF16), v.astype(BF16)
        p["vw"] = (v * w_col).astype(BF16)
        p["c_old"] = [C0_of(g, h) for g in range(n_groups)]
        p["n_old"] = [n0_of(g, h) for g in range(n_groups)]
        return p

    def first_matmuls(p):
        p["qk"] = lax.dot_general(p["qb"], p["kb"], NT_DIMS, preferred_element_type=F32)
        p["qc"] = [lax.dot_general(rows_of(p["qb"], g), p["c_old"][g].astype(BF16), NT_DIMS,
                                   preferred_element_type=F32) for g in range(n_groups)]
        if single:
            n_rows = jnp.broadcast_to(p["n_old"][0], (LANES, ML_HEAD_DIM)).astype(BF16)
            p["qn"] = lax.dot_general(p["qb"], n_rows, NT_DIMS, preferred_element_type=F32)[:, 0:1]
        else:
            p["qn"] = jnp.sum(p["q"] * per_row(p["n_old"]), axis=1, keepdims=True)

    def second_matmuls(p):
        p["s"] = s = p["dm"] * p["qk"]
        p["sv"] = _bdot(s.astype(BF16), p["vb"])
        p["cupd"] = [lax.dot_general(rows_of(p["vw"], g), rows_of(p["kb"], g), TN_DIMS,
                                     preferred_element_type=F32) for g in range(n_groups)]

    def finish(h, p):
        qc_rows = p["qc"][0] if single else jnp.concatenate(p["qc"], axis=0)
        num = p["w_inter"] * qc_rows + p["sv"]
        den = p["w_inter"] * p["qn"] + jnp.sum(p["s"], axis=1, keepdims=True)
        hcur = num / jnp.maximum(jnp.abs(den), jnp.exp(-(bcum[:, h:h + 1] + p["mc"])))
        mu = jnp.mean(hcur, axis=1, keepdims=True)
        hc = hcur - mu
        var = jnp.mean(hc * hc, axis=1, keepdims=True)
        hs = slice(h * ML_HEAD_DIM, (h + 1) * ML_HEAD_DIM)
        out = jax.nn.sigmoid(o_of(h)) * (hc * lax.rsqrt(var + EPS) * hng_ref[:, hs])
        kw = p["k"] * p["w_col"]
        c_new = [p["decay"][g] * p["c_old"][g] + p["cupd"][g] for g in range(n_groups)]
        n_new = [p["decay"][g] * p["n_old"][g] + jnp.sum(rows_of(kw, g), axis=0, keepdims=True)
                 for g in range(n_groups)]
        return out, c_new, n_new

    H = ML_HEADS
    per_head = [None] * H
    done = [None] * H
    if single:
        per_head[0] = prepare(0)
        fill()
        first_matmuls(per_head[0])
        if H > 1:
            per_head[1] = prepare(1)
        fill()
        for h in range(H):
            second_matmuls(per_head[h])
            if h + 1 < H:
                first_matmuls(per_head[h + 1])
            fill()
            if h + 2 < H:
                per_head[h + 2] = prepare(h + 2)
            done[h] = finish(h, per_head[h])
            fill()
    else:
        per_head = [prepare(h) for h in range(H)]
        for stage in (first_matmuls, second_matmuls):
            for h in range(H):
                stage(per_head[h])
        done = [finish(h, per_head[h]) for h in range(H)]

    outs = [done[h][0] for h in range(H)]
    C_new = [[done[h][1][g] for h in range(H)] for g in range(n_groups)]
    n_new = [[done[h][2][g] for h in range(H)] for g in range(n_groups)]
    m_new = []
    for g, grp in enumerate(groups):
        row = m0_of(g)
        lane = lax.broadcasted_iota(jnp.int32, row.shape, 1)
        for h in range(H):
            row = jnp.where(lane == h, bcum[grp[2]:grp[2] + 1, h:h + 1] + per_head[h]["m_last"][g], row)
        m_new.append(row)
    return outs, C_new, n_new, m_new


def _mlstm_kernel(zq_ref, cst_ref, C0_ref, n0_ref, m0_ref, bif_ref, cw_ref, cb_ref, hng_ref,
                  bo_ref, conv_ref, C_ref, n_ref, m_ref, zp, xp, *, t_valid):
    nb = C0_ref.shape[0]
    L = BF16_ROWS
    R = nb * L

    @pl.when(pl.program_id(0) == 0)
    def _():
        zp[...] = jnp.zeros(zp.shape, F32)

    for bb in range(nb):
        zp[bb, 0:t_valid, :] = zq_ref[bb * t_valid:(bb + 1) * t_valid, :]
        for j in range(CONV_W - 1):
            row = SUBLANES - (CONV_W - 1) + j
            xp[bb, row:row + 1, :] = cst_ref[j, bb:bb + 1, :]
        xp[bb, SUBLANES:2 * SUBLANES, :] = zp[bb, 0:SUBLANES, 0:2 * ML_WIDTH]
        for j in range(CONV_W - 1):
            row = SUBLANES + t_valid - (CONV_W - 1) + j
            conv_ref[j, bb:bb + 1, :] = xp[bb, row:row + 1, :]
    qk_rows = []
    for bb in range(nb):
        acc = cb_ref[...]
        for j in range(CONV_W):
            off = SUBLANES - (CONV_W - 1) + j
            acc = acc + cw_ref[j:j + 1, :] * xp[bb, off:off + SUBLANES, :]
        qk_rows += [acc, jnp.zeros((L - SUBLANES, 2 * ML_WIDTH), F32)]
    qk = jnp.concatenate(qk_rows, axis=0)
    qk = qk * jax.nn.sigmoid(qk)

    def cols(c0, width):
        return zp[:, :, c0:c0 + width].reshape(R, width)

    zif = cols(4 * ML_WIDTH, GATE_COLS) + bif_ref[...]
    live = (lax.broadcasted_iota(jnp.int32, (R, GATE_COLS), 0) & (L - 1)) < t_valid
    ipre = jnp.where(live, zif, NEG_INF)
    logf = jnp.where(live, pltpu.roll(_log_sigmoid(zif), GATE_COLS - ML_HEADS, axis=1), 0.0)
    assert t_valid <= SUBLANES
    sub_g = lax.broadcasted_iota(jnp.int32, (SUBLANES, GATE_COLS), 0)
    parts = []
    for bb in range(nb):
        incl = logf[bb * L:bb * L + SUBLANES, :]
        for step in (1, 2, 4):
            incl = incl + jnp.where(sub_g >= step, pltpu.roll(incl, step, axis=0), 0.0)
        parts += [incl, jnp.broadcast_to(incl[SUBLANES - 1:SUBLANES, :], (L - SUBLANES, GATE_COLS))]
    bcum = jnp.concatenate(parts, axis=0)
    r = lax.broadcasted_iota(jnp.int32, (R, R), 0)
    c = lax.broadcasted_iota(jnp.int32, (R, R), 1)
    sh = L.bit_length() - 1
    causal = jnp.logical_and(c <= r, (r >> sh) == (c >> sh))
    live_w = (lax.broadcasted_iota(jnp.int32, (R, ML_HEAD_DIM), 0) & (L - 1)) < t_valid

    def head_cols(x, base, h):
        return x[:, base + h * ML_HEAD_DIM:base + (h + 1) * ML_HEAD_DIM]

    groups = [(bb * L, L, bb * L + L - 1) for bb in range(nb)]
    outs, C_new, n_new, m_new = _mlstm_heads(
        q_of=lambda h: head_cols(qk, 0, h),
        k_of=lambda h: jnp.where(live_w, head_cols(qk, ML_WIDTH, h), 0.0),
        v_of=lambda h: jnp.where(live_w, cols(2 * ML_WIDTH + h * ML_HEAD_DIM, ML_HEAD_DIM), 0.0),
        o_of=lambda h: cols(3 * ML_WIDTH + h * ML_HEAD_DIM, ML_HEAD_DIM),
        causal=causal, ipre=ipre, bcum=bcum, groups=groups,
        m0_of=lambda g: m0_ref[g:g + 1, :], C0_of=lambda g, h: C0_ref[g, h],
        n0_of=lambda g, h: n0_ref[g, h:h + 1, :], hng_ref=hng_ref)
    for bb in range(nb):
        for h in range(ML_HEADS):
            hs = slice(h * ML_HEAD_DIM, (h + 1) * ML_HEAD_DIM)
            bo_ref[bb * t_valid:(bb + 1) * t_valid, hs] = outs[h][bb * L:bb * L + t_valid, :]
            C_ref[bb, h] = C_new[bb][h]
            n_ref[bb, h:h + 1, :] = n_new[bb][h]
        m_ref[bb:bb + 1, :] = m_new[bb]


def _mlstm_call(zq, cst, state, bif, cw, cb, hng, *, t_valid, nb):
    T = t_valid
    B = zq.shape[0] // T
    kern = functools.partial(_mlstm_kernel, t_valid=T)
    per_b = lambda shape: pl.BlockSpec((nb,) + shape, lambda b: (b,) + (0,) * len(shape))
    rows = lambda width: pl.BlockSpec((nb * T, width), lambda b: (b, 0))
    conv_spec = pl.BlockSpec((CONV_W - 1, nb, 2 * ML_WIDTH), lambda b: (0, b, 0))
    st_specs = [per_b((ML_HEADS, ML_HEAD_DIM, ML_HEAD_DIM)), per_b((ML_HEADS, ML_HEAD_DIM)),
                per_b((ML_HEADS,))]
    in_specs = ([rows(ML_COLS), conv_spec] + st_specs
                + [_const_spec((1, GATE_COLS)), _const_spec((CONV_W, 2 * ML_WIDTH)),
                   _const_spec((1, 2 * ML_WIDTH)), _const_spec((1, ML_WIDTH))])
    out_shape = [jax.ShapeDtypeStruct((B * T, ML_WIDTH), F32),
                 jax.ShapeDtypeStruct((CONV_W - 1, B, 2 * ML_WIDTH), F32),
                 jax.ShapeDtypeStruct((B, ML_HEADS, ML_HEAD_DIM, ML_HEAD_DIM), F32),
                 jax.ShapeDtypeStruct((B, ML_HEADS, ML_HEAD_DIM), F32),
                 jax.ShapeDtypeStruct((B, ML_HEADS), F32)]
    out_specs = [rows(ML_WIDTH), conv_spec] + st_specs
    return pl.pallas_call(
        kern,
        grid=(B // nb,),
        in_specs=in_specs,
        out_specs=out_specs,
        out_shape=out_shape,
        scratch_shapes=[pltpu.VMEM((nb, BF16_ROWS, ML_COLS), F32),
                        pltpu.VMEM((nb, 2 * SUBLANES, 2 * ML_WIDTH), F32)],
        compiler_params=pltpu.CompilerParams(dimension_semantics=("arbitrary",),
                                             vmem_limit_bytes=VMEM_LIMIT),
        name="mlstm",
    )(zq, cst, *state, bif, cw, cb, hng)


PL = 256
PG = PL // SUBLANES
TAIL = (CONV_W - 1) * SUBLANES
PROJ_PIECE_COLS = 512
DMA_PRIORITIES = 2


def _mixer_ml_kernel(h_hbm, wqk_ref, wvo_ref, wif_ref, bif_ref, cw_ref, cb_ref, hng_ref,
                     bo_hbm, conv_ref, C_ref, n_ref, m_ref,
                     zqk_buf, zvo_buf, zif_buf, tail, hbuf, obuf, sem_in, sem_out,
                     *, chunks_per_seq, n_chunks):
    s = pl.program_id(0)

    def chunk_dmas(chunk, slot, inbound):
        row0 = chunk * PL
        copies = []
        for i in range(SUBLANES):
            hbm = (h_hbm if inbound else bo_hbm).at[pl.ds(row0 + PG * i, PG), :]
            vmem = (hbuf if inbound else obuf).at[slot, :, i, :]
            sem = (sem_in if inbound else sem_out).at[slot]
            copies.append(pltpu.make_async_copy(hbm, vmem, sem) if inbound
                          else pltpu.make_async_copy(vmem, hbm, sem))
        return copies

    def start_all(copies):
        for i, cp in enumerate(copies):
            cp.start(priority=i % DMA_PRIORITIES)

    @pl.when(s == 0)
    def _():
        zqk_buf[1] = jnp.zeros(zqk_buf.shape[1:], F32)
        zvo_buf[1] = jnp.zeros(zvo_buf.shape[1:], F32)
        zif_buf[1] = jnp.zeros(zif_buf.shape[1:], F32)
        start_all(chunk_dmas(0, 0, True))

    @pl.when(jnp.maximum(s - 1, 0) % chunks_per_seq == 0)
    def _():
        tail[...] = jnp.zeros(tail.shape, F32)
        C_ref[...] = jnp.zeros(C_ref.shape, F32)
        n_ref[...] = jnp.zeros(n_ref.shape, F32)
        m_ref[...] = jnp.zeros(m_ref.shape, F32)

    def step(rd, wr):
        start_all(chunk_dmas(jnp.minimum(s + 1, n_chunks - 1), rd, True))
        for cp in chunk_dmas(jnp.minimum(s, n_chunks - 1), wr, True):
            cp.wait()

        _mixer_ml_step(hbuf.at[wr], wqk_ref, wvo_ref, wif_ref, bif_ref, cw_ref, cb_ref, hng_ref,
                       obuf.at[rd], conv_ref, C_ref, n_ref, m_ref, zqk_buf, zvo_buf, zif_buf, tail,
                       rd=rd, wr=wr)

        @pl.when(s >= 1)
        def _():
            start_all(chunk_dmas(s - 1, rd, False))

        @pl.when(s >= 2)
        def _():
            for cp in chunk_dmas(s - 2, wr, False):
                cp.wait()

        @pl.when(s == n_chunks)
        def _():
            for cp in chunk_dmas(n_chunks - 1, rd, True) + chunk_dmas(s - 1, rd, False):
                cp.wait()

    @pl.when(s % 2 == 0)
    def _():
        step(rd=1, wr=0)

    @pl.when(s % 2 == 1)
    def _():
        step(rd=0, wr=1)


def _mixer_ml_step(h_ref, wqk_ref, wvo_ref, wif_ref, bif_ref, cw_ref, cb_ref, hng_ref,
                   bo_ref, conv_ref, C_ref, n_ref, m_ref, zqk_buf, zvo_buf, zif_buf, tail,
                   *, rd, wr):
    h = h_ref[...].reshape(PL, D_MODEL).astype(BF16)

    def piece(w_ref, buf, c0, width):
        def emit():
            buf[wr, :, c0:c0 + width] = _bdot(h, w_ref[:, c0:c0 + width])
        return emit

    pieces = [piece(w_ref, buf, c0, PROJ_PIECE_COLS)
              for w_ref, buf in ((wqk_ref, zqk_buf), (wvo_ref, zvo_buf))
              for c0 in range(0, 2 * ML_WIDTH, PROJ_PIECE_COLS)]
    pieces.append(piece(wif_ref, zif_buf, 0, GATE_COLS))
    pieces = iter(pieces)

    def fill():
        emit = next(pieces, None)
        if emit is not None:
            emit()

    fill()
    zqk = zqk_buf.at[rd]
    zvo = zvo_buf.at[rd]
    zif = zif_buf[rd] + bif_ref[...]

    zqk_tail = zqk[PL - TAIL:, :]
    sub = lax.broadcasted_iota(jnp.int32, (SUBLANES, 2 * ML_WIDTH), 0)
    wrapped = []
    for g in range(CONV_W - 1):
        cur = pltpu.roll(zqk_tail[g * SUBLANES:(g + 1) * SUBLANES], 1, axis=0)
        prev = pltpu.roll(tail[g * SUBLANES:(g + 1) * SUBLANES, :], 1, axis=0)
        wrapped.append(jnp.where(sub == 0, prev, cur))
    wrapped = jnp.concatenate(wrapped, axis=0)
    tail[...] = zqk_tail
    conv_ref[...] = jnp.concatenate(
        [zqk_tail[g * SUBLANES + SUBLANES - 1:(g + 1) * SUBLANES, :] for g in range(CONV_W - 1)], axis=0)

    def conv_silu(c0, width):
        cs = slice(c0, c0 + width)
        acc = cb_ref[:, cs] + cw_ref[CONV_W - 1:CONV_W, cs] * zqk[:, cs]
        for d in range(1, CONV_W):
            shifted = jnp.concatenate(
                [wrapped[TAIL - d * SUBLANES:, cs], zqk[:PL - d * SUBLANES, cs]], axis=0)
            acc = acc + cw_ref[CONV_W - 1 - d:CONV_W - d, cs] * shifted
        return acc * jax.nn.sigmoid(acc)

    pr = lax.broadcasted_iota(jnp.int32, (PL, PL), 0)
    pc = lax.broadcasted_iota(jnp.int32, (PL, PL), 1)
    bits, low = SUBLANES.bit_length() - 1, SUBLANES - 1
    causal = ((pc >> bits) + PG * (pc & low)) <= ((pr >> bits) + PG * (pr & low))

    logf = pltpu.roll(_log_sigmoid(zif), GATE_COLS - ML_HEADS, axis=1)
    run, partial = None, []
    for n in range(PG):
        blk = logf[n * SUBLANES:(n + 1) * SUBLANES, :]
        run = blk if run is None else run + blk
        partial.append(run)
    sub_g = lax.broadcasted_iota(jnp.int32, (SUBLANES, GATE_COLS), 0)
    incl = run
    for step in (1, 2, 4):
        incl = incl + jnp.where(sub_g >= step, pltpu.roll(incl, step, axis=0), 0.0)
    earlier = incl - run
    bcum = jnp.concatenate([p + earlier for p in partial], axis=0)

    def head_cols(base, h):
        return zvo[:, base + h * ML_HEAD_DIM:base + (h + 1) * ML_HEAD_DIM]

    outs, C_new, n_new, m_new = _mlstm_heads(
        q_of=lambda h: conv_silu(h * ML_HEAD_DIM, ML_HEAD_DIM),
        k_of=lambda h: conv_silu(ML_WIDTH + h * ML_HEAD_DIM, ML_HEAD_DIM),
        v_of=lambda h: head_cols(0, h),
        o_of=lambda h: head_cols(ML_WIDTH, h),
        causal=causal, ipre=zif, bcum=bcum, groups=[(0, PL, PL - 1)],
        m0_of=lambda g: m_ref[...], C0_of=lambda g, h: C_ref[h], n0_of=lambda g, h: n_ref[h:h + 1, :],
        hng_ref=hng_ref, fill=fill)
    assert next(pieces, None) is None, "projection pieces left over"
    for h in range(ML_HEADS):
        bo_ref[:, :, h * ML_HEAD_DIM:(h + 1) * ML_HEAD_DIM] = outs[h].reshape(PG, SUBLANES, ML_HEAD_DIM)
        C_ref[h] = C_new[0][h]
        n_ref[h:h + 1, :] = n_new[0][h]
    m_ref[...] = m_new[0]


def _mixer_ml_call(hp, n_seq, w_all, bif, cw, cb, hng):
    B = n_seq
    T = hp.shape[0] // B
    cps = T // PL
    n_chunks = B * cps
    math = lambda s: jnp.maximum(s - 1, 0)
    per_b = lambda shape: pl.BlockSpec((None,) + shape,
                                       lambda s: (math(s) // cps,) + (0,) * len(shape))
    out_shape = [jax.ShapeDtypeStruct((B * T, ML_WIDTH), F32),
                 jax.ShapeDtypeStruct((B, CONV_W - 1, 2 * ML_WIDTH), F32),
                 jax.ShapeDtypeStruct((B, ML_HEADS, ML_HEAD_DIM, ML_HEAD_DIM), F32),
                 jax.ShapeDtypeStruct((B, ML_HEADS, ML_HEAD_DIM), F32),
                 jax.ShapeDtypeStruct((B, 1, ML_HEADS), F32)]
    out_specs = [pl.BlockSpec(memory_space=pl.ANY),
                 per_b((CONV_W - 1, 2 * ML_WIDTH)),
                 per_b((ML_HEADS, ML_HEAD_DIM, ML_HEAD_DIM)), per_b((ML_HEADS, ML_HEAD_DIM)),
                 per_b((1, ML_HEADS))]
    return pl.pallas_call(
        functools.partial(_mixer_ml_kernel, chunks_per_seq=cps, n_chunks=n_chunks),
        grid=(n_chunks + 1,),
        in_specs=[pl.BlockSpec(memory_space=pl.ANY)]
        + _ml_weight_specs()
        + [_const_spec((1, GATE_COLS)), _const_spec((CONV_W, 2 * ML_WIDTH)),
           _const_spec((1, 2 * ML_WIDTH)), _const_spec((1, ML_WIDTH))],
        out_specs=out_specs,
        out_shape=out_shape,
        scratch_shapes=[pltpu.VMEM((2, PL, 2 * ML_WIDTH), F32),
                        pltpu.VMEM((2, PL, 2 * ML_WIDTH), F32),
                        pltpu.VMEM((2, PL, GATE_COLS), F32),
                        pltpu.VMEM((TAIL, 2 * ML_WIDTH), F32),
                        pltpu.VMEM((2, PG, SUBLANES, D_MODEL), F32),
                        pltpu.VMEM((2, PG, SUBLANES, ML_WIDTH), F32),
                        pltpu.SemaphoreType.DMA((2,)),
                        pltpu.SemaphoreType.DMA((2,))],
        compiler_params=pltpu.CompilerParams(dimension_semantics=("arbitrary",),
                                             vmem_limit_bytes=VMEM_LIMIT),
        name="mixer_ml",
    )(hp, w_all, w_all, w_all, bif, cw, cb, hng)


def _merge_ffn_kernel(x_ref, pa_ref, bo_ref, g1_ref, wg_ref, bg_ref, wpb_ref, wout_ref,
                      g2_ref, wfi_ref, wfo_ref, gf_ref, y_ref, *fetch_scratch):
    ffn_weight_copies = []
    if fetch_scratch:
        wfi_vmem, wfo_vmem, wsem = fetch_scratch
        ffn_weight_copies = [pltpu.make_async_copy(wfi_ref, wfi_vmem, wsem.at[0]),
                             pltpu.make_async_copy(wfo_ref, wfo_vmem, wsem.at[1])]
        wfi_ref, wfo_ref = wfi_vmem, wfo_vmem
    for cp in ffn_weight_copies:
        cp.start()
    tm = _n_tokens(x_ref)
    subs = [slice(r0, r0 + FFN_SUB_ROWS) for r0 in range(0, tm, FFN_SUB_ROWS)]
    x = [_load_rows(x_ref, rs) for rs in subs]
    h = [_rms(xi, g1_ref[...]).astype(BF16) for xi in x]
    gab = [_bdot(hi, wg_ref[...]) + bg_ref[...] for hi in h]
    pb = [_bdot(bo_ref[rs, :].astype(BF16), wpb_ref[...]) for rs in subs]
    merged = [(jax.nn.sigmoid(g[:, :D_MODEL]) * pa_ref[rs, :]
               + jax.nn.sigmoid(g[:, D_MODEL:]) * p).astype(BF16) for g, p, rs in zip(gab, pb, subs)]
    x1 = [xi + _bdot(mi, wout_ref[...]) for xi, mi in zip(x, merged)]
    h2 = [_rms(xi, g2_ref[...]).astype(BF16) for xi in x1]
    for cp in ffn_weight_copies:
        cp.wait()
    gu = [_bdot(hi, wfi_ref[...]) for hi in h2]
    hid = [(g[:, :D_FF] * jax.nn.sigmoid(g[:, :D_FF]) * g[:, D_FF:]).astype(BF16) for g in gu]
    x2 = [xi + _bdot(hi, wfo_ref[...]) for xi, hi in zip(x1, hid)]
    for rs, xi in zip(subs, x2):
        _store_rows(y_ref, rs, _rms(xi, gf_ref[...]))


def _merge_ffn_call(x, pa, bo, g1, wg, bg, wpb, wout, g2, wfi, wfo, gf, *, tm):
    m = _n_tokens(x)
    row = pl.BlockSpec((tm, D_MODEL), lambda i: (i, 0))
    x_spec = _token_spec(x.shape, tm)
    if m == tm:
        ffn_weight_spec = lambda shape: pl.BlockSpec(memory_space=pl.ANY)
        fetch_scratch = [pltpu.VMEM(wfi.shape, BF16), pltpu.VMEM(wfo.shape, BF16),
                         pltpu.SemaphoreType.DMA((2,))]
    else:
        ffn_weight_spec, fetch_scratch = _const_spec, []
    return pl.pallas_call(
        _merge_ffn_kernel,
        grid=(m // tm,),
        in_specs=[x_spec, row, row, _const_spec((1, D_MODEL)), _const_spec((D_MODEL, 2 * D_MODEL)),
                  _const_spec((1, 2 * D_MODEL)), _const_spec((ML_WIDTH, D_MODEL)),
                  _const_spec((D_MODEL, D_MODEL)), _const_spec((1, D_MODEL)),
                  ffn_weight_spec(wfi.shape), ffn_weight_spec(wfo.shape),
                  _const_spec((1, D_MODEL))],
        out_specs=x_spec,
        out_shape=jax.ShapeDtypeStruct(x.shape, F32),
        scratch_shapes=fetch_scratch,
        compiler_params=pltpu.CompilerParams(dimension_semantics=("arbitrary",),
                                             vmem_limit_bytes=VMEM_LIMIT),
        name="merge_ffn",
    )(x, pa, bo, g1, wg, bg, wpb, wout, g2, wfi, wfo, gf)


SAMPLE_SEQS_PER_STEP = 8
TM_GMLP = 512
TM_FFN = 512


def _spatial_tiles(w_s, b_s, chunk):
    if chunk == GM_CHUNK:
        ws_t, b_pos = w_s[:, :chunk, :chunk], b_s[:, :chunk].T
    else:
        onehot = jnp.asarray(np.arange(GM_CHUNK)[:, None] % chunk == np.arange(chunk)[None, :], F32)
        hp = lax.Precision.HIGHEST
        ws_t = jnp.einsum("ri,gij,cj->grc", onehot, w_s[:, :chunk, :chunk], onehot, precision=hp)
        b_pos = jnp.dot(onehot, b_s[:, :chunk].T, precision=hp)
    bs_t = jnp.repeat(b_pos, GM_GROUP_W, axis=1)
    return ws_t, bs_t


def _gmlp_branch(xf, w, chunk, *, emit_v, emit_h, **side_jobs):
    ws_t, bs_t = _spatial_tiles(w["w_s"], w["b_s"], chunk)
    return _gmlp_call(xf, w["g1"], w["wuv"], w["lng"], w["lnb"], ws_t, bs_t, w["wpa"],
                      chunk=chunk, emit_v=emit_v, emit_h=emit_h,
                      tm=min(TM_GMLP, _n_tokens(xf)), **side_jobs)


def _merge_branch(xf, pa, bo, w):
    return _merge_ffn_call(xf, pa, bo, w["g1"], w["wg"], w["bg"], w["wpb"], w["wout"], w["g2"],
                           w["wfi"], w["wfo"], w["gf"], tm=TM_FFN)


def kernel(x_prompt, x_sample, state_conv, state_C, state_n, state_m, g_norm1, w_in, b_i, b_f, ln_g, ln_b, w_s, b_s, conv_w, conv_b, hn_g, b_gate, w_proj_a, w_proj_b, w_out, g_norm2, w_ffn_in, w_ffn_out, g_final):
    Bp, Tp, _ = x_prompt.shape
    Bs, Ts, _ = x_sample.shape
    win = w_in[0]
    c_ml = 2 * GM_WIDTH
    c_gate = c_ml + 4 * ML_WIDTH + 2 * ML_HEADS
    n_ml_blocks = -(-(ML_COLS) // TCAST_ROWS)
    win_t = jnp.transpose(win)
    w = dict(
        g1=g_norm1[0][None], g2=g_norm2[0][None], gf=g_final[None],
        wuv=_tcast_call(win_t, 0, c_ml // TCAST_ROWS),
        bg=b_gate[0].reshape(1, 2 * D_MODEL),
        lng=ln_g[0][None], lnb=ln_b[0][None], w_s=w_s[0], b_s=b_s[0],
        bif=jnp.pad(jnp.concatenate([b_i[0], b_f[0]]), (0, GATE_COLS - 2 * ML_HEADS))[None],
        cw=conv_w[0], cb=conv_b[0][None], hng=hn_g[0][None],
        wpa=w_proj_a[0].astype(BF16),
    )

    xpf = x_prompt.reshape(Bp * Tp, D_MODEL)
    pa_p, hp_p, w["wpb"], w["wout"], w["wfi"], w["wfo"], w["w_ml"], w["wg"] = _gmlp_branch(
        xpf, w, GM_CHUNK, emit_v=False, emit_h=True,
        cast_weights=(w_proj_b[0], w_out[0], w_ffn_in[0], w_ffn_out[0]),
        tcast_weight=win_t,
        tcast_jobs=((c_ml, n_ml_blocks, 0), (c_gate, 2 * D_MODEL // TCAST_ROWS, n_ml_blocks)))
    bo_p, conv_p, C_p, n_p, m_p = _mixer_ml_call(hp_p, Bp, w["w_ml"], w["bif"], w["cw"], w["cb"],
                                                 w["hng"])
    y_p = _merge_branch(xpf, pa_p, bo_p, w)

    pa_s, vn_s, zq = _gmlp_branch(x_sample, w, Ts, emit_v=True, emit_h=False, w_ml=w["w_ml"])
    st = (state_C[0], state_n[0], state_m[0])
    bo_s, conv_s, C_s, n_s, m_s = _mlstm_call(zq, jnp.transpose(state_conv[0], (1, 0, 2)), st, w["bif"],
                                              w["cw"], w["cb"], w["hng"], t_valid=Ts,
                                              nb=SAMPLE_SEQS_PER_STEP)
    y_s = _merge_branch(x_sample, pa_s, bo_s, w)

    return (y_p.reshape(Bp, Tp, D_MODEL), y_s,
            conv_p[None], C_p[None], n_p[None], m_p.reshape(1, Bp, ML_HEADS),
            jnp.transpose(conv_s, (1, 0, 2))[None], C_s[None], n_s[None], m_s[None],
            vn_s[None])
```
